```python
import math
import jax
import jax.numpy as jnp
from jax import lax
import numpy as np

D_MODEL = 1024
BATCH = 8
SEQ = 2048
DEPTH = 2
DEC_BATCH = 128
DEC_SEQ = 4
PAST_LEN = 2048
PAGE_SIZE = 128

HEAD_DIM = 64
A_HEADS = 4
B_HEADS = 8
IDX_HEADS = 8
IDX_DIM = 64
IDX_TOPK_MAX = 256
C_HEADS = D_MODEL // HEAD_DIM
REL_BUCKETS = 32
REL_MAX_DIST = 128
N_BIAS_HEADS = A_HEADS + B_HEADS
N_GROUPS = 4
EXPERTS_PER_GROUP = 8
EXPERT_FF = 256
TOP_K_EXPERTS = 2
Q_BLOCK = 128
N_EVEN = (DEPTH + 1) // 2
N_ODD = DEPTH // 2
EPS = 1e-6

A_QK = A_HEADS * 2 * HEAD_DIM
A_V = A_HEADS * 2 * HEAD_DIM
B_Q = B_HEADS * HEAD_DIM
I_Q = IDX_HEADS * IDX_DIM
AB_SIZES = (A_QK, A_QK, A_V, B_Q, HEAD_DIM, HEAD_DIM, I_Q, IDX_DIM, IDX_HEADS)
AB_SPLIT_POINTS = tuple(int(v) for v in np.cumsum(AB_SIZES)[:-1])
AB_IN = int(sum(AB_SIZES))
AB_OUT = A_V + B_Q
C_W = C_HEADS * HEAD_DIM
C_SPLIT_POINTS = (C_W, 2 * C_W, 3 * C_W)
C_IN = 3 * C_W + C_HEADS
C_OUT = C_W

kernel_name = "hybrid_diff_dsa_fox_hmoe_step"


def rms_norm(x, g):
    xf = x.astype(jnp.float32)
    y = xf * lax.rsqrt(jnp.mean(xf * xf, axis=-1, keepdims=True) + EPS)
    return (y * g.astype(jnp.float32)).astype(x.dtype)


def t5_bucket(rel):
    n = jnp.maximum(rel, 0)
    exact = REL_BUCKETS // 2
    nf = jnp.maximum(n, 1).astype(jnp.float32)
    large = exact + (jnp.log(nf / exact) / math.log(REL_MAX_DIST / exact) * (REL_BUCKETS - exact)).astype(jnp.int32)
    large = jnp.minimum(large, REL_BUCKETS - 1)
    return jnp.where(n < exact, n, large)


def sweep_query_blocks(fn, q_arrays, q_pos):
    n_blk = q_pos.shape[0] // Q_BLOCK

    def split(a):
        return jnp.moveaxis(a.reshape(a.shape[0], n_blk, Q_BLOCK, *a.shape[2:]), 1, 0)

    blocks = tuple(split(a) for a in q_arrays)
    out = lax.map(lambda args: fn(args[0], args[1]), (blocks, q_pos.reshape(n_blk, Q_BLOCK)))
    out = jnp.moveaxis(out, 0, 1)
    return out.reshape(out.shape[0], -1, *out.shape[3:])


def gather_pages(cache, layer, page_table):
    rows = cache[layer, page_table]
    return rows.reshape(rows.shape[0], rows.shape[1] * rows.shape[2], *rows.shape[3:])


def diff_attention(q, k, v, q_pos, k_pos, rel_bias, lam, lam_init, g_sub):
    s = jnp.einsum('bqhcd,bkhcd->bhcqk', q, k).astype(jnp.float32) * HEAD_DIM ** -0.5
    bias = rel_bias[t5_bucket(q_pos[:, None] - k_pos[None, :])][..., :A_HEADS].astype(jnp.float32)
    s = s + jnp.transpose(bias, (2, 0, 1))[None, :, None]
    s = jnp.where(k_pos[None, :] <= q_pos[:, None], s, -jnp.inf)
    p = jax.nn.softmax(s, axis=-1)
    w = p[:, :, 0] - lam * p[:, :, 1]
    o = jnp.einsum('bhqk,bkhe->bqhe', w.astype(v.dtype), v)
    o = rms_norm(o, g_sub) * (1.0 - lam_init)
    return o.reshape(o.shape[0], o.shape[1], -1)


def dsa_attention(q, k, v, qi, ki, wi, q_pos, k_pos, rel_bias, topk):
    dots = jnp.einsum('bqhd,bkd->bqhk', qi, ki).astype(jnp.float32) * IDX_DIM ** -0.5
    score = jnp.einsum('bqh,bqhk->bqk', wi.astype(jnp.float32) * IDX_HEADS ** -0.5, jax.nn.relu(dots))
    score = jnp.where((k_pos[None, :] <= q_pos[:, None])[None], score, -jnp.inf)
    _, sel = lax.top_k(score, topk)
    sel_pos = k_pos[sel]
    valid = sel_pos <= q_pos[None, :, None]
    gather = jax.vmap(lambda rows, idx: rows[idx])
    k_sel = gather(k, sel)
    v_sel = gather(v, sel)
    s = jnp.einsum('bqhd,bqkd->bhqk', q, k_sel).astype(jnp.float32) * HEAD_DIM ** -0.5
    bias = rel_bias[t5_bucket(q_pos[None, :, None] - sel_pos)][..., A_HEADS:].astype(jnp.float32)
    s = s + jnp.transpose(bias, (0, 3, 1, 2))
    s = jnp.where(valid[:, None], s, -jnp.inf)
    p = jax.nn.softmax(s, axis=-1)
    o = jnp.einsum('bhqk,bqkd->bqhd', p.astype(v.dtype), v_sel)
    return o.reshape(o.shape[0], o.shape[1], -1)


def fox_attention(q, k, v, cq, ck, q_pos, k_pos):
    s = jnp.einsum('bqhd,bkhd->bhqk', q, k).astype(jnp.float32) * HEAD_DIM ** -0.5
    s = s + jnp.transpose(cq, (0, 2, 1))[..., None] - jnp.transpose(ck, (0, 2, 1))[:, :, None, :]
    s = jnp.where(k_pos[None, :] <= q_pos[:, None], s, -jnp.inf)
    p = jax.nn.softmax(s, axis=-1)
    o = jnp.einsum('bhqk,bkhd->bqhd', p.astype(v.dtype), v)
    return o.reshape(o.shape[0], o.shape[1], -1)


def ab_project(x, norm_g, w_in, qn_a, kn_a, qn_b, kn_b):
    b, t = x.shape[:2]
    h = rms_norm(x, norm_g)
    proj = jnp.einsum('btd,de->bte', h, w_in)
    p = jnp.split(proj, AB_SPLIT_POINTS, axis=-1)
    qa = rms_norm(p[0].reshape(b, t, A_HEADS, 2, HEAD_DIM), qn_a)
    ka = rms_norm(p[1].reshape(b, t, A_HEADS, 2, HEAD_DIM), kn_a)
    va = p[2].reshape(b, t, A_HEADS, 2 * HEAD_DIM)
    qb = rms_norm(p[3].reshape(b, t, B_HEADS, HEAD_DIM), qn_b)
    kb = rms_norm(p[4], kn_b)
    vb = p[5]
    qi = p[6].reshape(b, t, IDX_HEADS, IDX_DIM)
    return qa, ka, va, qb, kb, vb, qi, p[7], p[8]


def c_project(x, norm_g, w_in, b_f, qn, kn):
    b, t = x.shape[:2]
    h = rms_norm(x, norm_g)
    proj = jnp.einsum('btd,de->bte', h, w_in)
    q, k, v, f = jnp.split(proj, C_SPLIT_POINTS, axis=-1)
    q = rms_norm(q.reshape(b, t, C_HEADS, HEAD_DIM), qn)
    k = rms_norm(k.reshape(b, t, C_HEADS, HEAD_DIM), kn)
    v = v.reshape(b, t, C_HEADS, HEAD_DIM)
    logf = jax.nn.log_sigmoid(f.astype(jnp.float32) + b_f.astype(jnp.float32))
    return q, k, v, logf


def hier_moe(x, norm_g, w_group, b_group, w_router, b_router, w_gate, w_up, w_down):
    shape = x.shape
    t = rms_norm(x, norm_g).reshape(-1, shape[-1])
    n = t.shape[0]
    grp_logits = jnp.einsum('nd,dg->ng', t, w_group).astype(jnp.float32) + b_group.astype(jnp.float32)
    grp_prob = jax.nn.softmax(grp_logits, axis=-1)
    grp = jnp.argmax(grp_logits, axis=-1)
    grp_w = jnp.take_along_axis(grp_prob, grp[:, None], axis=1)
    exp_logits = (jnp.einsum('nd,de->ne', t, w_router).astype(jnp.float32) + b_router.astype(jnp.float32)).reshape(n, N_GROUPS, EXPERTS_PER_GROUP)
    in_grp = jnp.take_along_axis(exp_logits, grp[:, None, None], axis=1)[:, 0]
    top_v, top_i = lax.top_k(in_grp, TOP_K_EXPERTS)
    top_w = jax.nn.softmax(top_v, axis=-1)
    exp_gate = jnp.sum(jax.nn.one_hot(top_i, EXPERTS_PER_GROUP, dtype=jnp.float32) * top_w[..., None], axis=1)
    gate = jax.nn.one_hot(grp, N_GROUPS, dtype=jnp.float32)[:, :, None] * (grp_w * exp_gate)[:, None, :]
    out = jnp.zeros_like(t)
    for g in range(N_GROUPS):
        a = jnp.einsum('nd,edf->nef', t, w_gate[g])
        u = jnp.einsum('nd,edf->nef', t, w_up[g])
        hdn = jax.nn.silu(a) * u * gate[:, g, :, None].astype(t.dtype)
        out = out + jnp.einsum('nef,efd->nd', hdn, w_down[g])
    return x + out.reshape(shape)


def setup_inputs(seed: int = 0) -> dict:
    key = jax.random.key(seed)
    ks = iter(jax.random.split(key, 48))
    f32 = jnp.float32

    def nrm(shape, scale=1.0):
        return jax.random.normal(next(ks), shape, f32) * scale

    def gain(shape):
        return 1.0 + 0.05 * jax.random.normal(next(ks), shape, f32)

    n_pages = PAST_LEN // PAGE_SIZE
    n_used = DEC_BATCH * n_pages
    pool = n_used + (n_used + 3) // 4
    d = D_MODEL
    inp = {}
    inp['x_prompt'] = nrm((BATCH, SEQ, d))
    inp['x_sample'] = nrm((DEC_BATCH, DEC_SEQ, d))
    inp['cache_a_k'] = nrm((N_EVEN, pool, PAGE_SIZE, A_HEADS, 2, HEAD_DIM))
    inp['cache_a_v'] = nrm((N_EVEN, pool, PAGE_SIZE, A_HEADS, 2 * HEAD_DIM))
    inp['cache_b_k'] = nrm((N_EVEN, pool, PAGE_SIZE, HEAD_DIM))
    inp['cache_b_v'] = nrm((N_EVEN, pool, PAGE_SIZE, HEAD_DIM))
    inp['cache_b_kidx'] = nrm((N_EVEN, pool, PAGE_SIZE, IDX_DIM))
    inp['cache_c_k'] = nrm((N_ODD, pool, PAGE_SIZE, C_HEADS, HEAD_DIM))
    inp['cache_c_v'] = nrm((N_ODD, pool, PAGE_SIZE, C_HEADS, HEAD_DIM))
    inp['cache_c_logf'] = jax.nn.log_sigmoid(nrm((N_ODD, pool, PAGE_SIZE, C_HEADS)) + 2.5)
    inp['page_table'] = jax.random.permutation(next(ks), pool)[:n_used].reshape(DEC_BATCH, n_pages).astype(jnp.int32)
    inp['rel_bias'] = nrm((REL_BUCKETS, N_BIAS_HEADS), 0.5)
    inp['ab_norm'] = gain((N_EVEN, d))
    inp['ab_w_in'] = nrm((N_EVEN, d, AB_IN), d ** -0.5)
    inp['a_q_norm'] = gain((N_EVEN, HEAD_DIM))
    inp['a_k_norm'] = gain((N_EVEN, HEAD_DIM))
    inp['b_q_norm'] = gain((N_EVEN, HEAD_DIM))
    inp['b_k_norm'] = gain((N_EVEN, HEAD_DIM))
    inp['a_lambda_q1'] = nrm((N_EVEN, HEAD_DIM), 0.1)
    inp['a_lambda_k1'] = nrm((N_EVEN, HEAD_DIM), 0.1)
    inp['a_lambda_q2'] = nrm((N_EVEN, HEAD_DIM), 0.1)
    inp['a_lambda_k2'] = nrm((N_EVEN, HEAD_DIM), 0.1)
    inp['a_sub_norm'] = gain((N_EVEN, 2 * HEAD_DIM))
    inp['ab_w_out'] = nrm((N_EVEN, AB_OUT, d), AB_OUT ** -0.5)
    inp['c_norm'] = gain((N_ODD, d))
    inp['c_w_in'] = nrm((N_ODD, d, C_IN), d ** -0.5)
    inp['c_forget_bias'] = jax.random.uniform(next(ks), (N_ODD, C_HEADS), f32, 1.0, 4.0)
    inp['c_q_norm'] = gain((N_ODD, HEAD_DIM))
    inp['c_k_norm'] = gain((N_ODD, HEAD_DIM))
    inp['c_w_out'] = nrm((N_ODD, C_OUT, d), C_OUT ** -0.5)
    inp['ffn_norm'] = gain((DEPTH, d))
    inp['moe_w_group'] = nrm((DEPTH, d, N_GROUPS), d ** -0.5)
    inp['moe_b_group'] = nrm((DEPTH, N_GROUPS), 0.01)
    inp['moe_w_router'] = nrm((DEPTH, d, N_GROUPS * EXPERTS_PER_GROUP), d ** -0.5)
    inp['moe_b_router'] = nrm((DEPTH, N_GROUPS * EXPERTS_PER_GROUP), 0.01)
    inp['moe_w_gate'] = nrm((DEPTH, N_GROUPS, EXPERTS_PER_GROUP, d, EXPERT_FF), d ** -0.5)
    inp['moe_w_up'] = nrm((DEPTH, N_GROUPS, EXPERTS_PER_GROUP, d, EXPERT_FF), d ** -0.5)
    inp['moe_w_down'] = nrm((DEPTH, N_GROUPS, EXPERTS_PER_GROUP, EXPERT_FF, d), EXPERT_FF ** -0.5)
    return inp


def reference(x_prompt, x_sample, cache_a_k, cache_a_v, cache_b_k, cache_b_v, cache_b_kidx,
              cache_c_k, cache_c_v, cache_c_logf, page_table, rel_bias,
              ab_norm, ab_w_in, a_q_norm, a_k_norm, b_q_norm, b_k_norm,
              a_lambda_q1, a_lambda_k1, a_lambda_q2, a_lambda_k2, a_sub_norm, ab_w_out,
              c_norm, c_w_in, c_forget_bias, c_q_norm, c_k_norm, c_w_out,
              ffn_norm, moe_w_group, moe_b_group, moe_w_router, moe_b_router,
              moe_w_gate, moe_w_up, moe_w_down):
    f32 = jnp.float32
    s_p = x_prompt.shape[1]
    s_s = x_sample.shape[1]
    past_len = page_table.shape[1] * cache_a_k.shape[2]
    pos_p = jnp.arange(s_p, dtype=jnp.int32)
    kpos_s = jnp.arange(past_len + s_s, dtype=jnp.int32)
    qpos_s = past_len + jnp.arange(s_s, dtype=jnp.int32)
    topk_p = min(IDX_TOPK_MAX, s_p // 4)
    topk_s = min(IDX_TOPK_MAX, (past_len + s_s) // 4)

    ak_p, av_p, bk_p, bv_p, bi_p, ck_p, cv_p, cf_p = [], [], [], [], [], [], [], []
    ak_s, av_s, bk_s, bv_s, bi_s, ck_s, cv_s, cf_s = [], [], [], [], [], [], [], []
    xp, xs = x_prompt, x_sample
    for layer in range(DEPTH):
        if layer % 2 == 0:
            e = layer // 2
            lam_init = 0.8 - 0.6 * math.exp(-0.3 * layer)
            lam = (jnp.exp(jnp.sum(a_lambda_q1[e].astype(f32) * a_lambda_k1[e].astype(f32)))
                   - jnp.exp(jnp.sum(a_lambda_q2[e].astype(f32) * a_lambda_k2[e].astype(f32))) + lam_init)
            proj_args = (ab_norm[e], ab_w_in[e], a_q_norm[e], a_k_norm[e], b_q_norm[e], b_k_norm[e])
            qa, ka, va, qb, kb, vb, qi, ki, wi = ab_project(xp, *proj_args)
            oa = sweep_query_blocks(
                lambda qs, qp: diff_attention(qs[0], ka, va, qp, pos_p, rel_bias, lam, lam_init, a_sub_norm[e]),
                (qa,), pos_p)
            ob = sweep_query_blocks(
                lambda qs, qp: dsa_attention(qs[0], kb, vb, qs[1], ki, qs[2], qp, pos_p, rel_bias, topk_p),
                (qb, qi, wi), pos_p)
            xp = xp + jnp.einsum('bte,ed->btd', jnp.concatenate([oa, ob], axis=-1), ab_w_out[e])
            ak_p.append(ka); av_p.append(va); bk_p.append(kb); bv_p.append(vb); bi_p.append(ki)
            qa, ka, va, qb, kb, vb, qi, ki, wi = ab_project(xs, *proj_args)
            ka_all = jnp.concatenate([gather_pages(cache_a_k, e, page_table).astype(ka.dtype), ka], axis=1)
            va_all = jnp.concatenate([gather_pages(cache_a_v, e, page_table).astype(va.dtype), va], axis=1)
            kb_all = jnp.concatenate([gather_pages(cache_b_k, e, page_table).astype(kb.dtype), kb], axis=1)
            vb_all = jnp.concatenate([gather_pages(cache_b_v, e, page_table).astype(vb.dtype), vb], axis=1)
            ki_all = jnp.concatenate([gather_pages(cache_b_kidx, e, page_table).astype(ki.dtype), ki], axis=1)
            oa = diff_attention(qa, ka_all, va_all, qpos_s, kpos_s, rel_bias, lam, lam_init, a_sub_norm[e])
            ob = dsa_attention(qb, kb_all, vb_all, qi, ki_all, wi, qpos_s, kpos_s, rel_bias, topk_s)
            xs = xs + jnp.einsum('bte,ed->btd', jnp.concatenate([oa, ob], axis=-1), ab_w_out[e])
            ak_s.append(ka); av_s.append(va); bk_s.append(kb); bv_s.append(vb); bi_s.append(ki)
        else:
            o = layer // 2
            proj_args = (c_norm[o], c_w_in[o], c_forget_bias[o], c_q_norm[o], c_k_norm[o])
            q, k, v, logf = c_project(xp, *proj_args)
            c_cum = jnp.cumsum(logf, axis=1)
            oc = sweep_query_blocks(
                lambda qs, qp: fox_attention(qs[0], k, v, qs[1], c_cum, qp, pos_p),
                (q, c_cum), pos_p)
            xp = xp + jnp.einsum('bte,ed->btd', oc, c_w_out[o])
            ck_p.append(k); cv_p.append(v); cf_p.append(logf)
            q, k, v, logf = c_project(xs, *proj_args)
            k_all = jnp.concatenate([gather_pages(cache_c_k, o, page_table).astype(k.dtype), k], axis=1)
            v_all = jnp.concatenate([gather_pages(cache_c_v, o, page_table).astype(v.dtype), v], axis=1)
            f_all = jnp.concatenate([gather_pages(cache_c_logf, o, page_table).astype(f32), logf], axis=1)
            c_cum = jnp.cumsum(f_all, axis=1)
            oc = fox_attention(q, k_all, v_all, c_cum[:, past_len:], c_cum, qpos_s, kpos_s)
            xs = xs + jnp.einsum('bte,ed->btd', oc, c_w_out[o])
            ck_s.append(k); cv_s.append(v); cf_s.append(logf)
        moe_args = (ffn_norm[layer], moe_w_group[layer], moe_b_group[layer], moe_w_router[layer],
                    moe_b_router[layer], moe_w_gate[layer], moe_w_up[layer], moe_w_down[layer])
        xp = hier_moe(xp, *moe_args)
        xs = hier_moe(xs, *moe_args)

    a_k_p = jnp.stack(ak_p); a_v_p = jnp.stack(av_p)
    b_k_p = jnp.stack(bk_p); b_v_p = jnp.stack(bv_p); b_kidx_p = jnp.stack(bi_p)
    c_k_p = jnp.stack(ck_p); c_v_p = jnp.stack(cv_p); c_logf_p = jnp.stack(cf_p)
    a_k_s = jnp.stack(ak_s); a_v_s = jnp.stack(av_s)
    b_k_s = jnp.stack(bk_s); b_v_s = jnp.stack(bv_s); b_kidx_s = jnp.stack(bi_s)
    c_k_s = jnp.stack(ck_s); c_v_s = jnp.stack(cv_s); c_logf_s = jnp.stack(cf_s)
    return (xp, xs, a_k_p, a_v_p, b_k_p, b_v_p, b_kidx_p, c_k_p, c_v_p, c_logf_p,
            a_k_s, a_v_s, b_k_s, b_v_s, b_kidx_s, c_k_s, c_v_s, c_logf_s)
```

```python
import functools
import math

import numpy as np
import jax
import jax.numpy as jnp
from jax import lax
from jax.experimental import pallas as pl
from jax.experimental.pallas import tpu as pltpu

F32 = jnp.float32
I32 = jnp.int32
MXU_DT = jnp.bfloat16

D_MODEL = 1024
HEAD_DIM = 64
A_HEADS = 4
B_HEADS = 8
IDX_HEADS = 8
IDX_DIM = 64
IDX_TOPK_MAX = 256
C_HEADS = 16
REL_BUCKETS = 32
REL_MAX_DIST = 128
N_GROUPS = 4
EXPERTS_PER_GROUP = 8
N_EXPERTS = N_GROUPS * EXPERTS_PER_GROUP
EXPERT_FF = 256
EPS = 1e-6
NEG = -1e30
INT_MIN = -2 ** 31
QK_SCALE = HEAD_DIM ** -0.5
IDX_SCALE = IDX_DIM ** -0.5
IDX_HEAD_SCALE = IDX_HEADS ** -0.5
LANES = 128
T8 = 8
VMEM_LIMIT = 56 * 1024 * 1024


def _tile(n, pref):
    best = 16
    for t in range(16, pref + 1, 16):
        if n % t == 0:
            best = t
    assert n % best == 0, (n, pref)
    return best


def _cparams(sem):
    return pltpu.CompilerParams(dimension_semantics=sem, vmem_limit_bytes=VMEM_LIMIT)


def _mx(x):
    return x.astype(MXU_DT)


def _dot(a, b):
    return jnp.dot(a, b, preferred_element_type=F32)


def _dot_t(a, b):
    return lax.dot_general(a, b, (((1,), (1,)), ((), ())), preferred_element_type=F32)


def _split2(x):
    hi = _mx(x)
    lo = _mx(x - hi.astype(F32))
    return hi, lo


def _split3(x):
    hi = _mx(x)
    r = x - hi.astype(F32)
    mid = _mx(r)
    lo = _mx(r - mid.astype(F32))
    return hi, mid, lo


def _lane_iota(shape):
    return lax.broadcasted_iota(I32, shape, len(shape) - 1)


def _row_iota(shape):
    return lax.broadcasted_iota(I32, shape, len(shape) - 2)


def _rms_rows(x, g):
    ms = jnp.mean(x * x, axis=-1, keepdims=True)
    return x * lax.rsqrt(ms + EPS) * g


def _seg_rsqrt(y, bd):
    hi, lo = _split2(y * y)
    ss = _dot(hi, bd) + _dot(lo, bd)
    return lax.rsqrt(ss * (1.0 / HEAD_DIM) + EPS)


def _block_diag_ones(n, seg):
    r = np.arange(n)
    return jnp.asarray((r[:, None] // seg) == (r[None, :] // seg), dtype=MXU_DT)


def _t5_bucket_np(d):
    n = np.maximum(d, 0)
    exact = REL_BUCKETS // 2
    nf = np.maximum(n, 1).astype(np.float64)
    large = exact + (np.log(nf / exact) / math.log(REL_MAX_DIST / exact) * (REL_BUCKETS - exact)).astype(np.int64)
    large = np.minimum(large, REL_BUCKETS - 1)
    return np.where(n < exact, n, large).astype(np.int32)


def _proj0_kernel(x_ref, g_ref, w_ref, gain_ref, bd_ref,
                  qa_ref, ka_ref, kabf_ref, qb_ref, small_ref, kk_ref, vv_ref, kiki_ref,
                  va_ref, vabf_ref, qi_ref):
    h = _mx(_rms_rows(x_ref[...], g_ref[...]))
    bd = bd_ref[...]

    def chunk(c):
        return _dot(h, w_ref[:, c * 256:(c + 1) * 256])

    def normed(c):
        y = chunk(c)
        return y * _seg_rsqrt(y, bd) * gain_ref[:, c * 256:(c + 1) * 256]

    for c in range(2):
        qa_ref[:, c * 256:(c + 1) * 256] = _mx(normed(c) * QK_SCALE)
    for c in range(2):
        y = normed(2 + c)
        ka_ref[:, c * 256:(c + 1) * 256] = y
        kabf_ref[:, c * 256:(c + 1) * 256] = _mx(y)
    for c in range(2):
        qb_ref[:, c * 256:(c + 1) * 256] = _mx(normed(4 + c) * QK_SCALE)
    y = chunk(6)
    yn = y * _seg_rsqrt(y, bd) * gain_ref[:, 6 * 256:7 * 256]
    lane = _lane_iota(y.shape)
    y = jnp.where(lane < HEAD_DIM, yn, y)
    small_ref[...] = y
    t0 = y[:, :LANES]
    t1 = y[:, LANES:]
    lo = _lane_iota(t0.shape) < HEAD_DIM
    r0 = pltpu.roll(t0, HEAD_DIM, 1)
    kk_ref[...] = _mx(jnp.where(lo, t0, r0))
    vv_ref[...] = _mx(jnp.where(lo, r0, t0))
    r1 = pltpu.roll(t1, HEAD_DIM, 1)
    kiki_ref[...] = _mx(jnp.where(lo, t1, r1))
    for c in range(2):
        y = chunk(7 + c)
        va_ref[:, c * 256:(c + 1) * 256] = y
        vabf_ref[:, c * 256:(c + 1) * 256] = _mx(y)
    for c in range(2):
        qi_ref[:, c * 256:(c + 1) * 256] = _mx(chunk(9 + c) * IDX_SCALE)


def _proj0(x, g, w_in, a_qn, a_kn, b_qn, b_kn, tm=256):
    n = x.shape[0]
    tm = _tile(n, tm)
    sp = np.cumsum([512, 512, 512, 512, 64, 64, 512, 64, 8])[:-1]
    wqa, wka, wva, wqb, wkb, wvb, wqi, wki, wwi = jnp.split(w_in, sp, axis=1)
    w = jnp.concatenate([wqa, wka, wqb, wkb, wvb, wki, wwi, jnp.zeros((D_MODEL, 56), F32), wva, wqi], axis=1)
    w = _mx(w)
    ncol = w.shape[1]
    gain = jnp.concatenate([jnp.tile(a_qn, 8), jnp.tile(a_kn, 8), jnp.tile(b_qn, 8), b_kn,
                            jnp.ones((192,), F32)])[None, :]
    bd = _block_diag_ones(256, HEAD_DIM)
    row = lambda i: (i, 0)
    fixed = lambda i: (0, 0)
    widths = [(512, MXU_DT), (512, F32), (512, MXU_DT), (512, MXU_DT), (256, F32), (128, MXU_DT),
              (128, MXU_DT), (128, MXU_DT), (512, F32), (512, MXU_DT), (512, MXU_DT)]
    return pl.pallas_call(
        _proj0_kernel,
        grid=(n // tm,),
        in_specs=[pl.BlockSpec((tm, D_MODEL), row), pl.BlockSpec((1, D_MODEL), fixed),
                  pl.BlockSpec((D_MODEL, ncol), fixed), pl.BlockSpec((1, gain.shape[1]), fixed),
                  pl.BlockSpec((256, 256), fixed)],
        out_specs=[pl.BlockSpec((tm, wd), row) for wd, _ in widths],
        out_shape=[jax.ShapeDtypeStruct((n, wd), dt) for wd, dt in widths],
        compiler_params=_cparams(("parallel",)),
        name="proj0",
    )(x, g[None, :], w, gain, bd)


def _proj1_kernel(x_ref, g_ref, w_ref, gain_ref, bf_ref, bd_ref,
                  q_ref, k_ref, kbf_ref, v_ref, vbf_ref, logf_ref):
    h = _mx(_rms_rows(x_ref[...], g_ref[...]))
    bd = bd_ref[...]

    def chunk(c):
        return _dot(h, w_ref[:, c * 256:(c + 1) * 256])

    def normed(c):
        y = chunk(c)
        return y * _seg_rsqrt(y, bd) * gain_ref[:, c * 256:(c + 1) * 256]

    for c in range(4):
        q_ref[:, c * 256:(c + 1) * 256] = _mx(normed(c) * QK_SCALE)
    for c in range(4):
        y = normed(4 + c)
        k_ref[:, c * 256:(c + 1) * 256] = y
        kbf_ref[:, c * 256:(c + 1) * 256] = _mx(y)
    for c in range(4):
        y = chunk(8 + c)
        v_ref[:, c * 256:(c + 1) * 256] = y
        vbf_ref[:, c * 256:(c + 1) * 256] = _mx(y)
    f = _dot(h, w_ref[:, 12 * 256:12 * 256 + LANES]) + bf_ref[...]
    logf_ref[...] = jnp.minimum(f, 0.0) - jnp.log(1.0 + jnp.exp(-jnp.abs(f)))


def _proj1(x, g, w_in, b_f, qn, kn, tm=256):
    n = x.shape[0]
    tm = _tile(n, tm)
    w = _mx(jnp.concatenate([w_in, jnp.zeros((D_MODEL, LANES - C_HEADS), F32)], axis=1))
    ncol = w.shape[1]
    gain = jnp.concatenate([jnp.tile(qn, C_HEADS), jnp.tile(kn, C_HEADS)])[None, :]
    bf = jnp.concatenate([b_f, jnp.zeros((LANES - C_HEADS,), F32)])[None, :]
    bd = _block_diag_ones(256, HEAD_DIM)
    row = lambda i: (i, 0)
    fixed = lambda i: (0, 0)
    widths = [(1024, MXU_DT), (1024, F32), (1024, MXU_DT), (1024, F32), (1024, MXU_DT), (LANES, F32)]
    return pl.pallas_call(
        _proj1_kernel,
        grid=(n // tm,),
        in_specs=[pl.BlockSpec((tm, D_MODEL), row), pl.BlockSpec((1, D_MODEL), fixed),
                  pl.BlockSpec((D_MODEL, ncol), fixed), pl.BlockSpec((1, gain.shape[1]), fixed),
                  pl.BlockSpec((1, LANES), fixed), pl.BlockSpec((256, 256), fixed)],
        out_specs=[pl.BlockSpec((tm, wd), row) for wd, _ in widths],
        out_shape=[jax.ShapeDtypeStruct((n, wd), dt) for wd, dt in widths],
        compiler_params=_cparams(("parallel",)),
        name="proj1",
    )(x, g[None, :], w, gain, bf, bd)


def _outproj_kernel(*refs, n_in):
    res_ref = refs[0]
    out_ref = refs[-1]
    acc = res_ref[...]
    for i in range(n_in):
        acc = acc + _dot(refs[1 + 2 * i][...], refs[2 + 2 * i][...])
    out_ref[...] = acc


def _outproj(res, pairs, tm=512):
    n = res.shape[0]
    tm = _tile(n, tm)
    row = lambda i: (i, 0)
    fixed = lambda i: (0, 0)
    in_specs = [pl.BlockSpec((tm, D_MODEL), row)]
    args = [res]
    for a, w in pairs:
        in_specs += [pl.BlockSpec((tm, a.shape[1]), row), pl.BlockSpec(w.shape, fixed)]
        args += [a, _mx(w)]
    return pl.pallas_call(
        functools.partial(_outproj_kernel, n_in=len(pairs)),
        grid=(n // tm,),
        in_specs=in_specs,
        out_specs=pl.BlockSpec((tm, D_MODEL), row),
        out_shape=jax.ShapeDtypeStruct((n, D_MODEL), F32),
        compiler_params=_cparams(("parallel",)),
        name="outproj",
    )(*args)


def _two_stream_flash(q, k_ref, v_ref, i, tq, tk, bias_fn, m_ref, l_ref, acc_ref):
    lane = _lane_iota(q.shape)
    zero = jnp.zeros_like(q)
    qc = (jnp.where(lane < HEAD_DIM, q, zero), jnp.where(lane >= HEAD_DIM, q, zero))
    m_ref[...] = jnp.full(m_ref.shape, NEG, F32)
    l_ref[...] = jnp.zeros(l_ref.shape, F32)
    acc_ref[...] = jnp.zeros(acc_ref.shape, F32)
    row = _row_iota((tq, tk)) + i * tq
    col = _lane_iota((tq, tk))

    def body(j, carry):
        off = pl.multiple_of(j * tk, tk)
        k = k_ref[pl.ds(off, tk), :]
        v = v_ref[pl.ds(off, tk), :]
        keep = (col + j * tk) <= row
        for c in range(2):
            s = _dot_t(qc[c], k) + bias_fn(c, j, off)
            s = jnp.where(keep, s, NEG)
            m_old = m_ref[c]
            m_new = jnp.maximum(m_old, jnp.max(s, axis=-1, keepdims=True))
            alpha = jnp.exp(m_old - m_new)
            p = jnp.exp(s - m_new)
            l_ref[c] = alpha * l_ref[c] + jnp.sum(p, axis=-1, keepdims=True)
            acc_ref[c] = alpha * acc_ref[c] + _dot(_mx(p), v)
            m_ref[c] = m_new
        return carry

    lax.fori_loop(0, i + 1, body, 0)


def _attn_a_kernel(lamv_ref, gsub_ref, q_ref, k_ref, v_ref, bias_ref, o_ref,
                   m_ref, l_ref, acc_ref, *, tq, tk, lam_init):
    i = pl.program_id(2)

    def bias_fn(c, j, off):
        return bias_ref[jnp.minimum(i - j, 2)]

    _two_stream_flash(q_ref[...], k_ref, v_ref, i, tq, tk, bias_fn, m_ref, l_ref, acc_ref)
    lv = lamv_ref[...]
    lam = (jnp.exp(jnp.sum(lv[0:1] * lv[1:2], axis=-1, keepdims=True))
           - jnp.exp(jnp.sum(lv[2:3] * lv[3:4], axis=-1, keepdims=True)) + lam_init)
    o = acc_ref[0] / l_ref[0] - lam * (acc_ref[1] / l_ref[1])
    o = _rms_rows(o, gsub_ref[...]) * (1.0 - lam_init)
    o_ref[...] = _mx(o)


def _attn_a_prompt(qa, ka, va, lamv, gsub, bias_tiles, nb, s, lam_init, tq=256):
    tk = tq
    nq = s // tq
    kern = functools.partial(_attn_a_kernel, tq=tq, tk=tk, lam_init=lam_init)
    return pl.pallas_call(
        kern,
        grid=(nb, A_HEADS, nq),
        in_specs=[pl.BlockSpec((4, HEAD_DIM), lambda b, h, i: (0, 0)),
                  pl.BlockSpec((1, LANES), lambda b, h, i: (0, 0)),
                  pl.BlockSpec((tq, LANES), lambda b, h, i: (b * nq + i, h)),
                  pl.BlockSpec((s, LANES), lambda b, h, i: (b, h)),
                  pl.BlockSpec((s, LANES), lambda b, h, i: (b, h)),
                  pl.BlockSpec((None, 3, tq, tk), lambda b, h, i: (h, 0, 0, 0))],
        out_specs=pl.BlockSpec((tq, LANES), lambda b, h, i: (b * nq + i, h)),
        out_shape=jax.ShapeDtypeStruct((nb * s, A_HEADS * LANES), MXU_DT),
        scratch_shapes=[pltpu.VMEM((2, tq, 1), F32), pltpu.VMEM((2, tq, 1), F32),
                        pltpu.VMEM((2, tq, LANES), F32)],
        compiler_params=_cparams(("parallel", "parallel", "arbitrary")),
        name="attn_a_prompt",
    )(lamv, gsub, qa, ka, va, bias_tiles)


def _attn_c_kernel(q_ref, k_ref, v_ref, cq_ref, ckt_ref, o_ref, m_ref, l_ref, acc_ref, *, tq, tk):
    hp = pl.program_id(1)
    i = pl.program_id(2)
    cq_tile = cq_ref[...]
    lane16 = _lane_iota(cq_tile.shape)
    cq = [jnp.sum(jnp.where(lane16 == 2 * hp + c, cq_tile, 0.0), axis=-1, keepdims=True) for c in range(2)]

    def bias_fn(c, j, off):
        return cq[c] - ckt_ref[pl.ds(c, 1), pl.ds(off, tk)]

    _two_stream_flash(q_ref[...], k_ref, v_ref, i, tq, tk, bias_fn, m_ref, l_ref, acc_ref)
    lane = _lane_iota((tq, LANES))
    o = jnp.where(lane < HEAD_DIM, acc_ref[0] / l_ref[0], acc_ref[1] / l_ref[1])
    o_ref[...] = _mx(o)


def _attn_c_prompt(q, k, v, cq, ckt, nb, s, tq=256):
    tk = tq
    nq = s // tq
    nhp = C_HEADS // 2
    kern = functools.partial(_attn_c_kernel, tq=tq, tk=tk)
    return pl.pallas_call(
        kern,
        grid=(nb, nhp, nq),
        in_specs=[pl.BlockSpec((tq, LANES), lambda b, h, i: (b * nq + i, h)),
                  pl.BlockSpec((s, LANES), lambda b, h, i: (b, h)),
                  pl.BlockSpec((s, LANES), lambda b, h, i: (b, h)),
                  pl.BlockSpec((tq, C_HEADS), lambda b, h, i: (b * nq + i, 0)),
                  pl.BlockSpec((None, None, 2, s), lambda b, h, i: (b, h, 0, 0))],
        out_specs=pl.BlockSpec((tq, LANES), lambda b, h, i: (b * nq + i, h)),
        out_shape=jax.ShapeDtypeStruct((nb * s, C_HEADS * HEAD_DIM), MXU_DT),
        scratch_shapes=[pltpu.VMEM((2, tq, 1), F32), pltpu.VMEM((2, tq, 1), F32),
                        pltpu.VMEM((2, tq, LANES), F32)],
        compiler_params=_cparams(("parallel", "parallel", "arbitrary")),
        name="attn_c_prompt",
    )(q, k, v, cq, ckt)


def _score_keys(score):
    score = jnp.where(score == 0.0, 0.0, score)
    bits = pltpu.bitcast(score, I32)
    return bits ^ (jnp.right_shift(bits, 31) & 0x7FFFFFFF)


def _topk_select(keys_ref, kcount, active, col):
    kf = float(kcount)
    nbits_col = int(keys_ref.shape[1] - 1).bit_length()

    def count(pred):
        return jnp.sum(jnp.where(pred, 1.0, 0.0), axis=1, keepdims=True)

    t0 = jnp.where(count(keys_ref[...] >= 0) >= kf, 0, INT_MIN).astype(I32)

    def body(it, t):
        cand = t + jnp.left_shift(jnp.int32(1), 30 - it)
        return jnp.where(count(keys_ref[...] >= cand) >= kf, cand, t)

    t = lax.fori_loop(0, 31, body, t0)
    t = jnp.where(active, t, INT_MIN)
    keys = keys_ref[...]
    gt = keys > t
    eq = keys == t
    need = kf - count(gt)
    excess = jnp.where(active, count(eq) - need, 0.0)

    def tie_break():
        def tb(it, jj):
            cand = jj + jnp.left_shift(jnp.int32(1), nbits_col - 1 - it)
            c = count((keys_ref[...] == t) & (col < cand))
            return jnp.where(c < need, cand, jj)
        return lax.fori_loop(0, nbits_col, tb, jnp.zeros(t.shape, I32))

    jmax = lax.cond(jnp.max(excess) > 0.0, tie_break,
                    lambda: jnp.full(t.shape, keys_ref.shape[1], I32))
    return gt | (eq & (col <= jmax))


def _attn_b_kernel(qb_ref, qi_ref, w_ref, kk_ref, vv_ref, kiki_ref, bias_ref, c31_ref, o_ref,
                   keys_ref, selm_ref, s_ref, *, tq, s_len, topk):
    i = pl.program_id(1)
    shape = (tq, s_len)
    row = _row_iota(shape) + i * tq
    col = _lane_iota(shape)
    causal = col <= row
    lane = _lane_iota((tq, LANES))
    halves = (lane < HEAD_DIM, lane >= HEAD_DIM)
    kiki = kiki_ref[...]
    wt = w_ref[...] * IDX_HEAD_SCALE

    score = jnp.zeros(shape, F32)
    for h in range(IDX_HEADS):
        qt = qi_ref[:, (h // 2) * LANES:(h // 2 + 1) * LANES]
        qh = jnp.where(halves[h % 2], qt, jnp.zeros_like(qt))
        dots = _dot_t(qh, kiki)
        score = score + wt[:, HEAD_DIM + h:HEAD_DIM + h + 1] * jnp.maximum(dots, 0.0)
    keys_ref[...] = _score_keys(jnp.where(causal, score, -jnp.inf))

    qpos = _row_iota((tq, 1)) + i * tq
    sel = _topk_select(keys_ref, topk, qpos >= topk, col)
    selm_ref[...] = jnp.where(sel & causal, 0.0, NEG)

    kk = kk_ref[...]
    vv = vv_ref[...]
    for hp in range(B_HEADS // 2):
        qt = qb_ref[:, hp * LANES:(hp + 1) * LANES]
        outs = []
        for c in range(2):
            h = 2 * hp + c
            qh = jnp.where(halves[c], qt, jnp.zeros_like(qt))
            s_ref[...] = _dot_t(qh, kk) + c31_ref[h]
            off = pl.multiple_of(i * tq, tq)
            s_ref[:, pl.ds(off, tq)] += bias_ref[h, 0]

            @pl.when(i > 0)
            def _():
                off1 = pl.multiple_of((i - 1) * tq, tq)
                s_ref[:, pl.ds(off1, tq)] += bias_ref[h, 1]

            s = s_ref[...] + selm_ref[...]
            m = jnp.max(s, axis=-1, keepdims=True)
            p = jnp.exp(s - m)
            l = jnp.sum(p, axis=-1, keepdims=True)
            outs.append(_dot(_mx(p), vv) / l)
        o_ref[:, hp * LANES:(hp + 1) * LANES] = _mx(jnp.where(halves[0], outs[0], outs[1]))


def _attn_b_prompt(qb, qi, small, kk, vv, kiki, bias_d, c31, nb, s, topk, tq=128):
    nq = s // tq
    kern = functools.partial(_attn_b_kernel, tq=tq, s_len=s, topk=topk)
    qrow = lambda b, i: (b * nq + i, 0)
    kv = lambda b, i: (b, 0)
    return pl.pallas_call(
        kern,
        grid=(nb, nq),
        in_specs=[pl.BlockSpec((tq, B_HEADS * HEAD_DIM), qrow),
                  pl.BlockSpec((tq, IDX_HEADS * IDX_DIM), qrow),
                  pl.BlockSpec((tq, LANES), lambda b, i: (b * nq + i, 1)),
                  pl.BlockSpec((s, LANES), kv), pl.BlockSpec((s, LANES), kv), pl.BlockSpec((s, LANES), kv),
                  pl.BlockSpec(bias_d.shape, lambda b, i: (0, 0, 0, 0)),
                  pl.BlockSpec(c31.shape, lambda b, i: (0, 0, 0))],
        out_specs=pl.BlockSpec((tq, B_HEADS * HEAD_DIM), qrow),
        out_shape=jax.ShapeDtypeStruct((nb * s, B_HEADS * HEAD_DIM), MXU_DT),
        scratch_shapes=[pltpu.VMEM((tq, s), I32), pltpu.VMEM((tq, s), F32), pltpu.VMEM((tq, s), F32)],
        compiler_params=_cparams(("parallel", "arbitrary")),
        name="attn_b_prompt",
    )(qb, qi, small, kk, vv, kiki, bias_d, c31)


def _cumsum_kernel(*refs, paged):
    x_ref, tri_ref, o_ref, carry_ref = refs[-4:]
    j = pl.program_id(1)

    @pl.when(j == 0)
    def _():
        carry_ref[...] = jnp.zeros(carry_ref.shape, F32)

    x = x_ref[...]
    tri = tri_ref[...]
    hi, mid, lo = _split3(x)
    cum = _dot(tri, hi) + _dot(tri, mid) + _dot(tri, lo) + carry_ref[...]
    o_ref[...] = cum
    carry_ref[...] = cum[-1:, :]


def _tri_incl(n):
    r = np.arange(n)
    return jnp.asarray(r[:, None] >= r[None, :], dtype=MXU_DT)


def _cumsum_prompt(x, nb, s, blk=128):
    nj = s // blk
    w = x.shape[1]
    return pl.pallas_call(
        functools.partial(_cumsum_kernel, paged=False),
        grid=(nb, nj),
        in_specs=[pl.BlockSpec((blk, w), lambda b, j: (b * nj + j, 0)),
                  pl.BlockSpec((blk, blk), lambda b, j: (0, 0))],
        out_specs=pl.BlockSpec((blk, w), lambda b, j: (b * nj + j, 0)),
        out_shape=jax.ShapeDtypeStruct((nb * s, w), F32),
        scratch_shapes=[pltpu.VMEM((1, w), F32)],
        compiler_params=_cparams(("parallel", "arbitrary")),
        name="cumsum_prompt",
    )(x, _tri_incl(blk))


def _cumsum_pages(cache, layer, page_table):
    db, npg = page_table.shape
    ps, w = cache.shape[2], cache.shape[3]
    grid_spec = pltpu.PrefetchScalarGridSpec(
        num_scalar_prefetch=1,
        grid=(db, npg),
        in_specs=[pl.BlockSpec((None, None, ps, w), lambda b, j, pt: (layer, pt[b * npg + j], 0, 0)),
                  pl.BlockSpec((ps, ps), lambda b, j, pt: (0, 0))],
        out_specs=pl.BlockSpec((None, ps, w), lambda b, j, pt: (b, j, 0)),
        scratch_shapes=[pltpu.VMEM((1, w), F32)],
    )
    return pl.pallas_call(
        functools.partial(_cumsum_kernel, paged=True),
        grid_spec=grid_spec,
        out_shape=jax.ShapeDtypeStruct((db, npg * ps, w), F32),
        compiler_params=_cparams(("parallel", "arbitrary")),
        name="cumsum_pages",
    )(page_table.reshape(-1), cache, _tri_incl(ps))


def _decode0_kernel(pt_ref, lamv_ref, gsub_ref, qa_ref, qi_ref, qb_ref, w8_ref, kna_ref, vna_ref, bnew_ref,
                    cak_ref, cav_ref, cbk_ref, cbv_ref, cbi_ref, taba_ref, tabb_ref, validb_ref,
                    oa_ref, ob_ref,
                    m_ref, l_ref, acc_ref, kb_ref, vb_ref, ki_ref, keys_ref,
                    *, n_pages, ps, topk, lam_init):
    p = pl.program_id(1)
    past = n_pages * ps
    lk = kb_ref.shape[0]

    @pl.when(p == 0)
    def _():
        m_ref[...] = jnp.full(m_ref.shape, NEG, F32)
        l_ref[...] = jnp.zeros(l_ref.shape, F32)
        acc_ref[...] = jnp.zeros(acc_ref.shape, F32)

    def online(s, v):
        m_old = m_ref[...]
        m_new = jnp.maximum(m_old, jnp.max(s, axis=-1, keepdims=True))
        alpha = jnp.exp(m_old - m_new)
        pr = jnp.exp(s - m_new)
        l_ref[...] = alpha * l_ref[...] + jnp.sum(pr, axis=-1, keepdims=True)
        acc_ref[...] = alpha * acc_ref[...] + _dot(_mx(pr), v)
        m_ref[...] = m_new

    qa = qa_ref[...]
    bias = taba_ref[jnp.where(p == n_pages - 1, 1, 0)]
    online(_dot_t(qa, _mx(cak_ref[...])) + bias, _mx(cav_ref[...]))

    off = pl.multiple_of(p * ps, ps)
    kb_ref[pl.ds(off, ps), :] = _mx(cbk_ref[...])
    vb_ref[pl.ds(off, ps), :] = _mx(cbv_ref[...])
    ki_ref[pl.ds(off, ps), :] = _mx(cbi_ref[...])

    @pl.when(p == n_pages - 1)
    def _():
        s_new = _dot_t(qa, _mx(kna_ref[...]))
        online(s_new + taba_ref[2][:, :s_new.shape[1]], _mx(vna_ref[...]))
        lv = lamv_ref[...]
        lam = (jnp.exp(jnp.sum(lv[0:1] * lv[1:2], axis=-1, keepdims=True))
               - jnp.exp(jnp.sum(lv[2:3] * lv[3:4], axis=-1, keepdims=True)) + lam_init)
        o_full = acc_ref[...] / l_ref[...]
        for h in range(A_HEADS):
            o1 = o_full[h * 16:h * 16 + T8, h * LANES:(h + 1) * LANES]
            o2 = o_full[h * 16 + T8:(h + 1) * 16, h * LANES:(h + 1) * LANES]
            o = o1 - lam * o2
            oa_ref[h] = _rms_rows(o, gsub_ref[...]) * (1.0 - lam_init)

        bn = bnew_ref[...]
        kb_ref[pl.ds(past, 16), :] = _mx(bn[:, 0:64])
        vb_ref[pl.ds(past, 16), :] = _mx(bn[:, 64:128])
        ki_ref[pl.ds(past, 16), :] = _mx(bn[:, 128:192])
        ztail = jnp.zeros((lk - past - 16, HEAD_DIM), MXU_DT)
        kb_ref[pl.ds(past + 16, lk - past - 16), :] = ztail
        vb_ref[pl.ds(past + 16, lk - past - 16), :] = ztail
        ki_ref[pl.ds(past + 16, lk - past - 16), :] = ztail

        dots = jnp.maximum(_dot_t(qi_ref[...], ki_ref[...]), 0.0)
        w8 = w8_ref[...] * IDX_HEAD_SCALE
        score = jnp.zeros((T8, lk), F32)
        for h in range(IDX_HEADS):
            score = score + w8[:, h:h + 1] * dots[h * T8:(h + 1) * T8]
        valid = validb_ref[...] == 0.0
        keys_ref[...] = _score_keys(jnp.where(valid, score, -jnp.inf))
        col = _lane_iota((T8, lk))
        sel = _topk_select(keys_ref, topk, jnp.full((T8, 1), True), col)
        selm = jnp.where(sel & valid, 0.0, NEG)
        s = _dot_t(qb_ref[...], kb_ref[...]) + tabb_ref[...] + jnp.concatenate([selm] * B_HEADS, axis=0)
        m = jnp.max(s, axis=-1, keepdims=True)
        pr = jnp.exp(s - m)
        l = jnp.sum(pr, axis=-1, keepdims=True)
        o = _dot(_mx(pr), vb_ref[...]) / l
        for h in range(B_HEADS):
            ob_ref[h] = o[h * T8:(h + 1) * T8]


def _decode0(page_table, lamv, gsub, qa_bd, qi64, qb64, w8, kna, vna, bnew,
             cache_a_k, cache_a_v, cache_b_k, cache_b_v, cache_b_kidx, taba, tabb, validb,
             layer, topk, lam_init):
    db, npg = page_table.shape
    ps = cache_a_k.shape[2]
    lk = tabb.shape[1]
    cak = cache_a_k.reshape(cache_a_k.shape[0], cache_a_k.shape[1], ps, -1)
    cav = cache_a_v.reshape(cache_a_v.shape[0], cache_a_v.shape[1], ps, -1)
    per_b = lambda b, p, pt: (b, 0, 0)
    fixed2 = lambda b, p, pt: (0, 0)
    fixed3 = lambda b, p, pt: (0, 0, 0)
    page = lambda b, p, pt: (layer, pt[b * npg + p], 0, 0)
    kern = functools.partial(_decode0_kernel, n_pages=npg, ps=ps, topk=topk, lam_init=lam_init)
    grid_spec = pltpu.PrefetchScalarGridSpec(
        num_scalar_prefetch=1,
        grid=(db, npg),
        in_specs=[pl.BlockSpec((4, HEAD_DIM), fixed2), pl.BlockSpec((1, LANES), fixed2),
                  pl.BlockSpec((None,) + qa_bd.shape[1:], per_b),
                  pl.BlockSpec((None,) + qi64.shape[1:], per_b),
                  pl.BlockSpec((None,) + qb64.shape[1:], per_b),
                  pl.BlockSpec((None,) + w8.shape[1:], per_b),
                  pl.BlockSpec((None,) + kna.shape[1:], per_b),
                  pl.BlockSpec((None,) + vna.shape[1:], per_b),
                  pl.BlockSpec((None,) + bnew.shape[1:], per_b),
                  pl.BlockSpec((None, None, ps, cak.shape[3]), page),
                  pl.BlockSpec((None, None, ps, cav.shape[3]), page),
                  pl.BlockSpec((None, None, ps, HEAD_DIM), page),
                  pl.BlockSpec((None, None, ps, HEAD_DIM), page),
                  pl.BlockSpec((None, None, ps, IDX_DIM), page),
                  pl.BlockSpec(taba.shape, fixed3), pl.BlockSpec(tabb.shape, fixed2),
                  pl.BlockSpec(validb.shape, fixed2)],
        out_specs=[pl.BlockSpec((None, A_HEADS, T8, LANES), lambda b, p, pt: (b, 0, 0, 0)),
                   pl.BlockSpec((None, B_HEADS, T8, HEAD_DIM), lambda b, p, pt: (b, 0, 0, 0))],
        scratch_shapes=[pltpu.VMEM((64, 1), F32), pltpu.VMEM((64, 1), F32),
                        pltpu.VMEM((64, A_HEADS * LANES), F32),
                        pltpu.VMEM((lk, HEAD_DIM), MXU_DT), pltpu.VMEM((lk, HEAD_DIM), MXU_DT),
                        pltpu.VMEM((lk, IDX_DIM), MXU_DT), pltpu.VMEM((T8, lk), I32)],
    )
    return pl.pallas_call(
        kern,
        grid_spec=grid_spec,
        out_shape=[jax.ShapeDtypeStruct((db, A_HEADS, T8, LANES), F32),
                   jax.ShapeDtypeStruct((db, B_HEADS, T8, HEAD_DIM), F32)],
        compiler_params=_cparams(("parallel", "arbitrary")),
        name="decode0",
    )(page_table.reshape(-1), lamv, gsub, qa_bd, qi64, qb64, w8, kna, vna, bnew,
      cak, cav, cache_b_k, cache_b_v, cache_b_kidx, taba, tabb, validb)


def _expand_rows(x):
    hh, ww = x.shape
    return jnp.broadcast_to(x[:, None, :], (hh, T8, ww)).reshape(hh * T8, ww)


def _decode1_kernel(pt_ref, q_ref, cq_ref, kn_ref, vn_ref, cnew_ref, ck_ref, cv_ref, ckt_ref, maskn_ref,
                    o_ref, m_ref, l_ref, acc_ref, *, n_pages):
    p = pl.program_id(1)

    @pl.when(p == 0)
    def _():
        m_ref[...] = jnp.full(m_ref.shape, NEG, F32)
        l_ref[...] = jnp.zeros(l_ref.shape, F32)
        acc_ref[...] = jnp.zeros(acc_ref.shape, F32)

    def online(s, v):
        m_old = m_ref[...]
        m_new = jnp.maximum(m_old, jnp.max(s, axis=-1, keepdims=True))
        alpha = jnp.exp(m_old - m_new)
        pr = jnp.exp(s - m_new)
        l_ref[...] = alpha * l_ref[...] + jnp.sum(pr, axis=-1, keepdims=True)
        acc_ref[...] = alpha * acc_ref[...] + _dot(_mx(pr), v)
        m_ref[...] = m_new

    q = q_ref[...]
    cq = cq_ref[...]
    online(_dot_t(q, _mx(ck_ref[...])) + cq - _expand_rows(ckt_ref[...]), _mx(cv_ref[...]))

    @pl.when(p == n_pages - 1)
    def _():
        s_new = _dot_t(q, _mx(kn_ref[...])) + cq - _expand_rows(cnew_ref[...]) + maskn_ref[...]
        online(s_new, _mx(vn_ref[...]))
        o_full = acc_ref[...] / l_ref[...]
        for h in range(C_HEADS):
            o_ref[h] = o_full[h * T8:(h + 1) * T8, h * HEAD_DIM:(h + 1) * HEAD_DIM]


def _decode1(page_table, q_bd, cq, kn, vn, cnew_t, cache_c_k, cache_c_v, ckt, maskn, layer):
    db, npg = page_table.shape
    ps = cache_c_k.shape[2]
    ck = cache_c_k.reshape(cache_c_k.shape[0], cache_c_k.shape[1], ps, -1)
    cv = cache_c_v.reshape(cache_c_v.shape[0], cache_c_v.shape[1], ps, -1)
    per_b = lambda b, p, pt: (b, 0, 0)
    page = lambda b, p, pt: (layer, pt[b * npg + p], 0, 0)
    kern = functools.partial(_decode1_kernel, n_pages=npg)
    grid_spec = pltpu.PrefetchScalarGridSpec(
        num_scalar_prefetch=1,
        grid=(db, npg),
        in_specs=[pl.BlockSpec((None,) + q_bd.shape[1:], per_b),
                  pl.BlockSpec((None,) + cq.shape[1:], per_b),
                  pl.BlockSpec((None,) + kn.shape[1:], per_b),
                  pl.BlockSpec((None,) + vn.shape[1:], per_b),
                  pl.BlockSpec((None,) + cnew_t.shape[1:], per_b),
                  pl.BlockSpec((None, None, ps, ck.shape[3]), page),
                  pl.BlockSpec((None, None, ps, cv.shape[3]), page),
                  pl.BlockSpec((None, C_HEADS, ps), lambda b, p, pt: (b, 0, p)),
                  pl.BlockSpec(maskn.shape, lambda b, p, pt: (0, 0))],
        out_specs=pl.BlockSpec((None, C_HEADS, T8, HEAD_DIM), lambda b, p, pt: (b, 0, 0, 0)),
        scratch_shapes=[pltpu.VMEM((C_HEADS * T8, 1), F32), pltpu.VMEM((C_HEADS * T8, 1), F32),
                        pltpu.VMEM((C_HEADS * T8, C_HEADS * HEAD_DIM), F32)],
    )
    return pl.pallas_call(
        kern,
        grid_spec=grid_spec,
        out_shape=jax.ShapeDtypeStruct((db, C_HEADS, T8, HEAD_DIM), F32),
        compiler_params=_cparams(("parallel", "arbitrary")),
        name="decode1",
    )(page_table.reshape(-1), q_bd, cq, kn, vn, cnew_t, ck, cv, ckt, maskn)


def _router_kernel(x_ref, g_ref, whi_ref, wlo_ref, b_ref, t_ref, route_ref):
    t = _rms_rows(x_ref[...], g_ref[...])
    thi, tlo = _split2(t)
    t_ref[...] = thi
    whi = whi_ref[...]
    logits = _dot(thi, whi) + _dot(tlo, whi) + _dot(thi, wlo_ref[...]) + b_ref[...]
    lane = _lane_iota(logits.shape)
    big = jnp.int32(1 << 20)

    def first_max(v):
        mx = jnp.max(v, axis=-1, keepdims=True)
        idx = jnp.min(jnp.where(v == mx, lane, big), axis=-1, keepdims=True)
        return mx, idx

    glog = jnp.where(lane < N_GROUPS, logits, -jnp.inf)
    gmax, gidx = first_max(glog)
    grp_w = 1.0 / jnp.sum(jnp.exp(glog - gmax), axis=-1, keepdims=True)
    el = lane - N_GROUPS
    in_grp = (el >= 0) & (el < N_EXPERTS) & (jnp.right_shift(el, 3) == gidx)
    v1 = jnp.where(in_grp, logits, -jnp.inf)
    top1, i1 = first_max(v1)
    v2 = jnp.where(lane == i1, -jnp.inf, v1)
    top2, i2 = first_max(v2)
    e2 = jnp.exp(top2 - top1)
    w1 = grp_w / (1.0 + e2)
    w2 = grp_w * e2 / (1.0 + e2)
    route = jnp.where(lane == 0, (i1 - N_GROUPS).astype(F32),
                      jnp.where(lane == 1, (i2 - N_GROUPS).astype(F32),
                                jnp.where(lane == 2, w1, jnp.where(lane == 3, w2, 0.0))))
    route_ref[...] = route


def _router(x, g, w_group, b_group, w_router, b_router, tm=512):
    n = x.shape[0]
    tm = _tile(n, tm)
    pad = LANES - N_GROUPS - N_EXPERTS
    w = jnp.concatenate([w_group, w_router, jnp.zeros((D_MODEL, pad), F32)], axis=1)
    whi = _mx(w)
    wlo = _mx(w - whi.astype(F32))
    b = jnp.concatenate([b_group, b_router, jnp.zeros((pad,), F32)])[None, :]
    row = lambda i: (i, 0)
    fixed = lambda i: (0, 0)
    return pl.pallas_call(
        _router_kernel,
        grid=(n // tm,),
        in_specs=[pl.BlockSpec((tm, D_MODEL), row), pl.BlockSpec((1, D_MODEL), fixed),
                  pl.BlockSpec((D_MODEL, LANES), fixed), pl.BlockSpec((D_MODEL, LANES), fixed),
                  pl.BlockSpec((1, LANES), fixed)],
        out_specs=[pl.BlockSpec((tm, D_MODEL), row), pl.BlockSpec((tm, LANES), row)],
        out_shape=[jax.ShapeDtypeStruct((n, D_MODEL), MXU_DT), jax.ShapeDtypeStruct((n, LANES), F32)],
        compiler_params=_cparams(("parallel",)),
        name="moe_router",
    )(x, g[None, :], whi, wlo, b)


def _moe_dense_kernel(x_ref, t_ref, route_ref, wg_ref, wu_ref, wd_ref, o_ref):
    e = pl.program_id(1)

    @pl.when(e == 0)
    def _():
        o_ref[...] = x_ref[...]

    t = t_ref[...]
    a = _dot(t, wg_ref[...])
    u = _dot(t, wu_ref[...])
    r = route_ref[...]
    ef = e.astype(F32)
    gate = jnp.where(r[:, 0:1] == ef, r[:, 2:3], 0.0) + jnp.where(r[:, 1:2] == ef, r[:, 3:4], 0.0)
    hdn = a * (1.0 / (1.0 + jnp.exp(-a))) * u * gate
    o_ref[...] += _dot(_mx(hdn), wd_ref[...])


def _moe_dense(x, t, route, w_gate, w_up, w_down, tm=512):
    n = x.shape[0]
    tm = _tile(n, tm)
    wg = _mx(w_gate.reshape(N_EXPERTS, D_MODEL, EXPERT_FF))
    wu = _mx(w_up.reshape(N_EXPERTS, D_MODEL, EXPERT_FF))
    wd = _mx(w_down.reshape(N_EXPERTS, EXPERT_FF, D_MODEL))
    row = lambda i, e: (i, 0)
    return pl.pallas_call(
        _moe_dense_kernel,
        grid=(n // tm, N_EXPERTS),
        in_specs=[pl.BlockSpec((tm, D_MODEL), row), pl.BlockSpec((tm, D_MODEL), row),
                  pl.BlockSpec((tm, LANES), row),
                  pl.BlockSpec((None, D_MODEL, EXPERT_FF), lambda i, e: (e, 0, 0)),
                  pl.BlockSpec((None, D_MODEL, EXPERT_FF), lambda i, e: (e, 0, 0)),
                  pl.BlockSpec((None, EXPERT_FF, D_MODEL), lambda i, e: (e, 0, 0))],
        out_specs=pl.BlockSpec((tm, D_MODEL), row),
        out_shape=jax.ShapeDtypeStruct((n, D_MODEL), F32),
        compiler_params=_cparams(("parallel", "arbitrary")),
        name="moe_dense",
    )(x, t, route, wg, wu, wd)


def _hier_moe(x, g, w_group, b_group, w_router, b_router, w_gate, w_up, w_down):
    t, route = _router(x, g, w_group, b_group, w_router, b_router)
    return _moe_dense(x, t, route, w_gate, w_up, w_down)


def _prompt_bias_tiles(rel_bias_heads, t):
    r = np.arange(t)
    d = r[:, None] - r[None, :]
    idx = np.stack([_t5_bucket_np(d), _t5_bucket_np(d + t), np.full((t, t), REL_BUCKETS - 1, np.int32)])
    return jnp.transpose(rel_bias_heads[idx], (3, 0, 1, 2))


def _decode_tables(rel_bias, past, t_new, lk):
    nb = REL_BUCKETS - 1
    t8 = np.arange(T8)
    ps = LANES
    d_last = (past + t8[:, None]) - (past - ps + np.arange(ps)[None, :])
    d_new = t8[:, None] - np.arange(LANES)[None, :]
    ok_new = (d_new >= 0) & (np.arange(LANES)[None, :] < t_new)
    ra = rel_bias[:, :A_HEADS]
    far = jnp.broadcast_to(ra[nb][:, None, None, None], (A_HEADS, 2, T8, LANES))
    last = jnp.broadcast_to(jnp.transpose(ra[_t5_bucket_np(d_last)], (2, 0, 1))[:, None], (A_HEADS, 2, T8, LANES))
    new = jnp.where(ok_new[None], jnp.transpose(ra[_t5_bucket_np(d_new)], (2, 0, 1)), NEG)
    new = jnp.broadcast_to(new[:, None], (A_HEADS, 2, T8, LANES))
    taba = jnp.stack([far, last, new]).reshape(3, A_HEADS * 2 * T8, LANES)
    kpos = np.arange(lk)
    d_b = (past + t8[:, None]) - kpos[None, :]
    rb = rel_bias[:, A_HEADS:]
    tabb = jnp.transpose(rb[_t5_bucket_np(d_b)], (2, 0, 1)).reshape(B_HEADS * T8, lk)
    valid = (kpos[None, :] < past) | ((d_b >= 0) & (kpos[None, :] < past + t_new))
    validb = jnp.asarray(np.where(valid, 0.0, NEG), F32)
    return taba, tabb, validb


def _pad_rows(x, rows):
    pad = [(0, 0)] * x.ndim
    pad[1] = (0, rows - x.shape[1])
    return jnp.pad(x, pad)


def kernel(x_prompt, x_sample, cache_a_k, cache_a_v, cache_b_k, cache_b_v, cache_b_kidx, cache_c_k, cache_c_v, cache_c_logf, page_table, rel_bias, ab_norm, ab_w_in, a_q_norm, a_k_norm, b_q_norm, b_k_norm, a_lambda_q1, a_lambda_k1, a_lambda_q2, a_lambda_k2, a_sub_norm, ab_w_out, c_norm, c_w_in, c_forget_bias, c_q_norm, c_k_norm, c_w_out, ffn_norm, moe_w_group, moe_b_group, moe_w_router, moe_b_router, moe_w_gate, moe_w_up, moe_w_down):
    nb, s, d = x_prompt.shape
    db, ts, _ = x_sample.shape
    npg = page_table.shape[1]
    ps = cache_a_k.shape[2]
    past = npg * ps
    n_p = nb * s
    n_s = db * ts
    depth = ffn_norm.shape[0]
    topk_p = min(IDX_TOPK_MAX, s // 4)
    topk_s = min(IDX_TOPK_MAX, (past + ts) // 4)
    lk = past + LANES

    x = jnp.concatenate([x_prompt.reshape(n_p, d), x_sample.reshape(n_s, d)], axis=0)
    outs_p = {k: [] for k in ("ak", "av", "bk", "bv", "bi", "ck", "cv", "cf")}
    outs_s = {k: [] for k in ("ak", "av", "bk", "bv", "bi", "ck", "cv", "cf")}

    def smp(a):
        return a[n_p:].reshape(db, ts, a.shape[1])

    for layer in range(depth):
        if layer % 2 == 0:
            e = layer // 2
            lam_init = 0.8 - 0.6 * math.exp(-0.3 * layer)
            lamv = jnp.stack([a_lambda_q1[e], a_lambda_k1[e], a_lambda_q2[e], a_lambda_k2[e]])
            gsub = a_sub_norm[e][None, :]
            (qa, ka, kabf, qb, small, kk, vv, kiki, va, vabf, qi) = _proj0(
                x, ab_norm[e], ab_w_in[e], a_q_norm[e], a_k_norm[e], b_q_norm[e], b_k_norm[e])
            kb, vb, ki, wi = small[:, 0:64], small[:, 64:128], small[:, 128:192], small[:, 192:200]
            tq_a = 256
            bias_a = _prompt_bias_tiles(rel_bias[:, :A_HEADS], tq_a)
            oa_p = _attn_a_prompt(qa, kabf, vabf, lamv, gsub, bias_a, nb, s, lam_init, tq=tq_a)
            tq_b = 128
            bias_b = _prompt_bias_tiles(rel_bias[:, A_HEADS:], tq_b)
            c31 = bias_b[:, 2:3, 0:1, 0:1].reshape(B_HEADS, 1, 1)
            bias_bd = bias_b[:, 0:2] - c31[:, :, :, None]
            ob_p = _attn_b_prompt(qb, qi, small, kk, vv, kiki, bias_bd, c31, nb, s, topk_p, tq=tq_b)
            qa_s = _pad_rows(smp(qa), T8)
            hc = np.arange(2 * A_HEADS)
            colmask = jnp.asarray((np.arange(qa_s.shape[2])[None, :] // HEAD_DIM) == hc[:, None], MXU_DT)
            qa_bd = (qa_s[:, None, :, :] * colmask[None, :, None, :]).reshape(db, 2 * A_HEADS * T8, -1)

            def heads_rows(a, nh):
                a = _pad_rows(a, T8).reshape(db, T8, nh, HEAD_DIM)
                return jnp.transpose(a, (0, 2, 1, 3)).reshape(db, nh * T8, HEAD_DIM)

            qi64 = heads_rows(smp(qi), IDX_HEADS)
            qb64 = heads_rows(smp(qb), B_HEADS)
            small_s = smp(small)
            w8 = _pad_rows(small_s[:, :, 192:200], T8)
            kna = _pad_rows(smp(ka), 16)
            vna = _pad_rows(smp(va), 16)
            bnew = _pad_rows(small_s, 16)
            taba, tabb, validb = _decode_tables(rel_bias, past, ts, lk)
            oa_d, ob_d = _decode0(page_table, lamv, gsub, qa_bd, qi64, qb64, w8, kna, vna, bnew,
                                  cache_a_k, cache_a_v, cache_b_k, cache_b_v, cache_b_kidx,
                                  taba, tabb, validb, e, topk_s, lam_init)
            oa_s = jnp.transpose(oa_d[:, :, :ts], (0, 2, 1, 3)).reshape(n_s, A_HEADS * LANES)
            ob_s = jnp.transpose(ob_d[:, :, :ts], (0, 2, 1, 3)).reshape(n_s, B_HEADS * HEAD_DIM)
            oa = jnp.concatenate([oa_p, _mx(oa_s)], axis=0)
            ob = jnp.concatenate([ob_p, _mx(ob_s)], axis=0)
            w_out = ab_w_out[e]
            x = _outproj(x, [(oa, w_out[:A_HEADS * LANES]), (ob, w_out[A_HEADS * LANES:])])
            for dst, lo, n_, shp in ((outs_p, 0, n_p, (nb, s)), (outs_s, n_p, n_s, (db, ts))):
                dst["ak"].append(ka[lo:lo + n_].reshape(*shp, A_HEADS, 2, HEAD_DIM))
                dst["av"].append(va[lo:lo + n_].reshape(*shp, A_HEADS, 2 * HEAD_DIM))
                dst["bk"].append(kb[lo:lo + n_].reshape(*shp, HEAD_DIM))
                dst["bv"].append(vb[lo:lo + n_].reshape(*shp, HEAD_DIM))
                dst["bi"].append(ki[lo:lo + n_].reshape(*shp, IDX_DIM))
        else:
            o = layer // 2
            q, k, kbf, v, vbf, logf128 = _proj1(x, c_norm[o], c_w_in[o], c_forget_bias[o], c_q_norm[o], c_k_norm[o])
            logf = logf128[:, :C_HEADS]
            cum_p = _cumsum_prompt(logf128[:n_p], nb, s)[:, :C_HEADS]
            ckt = jnp.transpose(cum_p.reshape(nb, s, C_HEADS // 2, 2), (0, 2, 3, 1))
            oc_p = _attn_c_prompt(q, kbf, vbf, cum_p, ckt, nb, s)
            cum_s = _cumsum_pages(cache_c_logf, o, page_table)
            logf_s = smp(logf)
            c_new = cum_s[:, -1:, :] + jnp.cumsum(logf_s, axis=1)
            c_new8 = _pad_rows(c_new, T8)
            cq = jnp.transpose(c_new8, (0, 2, 1)).reshape(db, C_HEADS * T8, 1)
            cnew_t = jnp.transpose(_pad_rows(c_new, 16), (0, 2, 1))
            ckt_s = jnp.transpose(cum_s, (0, 2, 1))
            q_s = _pad_rows(smp(q), T8)
            hmask = jnp.asarray((np.arange(q_s.shape[2])[None, :] // HEAD_DIM) == np.arange(C_HEADS)[:, None], MXU_DT)
            q_bd = (q_s[:, None, :, :] * hmask[None, :, None, :]).reshape(db, C_HEADS * T8, -1)
            kn = _pad_rows(smp(k), 16)
            vn = _pad_rows(smp(v), 16)
            t8 = np.arange(T8)
            okn = (t8[:, None] >= np.arange(16)[None, :]) & (np.arange(16)[None, :] < ts)
            maskn = jnp.asarray(np.tile(np.where(okn, 0.0, NEG), (C_HEADS, 1)), F32)
            oc_d = _decode1(page_table, q_bd, cq, kn, vn, cnew_t, cache_c_k, cache_c_v, ckt_s, maskn, o)
            oc_s = jnp.transpose(oc_d[:, :, :ts], (0, 2, 1, 3)).reshape(n_s, C_HEADS * HEAD_DIM)
            oc = jnp.concatenate([oc_p, _mx(oc_s)], axis=0)
            x = _outproj(x, [(oc, c_w_out[o])])
            for dst, lo, n_, shp in ((outs_p, 0, n_p, (nb, s)), (outs_s, n_p, n_s, (db, ts))):
                dst["ck"].append(k[lo:lo + n_].reshape(*shp, C_HEADS, HEAD_DIM))
                dst["cv"].append(v[lo:lo + n_].reshape(*shp, C_HEADS, HEAD_DIM))
                dst["cf"].append(logf[lo:lo + n_].reshape(*shp, C_HEADS))
        x = _hier_moe(x, ffn_norm[layer], moe_w_group[layer], moe_b_group[layer], moe_w_router[layer],
                      moe_b_router[layer], moe_w_gate[layer], moe_w_up[layer], moe_w_down[layer])

    keys = ("ak", "av", "bk", "bv", "bi", "ck", "cv", "cf")
    return ((x[:n_p].reshape(nb, s, d), x[n_p:].reshape(db, ts, d))
            + tuple(jnp.stack(outs_p[k]) for k in keys)
            + tuple(jnp.stack(outs_s[k]) for k in keys))
```

```python
import functools
import math

import numpy as np
import jax
import jax.numpy as jnp
from jax import lax
from jax.experimental import pallas as pl
from jax.experimental.pallas import tpu as pltpu

F32 = jnp.float32
I32 = jnp.int32
MXU_DT = jnp.bfloat16

D_MODEL = 1024
HEAD_DIM = 64
A_HEADS = 4
B_HEADS = 8
IDX_HEADS = 8
IDX_DIM = 64
IDX_TOPK_MAX = 256
C_HEADS = 16
REL_BUCKETS = 32
REL_MAX_DIST = 128
N_GROUPS = 4
EXPERTS_PER_GROUP = 8
N_EXPERTS = N_GROUPS * EXPERTS_PER_GROUP
EXPERT_FF = 256
EPS = 1e-6
NEG = -1e30
INT_MIN = -2 ** 31
QK_SCALE = HEAD_DIM ** -0.5
IDX_SCALE = IDX_DIM ** -0.5
IDX_HEAD_SCALE = IDX_HEADS ** -0.5
LANES = 128
T8 = 8
VMEM_LIMIT = 56 * 1024 * 1024
PAGES_PER_STEP = 4


def _tile(n, pref):
    best = 16
    for t in range(16, pref + 1, 16):
        if n % t == 0:
            best = t
    assert n % best == 0, (n, pref)
    return best


def _cparams(sem):
    return pltpu.CompilerParams(dimension_semantics=sem, vmem_limit_bytes=VMEM_LIMIT)


def _mx(x):
    return x.astype(MXU_DT)


def _dot(a, b):
    return jnp.dot(a, b, preferred_element_type=F32)


def _dot_t(a, b):
    return lax.dot_general(a, b, (((1,), (1,)), ((), ())), preferred_element_type=F32)


def _split2(x):
    hi = _mx(x)
    lo = _mx(x - hi.astype(F32))
    return hi, lo


def _split3(x):
    hi = _mx(x)
    r = x - hi.astype(F32)
    mid = _mx(r)
    lo = _mx(r - mid.astype(F32))
    return hi, mid, lo


def _lane_iota(shape):
    return lax.broadcasted_iota(I32, shape, len(shape) - 1)


def _row_iota(shape):
    return lax.broadcasted_iota(I32, shape, len(shape) - 2)


def _lanes(x, width):
    if width <= LANES:
        return x[:, :width]
    return jnp.tile(x, (1, width // LANES))


def _rms_rows(x, g):
    ms = jnp.mean(x * x, axis=-1, keepdims=True)
    return x * lax.rsqrt(ms + EPS) * g


def _seg_rsqrt(y, bd):
    hi, lo = _split2(y * y)
    ss = _dot(hi, bd) + _dot(lo, bd)
    return lax.rsqrt(ss * (1.0 / HEAD_DIM) + EPS)


def _block_diag_ones(n, seg):
    r = np.arange(n)
    return jnp.asarray((r[:, None] // seg) == (r[None, :] // seg), dtype=MXU_DT)


def _lambda(lamv_ref, lam_init):
    lv = lamv_ref[...]
    return (jnp.exp(jnp.sum(lv[0:1] * lv[1:2], axis=-1, keepdims=True))
            - jnp.exp(jnp.sum(lv[2:3] * lv[3:4], axis=-1, keepdims=True)) + lam_init)


def _t5_bucket_np(d):
    n = np.maximum(d, 0)
    exact = REL_BUCKETS // 2
    nf = np.maximum(n, 1).astype(np.float64)
    large = exact + (np.log(nf / exact) / math.log(REL_MAX_DIST / exact) * (REL_BUCKETS - exact)).astype(np.int64)
    large = np.minimum(large, REL_BUCKETS - 1)
    return np.where(n < exact, n, large).astype(np.int32)


def _bias_expand_kernel(relb_ref, idx_ref, o_ref, *, head0):
    h = pl.program_id(0) + head0
    idx = idx_ref[...]
    acc = jnp.zeros(idx.shape, F32)
    for b in range(REL_BUCKETS):
        acc = jnp.where(idx == b, relb_ref[b, h], acc)
    o_ref[...] = acc


def _bias_expand(rel_bias, idx_np, head0, n_heads):
    idx = jnp.asarray(idx_np, I32)
    nd = idx.ndim
    zeros = (0,) * nd
    return pl.pallas_call(
        functools.partial(_bias_expand_kernel, head0=head0),
        grid=(n_heads,),
        in_specs=[pl.BlockSpec(memory_space=pltpu.SMEM),
                  pl.BlockSpec(idx.shape, lambda h: zeros)],
        out_specs=pl.BlockSpec((None,) + idx.shape, lambda h: (h,) + zeros),
        out_shape=jax.ShapeDtypeStruct((n_heads,) + idx.shape, F32),
        compiler_params=_cparams(("arbitrary",)),
        name="bias_expand",
    )(rel_bias, idx)


def _prompt_bucket_tiles(t):
    r = np.arange(t)
    d = r[:, None] - r[None, :]
    return np.stack([_t5_bucket_np(d), _t5_bucket_np(d + t), np.full((t, t), REL_BUCKETS - 1, np.int32)])


def _proj0_kernel(x_ref, g_ref, w_ref, gain_ref, bd_ref,
                  qa_ref, ka_ref, kabf_ref, qb_ref, small_ref, kk_ref, vv_ref, kiki_ref,
                  va_ref, vabf_ref, qi_ref):
    h = _mx(_rms_rows(x_ref[...], g_ref[...]))
    bd = bd_ref[...]

    def chunk(c):
        return _dot(h, w_ref[:, c * 256:(c + 1) * 256])

    def normed(c):
        y = chunk(c)
        return y * _seg_rsqrt(y, bd) * gain_ref[:, c * 256:(c + 1) * 256]

    for c in range(2):
        qa_ref[:, c * 256:(c + 1) * 256] = _mx(normed(c) * QK_SCALE)
    for c in range(2):
        y = normed(2 + c)
        ka_ref[:, c * 256:(c + 1) * 256] = y
        kabf_ref[:, c * 256:(c + 1) * 256] = _mx(y)
    for c in range(2):
        qb_ref[:, c * 256:(c + 1) * 256] = _mx(normed(4 + c) * QK_SCALE)
    y = chunk(6)
    yn = y * _seg_rsqrt(y, bd) * gain_ref[:, 6 * 256:7 * 256]
    lane = _lane_iota(y.shape)
    y = jnp.where(lane < HEAD_DIM, yn, y)
    small_ref[...] = y
    t0 = y[:, :LANES]
    t1 = y[:, LANES:]
    lo = _lane_iota(t0.shape) < HEAD_DIM
    r0 = pltpu.roll(t0, HEAD_DIM, 1)
    kk_ref[...] = _mx(jnp.where(lo, t0, r0))
    vv_ref[...] = _mx(jnp.where(lo, r0, t0))
    r1 = pltpu.roll(t1, HEAD_DIM, 1)
    kiki_ref[...] = _mx(jnp.where(lo, t1, r1))
    for c in range(2):
        y = chunk(7 + c)
        va_ref[:, c * 256:(c + 1) * 256] = y
        vabf_ref[:, c * 256:(c + 1) * 256] = _mx(y)
    for c in range(2):
        qi_ref[:, c * 256:(c + 1) * 256] = _mx(chunk(9 + c) * IDX_SCALE)


def _proj0(x, g, w_in, a_qn, a_kn, b_qn, b_kn, tm=256):
    n = x.shape[0]
    tm = _tile(n, tm)
    sp = np.cumsum([512, 512, 512, 512, 64, 64, 512, 64, 8])[:-1]
    wqa, wka, wva, wqb, wkb, wvb, wqi, wki, wwi = jnp.split(w_in, sp, axis=1)
    w = jnp.concatenate([wqa, wka, wqb, wkb, wvb, wki, wwi, jnp.zeros((D_MODEL, 56), F32), wva, wqi], axis=1)
    w = _mx(w)
    ncol = w.shape[1]
    gain = jnp.concatenate([jnp.tile(a_qn, 8), jnp.tile(a_kn, 8), jnp.tile(b_qn, 8), b_kn,
                            jnp.ones((192,), F32)])[None, :]
    bd = _block_diag_ones(256, HEAD_DIM)
    row = lambda i: (i, 0)
    fixed = lambda i: (0, 0)
    widths = [(512, MXU_DT), (512, F32), (512, MXU_DT), (512, MXU_DT), (256, F32), (128, MXU_DT),
              (128, MXU_DT), (128, MXU_DT), (512, F32), (512, MXU_DT), (512, MXU_DT)]
    return pl.pallas_call(
        _proj0_kernel,
        grid=(n // tm,),
        in_specs=[pl.BlockSpec((tm, D_MODEL), row), pl.BlockSpec((1, D_MODEL), fixed),
                  pl.BlockSpec((D_MODEL, ncol), fixed), pl.BlockSpec((1, gain.shape[1]), fixed),
                  pl.BlockSpec((256, 256), fixed)],
        out_specs=[pl.BlockSpec((tm, wd), row) for wd, _ in widths],
        out_shape=[jax.ShapeDtypeStruct((n, wd), dt) for wd, dt in widths],
        compiler_params=_cparams(("parallel",)),
        name="proj0",
    )(x, g[None, :], w, gain, bd)


def _proj1_kernel(x_ref, g_ref, w_ref, gain_ref, bf_ref, bd_ref,
                  q_ref, k_ref, kbf_ref, v_ref, vbf_ref, logf_ref):
    h = _mx(_rms_rows(x_ref[...], g_ref[...]))
    bd = bd_ref[...]

    def chunk(c):
        return _dot(h, w_ref[:, c * 256:(c + 1) * 256])

    def normed(c):
        y = chunk(c)
        return y * _seg_rsqrt(y, bd) * gain_ref[:, c * 256:(c + 1) * 256]

    for c in range(4):
        q_ref[:, c * 256:(c + 1) * 256] = _mx(normed(c) * QK_SCALE)
    for c in range(4):
        y = normed(4 + c)
        k_ref[:, c * 256:(c + 1) * 256] = y
        kbf_ref[:, c * 256:(c + 1) * 256] = _mx(y)
    for c in range(4):
        y = chunk(8 + c)
        v_ref[:, c * 256:(c + 1) * 256] = y
        vbf_ref[:, c * 256:(c + 1) * 256] = _mx(y)
    f = _dot(h, w_ref[:, 12 * 256:12 * 256 + LANES]) + bf_ref[...]
    logf_ref[...] = jnp.minimum(f, 0.0) - jnp.log(1.0 + jnp.exp(-jnp.abs(f)))


def _proj1(x, g, w_in, b_f, qn, kn, tm=256):
    n = x.shape[0]
    tm = _tile(n, tm)
    w = _mx(jnp.concatenate([w_in, jnp.zeros((D_MODEL, LANES - C_HEADS), F32)], axis=1))
    ncol = w.shape[1]
    gain = jnp.concatenate([jnp.tile(qn, C_HEADS), jnp.tile(kn, C_HEADS)])[None, :]
    bf = jnp.concatenate([b_f, jnp.zeros((LANES - C_HEADS,), F32)])[None, :]
    bd = _block_diag_ones(256, HEAD_DIM)
    row = lambda i: (i, 0)
    fixed = lambda i: (0, 0)
    widths = [(1024, MXU_DT), (1024, F32), (1024, MXU_DT), (1024, F32), (1024, MXU_DT), (LANES, F32)]
    return pl.pallas_call(
        _proj1_kernel,
        grid=(n // tm,),
        in_specs=[pl.BlockSpec((tm, D_MODEL), row), pl.BlockSpec((1, D_MODEL), fixed),
                  pl.BlockSpec((D_MODEL, ncol), fixed), pl.BlockSpec((1, gain.shape[1]), fixed),
                  pl.BlockSpec((1, LANES), fixed), pl.BlockSpec((256, 256), fixed)],
        out_specs=[pl.BlockSpec((tm, wd), row) for wd, _ in widths],
        out_shape=[jax.ShapeDtypeStruct((n, wd), dt) for wd, dt in widths],
        compiler_params=_cparams(("parallel",)),
        name="proj1",
    )(x, g[None, :], w, gain, bf, bd)


def _outproj_kernel(*refs, n_in):
    res_ref = refs[0]
    out_ref = refs[-1]
    acc = res_ref[...]
    for i in range(n_in):
        acc = acc + _dot(refs[1 + 2 * i][...], refs[2 + 2 * i][...])
    out_ref[...] = acc


def _outproj(res, pairs, tm=512):
    n = res.shape[0]
    tm = _tile(n, tm)
    row = lambda i: (i, 0)
    fixed = lambda i: (0, 0)
    in_specs = [pl.BlockSpec((tm, D_MODEL), row)]
    args = [res]
    for a, w in pairs:
        in_specs += [pl.BlockSpec((tm, a.shape[1]), row), pl.BlockSpec(w.shape, fixed)]
        args += [a, _mx(w)]
    return pl.pallas_call(
        functools.partial(_outproj_kernel, n_in=len(pairs)),
        grid=(n // tm,),
        in_specs=in_specs,
        out_specs=pl.BlockSpec((tm, D_MODEL), row),
        out_shape=jax.ShapeDtypeStruct((n, D_MODEL), F32),
        compiler_params=_cparams(("parallel",)),
        name="outproj",
    )(*args)


def _stack_streams(q):
    lane = _lane_iota(q.shape)
    zero = jnp.zeros_like(q)
    return jnp.concatenate([jnp.where(lane < HEAD_DIM, q, zero), jnp.where(lane >= HEAD_DIM, q, zero)], axis=0)


def _flash_stacked(q2, k_ref, v_ref, i, tq, tk, bias_fn, m_ref, l_ref, acc_ref):
    rows = 2 * tq
    m_ref[...] = jnp.full(m_ref.shape, NEG, F32)
    l_ref[...] = jnp.zeros(l_ref.shape, F32)
    acc_ref[...] = jnp.zeros(acc_ref.shape, F32)
    r = _row_iota((rows, tk))
    row = jnp.where(r >= tq, r - tq, r) + i * tq
    col = _lane_iota((rows, tk))

    def body(j, carry):
        off = pl.multiple_of(j * tk, tk)
        k = k_ref[pl.ds(off, tk), :]
        v = v_ref[pl.ds(off, tk), :]
        s = _dot_t(q2, k) + bias_fn(j, off)
        s = jnp.where((col + j * tk) <= row, s, NEG)
        m_old = m_ref[...]
        m_new = jnp.maximum(m_old, jnp.max(s, axis=-1, keepdims=True))
        alpha = jnp.exp(m_old - m_new)
        p = jnp.exp(s - _lanes(m_new, tk))
        l_ref[...] = alpha * l_ref[...] + jnp.sum(p, axis=-1, keepdims=True)
        acc_ref[...] = alpha * acc_ref[...] + _dot(_mx(p), v)
        m_ref[...] = m_new
        return carry

    lax.fori_loop(0, i + 1, body, 0)


def _attn_a_kernel(lamv_ref, gsub_ref, q_ref, k_ref, v_ref, bias_ref, o_ref,
                   m_ref, l_ref, acc_ref, *, tq, tk, lam_init):
    i = pl.program_id(2)

    def bias_fn(j, off):
        b = bias_ref[jnp.minimum(i - j, 2)]
        return jnp.concatenate([b, b], axis=0)

    _flash_stacked(_stack_streams(q_ref[...]), k_ref, v_ref, i, tq, tk, bias_fn, m_ref, l_ref, acc_ref)
    lam = _lambda(lamv_ref, lam_init)
    o = acc_ref[:tq] / l_ref[:tq] - lam * (acc_ref[tq:] / l_ref[tq:])
    o = _rms_rows(o, gsub_ref[...]) * (1.0 - lam_init)
    o_ref[...] = _mx(o)


def _attn_a_prompt(qa, ka, va, lamv, gsub, bias_tiles, nb, s, lam_init, tq=256):
    tk = tq
    nq = s // tq
    kern = functools.partial(_attn_a_kernel, tq=tq, tk=tk, lam_init=lam_init)
    return pl.pallas_call(
        kern,
        grid=(nb, A_HEADS, nq),
        in_specs=[pl.BlockSpec((4, HEAD_DIM), lambda b, h, i: (0, 0)),
                  pl.BlockSpec((1, LANES), lambda b, h, i: (0, 0)),
                  pl.BlockSpec((tq, LANES), lambda b, h, i: (b * nq + i, h)),
                  pl.BlockSpec((s, LANES), lambda b, h, i: (b, h)),
                  pl.BlockSpec((s, LANES), lambda b, h, i: (b, h)),
                  pl.BlockSpec((None, 3, tq, tk), lambda b, h, i: (h, 0, 0, 0))],
        out_specs=pl.BlockSpec((tq, LANES), lambda b, h, i: (b * nq + i, h)),
        out_shape=jax.ShapeDtypeStruct((nb * s, A_HEADS * LANES), MXU_DT),
        scratch_shapes=[pltpu.VMEM((2 * tq, LANES), F32), pltpu.VMEM((2 * tq, LANES), F32),
                        pltpu.VMEM((2 * tq, LANES), F32)],
        compiler_params=_cparams(("parallel", "parallel", "arbitrary")),
        name="attn_a_prompt",
    )(lamv, gsub, qa, ka, va, bias_tiles)


def _attn_c_kernel(q_ref, k_ref, v_ref, cq_ref, ckt_ref, o_ref, m_ref, l_ref, acc_ref, *, tq, tk):
    hp = pl.program_id(1)
    i = pl.program_id(2)
    cq_tile = cq_ref[...]
    lane16 = _lane_iota(cq_tile.shape)
    cq2 = jnp.concatenate(
        [jnp.broadcast_to(jnp.sum(jnp.where(lane16 == 2 * hp + c, cq_tile, 0.0), axis=-1, keepdims=True),
                          (tq, LANES)) for c in range(2)], axis=0)

    def bias_fn(j, off):
        ck = ckt_ref[:, pl.ds(off, tk)]
        ck2 = jnp.concatenate([jnp.broadcast_to(ck[0:1], (tq, tk)), jnp.broadcast_to(ck[1:2], (tq, tk))], axis=0)
        return _lanes(cq2, tk) - ck2

    _flash_stacked(_stack_streams(q_ref[...]), k_ref, v_ref, i, tq, tk, bias_fn, m_ref, l_ref, acc_ref)
    lane = _lane_iota((tq, LANES))
    o = jnp.where(lane < HEAD_DIM, acc_ref[:tq] / l_ref[:tq], acc_ref[tq:] / l_ref[tq:])
    o_ref[...] = _mx(o)


def _attn_c_prompt(q, k, v, cq, ckt, nb, s, tq=256):
    tk = tq
    nq = s // tq
    nhp = C_HEADS // 2
    kern = functools.partial(_attn_c_kernel, tq=tq, tk=tk)
    return pl.pallas_call(
        kern,
        grid=(nb, nhp, nq),
        in_specs=[pl.BlockSpec((tq, LANES), lambda b, h, i: (b * nq + i, h)),
                  pl.BlockSpec((s, LANES), lambda b, h, i: (b, h)),
                  pl.BlockSpec((s, LANES), lambda b, h, i: (b, h)),
                  pl.BlockSpec((tq, C_HEADS), lambda b, h, i: (b * nq + i, 0)),
                  pl.BlockSpec((None, None, 2, s), lambda b, h, i: (b, h, 0, 0))],
        out_specs=pl.BlockSpec((tq, LANES), lambda b, h, i: (b * nq + i, h)),
        out_shape=jax.ShapeDtypeStruct((nb * s, C_HEADS * HEAD_DIM), MXU_DT),
        scratch_shapes=[pltpu.VMEM((2 * tq, LANES), F32), pltpu.VMEM((2 * tq, LANES), F32),
                        pltpu.VMEM((2 * tq, LANES), F32)],
        compiler_params=_cparams(("parallel", "parallel", "arbitrary")),
        name="attn_c_prompt",
    )(q, k, v, cq, ckt)


def _score_keys(score):
    score = jnp.where(score == 0.0, 0.0, score)
    bits = pltpu.bitcast(score, I32)
    return bits ^ (jnp.right_shift(bits, 31) & 0x7FFFFFFF)


def _topk_select(keys_ref, kcount, active, col):
    kf = float(kcount)
    nbits_col = int(keys_ref.shape[1] - 1).bit_length()

    def count(pred):
        return jnp.sum(jnp.where(pred, 1.0, 0.0), axis=1, keepdims=True)

    t0 = jnp.where(count(keys_ref[...] >= 0) >= kf, 0, INT_MIN).astype(I32)

    def body(it, t):
        cand = t + jnp.left_shift(jnp.int32(1), 30 - it)
        return jnp.where(count(keys_ref[...] >= cand) >= kf, cand, t)

    t = lax.fori_loop(0, 31, body, t0)
    t = jnp.where(active, t, INT_MIN)
    keys = keys_ref[...]
    gt = keys > t
    eq = keys == t
    need = kf - count(gt)
    excess = jnp.where(active, count(eq) - need, 0.0)

    def tie_break():
        def tb(it, jj):
            cand = jj + jnp.left_shift(jnp.int32(1), nbits_col - 1 - it)
            c = count((keys_ref[...] == t) & (col < cand))
            return jnp.where(c < need, cand, jj)
        return lax.fori_loop(0, nbits_col, tb, jnp.zeros(t.shape, I32))

    jmax = lax.cond(jnp.max(excess) > 0.0, tie_break,
                    lambda: jnp.full(t.shape, keys_ref.shape[1], I32))
    return gt | (eq & (col <= jmax))


def _attn_b_kernel(qb_ref, qi_ref, w_ref, kk_ref, vv_ref, kiki_ref, bias_ref, c31_ref, o_ref,
                   keys_ref, selm_ref, s_ref, *, tq, s_len, topk):
    i = pl.program_id(1)
    shape = (tq, s_len)
    row = _row_iota(shape) + i * tq
    col = _lane_iota(shape)
    causal = col <= row
    lane = _lane_iota((tq, LANES))
    halves = (lane < HEAD_DIM, lane >= HEAD_DIM)
    kiki = kiki_ref[...]
    wt = w_ref[...] * IDX_HEAD_SCALE

    score = jnp.zeros(shape, F32)
    for h in range(IDX_HEADS):
        qt = qi_ref[:, (h // 2) * LANES:(h // 2 + 1) * LANES]
        qh = jnp.where(halves[h % 2], qt, jnp.zeros_like(qt))
        dots = _dot_t(qh, kiki)
        score = score + wt[:, HEAD_DIM + h:HEAD_DIM + h + 1] * jnp.maximum(dots, 0.0)
    keys_ref[...] = _score_keys(jnp.where(causal, score, -jnp.inf))

    qpos = _row_iota((tq, 1)) + i * tq
    sel = _topk_select(keys_ref, topk, qpos >= topk, col)
    selm_ref[...] = jnp.where(sel & causal, 0.0, NEG)

    kk = kk_ref[...]
    vv = vv_ref[...]
    for hp in range(B_HEADS // 2):
        qt = qb_ref[:, hp * LANES:(hp + 1) * LANES]
        outs = []
        for c in range(2):
            h = 2 * hp + c
            qh = jnp.where(halves[c], qt, jnp.zeros_like(qt))
            s_ref[...] = _dot_t(qh, kk) + c31_ref[h]
            off = pl.multiple_of(i * tq, tq)
            s_ref[:, pl.ds(off, tq)] += bias_ref[h, 0]

            @pl.when(i > 0)
            def _():
                off1 = pl.multiple_of((i - 1) * tq, tq)
                s_ref[:, pl.ds(off1, tq)] += bias_ref[h, 1]

            s = s_ref[...] + selm_ref[...]
            m = jnp.max(s, axis=-1, keepdims=True)
            p = jnp.exp(s - m)
            l = jnp.sum(p, axis=-1, keepdims=True)
            outs.append(_dot(_mx(p), vv) / l)
        o_ref[:, hp * LANES:(hp + 1) * LANES] = _mx(jnp.where(halves[0], outs[0], outs[1]))


def _attn_b_prompt(qb, qi, small, kk, vv, kiki, bias_d, c31, nb, s, topk, tq=128):
    nq = s // tq
    kern = functools.partial(_attn_b_kernel, tq=tq, s_len=s, topk=topk)
    qrow = lambda b, i: (b * nq + i, 0)
    kv = lambda b, i: (b, 0)
    return pl.pallas_call(
        kern,
        grid=(nb, nq),
        in_specs=[pl.BlockSpec((tq, B_HEADS * HEAD_DIM), qrow),
                  pl.BlockSpec((tq, IDX_HEADS * IDX_DIM), qrow),
                  pl.BlockSpec((tq, LANES), lambda b, i: (b * nq + i, 1)),
                  pl.BlockSpec((s, LANES), kv), pl.BlockSpec((s, LANES), kv), pl.BlockSpec((s, LANES), kv),
                  pl.BlockSpec(bias_d.shape, lambda b, i: (0, 0, 0, 0)),
                  pl.BlockSpec(c31.shape, lambda b, i: (0, 0, 0))],
        out_specs=pl.BlockSpec((tq, B_HEADS * HEAD_DIM), qrow),
        out_shape=jax.ShapeDtypeStruct((nb * s, B_HEADS * HEAD_DIM), MXU_DT),
        scratch_shapes=[pltpu.VMEM((tq, s), I32), pltpu.VMEM((tq, s), F32), pltpu.VMEM((tq, s), F32)],
        compiler_params=_cparams(("parallel", "arbitrary")),
        name="attn_b_prompt",
    )(qb, qi, small, kk, vv, kiki, bias_d, c31)


def _cumsum_rows_kernel(x_ref, tri_ref, o_ref, carry_ref):
    j = pl.program_id(1)

    @pl.when(j == 0)
    def _():
        carry_ref[...] = jnp.zeros(carry_ref.shape, F32)

    tri = tri_ref[...]
    hi, mid, lo = _split3(x_ref[...])
    cum = _dot(tri, hi) + _dot(tri, mid) + _dot(tri, lo) + carry_ref[...]
    o_ref[...] = cum
    carry_ref[...] = cum[-1:, :]


def _cumsum_prompt(x, nb, s, blk=128):
    nj = s // blk
    w = x.shape[1]
    r = np.arange(blk)
    tri = jnp.asarray(r[:, None] >= r[None, :], dtype=MXU_DT)
    return pl.pallas_call(
        _cumsum_rows_kernel,
        grid=(nb, nj),
        in_specs=[pl.BlockSpec((blk, w), lambda b, j: (b * nj + j, 0)),
                  pl.BlockSpec((blk, blk), lambda b, j: (0, 0))],
        out_specs=pl.BlockSpec((blk, w), lambda b, j: (b * nj + j, 0)),
        out_shape=jax.ShapeDtypeStruct((nb * s, w), F32),
        scratch_shapes=[pltpu.VMEM((1, w), F32)],
        compiler_params=_cparams(("parallel", "arbitrary")),
        name="cumsum_prompt",
    )(x, tri)


def _cumsum_pages_kernel(*refs, n_pages, ps):
    pt_ref = refs[0]
    page_refs = refs[1:1 + n_pages]
    tri_ref, o_ref = refs[1 + n_pages:]
    tri = tri_ref[...]
    carry = jnp.zeros((page_refs[0].shape[0], 1), F32)
    for j in range(n_pages):
        hi, mid, lo = _split3(page_refs[j][...])
        cum = _dot(hi, tri) + _dot(mid, tri) + _dot(lo, tri) + carry
        o_ref[:, j * ps:(j + 1) * ps] = cum
        carry = cum[:, ps - 1:ps]


def _cumsum_pages(logf_t, layer, page_table):
    db, npg = page_table.shape
    nh, ps = logf_t.shape[2], logf_t.shape[3]
    r = np.arange(ps)
    tri = jnp.asarray(r[:, None] <= r[None, :], dtype=MXU_DT)
    page_specs = [pl.BlockSpec((None, None, nh, ps), lambda b, pt, j=j: (layer, pt[b * npg + j], 0, 0))
                  for j in range(npg)]
    grid_spec = pltpu.PrefetchScalarGridSpec(
        num_scalar_prefetch=1,
        grid=(db,),
        in_specs=page_specs + [pl.BlockSpec((ps, ps), lambda b, pt: (0, 0))],
        out_specs=pl.BlockSpec((None, nh, npg * ps), lambda b, pt: (b, 0, 0)),
    )
    return pl.pallas_call(
        functools.partial(_cumsum_pages_kernel, n_pages=npg, ps=ps),
        grid_spec=grid_spec,
        out_shape=jax.ShapeDtypeStruct((db, nh, npg * ps), F32),
        compiler_params=_cparams(("parallel",)),
        name="cumsum_pages",
    )(page_table.reshape(-1), *([logf_t] * npg), tri)


def _online_update(s, m_ref, l_ref):
    m_old = m_ref[...]
    m_new = jnp.maximum(m_old, jnp.max(s, axis=-1, keepdims=True))
    alpha = jnp.exp(m_old - m_new)
    p = jnp.exp(s - _lanes(m_new, s.shape[1]))
    l_ref[...] = alpha * l_ref[...] + jnp.sum(p, axis=-1, keepdims=True)
    m_ref[...] = m_new
    return alpha, p


def _decode0_kernel(*refs, n_steps, npp, ps, topk, lam_init):
    (pt_ref, lamv_ref, gsub_ref, qa_ref, qi_ref, qb_ref, w8_ref, kna_ref, vna_ref, bnew_ref,
     taba_ref, tabb_ref, validb_ref) = refs[:13]
    pages = refs[13:13 + 5 * npp]
    cak, cav, cbk, cbv, cbi = (pages[0:npp], pages[npp:2 * npp], pages[2 * npp:3 * npp],
                               pages[3 * npp:4 * npp], pages[4 * npp:5 * npp])
    oa_ref, ob_ref, m_ref, l_ref, acc_ref, kb_ref, vb_ref, ki_ref, keys_ref = refs[13 + 5 * npp:]
    step = pl.program_id(1)
    past = n_steps * npp * ps
    lk = kb_ref.shape[1]

    @pl.when(step == 0)
    def _():
        m_ref[...] = jnp.full(m_ref.shape, NEG, F32)
        l_ref[...] = jnp.zeros(l_ref.shape, F32)
        acc_ref[...] = jnp.zeros(acc_ref.shape, F32)

    qa = qa_ref[...]
    far = taba_ref[0]
    tail = taba_ref[jnp.where(step == n_steps - 1, 1, 0)]
    s = jnp.concatenate([_dot(qa, _mx(cak[j][...])) + (tail if j == npp - 1 else far) for j in range(npp)], axis=1)
    alpha, p = _online_update(s, m_ref, l_ref)
    p = _mx(p)
    for h in range(A_HEADS):
        rows = slice(h * 2 * T8, (h + 1) * 2 * T8)
        upd = alpha[rows] * acc_ref[rows]
        for j in range(npp):
            vh = _mx(cav[j][pl.ds(h, ps, stride=A_HEADS), :])
            upd = upd + _dot(p[rows, j * ps:(j + 1) * ps], vh)
        acc_ref[rows] = upd

    for j in range(npp):
        off = pl.multiple_of((step * npp + j) * ps, ps)
        kb_ref[:, pl.ds(off, ps)] = _mx(cbk[j][...])
        vb_ref[:, pl.ds(off, ps)] = _mx(cbv[j][...])
        ki_ref[:, pl.ds(off, ps)] = _mx(cbi[j][...])

    @pl.when(step == n_steps - 1)
    def _():
        s_new = _dot_t(qa, _mx(kna_ref[...])) + taba_ref[2][:, :kna_ref.shape[0]]
        alpha2, p2 = _online_update(s_new, m_ref, l_ref)
        vn = _mx(vna_ref[...])
        lam = _lambda(lamv_ref, lam_init)
        for h in range(A_HEADS):
            rows = slice(h * 2 * T8, (h + 1) * 2 * T8)
            o16 = (alpha2[rows] * acc_ref[rows] + _dot(_mx(p2[rows]), vn[:, h * LANES:(h + 1) * LANES])) / l_ref[rows]
            o = o16[:T8] - lam * o16[T8:]
            oa_ref[h] = _rms_rows(o, gsub_ref[...]) * (1.0 - lam_init)

        bn = bnew_ref[...]
        kb_ref[:, past:past + LANES] = _mx(bn[0])
        vb_ref[:, past:past + LANES] = _mx(bn[1])
        ki_ref[:, past:past + LANES] = _mx(bn[2])
        dots = jnp.maximum(_dot(qi_ref[...], ki_ref[...]), 0.0)
        w8 = w8_ref[...] * IDX_HEAD_SCALE
        score = jnp.zeros((T8, lk), F32)
        for h in range(IDX_HEADS):
            score = score + w8[:, h:h + 1] * dots[h * T8:(h + 1) * T8]
        valid = validb_ref[...] == 0.0
        keys_ref[...] = _score_keys(jnp.where(valid, score, -jnp.inf))
        col = _lane_iota((T8, lk))
        sel = _topk_select(keys_ref, topk, jnp.full((T8, 1), True), col)
        selm = jnp.where(sel & valid, 0.0, NEG)
        sb = _dot(qb_ref[...], kb_ref[...]) + tabb_ref[...] + jnp.concatenate([selm] * B_HEADS, axis=0)
        mb = jnp.max(sb, axis=-1, keepdims=True)
        pb = jnp.exp(sb - mb)
        lb = jnp.sum(pb, axis=-1, keepdims=True)
        ob = _dot_t(_mx(pb), vb_ref[...]) / lb
        for h in range(B_HEADS):
            ob_ref[h] = ob[h * T8:(h + 1) * T8]


def _decode0(page_table, lamv, gsub, qa_bd, qi64, qb64, w8, kna, vna, bnew_t,
             cak_t, cav_r, cbk_t, cbv_t, cbi_t, taba, tabb, validb, layer, topk, lam_init):
    db, npg = page_table.shape
    ps = cak_t.shape[3]
    npp = PAGES_PER_STEP if npg % PAGES_PER_STEP == 0 else 1
    n_steps = npg // npp
    lk = tabb.shape[1]
    per_b = lambda b, p, pt: (b, 0, 0)
    per_b4 = lambda b, p, pt: (b, 0, 0, 0)
    fixed2 = lambda b, p, pt: (0, 0)
    fixed3 = lambda b, p, pt: (0, 0, 0)

    def page_specs(arr):
        blk = (None, None) + arr.shape[2:]
        return [pl.BlockSpec(blk, lambda b, p, pt, j=j: (layer, pt[b * npg + p * npp + j], 0, 0)) for j in range(npp)]

    caches = (cak_t, cav_r, cbk_t, cbv_t, cbi_t)
    kern = functools.partial(_decode0_kernel, n_steps=n_steps, npp=npp, ps=ps, topk=topk, lam_init=lam_init)
    grid_spec = pltpu.PrefetchScalarGridSpec(
        num_scalar_prefetch=1,
        grid=(db, n_steps),
        in_specs=[pl.BlockSpec((4, HEAD_DIM), fixed2), pl.BlockSpec((1, LANES), fixed2),
                  pl.BlockSpec((None,) + qa_bd.shape[1:], per_b),
                  pl.BlockSpec((None,) + qi64.shape[1:], per_b),
                  pl.BlockSpec((None,) + qb64.shape[1:], per_b),
                  pl.BlockSpec((None,) + w8.shape[1:], per_b),
                  pl.BlockSpec((None,) + kna.shape[1:], per_b),
                  pl.BlockSpec((None,) + vna.shape[1:], per_b),
                  pl.BlockSpec((None,) + bnew_t.shape[1:], per_b4),
                  pl.BlockSpec(taba.shape, fixed3), pl.BlockSpec(tabb.shape, fixed2),
                  pl.BlockSpec(validb.shape, fixed2)]
                 + [sp for c in caches for sp in page_specs(c)],
        out_specs=[pl.BlockSpec((None, A_HEADS, T8, LANES), lambda b, p, pt: (b, 0, 0, 0)),
                   pl.BlockSpec((None, B_HEADS, T8, HEAD_DIM), lambda b, p, pt: (b, 0, 0, 0))],
        scratch_shapes=[pltpu.VMEM((64, LANES), F32), pltpu.VMEM((64, LANES), F32),
                        pltpu.VMEM((64, LANES), F32),
                        pltpu.VMEM((HEAD_DIM, lk), MXU_DT), pltpu.VMEM((HEAD_DIM, lk), MXU_DT),
                        pltpu.VMEM((IDX_DIM, lk), MXU_DT), pltpu.VMEM((T8, lk), I32)],
    )
    return pl.pallas_call(
        kern,
        grid_spec=grid_spec,
        out_shape=[jax.ShapeDtypeStruct((db, A_HEADS, T8, LANES), F32),
                   jax.ShapeDtypeStruct((db, B_HEADS, T8, HEAD_DIM), F32)],
        compiler_params=_cparams(("parallel", "arbitrary")),
        name="decode0",
    )(page_table.reshape(-1), lamv, gsub, qa_bd, qi64, qb64, w8, kna, vna, bnew_t, taba, tabb, validb,
      *[c for c in caches for _ in range(npp)])


def _expand_rows(x):
    hh, ww = x.shape
    return jnp.broadcast_to(x[:, None, :], (hh, T8, ww)).reshape(hh * T8, ww)


def _decode1_kernel(*refs, n_steps, npp, ps):
    pt_ref, q_ref, cq_ref, kn_ref, vn_ref, cnew_ref, ckt_ref, maskn_ref = refs[:8]
    ck = refs[8:8 + npp]
    cv = refs[8 + npp:8 + 2 * npp]
    o_ref, m_ref, l_ref, acc_ref = refs[8 + 2 * npp:]
    step = pl.program_id(1)

    @pl.when(step == 0)
    def _():
        m_ref[...] = jnp.full(m_ref.shape, NEG, F32)
        l_ref[...] = jnp.zeros(l_ref.shape, F32)
        acc_ref[...] = jnp.zeros(acc_ref.shape, F32)

    q = q_ref[...]
    cq = jnp.broadcast_to(cq_ref[...], (q.shape[0], LANES))
    s = jnp.concatenate([_dot(q, _mx(ck[j][...])) for j in range(npp)], axis=1)
    s = s + _lanes(cq, npp * ps) - _expand_rows(ckt_ref[...])
    alpha, p = _online_update(s, m_ref, l_ref)
    p = _mx(p)
    upd = _lanes(alpha, acc_ref.shape[1]) * acc_ref[...]
    for j in range(npp):
        upd = upd + _dot_t(p[:, j * ps:(j + 1) * ps], _mx(cv[j][...]))
    acc_ref[...] = upd

    @pl.when(step == n_steps - 1)
    def _():
        nn = kn_ref.shape[0]
        s_new = (_dot_t(q, _mx(kn_ref[...])) + cq[:, :nn] - _expand_rows(cnew_ref[...]) + maskn_ref[...])
        alpha2, p2 = _online_update(s_new, m_ref, l_ref)
        o_full = ((_lanes(alpha2, acc_ref.shape[1]) * acc_ref[...] + _dot(_mx(p2), _mx(vn_ref[...])))
                  / _lanes(l_ref[...], acc_ref.shape[1]))
        for h in range(C_HEADS):
            o_ref[h] = o_full[h * T8:(h + 1) * T8, h * HEAD_DIM:(h + 1) * HEAD_DIM]


def _decode1(page_table, q_bd, cq, kn, vn, cnew_t, cck_t, ccv_t, ckt, maskn, layer):
    db, npg = page_table.shape
    ps = cck_t.shape[3]
    npp = PAGES_PER_STEP if npg % PAGES_PER_STEP == 0 else 1
    n_steps = npg // npp
    per_b = lambda b, p, pt: (b, 0, 0)

    def page_specs(arr):
        blk = (None, None) + arr.shape[2:]
        return [pl.BlockSpec(blk, lambda b, p, pt, j=j: (layer, pt[b * npg + p * npp + j], 0, 0)) for j in range(npp)]

    kern = functools.partial(_decode1_kernel, n_steps=n_steps, npp=npp, ps=ps)
    rows = C_HEADS * T8
    grid_spec = pltpu.PrefetchScalarGridSpec(
        num_scalar_prefetch=1,
        grid=(db, n_steps),
        in_specs=[pl.BlockSpec((None,) + q_bd.shape[1:], per_b),
                  pl.BlockSpec((None,) + cq.shape[1:], per_b),
                  pl.BlockSpec((None,) + kn.shape[1:], per_b),
                  pl.BlockSpec((None,) + vn.shape[1:], per_b),
                  pl.BlockSpec((None,) + cnew_t.shape[1:], per_b),
                  pl.BlockSpec((None, C_HEADS, npp * ps), lambda b, p, pt: (b, 0, p)),
                  pl.BlockSpec(maskn.shape, lambda b, p, pt: (0, 0))]
                 + page_specs(cck_t) + page_specs(ccv_t),
        out_specs=pl.BlockSpec((None, C_HEADS, T8, HEAD_DIM), lambda b, p, pt: (b, 0, 0, 0)),
        scratch_shapes=[pltpu.VMEM((rows, LANES), F32), pltpu.VMEM((rows, LANES), F32),
                        pltpu.VMEM((rows, C_HEADS * HEAD_DIM), F32)],
    )
    return pl.pallas_call(
        kern,
        grid_spec=grid_spec,
        out_shape=jax.ShapeDtypeStruct((db, C_HEADS, T8, HEAD_DIM), F32),
        compiler_params=_cparams(("parallel", "arbitrary")),
        name="decode1",
    )(page_table.reshape(-1), q_bd, cq, kn, vn, cnew_t, ckt, maskn, *([cck_t] * npp), *([ccv_t] * npp))


def _router_kernel(x_ref, g_ref, whi_ref, wlo_ref, b_ref, t_ref, route_ref):
    t = _rms_rows(x_ref[...], g_ref[...])
    thi, tlo = _split2(t)
    t_ref[...] = thi
    whi = whi_ref[...]
    logits = _dot(thi, whi) + _dot(tlo, whi) + _dot(thi, wlo_ref[...]) + b_ref[...]
    lane = _lane_iota(logits.shape)
    big = jnp.int32(1 << 20)

    def first_max(v):
        mx = jnp.max(v, axis=-1, keepdims=True)
        idx = jnp.min(jnp.where(v == mx, lane, big), axis=-1, keepdims=True)
        return mx, idx

    glog = jnp.where(lane < N_GROUPS, logits, -jnp.inf)
    gmax, gidx = first_max(glog)
    grp_w = 1.0 / jnp.sum(jnp.exp(glog - gmax), axis=-1, keepdims=True)
    el = lane - N_GROUPS
    in_grp = (el >= 0) & (el < N_EXPERTS) & (jnp.right_shift(el, 3) == gidx)
    v1 = jnp.where(in_grp, logits, -jnp.inf)
    top1, i1 = first_max(v1)
    v2 = jnp.where(lane == i1, -jnp.inf, v1)
    top2, i2 = first_max(v2)
    e2 = jnp.exp(top2 - top1)
    w1 = grp_w / (1.0 + e2)
    w2 = grp_w * e2 / (1.0 + e2)
    route = jnp.where(lane == 0, (i1 - N_GROUPS).astype(F32),
                      jnp.where(lane == 1, (i2 - N_GROUPS).astype(F32),
                                jnp.where(lane == 2, w1, jnp.where(lane == 3, w2, 0.0))))
    route_ref[...] = route


def _router(x, g, w_group, b_group, w_router, b_router, tm=512):
    n = x.shape[0]
    tm = _tile(n, tm)
    pad = LANES - N_GROUPS - N_EXPERTS
    w = jnp.concatenate([w_group, w_router, jnp.zeros((D_MODEL, pad), F32)], axis=1)
    whi = _mx(w)
    wlo = _mx(w - whi.astype(F32))
    b = jnp.concatenate([b_group, b_router, jnp.zeros((pad,), F32)])[None, :]
    row = lambda i: (i, 0)
    fixed = lambda i: (0, 0)
    return pl.pallas_call(
        _router_kernel,
        grid=(n // tm,),
        in_specs=[pl.BlockSpec((tm, D_MODEL), row), pl.BlockSpec((1, D_MODEL), fixed),
                  pl.BlockSpec((D_MODEL, LANES), fixed), pl.BlockSpec((D_MODEL, LANES), fixed),
                  pl.BlockSpec((1, LANES), fixed)],
        out_specs=[pl.BlockSpec((tm, D_MODEL), row), pl.BlockSpec((tm, LANES), row)],
        out_shape=[jax.ShapeDtypeStruct((n, D_MODEL), MXU_DT), jax.ShapeDtypeStruct((n, LANES), F32)],
        compiler_params=_cparams(("parallel",)),
        name="moe_router",
    )(x, g[None, :], whi, wlo, b)


def _moe_dense_kernel(x_ref, t_ref, route_ref, wg_ref, wu_ref, wd_ref, o_ref):
    e = pl.program_id(1)

    @pl.when(e == 0)
    def _():
        o_ref[...] = x_ref[...]

    t = t_ref[...]
    a = _dot(t, wg_ref[...])
    u = _dot(t, wu_ref[...])
    r = route_ref[...]
    ef = e.astype(F32)
    gate = jnp.where(r[:, 0:1] == ef, r[:, 2:3], 0.0) + jnp.where(r[:, 1:2] == ef, r[:, 3:4], 0.0)
    hdn = a * (1.0 / (1.0 + jnp.exp(-a))) * u * gate
    o_ref[...] += _dot(_mx(hdn), wd_ref[...])


def _moe_dense(x, t, route, w_gate, w_up, w_down, tm=512):
    n = x.shape[0]
    tm = _tile(n, tm)
    wg = _mx(w_gate.reshape(N_EXPERTS, D_MODEL, EXPERT_FF))
    wu = _mx(w_up.reshape(N_EXPERTS, D_MODEL, EXPERT_FF))
    wd = _mx(w_down.reshape(N_EXPERTS, EXPERT_FF, D_MODEL))
    row = lambda i, e: (i, 0)
    return pl.pallas_call(
        _moe_dense_kernel,
        grid=(n // tm, N_EXPERTS),
        in_specs=[pl.BlockSpec((tm, D_MODEL), row), pl.BlockSpec((tm, D_MODEL), row),
                  pl.BlockSpec((tm, LANES), row),
                  pl.BlockSpec((None, D_MODEL, EXPERT_FF), lambda i, e: (e, 0, 0)),
                  pl.BlockSpec((None, D_MODEL, EXPERT_FF), lambda i, e: (e, 0, 0)),
                  pl.BlockSpec((None, EXPERT_FF, D_MODEL), lambda i, e: (e, 0, 0))],
        out_specs=pl.BlockSpec((tm, D_MODEL), row),
        out_shape=jax.ShapeDtypeStruct((n, D_MODEL), F32),
        compiler_params=_cparams(("parallel", "arbitrary")),
        name="moe_dense",
    )(x, t, route, wg, wu, wd)


def _hier_moe(x, g, w_group, b_group, w_router, b_router, w_gate, w_up, w_down):
    t, route = _router(x, g, w_group, b_group, w_router, b_router)
    return _moe_dense(x, t, route, w_gate, w_up, w_down)


def _decode_tables(rel_bias, past, t_new, lk):
    t8 = np.arange(T8)
    lane = np.arange(LANES)
    far = np.full((T8, LANES), REL_BUCKETS - 1, np.int32)
    d_last = (past + t8[:, None]) - (past - LANES + lane[None, :])
    d_new = t8[:, None] - lane[None, :]
    ok_new = (d_new >= 0) & (lane[None, :] < t_new)
    ta = _bias_expand(rel_bias, np.stack([far, _t5_bucket_np(d_last), _t5_bucket_np(d_new)]), 0, A_HEADS)
    ta = jnp.where(jnp.asarray(ok_new)[None, None] | (jnp.arange(3) < 2)[None, :, None, None], ta, NEG)
    taba = jnp.broadcast_to(jnp.transpose(ta, (1, 0, 2, 3))[:, :, None], (3, A_HEADS, 2, T8, LANES))
    taba = taba.reshape(3, A_HEADS * 2 * T8, LANES)
    kpos = np.arange(lk)
    d_b = (past + t8[:, None]) - kpos[None, :]
    tabb = _bias_expand(rel_bias, _t5_bucket_np(d_b), A_HEADS, B_HEADS).reshape(B_HEADS * T8, lk)
    valid = (kpos[None, :] < past) | ((d_b >= 0) & (kpos[None, :] < past + t_new))
    validb = jnp.asarray(np.where(valid, 0.0, NEG), F32)
    return taba, tabb, validb


def _pad_rows(x, rows):
    pad = [(0, 0)] * x.ndim
    pad[1] = (0, rows - x.shape[1])
    return jnp.pad(x, pad)


def kernel(x_prompt, x_sample, cache_a_k, cache_a_v, cache_b_k, cache_b_v, cache_b_kidx, cache_c_k, cache_c_v, cache_c_logf, page_table, rel_bias, ab_norm, ab_w_in, a_q_norm, a_k_norm, b_q_norm, b_k_norm, a_lambda_q1, a_lambda_k1, a_lambda_q2, a_lambda_k2, a_sub_norm, ab_w_out, c_norm, c_w_in, c_forget_bias, c_q_norm, c_k_norm, c_w_out, ffn_norm, moe_w_group, moe_b_group, moe_w_router, moe_b_router, moe_w_gate, moe_w_up, moe_w_down):
    nb, s, d = x_prompt.shape
    db, ts, _ = x_sample.shape
    npg = page_table.shape[1]
    pool, ps = cache_a_k.shape[1], cache_a_k.shape[2]
    past = npg * ps
    n_p = nb * s
    n_s = db * ts
    depth = ffn_norm.shape[0]
    topk_p = min(IDX_TOPK_MAX, s // 4)
    topk_s = min(IDX_TOPK_MAX, (past + ts) // 4)
    lk = past + LANES

    cak_t = jnp.transpose(cache_a_k, (0, 1, 3, 4, 5, 2)).reshape(-1, pool, 2 * A_HEADS * HEAD_DIM, ps)
    cav_r = cache_a_v.reshape(-1, pool, ps * A_HEADS, 2 * HEAD_DIM)
    cbk_t = jnp.transpose(cache_b_k, (0, 1, 3, 2))
    cbv_t = jnp.transpose(cache_b_v, (0, 1, 3, 2))
    cbi_t = jnp.transpose(cache_b_kidx, (0, 1, 3, 2))
    cck_t = jnp.transpose(cache_c_k, (0, 1, 3, 4, 2)).reshape(-1, pool, C_HEADS * HEAD_DIM, ps)
    ccv_t = jnp.transpose(cache_c_v, (0, 1, 3, 4, 2)).reshape(-1, pool, C_HEADS * HEAD_DIM, ps)
    ccf_t = jnp.transpose(cache_c_logf, (0, 1, 3, 2))

    x = jnp.concatenate([x_prompt.reshape(n_p, d), x_sample.reshape(n_s, d)], axis=0)
    outs_p = {k: [] for k in ("ak", "av", "bk", "bv", "bi", "ck", "cv", "cf")}
    outs_s = {k: [] for k in ("ak", "av", "bk", "bv", "bi", "ck", "cv", "cf")}

    def smp(a):
        return a[n_p:].reshape(db, ts, a.shape[1])

    for layer in range(depth):
        if layer % 2 == 0:
            e = layer // 2
            lam_init = 0.8 - 0.6 * math.exp(-0.3 * layer)
            lamv = jnp.stack([a_lambda_q1[e], a_lambda_k1[e], a_lambda_q2[e], a_lambda_k2[e]])
            gsub = a_sub_norm[e][None, :]
            (qa, ka, kabf, qb, small, kk, vv, kiki, va, vabf, qi) = _proj0(
                x, ab_norm[e], ab_w_in[e], a_q_norm[e], a_k_norm[e], b_q_norm[e], b_k_norm[e])
            kb, vb, ki = small[:, 0:64], small[:, 64:128], small[:, 128:192]
            tq_a = 256
            bias_a = _bias_expand(rel_bias, _prompt_bucket_tiles(tq_a), 0, A_HEADS)
            oa_p = _attn_a_prompt(qa, kabf, vabf, lamv, gsub, bias_a, nb, s, lam_init, tq=tq_a)
            tq_b = 128
            bias_b = _bias_expand(rel_bias, _prompt_bucket_tiles(tq_b), A_HEADS, B_HEADS)
            c31 = bias_b[:, 2:3, 0:1, 0:1].reshape(B_HEADS, 1, 1)
            bias_bd = bias_b[:, 0:2] - c31[:, :, :, None]
            ob_p = _attn_b_prompt(qb, qi, small, kk, vv, kiki, bias_bd, c31, nb, s, topk_p, tq=tq_b)
            qa_s = _pad_rows(smp(qa), T8)
            hc = np.arange(2 * A_HEADS)
            colmask = jnp.asarray((np.arange(qa_s.shape[2])[None, :] // HEAD_DIM) == hc[:, None], MXU_DT)
            qa_bd = (qa_s[:, None, :, :] * colmask[None, :, None, :]).reshape(db, 2 * A_HEADS * T8, -1)

            def heads_rows(a, nh):
                a = _pad_rows(a, T8).reshape(db, T8, nh, HEAD_DIM)
                return jnp.transpose(a, (0, 2, 1, 3)).reshape(db, nh * T8, HEAD_DIM)

            qi64 = heads_rows(smp(qi), IDX_HEADS)
            qb64 = heads_rows(smp(qb), B_HEADS)
            small_s = smp(small)
            w8 = _pad_rows(small_s[:, :, 192:200], T8)
            kna = _pad_rows(smp(ka), 16)
            vna = _pad_rows(smp(va), 16)
            bnew_t = jnp.transpose(_pad_rows(small_s[:, :, :192], LANES).reshape(db, LANES, 3, HEAD_DIM), (0, 2, 3, 1))
            taba, tabb, validb = _decode_tables(rel_bias, past, ts, lk)
            oa_d, ob_d = _decode0(page_table, lamv, gsub, qa_bd, qi64, qb64, w8, kna, vna, bnew_t,
                                  cak_t, cav_r, cbk_t, cbv_t, cbi_t, taba, tabb, validb, e, topk_s, lam_init)
            oa_s = jnp.transpose(oa_d[:, :, :ts], (0, 2, 1, 3)).reshape(n_s, A_HEADS * LANES)
            ob_s = jnp.transpose(ob_d[:, :, :ts], (0, 2, 1, 3)).reshape(n_s, B_HEADS * HEAD_DIM)
            oa = jnp.concatenate([oa_p, _mx(oa_s)], axis=0)
            ob = jnp.concatenate([ob_p, _mx(ob_s)], axis=0)
            w_out = ab_w_out[e]
            x = _outproj(x, [(oa, w_out[:A_HEADS * LANES]), (ob, w_out[A_HEADS * LANES:])])
            for dst, lo, n_, shp in ((outs_p, 0, n_p, (nb, s)), (outs_s, n_p, n_s, (db, ts))):
                dst["ak"].append(ka[lo:lo + n_].reshape(*shp, A_HEADS, 2, HEAD_DIM))
                dst["av"].append(va[lo:lo + n_].reshape(*shp, A_HEADS, 2 * HEAD_DIM))
                dst["bk"].append(kb[lo:lo + n_].reshape(*shp, HEAD_DIM))
                dst["bv"].append(vb[lo:lo + n_].reshape(*shp, HEAD_DIM))
                dst["bi"].append(ki[lo:lo + n_].reshape(*shp, IDX_DIM))
        else:
            o = layer // 2
            q, k, kbf, v, vbf, logf128 = _proj1(x, c_norm[o], c_w_in[o], c_forget_bias[o], c_q_norm[o], c_k_norm[o])
            logf = logf128[:, :C_HEADS]
            cum_p = _cumsum_prompt(logf128[:n_p], nb, s)[:, :C_HEADS]
            ckt = jnp.transpose(cum_p.reshape(nb, s, C_HEADS // 2, 2), (0, 2, 3, 1))
            oc_p = _attn_c_prompt(q, kbf, vbf, cum_p, ckt, nb, s)
            ckt_s = _cumsum_pages(ccf_t, o, page_table)
            logf_s = smp(logf)
            c_new = ckt_s[:, :, -1][:, None, :] + jnp.cumsum(logf_s, axis=1)
            cq = jnp.transpose(_pad_rows(c_new, T8), (0, 2, 1)).reshape(db, C_HEADS * T8, 1)
            cnew_t = jnp.transpose(_pad_rows(c_new, 16), (0, 2, 1))
            q_s = _pad_rows(smp(q), T8)
            hmask = jnp.asarray((np.arange(q_s.shape[2])[None, :] // HEAD_DIM) == np.arange(C_HEADS)[:, None], MXU_DT)
            q_bd = (q_s[:, None, :, :] * hmask[None, :, None, :]).reshape(db, C_HEADS * T8, -1)
            kn = _pad_rows(smp(k), 16)
            vn = _pad_rows(smp(v), 16)
            t8 = np.arange(T8)
            okn = (t8[:, None] >= np.arange(16)[None, :]) & (np.arange(16)[None, :] < ts)
            maskn = jnp.asarray(np.tile(np.where(okn, 0.0, NEG), (C_HEADS, 1)), F32)
            oc_d = _decode1(page_table, q_bd, cq, kn, vn, cnew_t, cck_t, ccv_t, ckt_s, maskn, o)
            oc_s = jnp.transpose(oc_d[:, :, :ts], (0, 2, 1, 3)).reshape(n_s, C_HEADS * HEAD_DIM)
            oc = jnp.concatenate([oc_p, _mx(oc_s)], axis=0)
            x = _outproj(x, [(oc, c_w_out[o])])
            for dst, lo, n_, shp in ((outs_p, 0, n_p, (nb, s)), (outs_s, n_p, n_s, (db, ts))):
                dst["ck"].append(k[lo:lo + n_].reshape(*shp, C_HEADS, HEAD_DIM))
                dst["cv"].append(v[lo:lo + n_].reshape(*shp, C_HEADS, HEAD_DIM))
                dst["cf"].append(logf[lo:lo + n_].reshape(*shp, C_HEADS))
        x = _hier_moe(x, ffn_norm[layer], moe_w_group[layer], moe_b_group[layer], moe_w_router[layer],
                      moe_b_router[layer], moe_w_gate[layer], moe_w_up[layer], moe_w_down[layer])

    keys = ("ak", "av", "bk", "bv", "bi", "ck", "cv", "cf")
    return ((x[:n_p].reshape(nb, s, d), x[n_p:].reshape(db, ts, d))
            + tuple(jnp.stack(outs_p[k]) for k in keys)
            + tuple(jnp.stack(outs_s[k]) for k in keys))
```

```python
import functools
import math

import numpy as np
import jax
import jax.numpy as jnp
from jax import lax
from jax.experimental import pallas as pl
from jax.experimental.pallas import tpu as pltpu

F32 = jnp.float32
I32 = jnp.int32
MXU_DT = jnp.bfloat16

D_MODEL = 1024
HEAD_DIM = 64
A_HEADS = 4
B_HEADS = 8
IDX_HEADS = 8
IDX_DIM = 64
IDX_TOPK_MAX = 256
C_HEADS = 16
REL_BUCKETS = 32
REL_MAX_DIST = 128
N_GROUPS = 4
EXPERTS_PER_GROUP = 8
N_EXPERTS = N_GROUPS * EXPERTS_PER_GROUP
EXPERT_FF = 256
EPS = 1e-6
NEG = -1e30
INT_MIN = -2 ** 31
QK_SCALE = HEAD_DIM ** -0.5
IDX_SCALE = IDX_DIM ** -0.5
IDX_HEAD_SCALE = IDX_HEADS ** -0.5
LANES = 128
T8 = 8
VMEM_LIMIT = 56 * 1024 * 1024
PAGES_PER_STEP = 4


def _tile(n, pref):
    best = 16
    for t in range(16, pref + 1, 16):
        if n % t == 0:
            best = t
    assert n % best == 0, (n, pref)
    return best


def _cparams(sem):
    return pltpu.CompilerParams(dimension_semantics=sem, vmem_limit_bytes=VMEM_LIMIT)


def _mx(x):
    return x.astype(MXU_DT)


def _dot(a, b):
    return jnp.dot(a, b, preferred_element_type=F32)


def _dot_t(a, b):
    return lax.dot_general(a, b, (((1,), (1,)), ((), ())), preferred_element_type=F32)


def _split2(x):
    hi = _mx(x)
    lo = _mx(x - hi.astype(F32))
    return hi, lo


def _split3(x):
    hi = _mx(x)
    r = x - hi.astype(F32)
    mid = _mx(r)
    lo = _mx(r - mid.astype(F32))
    return hi, mid, lo


def _lane_iota(shape):
    return lax.broadcasted_iota(I32, shape, len(shape) - 1)


def _row_iota(shape):
    return lax.broadcasted_iota(I32, shape, len(shape) - 2)


def _lanes(x, width):
    if width <= LANES:
        return x[:, :width]
    return jnp.tile(x, (1, width // LANES))


def _rms_rows(x, g):
    ms = jnp.mean(x * x, axis=-1, keepdims=True)
    return x * lax.rsqrt(ms + EPS) * g


def _seg_rsqrt(y, bd):
    hi, lo = _split2(y * y)
    ss = _dot(hi, bd) + _dot(lo, bd)
    return lax.rsqrt(ss * (1.0 / HEAD_DIM) + EPS)


def _block_diag_ones(n, seg):
    r = np.arange(n)
    return jnp.asarray((r[:, None] // seg) == (r[None, :] // seg), dtype=MXU_DT)


def _lambda(lamv_ref, lam_init):
    lv = lamv_ref[...]
    return (jnp.exp(jnp.sum(lv[0:1] * lv[1:2], axis=-1, keepdims=True))
            - jnp.exp(jnp.sum(lv[2:3] * lv[3:4], axis=-1, keepdims=True)) + lam_init)


def _t5_bucket_np(d):
    n = np.maximum(d, 0)
    exact = REL_BUCKETS // 2
    nf = np.maximum(n, 1).astype(np.float64)
    large = exact + (np.log(nf / exact) / math.log(REL_MAX_DIST / exact) * (REL_BUCKETS - exact)).astype(np.int64)
    large = np.minimum(large, REL_BUCKETS - 1)
    return np.where(n < exact, n, large).astype(np.int32)


def _bias_expand_kernel(relb_ref, idx_ref, o_ref, *, head0):
    h = pl.program_id(0) + head0
    idx = idx_ref[...]
    acc = jnp.zeros(idx.shape, F32)
    for b in range(REL_BUCKETS):
        acc = jnp.where(idx == b, relb_ref[b, h], acc)
    o_ref[...] = acc


def _bias_expand(rel_bias, idx_np, head0, n_heads):
    idx = jnp.asarray(idx_np, I32)
    nd = idx.ndim
    zeros = (0,) * nd
    return pl.pallas_call(
        functools.partial(_bias_expand_kernel, head0=head0),
        grid=(n_heads,),
        in_specs=[pl.BlockSpec(memory_space=pltpu.SMEM),
                  pl.BlockSpec(idx.shape, lambda h: zeros)],
        out_specs=pl.BlockSpec((None,) + idx.shape, lambda h: (h,) + zeros),
        out_shape=jax.ShapeDtypeStruct((n_heads,) + idx.shape, F32),
        compiler_params=_cparams(("arbitrary",)),
        name="bias_expand",
    )(rel_bias, idx)


def _prompt_bucket_tiles(t):
    r = np.arange(t)
    d = r[:, None] - r[None, :]
    return np.stack([_t5_bucket_np(d), _t5_bucket_np(d + t), np.full((t, t), REL_BUCKETS - 1, np.int32)])


def _proj0_kernel(x_ref, g_ref, w_ref, gain_ref, bd_ref,
                  qa_ref, ka_ref, kabf_ref, qb_ref, small_ref, kk_ref, vv_ref, kiki_ref,
                  va_ref, vabf_ref, qi_ref):
    h = _mx(_rms_rows(x_ref[...], g_ref[...]))
    bd = bd_ref[...]

    def chunk(c):
        return _dot(h, w_ref[:, c * 256:(c + 1) * 256])

    def normed(c):
        y = chunk(c)
        return y * _seg_rsqrt(y, bd) * gain_ref[:, c * 256:(c + 1) * 256]

    for c in range(2):
        qa_ref[:, c * 256:(c + 1) * 256] = _mx(normed(c) * QK_SCALE)
    for c in range(2):
        y = normed(2 + c)
        ka_ref[:, c * 256:(c + 1) * 256] = y
        kabf_ref[:, c * 256:(c + 1) * 256] = _mx(y)
    for c in range(2):
        qb_ref[:, c * 256:(c + 1) * 256] = _mx(normed(4 + c) * QK_SCALE)
    y = chunk(6)
    yn = y * _seg_rsqrt(y, bd) * gain_ref[:, 6 * 256:7 * 256]
    lane = _lane_iota(y.shape)
    y = jnp.where(lane < HEAD_DIM, yn, y)
    small_ref[...] = y
    t0 = y[:, :LANES]
    t1 = y[:, LANES:]
    lo = _lane_iota(t0.shape) < HEAD_DIM
    r0 = pltpu.roll(t0, HEAD_DIM, 1)
    kk_ref[...] = _mx(jnp.where(lo, t0, r0))
    vv_ref[...] = _mx(jnp.where(lo, r0, t0))
    r1 = pltpu.roll(t1, HEAD_DIM, 1)
    kiki_ref[...] = _mx(jnp.where(lo, t1, r1))
    for c in range(2):
        y = chunk(7 + c)
        va_ref[:, c * 256:(c + 1) * 256] = y
        vabf_ref[:, c * 256:(c + 1) * 256] = _mx(y)
    for c in range(2):
        qi_ref[:, c * 256:(c + 1) * 256] = _mx(chunk(9 + c) * IDX_SCALE)


def _proj0(x, g, w_in, a_qn, a_kn, b_qn, b_kn, tm=256):
    n = x.shape[0]
    tm = _tile(n, tm)
    sp = np.cumsum([512, 512, 512, 512, 64, 64, 512, 64, 8])[:-1]
    wqa, wka, wva, wqb, wkb, wvb, wqi, wki, wwi = jnp.split(w_in, sp, axis=1)
    w = jnp.concatenate([wqa, wka, wqb, wkb, wvb, wki, wwi, jnp.zeros((D_MODEL, 56), F32), wva, wqi], axis=1)
    w = _mx(w)
    ncol = w.shape[1]
    gain = jnp.concatenate([jnp.tile(a_qn, 8), jnp.tile(a_kn, 8), jnp.tile(b_qn, 8), b_kn,
                            jnp.ones((192,), F32)])[None, :]
    bd = _block_diag_ones(256, HEAD_DIM)
    row = lambda i: (i, 0)
    fixed = lambda i: (0, 0)
    widths = [(512, MXU_DT), (512, F32), (512, MXU_DT), (512, MXU_DT), (256, F32), (128, MXU_DT),
              (128, MXU_DT), (128, MXU_DT), (512, F32), (512, MXU_DT), (512, MXU_DT)]
    return pl.pallas_call(
        _proj0_kernel,
        grid=(n // tm,),
        in_specs=[pl.BlockSpec((tm, D_MODEL), row), pl.BlockSpec((1, D_MODEL), fixed),
                  pl.BlockSpec((D_MODEL, ncol), fixed), pl.BlockSpec((1, gain.shape[1]), fixed),
                  pl.BlockSpec((256, 256), fixed)],
        out_specs=[pl.BlockSpec((tm, wd), row) for wd, _ in widths],
        out_shape=[jax.ShapeDtypeStruct((n, wd), dt) for wd, dt in widths],
        compiler_params=_cparams(("parallel",)),
        name="proj0",
    )(x, g[None, :], w, gain, bd)


def _proj1_kernel(x_ref, g_ref, w_ref, gain_ref, bf_ref, bd_ref,
                  q_ref, k_ref, kbf_ref, v_ref, vbf_ref, logf_ref):
    h = _mx(_rms_rows(x_ref[...], g_ref[...]))
    bd = bd_ref[...]

    def chunk(c):
        return _dot(h, w_ref[:, c * 256:(c + 1) * 256])

    def normed(c):
        y = chunk(c)
        return y * _seg_rsqrt(y, bd) * gain_ref[:, c * 256:(c + 1) * 256]

    for c in range(4):
        q_ref[:, c * 256:(c + 1) * 256] = _mx(normed(c) * QK_SCALE)
    for c in range(4):
        y = normed(4 + c)
        k_ref[:, c * 256:(c + 1) * 256] = y
        kbf_ref[:, c * 256:(c + 1) * 256] = _mx(y)
    for c in range(4):
        y = chunk(8 + c)
        v_ref[:, c * 256:(c + 1) * 256] = y
        vbf_ref[:, c * 256:(c + 1) * 256] = _mx(y)
    f = _dot(h, w_ref[:, 12 * 256:12 * 256 + LANES]) + bf_ref[...]
    logf_ref[...] = jnp.minimum(f, 0.0) - jnp.log(1.0 + jnp.exp(-jnp.abs(f)))


def _proj1(x, g, w_in, b_f, qn, kn, tm=256):
    n = x.shape[0]
    tm = _tile(n, tm)
    w = _mx(jnp.concatenate([w_in, jnp.zeros((D_MODEL, LANES - C_HEADS), F32)], axis=1))
    ncol = w.shape[1]
    gain = jnp.concatenate([jnp.tile(qn, C_HEADS), jnp.tile(kn, C_HEADS)])[None, :]
    bf = jnp.concatenate([b_f, jnp.zeros((LANES - C_HEADS,), F32)])[None, :]
    bd = _block_diag_ones(256, HEAD_DIM)
    row = lambda i: (i, 0)
    fixed = lambda i: (0, 0)
    widths = [(1024, MXU_DT), (1024, F32), (1024, MXU_DT), (1024, F32), (1024, MXU_DT), (LANES, F32)]
    return pl.pallas_call(
        _proj1_kernel,
        grid=(n // tm,),
        in_specs=[pl.BlockSpec((tm, D_MODEL), row), pl.BlockSpec((1, D_MODEL), fixed),
                  pl.BlockSpec((D_MODEL, ncol), fixed), pl.BlockSpec((1, gain.shape[1]), fixed),
                  pl.BlockSpec((1, LANES), fixed), pl.BlockSpec((256, 256), fixed)],
        out_specs=[pl.BlockSpec((tm, wd), row) for wd, _ in widths],
        out_shape=[jax.ShapeDtypeStruct((n, wd), dt) for wd, dt in widths],
        compiler_params=_cparams(("parallel",)),
        name="proj1",
    )(x, g[None, :], w, gain, bf, bd)


def _outproj_kernel(*refs, n_in):
    res_ref = refs[0]
    out_ref = refs[-1]
    acc = res_ref[...]
    for i in range(n_in):
        acc = acc + _dot(refs[1 + 2 * i][...], refs[2 + 2 * i][...])
    out_ref[...] = acc


def _outproj(res, pairs, tm=512):
    n = res.shape[0]
    tm = _tile(n, tm)
    row = lambda i: (i, 0)
    fixed = lambda i: (0, 0)
    in_specs = [pl.BlockSpec((tm, D_MODEL), row)]
    args = [res]
    for a, w in pairs:
        in_specs += [pl.BlockSpec((tm, a.shape[1]), row), pl.BlockSpec(w.shape, fixed)]
        args += [a, _mx(w)]
    return pl.pallas_call(
        functools.partial(_outproj_kernel, n_in=len(pairs)),
        grid=(n // tm,),
        in_specs=in_specs,
        out_specs=pl.BlockSpec((tm, D_MODEL), row),
        out_shape=jax.ShapeDtypeStruct((n, D_MODEL), F32),
        compiler_params=_cparams(("parallel",)),
        name="outproj",
    )(*args)


def _stack_streams(q):
    lane = _lane_iota(q.shape)
    zero = jnp.zeros_like(q)
    return jnp.concatenate([jnp.where(lane < HEAD_DIM, q, zero), jnp.where(lane >= HEAD_DIM, q, zero)], axis=0)


def _flash_stacked(q2, k_ref, v_ref, i, tq, tk, bias_fn, m_ref, l_ref, acc_ref):
    rows = 2 * tq
    m_ref[...] = jnp.full(m_ref.shape, NEG, F32)
    l_ref[...] = jnp.zeros(l_ref.shape, F32)
    acc_ref[...] = jnp.zeros(acc_ref.shape, F32)
    r = _row_iota((rows, tk))
    row = jnp.where(r >= tq, r - tq, r) + i * tq
    col = _lane_iota((rows, tk))

    def body(j, carry):
        off = pl.multiple_of(j * tk, tk)
        k = k_ref[pl.ds(off, tk), :]
        v = v_ref[pl.ds(off, tk), :]
        s = _dot_t(q2, k) + bias_fn(j, off)
        s = jnp.where((col + j * tk) <= row, s, NEG)
        m_old = m_ref[...]
        m_new = jnp.maximum(m_old, jnp.max(s, axis=-1, keepdims=True))
        alpha = jnp.exp(m_old - m_new)
        p = jnp.exp(s - _lanes(m_new, tk))
        l_ref[...] = alpha * l_ref[...] + jnp.sum(p, axis=-1, keepdims=True)
        acc_ref[...] = alpha * acc_ref[...] + _dot(_mx(p), v)
        m_ref[...] = m_new
        return carry

    lax.fori_loop(0, i + 1, body, 0)


def _attn_a_kernel(lamv_ref, gsub_ref, q_ref, k_ref, v_ref, bias_ref, o_ref,
                   m_ref, l_ref, acc_ref, *, tq, tk, lam_init):
    i = pl.program_id(2)

    def bias_fn(j, off):
        b = bias_ref[jnp.minimum(i - j, 2)]
        return jnp.concatenate([b, b], axis=0)

    _flash_stacked(_stack_streams(q_ref[...]), k_ref, v_ref, i, tq, tk, bias_fn, m_ref, l_ref, acc_ref)
    lam = _lambda(lamv_ref, lam_init)
    o = acc_ref[:tq] / l_ref[:tq] - lam * (acc_ref[tq:] / l_ref[tq:])
    o = _rms_rows(o, gsub_ref[...]) * (1.0 - lam_init)
    o_ref[...] = _mx(o)


def _attn_a_prompt(qa, ka, va, lamv, gsub, bias_tiles, nb, s, lam_init, tq=256):
    tk = tq
    nq = s // tq
    kern = functools.partial(_attn_a_kernel, tq=tq, tk=tk, lam_init=lam_init)
    return pl.pallas_call(
        kern,
        grid=(nb, A_HEADS, nq),
        in_specs=[pl.BlockSpec((4, HEAD_DIM), lambda b, h, i: (0, 0)),
                  pl.BlockSpec((1, LANES), lambda b, h, i: (0, 0)),
                  pl.BlockSpec((tq, LANES), lambda b, h, i: (b * nq + i, h)),
                  pl.BlockSpec((s, LANES), lambda b, h, i: (b, h)),
                  pl.BlockSpec((s, LANES), lambda b, h, i: (b, h)),
                  pl.BlockSpec((None, 3, tq, tk), lambda b, h, i: (h, 0, 0, 0))],
        out_specs=pl.BlockSpec((tq, LANES), lambda b, h, i: (b * nq + i, h)),
        out_shape=jax.ShapeDtypeStruct((nb * s, A_HEADS * LANES), MXU_DT),
        scratch_shapes=[pltpu.VMEM((2 * tq, LANES), F32), pltpu.VMEM((2 * tq, LANES), F32),
                        pltpu.VMEM((2 * tq, LANES), F32)],
        compiler_params=_cparams(("parallel", "parallel", "arbitrary")),
        name="attn_a_prompt",
    )(lamv, gsub, qa, ka, va, bias_tiles)


def _attn_c_kernel(q_ref, k_ref, v_ref, cq_ref, ckt_ref, o_ref, m_ref, l_ref, acc_ref, *, tq, tk):
    hp = pl.program_id(1)
    i = pl.program_id(2)
    cq_tile = cq_ref[...]
    lane16 = _lane_iota(cq_tile.shape)
    cq2 = jnp.concatenate(
        [jnp.broadcast_to(jnp.sum(jnp.where(lane16 == 2 * hp + c, cq_tile, 0.0), axis=-1, keepdims=True),
                          (tq, LANES)) for c in range(2)], axis=0)

    def bias_fn(j, off):
        ck = ckt_ref[:, pl.ds(off, tk)]
        ck2 = jnp.concatenate([jnp.broadcast_to(ck[0:1], (tq, tk)), jnp.broadcast_to(ck[1:2], (tq, tk))], axis=0)
        return _lanes(cq2, tk) - ck2

    _flash_stacked(_stack_streams(q_ref[...]), k_ref, v_ref, i, tq, tk, bias_fn, m_ref, l_ref, acc_ref)
    lane = _lane_iota((tq, LANES))
    o = jnp.where(lane < HEAD_DIM, acc_ref[:tq] / l_ref[:tq], acc_ref[tq:] / l_ref[tq:])
    o_ref[...] = _mx(o)


def _attn_c_prompt(q, k, v, cq, ckt, nb, s, tq=256):
    tk = tq
    nq = s // tq
    nhp = C_HEADS // 2
    kern = functools.partial(_attn_c_kernel, tq=tq, tk=tk)
    return pl.pallas_call(
        kern,
        grid=(nb, nhp, nq),
        in_specs=[pl.BlockSpec((tq, LANES), lambda b, h, i: (b * nq + i, h)),
                  pl.BlockSpec((s, LANES), lambda b, h, i: (b, h)),
                  pl.BlockSpec((s, LANES), lambda b, h, i: (b, h)),
                  pl.BlockSpec((tq, C_HEADS), lambda b, h, i: (b * nq + i, 0)),
                  pl.BlockSpec((None, None, 2, s), lambda b, h, i: (b, h, 0, 0))],
        out_specs=pl.BlockSpec((tq, LANES), lambda b, h, i: (b * nq + i, h)),
        out_shape=jax.ShapeDtypeStruct((nb * s, C_HEADS * HEAD_DIM), MXU_DT),
        scratch_shapes=[pltpu.VMEM((2 * tq, LANES), F32), pltpu.VMEM((2 * tq, LANES), F32),
                        pltpu.VMEM((2 * tq, LANES), F32)],
        compiler_params=_cparams(("parallel", "parallel", "arbitrary")),
        name="attn_c_prompt",
    )(q, k, v, cq, ckt)


def _score_keys(score):
    score = jnp.where(score == 0.0, 0.0, score)
    bits = pltpu.bitcast(score, I32)
    return bits ^ (jnp.right_shift(bits, 31) & 0x7FFFFFFF)


def _topk_select(keys_ref, width, kcount, active, col):
    kf = float(kcount)
    nbits_col = int(width - 1).bit_length()

    def count(pred):
        return jnp.sum(jnp.where(pred, 1.0, 0.0), axis=1, keepdims=True)

    t0 = jnp.where(count(keys_ref[:, :width] >= 0) >= kf, 0, INT_MIN).astype(I32)

    def body(it, t):
        cand = t + jnp.left_shift(jnp.int32(1), 30 - it)
        return jnp.where(count(keys_ref[:, :width] >= cand) >= kf, cand, t)

    t = lax.fori_loop(0, 31, body, t0)
    t = jnp.where(active, t, INT_MIN)
    keys = keys_ref[:, :width]
    gt = keys > t
    eq = keys == t
    need = kf - count(gt)
    excess = jnp.where(active, count(eq) - need, 0.0)

    def tie_break():
        def tb(it, jj):
            cand = jj + jnp.left_shift(jnp.int32(1), nbits_col - 1 - it)
            c = count((keys_ref[:, :width] == t) & (col < cand))
            return jnp.where(c < need, cand, jj)
        return lax.fori_loop(0, nbits_col, tb, jnp.zeros(t.shape, I32))

    jmax = lax.cond(jnp.max(excess) > 0.0, tie_break, lambda: jnp.full(t.shape, width, I32))
    return gt | (eq & (col <= jmax))


KEY_OF_NEG_INF = -2139095041
SELECT_UNIT = 2


def _attn_b_kernel(qb_ref, qi_ref, w_ref, kk_ref, vv_ref, kiki_ref, bias_ref, o_ref,
                   keys_ref, selm_ref, m_ref, l_ref, acc_ref, *, tq, s_len, topk):
    i = pl.program_id(1)
    cw = tq
    nh = B_HEADS
    n_chunks = s_len // cw
    lane = _lane_iota((tq, LANES))
    halves = (lane < HEAD_DIM, lane >= HEAD_DIM)

    def stack_heads(ref):
        parts = []
        for h in range(nh):
            t = ref[:, (h // 2) * LANES:(h // 2 + 1) * LANES]
            parts.append(jnp.where(halves[h % 2], t, jnp.zeros_like(t)))
        return jnp.concatenate(parts, axis=0)

    row = _row_iota((tq, cw)) + i * tq
    colc = _lane_iota((tq, cw))

    qi8 = stack_heads(qi_ref)
    wt = w_ref[...] * IDX_HEAD_SCALE
    wcol = jnp.concatenate([jnp.broadcast_to(wt[:, HEAD_DIM + h:HEAD_DIM + h + 1], (tq, LANES))
                            for h in range(nh)], axis=0)

    def score_chunk(j, carry):
        off = pl.multiple_of(j * cw, cw)
        d = jnp.maximum(_dot_t(qi8, kiki_ref[pl.ds(off, cw), :]), 0.0) * _lanes(wcol, cw)
        sc = d[0:tq]
        for h in range(1, nh):
            sc = sc + d[h * tq:(h + 1) * tq]
        keys_ref[:, pl.ds(off, cw)] = _score_keys(jnp.where((colc + j * cw) <= row, sc, -jnp.inf))
        return carry

    lax.fori_loop(0, i + 1, score_chunk, 0)
    unit = SELECT_UNIT if n_chunks % SELECT_UNIT == 0 else 1
    rem = (i + 1) % unit
    for u in range(1, unit):
        @pl.when((rem != 0) & (u <= unit - rem))
        def _():
            off = pl.multiple_of((i + u) * cw, cw)
            keys_ref[:, pl.ds(off, cw)] = jnp.full((tq, cw), KEY_OF_NEG_INF, I32)
    widths = [w * unit * cw for w in range(1, n_chunks // unit + 1)]

    qpos = _row_iota((tq, 1)) + i * tq

    def select_branch(width):
        def br():
            colw = _lane_iota((tq, width))
            causal = colw <= (_row_iota((tq, width)) + i * tq)
            sel = _topk_select(keys_ref, width, topk, qpos >= topk, colw)
            selm_ref[:, :width] = jnp.where(sel & causal, 0.0, NEG)
        return br

    def causal_only():
        col = _lane_iota((tq, s_len))
        selm_ref[...] = jnp.where(col <= (_row_iota((tq, s_len)) + i * tq), 0.0, NEG)

    branch = jnp.where((i + 1) * tq > topk, 1 + i // unit, 0)
    lax.switch(branch, [causal_only] + [select_branch(w) for w in widths])

    qb8 = stack_heads(qb_ref)
    m_ref[...] = jnp.full(m_ref.shape, NEG, F32)
    l_ref[...] = jnp.zeros(l_ref.shape, F32)
    acc_ref[...] = jnp.zeros(acc_ref.shape, F32)

    def attn_chunk(j, carry):
        off = pl.multiple_of(j * cw, cw)
        bidx = jnp.minimum(i - j, 2)
        bias = jnp.concatenate([bias_ref[h, bidx] for h in range(nh)], axis=0)
        selm = selm_ref[:, pl.ds(off, cw)]
        s = _dot_t(qb8, kk_ref[pl.ds(off, cw), :]) + bias + jnp.concatenate([selm] * nh, axis=0)
        alpha, p = _online_update(s, m_ref, l_ref)
        acc_ref[...] = alpha * acc_ref[...] + _dot(_mx(p), vv_ref[pl.ds(off, cw), :])
        return carry

    lax.fori_loop(0, i + 1, attn_chunk, 0)
    o = acc_ref[...] / l_ref[...]
    for hp in range(nh // 2):
        o_ref[:, hp * LANES:(hp + 1) * LANES] = _mx(
            jnp.where(halves[0], o[2 * hp * tq:(2 * hp + 1) * tq], o[(2 * hp + 1) * tq:(2 * hp + 2) * tq]))


def _attn_b_prompt(qb, qi, small, kk, vv, kiki, bias_tiles, nb, s, topk, tq=256):
    nq = s // tq
    kern = functools.partial(_attn_b_kernel, tq=tq, s_len=s, topk=topk)
    qrow = lambda b, i: (b * nq + i, 0)
    kv = lambda b, i: (b, 0)
    rows = B_HEADS * tq
    return pl.pallas_call(
        kern,
        grid=(nb, nq),
        in_specs=[pl.BlockSpec((tq, B_HEADS * HEAD_DIM), qrow),
                  pl.BlockSpec((tq, IDX_HEADS * IDX_DIM), qrow),
                  pl.BlockSpec((tq, LANES), lambda b, i: (b * nq + i, 1)),
                  pl.BlockSpec((s, LANES), kv), pl.BlockSpec((s, LANES), kv), pl.BlockSpec((s, LANES), kv),
                  pl.BlockSpec(bias_tiles.shape, lambda b, i: (0, 0, 0, 0))],
        out_specs=pl.BlockSpec((tq, B_HEADS * HEAD_DIM), qrow),
        out_shape=jax.ShapeDtypeStruct((nb * s, B_HEADS * HEAD_DIM), MXU_DT),
        scratch_shapes=[pltpu.VMEM((tq, s), I32), pltpu.VMEM((tq, s), F32),
                        pltpu.VMEM((rows, LANES), F32), pltpu.VMEM((rows, LANES), F32),
                        pltpu.VMEM((rows, LANES), F32)],
        compiler_params=_cparams(("parallel", "arbitrary")),
        name="attn_b_prompt",
    )(qb, qi, small, kk, vv, kiki, bias_tiles)


def _cumsum_rows_kernel(x_ref, tri_ref, o_ref, carry_ref):
    j = pl.program_id(1)

    @pl.when(j == 0)
    def _():
        carry_ref[...] = jnp.zeros(carry_ref.shape, F32)

    tri = tri_ref[...]
    hi, mid, lo = _split3(x_ref[...])
    cum = _dot(tri, hi) + _dot(tri, mid) + _dot(tri, lo) + carry_ref[...]
    o_ref[...] = cum
    carry_ref[...] = cum[-1:, :]


def _cumsum_prompt(x, nb, s, blk=128):
    nj = s // blk
    w = x.shape[1]
    r = np.arange(blk)
    tri = jnp.asarray(r[:, None] >= r[None, :], dtype=MXU_DT)
    return pl.pallas_call(
        _cumsum_rows_kernel,
        grid=(nb, nj),
        in_specs=[pl.BlockSpec((blk, w), lambda b, j: (b * nj + j, 0)),
                  pl.BlockSpec((blk, blk), lambda b, j: (0, 0))],
        out_specs=pl.BlockSpec((blk, w), lambda b, j: (b * nj + j, 0)),
        out_shape=jax.ShapeDtypeStruct((nb * s, w), F32),
        scratch_shapes=[pltpu.VMEM((1, w), F32)],
        compiler_params=_cparams(("parallel", "arbitrary")),
        name="cumsum_prompt",
    )(x, tri)


def _cumsum_pages_kernel(*refs, n_pages, ps):
    pt_ref = refs[0]
    page_refs = refs[1:1 + n_pages]
    tri_ref, o_ref = refs[1 + n_pages:]
    tri = tri_ref[...]
    carry = jnp.zeros((page_refs[0].shape[0], 1), F32)
    for j in range(n_pages):
        hi, mid, lo = _split3(page_refs[j][...])
        cum = _dot(hi, tri) + _dot(mid, tri) + _dot(lo, tri) + carry
        o_ref[:, j * ps:(j + 1) * ps] = cum
        carry = cum[:, ps - 1:ps]


def _cumsum_pages(logf_t, layer, page_table):
    db, npg = page_table.shape
    nh, ps = logf_t.shape[2], logf_t.shape[3]
    r = np.arange(ps)
    tri = jnp.asarray(r[:, None] <= r[None, :], dtype=MXU_DT)
    page_specs = [pl.BlockSpec((None, None, nh, ps), lambda b, pt, j=j: (layer, pt[b * npg + j], 0, 0))
                  for j in range(npg)]
    grid_spec = pltpu.PrefetchScalarGridSpec(
        num_scalar_prefetch=1,
        grid=(db,),
        in_specs=page_specs + [pl.BlockSpec((ps, ps), lambda b, pt: (0, 0))],
        out_specs=pl.BlockSpec((None, nh, npg * ps), lambda b, pt: (b, 0, 0)),
    )
    return pl.pallas_call(
        functools.partial(_cumsum_pages_kernel, n_pages=npg, ps=ps),
        grid_spec=grid_spec,
        out_shape=jax.ShapeDtypeStruct((db, nh, npg * ps), F32),
        compiler_params=_cparams(("parallel",)),
        name="cumsum_pages",
    )(page_table.reshape(-1), *([logf_t] * npg), tri)


def _online_update(s, m_ref, l_ref):
    m_old = m_ref[...]
    m_new = jnp.maximum(m_old, jnp.max(s, axis=-1, keepdims=True))
    alpha = jnp.exp(m_old - m_new)
    p = jnp.exp(s - _lanes(m_new, s.shape[1]))
    l_ref[...] = alpha * l_ref[...] + jnp.sum(p, axis=-1, keepdims=True)
    m_ref[...] = m_new
    return alpha, p


def _decode0_kernel(*refs, n_steps, npp, ps, topk, lam_init, group):
    (pt_ref, lamv_ref, gsub_ref, qa_ref, qi_ref, qb_ref, w8_ref, kna_ref, vna_ref, bnew_ref,
     taba_ref, tabb_ref, validb_ref) = refs[:13]
    pages = refs[13:13 + 5 * npp]
    cak, cav, cbk, cbv, cbi = (pages[0:npp], pages[npp:2 * npp], pages[2 * npp:3 * npp],
                               pages[3 * npp:4 * npp], pages[4 * npp:5 * npp])
    oa_ref, ob_ref, m_ref, l_ref, acc_ref, kb_ref, vb_ref, ki_ref, keys_ref = refs[13 + 5 * npp:]
    g = pl.program_id(0) % group
    step = pl.program_id(1)
    past = n_steps * npp * ps
    lk = kb_ref.shape[2]

    @pl.when(step == 0)
    def _():
        m_ref[...] = jnp.full(m_ref.shape, NEG, F32)
        l_ref[...] = jnp.zeros(l_ref.shape, F32)
        acc_ref[...] = jnp.zeros(acc_ref.shape, F32)

    qa = qa_ref[...]
    far = taba_ref[0]
    tail = taba_ref[jnp.where(step == n_steps - 1, 1, 0)]
    s = jnp.concatenate([_dot(qa, _mx(cak[j][...])) + (tail if j == npp - 1 else far) for j in range(npp)], axis=1)
    alpha, p = _online_update(s, m_ref, l_ref)
    p = _mx(p)
    for h in range(A_HEADS):
        rows = slice(h * 2 * T8, (h + 1) * 2 * T8)
        upd = alpha[rows] * acc_ref[rows]
        for j in range(npp):
            vh = _mx(cav[j][pl.ds(h, ps, stride=A_HEADS), :])
            upd = upd + _dot(p[rows, j * ps:(j + 1) * ps], vh)
        acc_ref[rows] = upd

    for j in range(npp):
        off = pl.multiple_of((step * npp + j) * ps, ps)
        kb_ref[g, :, pl.ds(off, ps)] = _mx(cbk[j][...])
        vb_ref[g, :, pl.ds(off, ps)] = _mx(cbv[j][...])
        ki_ref[g, :, pl.ds(off, ps)] = _mx(cbi[j][...])

    @pl.when(step == n_steps - 1)
    def _():
        s_new = _dot_t(qa, _mx(kna_ref[...])) + taba_ref[2][:, :kna_ref.shape[0]]
        alpha2, p2 = _online_update(s_new, m_ref, l_ref)
        vn = _mx(vna_ref[...])
        lam = _lambda(lamv_ref, lam_init)
        for h in range(A_HEADS):
            rows = slice(h * 2 * T8, (h + 1) * 2 * T8)
            o16 = (alpha2[rows] * acc_ref[rows] + _dot(_mx(p2[rows]), vn[:, h * LANES:(h + 1) * LANES])) / l_ref[rows]
            o = o16[:T8] - lam * o16[T8:]
            oa_ref[h] = _rms_rows(o, gsub_ref[...]) * (1.0 - lam_init)

        bn = bnew_ref[g]
        kb_ref[g, :, past:past + LANES] = _mx(bn[0])
        vb_ref[g, :, past:past + LANES] = _mx(bn[1])
        ki_ref[g, :, past:past + LANES] = _mx(bn[2])

    @pl.when((step == n_steps - 1) & (g == group - 1))
    def _():
        valid = validb_ref[...] == 0.0
        for gg in range(group):
            dots = jnp.maximum(_dot(qi_ref[gg], ki_ref[gg]), 0.0)
            w8 = w8_ref[gg] * IDX_HEAD_SCALE
            score = jnp.zeros((T8, lk), F32)
            for h in range(IDX_HEADS):
                score = score + w8[:, h:h + 1] * dots[h * T8:(h + 1) * T8]
            keys_ref[gg * T8:(gg + 1) * T8, :] = _score_keys(jnp.where(valid, score, -jnp.inf))
        rows = group * T8
        sel = _topk_select(keys_ref, lk, topk, jnp.full((rows, 1), True), _lane_iota((rows, lk)))
        selm = jnp.where(sel & jnp.concatenate([valid] * group, axis=0), 0.0, NEG)
        for gg in range(group):
            sg = selm[gg * T8:(gg + 1) * T8]
            sb = _dot(qb_ref[gg], kb_ref[gg]) + tabb_ref[...] + jnp.concatenate([sg] * B_HEADS, axis=0)
            mb = jnp.max(sb, axis=-1, keepdims=True)
            pb = jnp.exp(sb - mb)
            lb = jnp.sum(pb, axis=-1, keepdims=True)
            ob = _dot_t(_mx(pb), vb_ref[gg]) / lb
            for h in range(B_HEADS):
                ob_ref[gg, h] = ob[h * T8:(h + 1) * T8]


def _decode0(page_table, lamv, gsub, qa_bd, qi64, qb64, w8, kna, vna, bnew_t,
             cak_t, cav_r, cbk_t, cbv_t, cbi_t, taba, tabb, validb, layer, topk, lam_init):
    db, npg = page_table.shape
    ps = cak_t.shape[3]
    npp = PAGES_PER_STEP if npg % PAGES_PER_STEP == 0 else 1
    n_steps = npg // npp
    lk = tabb.shape[1]
    group = max(gsz for gsz in (8, 4, 2, 1) if db % gsz == 0)
    per_b = lambda b, p, pt: (b, 0, 0)
    per_g = lambda b, p, pt: (b // group, 0, 0)
    per_g4 = lambda b, p, pt: (b // group, 0, 0, 0)
    fixed2 = lambda b, p, pt: (0, 0)
    fixed3 = lambda b, p, pt: (0, 0, 0)

    def page_specs(arr):
        blk = (None, None) + arr.shape[2:]
        return [pl.BlockSpec(blk, lambda b, p, pt, j=j: (layer, pt[b * npg + p * npp + j], 0, 0)) for j in range(npp)]

    caches = (cak_t, cav_r, cbk_t, cbv_t, cbi_t)
    kern = functools.partial(_decode0_kernel, n_steps=n_steps, npp=npp, ps=ps, topk=topk, lam_init=lam_init,
                             group=group)
    grid_spec = pltpu.PrefetchScalarGridSpec(
        num_scalar_prefetch=1,
        grid=(db, n_steps),
        in_specs=[pl.BlockSpec((4, HEAD_DIM), fixed2), pl.BlockSpec((1, LANES), fixed2),
                  pl.BlockSpec((None,) + qa_bd.shape[1:], per_b),
                  pl.BlockSpec((group,) + qi64.shape[1:], per_g),
                  pl.BlockSpec((group,) + qb64.shape[1:], per_g),
                  pl.BlockSpec((group,) + w8.shape[1:], per_g),
                  pl.BlockSpec((None,) + kna.shape[1:], per_b),
                  pl.BlockSpec((None,) + vna.shape[1:], per_b),
                  pl.BlockSpec((group,) + bnew_t.shape[1:], per_g4),
                  pl.BlockSpec(taba.shape, fixed3), pl.BlockSpec(tabb.shape, fixed2),
                  pl.BlockSpec(validb.shape, fixed2)]
                 + [sp for c in caches for sp in page_specs(c)],
        out_specs=[pl.BlockSpec((None, A_HEADS, T8, LANES), lambda b, p, pt: (b, 0, 0, 0)),
                   pl.BlockSpec((group, B_HEADS, T8, HEAD_DIM), per_g4)],
        scratch_shapes=[pltpu.VMEM((64, LANES), F32), pltpu.VMEM((64, LANES), F32),
                        pltpu.VMEM((64, LANES), F32),
                        pltpu.VMEM((group, HEAD_DIM, lk), MXU_DT), pltpu.VMEM((group, HEAD_DIM, lk), MXU_DT),
                        pltpu.VMEM((group, IDX_DIM, lk), MXU_DT), pltpu.VMEM((group * T8, lk), I32)],
    )
    return pl.pallas_call(
        kern,
        grid_spec=grid_spec,
        out_shape=[jax.ShapeDtypeStruct((db, A_HEADS, T8, LANES), F32),
                   jax.ShapeDtypeStruct((db, B_HEADS, T8, HEAD_DIM), F32)],
        compiler_params=_cparams(("arbitrary", "arbitrary")),
        name="decode0",
    )(page_table.reshape(-1), lamv, gsub, qa_bd, qi64, qb64, w8, kna, vna, bnew_t, taba, tabb, validb,
      *[c for c in caches for _ in range(npp)])


def _expand_rows(x):
    hh, ww = x.shape
    return jnp.broadcast_to(x[:, None, :], (hh, T8, ww)).reshape(hh * T8, ww)


def _decode1_kernel(*refs, n_steps, npp, ps):
    pt_ref, q_ref, cq_ref, kn_ref, vn_ref, cnew_ref, ckt_ref, maskn_ref = refs[:8]
    ck = refs[8:8 + npp]
    cv = refs[8 + npp:8 + 2 * npp]
    o_ref, m_ref, l_ref, acc_ref = refs[8 + 2 * npp:]
    step = pl.program_id(1)

    @pl.when(step == 0)
    def _():
        m_ref[...] = jnp.full(m_ref.shape, NEG, F32)
        l_ref[...] = jnp.zeros(l_ref.shape, F32)
        acc_ref[...] = jnp.zeros(acc_ref.shape, F32)

    q = q_ref[...]
    cq = jnp.broadcast_to(cq_ref[...], (q.shape[0], LANES))
    s = jnp.concatenate([_dot(q, _mx(ck[j][...])) for j in range(npp)], axis=1)
    s = s + _lanes(cq, npp * ps) - _expand_rows(ckt_ref[...])
    alpha, p = _online_update(s, m_ref, l_ref)
    p = _mx(p)
    upd = _lanes(alpha, acc_ref.shape[1]) * acc_ref[...]
    for j in range(npp):
        upd = upd + _dot_t(p[:, j * ps:(j + 1) * ps], _mx(cv[j][...]))
    acc_ref[...] = upd

    @pl.when(step == n_steps - 1)
    def _():
        nn = kn_ref.shape[0]
        s_new = (_dot_t(q, _mx(kn_ref[...])) + cq[:, :nn] - _expand_rows(cnew_ref[...]) + maskn_ref[...])
        alpha2, p2 = _online_update(s_new, m_ref, l_ref)
        o_full = ((_lanes(alpha2, acc_ref.shape[1]) * acc_ref[...] + _dot(_mx(p2), _mx(vn_ref[...])))
                  / _lanes(l_ref[...], acc_ref.shape[1]))
        for h in range(C_HEADS):
            o_ref[h] = o_full[h * T8:(h + 1) * T8, h * HEAD_DIM:(h + 1) * HEAD_DIM]


def _decode1(page_table, q_bd, cq, kn, vn, cnew_t, cck_t, ccv_t, ckt, maskn, layer):
    db, npg = page_table.shape
    ps = cck_t.shape[3]
    npp = PAGES_PER_STEP if npg % PAGES_PER_STEP == 0 else 1
    n_steps = npg // npp
    per_b = lambda b, p, pt: (b, 0, 0)

    def page_specs(arr):
        blk = (None, None) + arr.shape[2:]
        return [pl.BlockSpec(blk, lambda b, p, pt, j=j: (layer, pt[b * npg + p * npp + j], 0, 0)) for j in range(npp)]

    kern = functools.partial(_decode1_kernel, n_steps=n_steps, npp=npp, ps=ps)
    rows = C_HEADS * T8
    grid_spec = pltpu.PrefetchScalarGridSpec(
        num_scalar_prefetch=1,
        grid=(db, n_steps),
        in_specs=[pl.BlockSpec((None,) + q_bd.shape[1:], per_b),
                  pl.BlockSpec((None,) + cq.shape[1:], per_b),
                  pl.BlockSpec((None,) + kn.shape[1:], per_b),
                  pl.BlockSpec((None,) + vn.shape[1:], per_b),
                  pl.BlockSpec((None,) + cnew_t.shape[1:], per_b),
                  pl.BlockSpec((None, C_HEADS, npp * ps), lambda b, p, pt: (b, 0, p)),
                  pl.BlockSpec(maskn.shape, lambda b, p, pt: (0, 0))]
                 + page_specs(cck_t) + page_specs(ccv_t),
        out_specs=pl.BlockSpec((None, C_HEADS, T8, HEAD_DIM), lambda b, p, pt: (b, 0, 0, 0)),
        scratch_shapes=[pltpu.VMEM((rows, LANES), F32), pltpu.VMEM((rows, LANES), F32),
                        pltpu.VMEM((rows, C_HEADS * HEAD_DIM), F32)],
    )
    return pl.pallas_call(
        kern,
        grid_spec=grid_spec,
        out_shape=jax.ShapeDtypeStruct((db, C_HEADS, T8, HEAD_DIM), F32),
        compiler_params=_cparams(("parallel", "arbitrary")),
        name="decode1",
    )(page_table.reshape(-1), q_bd, cq, kn, vn, cnew_t, ckt, maskn, *([cck_t] * npp), *([ccv_t] * npp))


def _router_kernel(x_ref, g_ref, whi_ref, wlo_ref, b_ref, t_ref, route_ref):
    t = _rms_rows(x_ref[...], g_ref[...])
    thi, tlo = _split2(t)
    t_ref[...] = thi
    whi = whi_ref[...]
    logits = _dot(thi, whi) + _dot(tlo, whi) + _dot(thi, wlo_ref[...]) + b_ref[...]
    lane = _lane_iota(logits.shape)
    big = jnp.int32(1 << 20)

    def first_max(v):
        mx = jnp.max(v, axis=-1, keepdims=True)
        idx = jnp.min(jnp.where(v == mx, lane, big), axis=-1, keepdims=True)
        return mx, idx

    glog = jnp.where(lane < N_GROUPS, logits, -jnp.inf)
    gmax, gidx = first_max(glog)
    grp_w = 1.0 / jnp.sum(jnp.exp(glog - gmax), axis=-1, keepdims=True)
    el = lane - N_GROUPS
    in_grp = (el >= 0) & (el < N_EXPERTS) & (jnp.right_shift(el, 3) == gidx)
    v1 = jnp.where(in_grp, logits, -jnp.inf)
    top1, i1 = first_max(v1)
    v2 = jnp.where(lane == i1, -jnp.inf, v1)
    top2, i2 = first_max(v2)
    e2 = jnp.exp(top2 - top1)
    w1 = grp_w / (1.0 + e2)
    w2 = grp_w * e2 / (1.0 + e2)
    route = jnp.where(lane == 0, (i1 - N_GROUPS).astype(F32),
                      jnp.where(lane == 1, (i2 - N_GROUPS).astype(F32),
                                jnp.where(lane == 2, w1, jnp.where(lane == 3, w2, 0.0))))
    route_ref[...] = route


def _router(x, g, w_group, b_group, w_router, b_router, tm=512):
    n = x.shape[0]
    tm = _tile(n, tm)
    pad = LANES - N_GROUPS - N_EXPERTS
    w = jnp.concatenate([w_group, w_router, jnp.zeros((D_MODEL, pad), F32)], axis=1)
    whi = _mx(w)
    wlo = _mx(w - whi.astype(F32))
    b = jnp.concatenate([b_group, b_router, jnp.zeros((pad,), F32)])[None, :]
    row = lambda i: (i, 0)
    fixed = lambda i: (0, 0)
    return pl.pallas_call(
        _router_kernel,
        grid=(n // tm,),
        in_specs=[pl.BlockSpec((tm, D_MODEL), row), pl.BlockSpec((1, D_MODEL), fixed),
                  pl.BlockSpec((D_MODEL, LANES), fixed), pl.BlockSpec((D_MODEL, LANES), fixed),
                  pl.BlockSpec((1, LANES), fixed)],
        out_specs=[pl.BlockSpec((tm, D_MODEL), row), pl.BlockSpec((tm, LANES), row)],
        out_shape=[jax.ShapeDtypeStruct((n, D_MODEL), MXU_DT), jax.ShapeDtypeStruct((n, LANES), F32)],
        compiler_params=_cparams(("parallel",)),
        name="moe_router",
    )(x, g[None, :], whi, wlo, b)


def _moe_dense_kernel(x_ref, t_ref, route_ref, wgu_ref, wd_ref, o_ref):
    e = pl.program_id(1)

    @pl.when(e == 0)
    def _():
        o_ref[...] = x_ref[...]

    au = _dot(t_ref[...], wgu_ref[...])
    a = au[:, :EXPERT_FF]
    u = au[:, EXPERT_FF:]
    r = route_ref[...]
    ef = e.astype(F32)
    gate = jnp.where(r[:, 0:1] == ef, r[:, 2:3], 0.0) + jnp.where(r[:, 1:2] == ef, r[:, 3:4], 0.0)
    hdn = a * (1.0 / (1.0 + jnp.exp(-a))) * u * gate
    o_ref[...] += _dot(_mx(hdn), wd_ref[...])


def _moe_dense(x, t, route, w_gate, w_up, w_down, tm=1536):
    n = x.shape[0]
    tm = _tile(n, tm)
    wgu = _mx(jnp.concatenate([w_gate.reshape(N_EXPERTS, D_MODEL, EXPERT_FF),
                               w_up.reshape(N_EXPERTS, D_MODEL, EXPERT_FF)], axis=-1))
    wd = _mx(w_down.reshape(N_EXPERTS, EXPERT_FF, D_MODEL))
    row = lambda i, e: (i, 0)
    once = pl.Buffered(1)
    return pl.pallas_call(
        _moe_dense_kernel,
        grid=(n // tm, N_EXPERTS),
        in_specs=[pl.BlockSpec((tm, D_MODEL), row, pipeline_mode=once),
                  pl.BlockSpec((tm, D_MODEL), row, pipeline_mode=once),
                  pl.BlockSpec((tm, LANES), row, pipeline_mode=once),
                  pl.BlockSpec((None, D_MODEL, 2 * EXPERT_FF), lambda i, e: (e, 0, 0)),
                  pl.BlockSpec((None, EXPERT_FF, D_MODEL), lambda i, e: (e, 0, 0))],
        out_specs=pl.BlockSpec((tm, D_MODEL), row),
        out_shape=jax.ShapeDtypeStruct((n, D_MODEL), F32),
        compiler_params=_cparams(("parallel", "arbitrary")),
        name="moe_dense",
    )(x, t, route, wgu, wd)


def _hier_moe(x, g, w_group, b_group, w_router, b_router, w_gate, w_up, w_down):
    t, route = _router(x, g, w_group, b_group, w_router, b_router)
    return _moe_dense(x, t, route, w_gate, w_up, w_down)


def _decode_tables(rel_bias, past, t_new, lk):
    t8 = np.arange(T8)
    lane = np.arange(LANES)
    far = np.full((T8, LANES), REL_BUCKETS - 1, np.int32)
    d_last = (past + t8[:, None]) - (past - LANES + lane[None, :])
    d_new = t8[:, None] - lane[None, :]
    ok_new = (d_new >= 0) & (lane[None, :] < t_new)
    ta = _bias_expand(rel_bias, np.stack([far, _t5_bucket_np(d_last), _t5_bucket_np(d_new)]), 0, A_HEADS)
    ta = jnp.where(jnp.asarray(ok_new)[None, None] | (jnp.arange(3) < 2)[None, :, None, None], ta, NEG)
    taba = jnp.broadcast_to(jnp.transpose(ta, (1, 0, 2, 3))[:, :, None], (3, A_HEADS, 2, T8, LANES))
    taba = taba.reshape(3, A_HEADS * 2 * T8, LANES)
    kpos = np.arange(lk)
    d_b = (past + t8[:, None]) - kpos[None, :]
    tabb = _bias_expand(rel_bias, _t5_bucket_np(d_b), A_HEADS, B_HEADS).reshape(B_HEADS * T8, lk)
    valid = (kpos[None, :] < past) | ((d_b >= 0) & (kpos[None, :] < past + t_new))
    validb = jnp.asarray(np.where(valid, 0.0, NEG), F32)
    return taba, tabb, validb


def _pad_rows(x, rows):
    pad = [(0, 0)] * x.ndim
    pad[1] = (0, rows - x.shape[1])
    return jnp.pad(x, pad)


def kernel(x_prompt, x_sample, cache_a_k, cache_a_v, cache_b_k, cache_b_v, cache_b_kidx, cache_c_k, cache_c_v, cache_c_logf, page_table, rel_bias, ab_norm, ab_w_in, a_q_norm, a_k_norm, b_q_norm, b_k_norm, a_lambda_q1, a_lambda_k1, a_lambda_q2, a_lambda_k2, a_sub_norm, ab_w_out, c_norm, c_w_in, c_forget_bias, c_q_norm, c_k_norm, c_w_out, ffn_norm, moe_w_group, moe_b_group, moe_w_router, moe_b_router, moe_w_gate, moe_w_up, moe_w_down):
    nb, s, d = x_prompt.shape
    db, ts, _ = x_sample.shape
    npg = page_table.shape[1]
    pool, ps = cache_a_k.shape[1], cache_a_k.shape[2]
    past = npg * ps
    n_p = nb * s
    n_s = db * ts
    depth = ffn_norm.shape[0]
    topk_p = min(IDX_TOPK_MAX, s // 4)
    topk_s = min(IDX_TOPK_MAX, (past + ts) // 4)
    lk = past + LANES

    cak_t = jnp.transpose(cache_a_k, (0, 1, 3, 4, 5, 2)).reshape(-1, pool, 2 * A_HEADS * HEAD_DIM, ps)
    cav_r = cache_a_v.reshape(-1, pool, ps * A_HEADS, 2 * HEAD_DIM)
    cbk_t = jnp.transpose(cache_b_k, (0, 1, 3, 2))
    cbv_t = jnp.transpose(cache_b_v, (0, 1, 3, 2))
    cbi_t = jnp.transpose(cache_b_kidx, (0, 1, 3, 2))
    cck_t = jnp.transpose(cache_c_k, (0, 1, 3, 4, 2)).reshape(-1, pool, C_HEADS * HEAD_DIM, ps)
    ccv_t = jnp.transpose(cache_c_v, (0, 1, 3, 4, 2)).reshape(-1, pool, C_HEADS * HEAD_DIM, ps)
    ccf_t = jnp.transpose(cache_c_logf, (0, 1, 3, 2))

    x = jnp.concatenate([x_prompt.reshape(n_p, d), x_sample.reshape(n_s, d)], axis=0)
    outs_p = {k: [] for k in ("ak", "av", "bk", "bv", "bi", "ck", "cv", "cf")}
    outs_s = {k: [] for k in ("ak", "av", "bk", "bv", "bi", "ck", "cv", "cf")}

    def smp(a):
        return a[n_p:].reshape(db, ts, a.shape[1])

    for layer in range(depth):
        if layer % 2 == 0:
            e = layer // 2
            lam_init = 0.8 - 0.6 * math.exp(-0.3 * layer)
            lamv = jnp.stack([a_lambda_q1[e], a_lambda_k1[e], a_lambda_q2[e], a_lambda_k2[e]])
            gsub = a_sub_norm[e][None, :]
            (qa, ka, kabf, qb, small, kk, vv, kiki, va, vabf, qi) = _proj0(
                x, ab_norm[e], ab_w_in[e], a_q_norm[e], a_k_norm[e], b_q_norm[e], b_k_norm[e])
            kb, vb, ki = small[:, 0:64], small[:, 64:128], small[:, 128:192]
            tq_a = 256
            bias_a = _bias_expand(rel_bias, _prompt_bucket_tiles(tq_a), 0, A_HEADS)
            oa_p = _attn_a_prompt(qa, kabf, vabf, lamv, gsub, bias_a, nb, s, lam_init, tq=tq_a)
            tq_b = 256
            bias_b = _bias_expand(rel_bias, _prompt_bucket_tiles(tq_b), A_HEADS, B_HEADS)
            ob_p = _attn_b_prompt(qb, qi, small, kk, vv, kiki, bias_b, nb, s, topk_p, tq=tq_b)
            qa_s = _pad_rows(smp(qa), T8)
            hc = np.arange(2 * A_HEADS)
            colmask = jnp.asarray((np.arange(qa_s.shape[2])[None, :] // HEAD_DIM) == hc[:, None], MXU_DT)
            qa_bd = (qa_s[:, None, :, :] * colmask[None, :, None, :]).reshape(db, 2 * A_HEADS * T8, -1)

            def heads_rows(a, nh):
                a = _pad_rows(a, T8).reshape(db, T8, nh, HEAD_DIM)
                return jnp.transpose(a, (0, 2, 1, 3)).reshape(db, nh * T8, HEAD_DIM)

            qi64 = heads_rows(smp(qi), IDX_HEADS)
            qb64 = heads_rows(smp(qb), B_HEADS)
            small_s = smp(small)
            w8 = _pad_rows(small_s[:, :, 192:200], T8)
            kna = _pad_rows(smp(ka), 16)
            vna = _pad_rows(smp(va), 16)
            bnew_t = jnp.transpose(_pad_rows(small_s[:, :, :192], LANES).reshape(db, LANES, 3, HEAD_DIM), (0, 2, 3, 1))
            taba, tabb, validb = _decode_tables(rel_bias, past, ts, lk)
            oa_d, ob_d = _decode0(page_table, lamv, gsub, qa_bd, qi64, qb64, w8, kna, vna, bnew_t,
                                  cak_t, cav_r, cbk_t, cbv_t, cbi_t, taba, tabb, validb, e, topk_s, lam_init)
            oa_s = jnp.transpose(oa_d[:, :, :ts], (0, 2, 1, 3)).reshape(n_s, A_HEADS * LANES)
            ob_s = jnp.transpose(ob_d[:, :, :ts], (0, 2, 1, 3)).reshape(n_s, B_HEADS * HEAD_DIM)
            oa = jnp.concatenate([oa_p, _mx(oa_s)], axis=0)
            ob = jnp.concatenate([ob_p, _mx(ob_s)], axis=0)
            w_out = ab_w_out[e]
            x = _outproj(x, [(oa, w_out[:A_HEADS * LANES]), (ob, w_out[A_HEADS * LANES:])])
            for dst, lo, n_, shp in ((outs_p, 0, n_p, (nb, s)), (outs_s, n_p, n_s, (db, ts))):
                dst["ak"].append(ka[lo:lo + n_].reshape(*shp, A_HEADS, 2, HEAD_DIM))
                dst["av"].append(va[lo:lo + n_].reshape(*shp, A_HEADS, 2 * HEAD_DIM))
                dst["bk"].append(kb[lo:lo + n_].reshape(*shp, HEAD_DIM))
                dst["bv"].append(vb[lo:lo + n_].reshape(*shp, HEAD_DIM))
                dst["bi"].append(ki[lo:lo + n_].reshape(*shp, IDX_DIM))
        else:
            o = layer // 2
            q, k, kbf, v, vbf, logf128 = _proj1(x, c_norm[o], c_w_in[o], c_forget_bias[o], c_q_norm[o], c_k_norm[o])
            logf = logf128[:, :C_HEADS]
            cum_p = _cumsum_prompt(logf128[:n_p], nb, s)[:, :C_HEADS]
            ckt = jnp.transpose(cum_p.reshape(nb, s, C_HEADS // 2, 2), (0, 2, 3, 1))
            oc_p = _attn_c_prompt(q, kbf, vbf, cum_p, ckt, nb, s)
            ckt_s = _cumsum_pages(ccf_t, o, page_table)
            logf_s = smp(logf)
            run = ckt_s[:, :, -1]
            c_rows = []
            for t in range(ts):
                run = run + logf_s[:, t]
                c_rows.append(run)
            c_new = jnp.stack(c_rows, axis=1)
            cq = jnp.transpose(_pad_rows(c_new, T8), (0, 2, 1)).reshape(db, C_HEADS * T8, 1)
            cnew_t = jnp.transpose(_pad_rows(c_new, 16), (0, 2, 1))
            q_s = _pad_rows(smp(q), T8)
            hmask = jnp.asarray((np.arange(q_s.shape[2])[None, :] // HEAD_DIM) == np.arange(C_HEADS)[:, None], MXU_DT)
            q_bd = (q_s[:, None, :, :] * hmask[None, :, None, :]).reshape(db, C_HEADS * T8, -1)
            kn = _pad_rows(smp(k), 16)
            vn = _pad_rows(smp(v), 16)
            t8 = np.arange(T8)
            okn = (t8[:, None] >= np.arange(16)[None, :]) & (np.arange(16)[None, :] < ts)
            maskn = jnp.asarray(np.tile(np.where(okn, 0.0, NEG), (C_HEADS, 1)), F32)
            oc_d = _decode1(page_table, q_bd, cq, kn, vn, cnew_t, cck_t, ccv_t, ckt_s, maskn, o)
            oc_s = jnp.transpose(oc_d[:, :, :ts], (0, 2, 1, 3)).reshape(n_s, C_HEADS * HEAD_DIM)
            oc = jnp.concatenate([oc_p, _mx(oc_s)], axis=0)
            x = _outproj(x, [(oc, c_w_out[o])])
            for dst, lo, n_, shp in ((outs_p, 0, n_p, (nb, s)), (outs_s, n_p, n_s, (db, ts))):
                dst["ck"].append(k[lo:lo + n_].reshape(*shp, C_HEADS, HEAD_DIM))
                dst["cv"].append(v[lo:lo + n_].reshape(*shp, C_HEADS, HEAD_DIM))
                dst["cf"].append(logf[lo:lo + n_].reshape(*shp, C_HEADS))
        x = _hier_moe(x, ffn_norm[layer], moe_w_group[layer], moe_b_group[layer], moe_w_router[layer],
                      moe_b_router[layer], moe_w_gate[layer], moe_w_up[layer], moe_w_down[layer])

    keys = ("ak", "av", "bk", "bv", "bi", "ck", "cv", "cf")
    return ((x[:n_p].reshape(nb, s, d), x[n_p:].reshape(db, ts, d))
            + tuple(jnp.stack(outs_p[k]) for k in keys)
            + tuple(jnp.stack(outs_s[k]) for k in keys))
```

```python
import functools
import math

import numpy as np
import jax
import jax.numpy as jnp
from jax import lax
from jax.experimental import pallas as pl
from jax.experimental.pallas import tpu as pltpu

F32 = jnp.float32
I32 = jnp.int32
MXU_DT = jnp.bfloat16

D_MODEL = 1024
HEAD_DIM = 64
A_HEADS = 4
B_HEADS = 8
IDX_HEADS = 8
IDX_DIM = 64
IDX_TOPK_MAX = 256
C_HEADS = 16
REL_BUCKETS = 32
REL_MAX_DIST = 128
N_GROUPS = 4
EXPERTS_PER_GROUP = 8
N_EXPERTS = N_GROUPS * EXPERTS_PER_GROUP
EXPERT_FF = 256
EPS = 1e-6
NEG = -1e30
INT_MIN = -2 ** 31
QK_SCALE = HEAD_DIM ** -0.5
IDX_SCALE = IDX_DIM ** -0.5
IDX_HEAD_SCALE = IDX_HEADS ** -0.5
LANES = 128
T8 = 8
VMEM_LIMIT = 56 * 1024 * 1024
PAGES_PER_STEP = 8


def _tile(n, pref):
    best = 16
    for t in range(16, pref + 1, 16):
        if n % t == 0:
            best = t
    assert n % best == 0, (n, pref)
    return best


def _cparams(sem):
    return pltpu.CompilerParams(dimension_semantics=sem, vmem_limit_bytes=VMEM_LIMIT)


def _mx(x):
    return x.astype(MXU_DT)


def _dot(a, b):
    return jnp.dot(a, b, preferred_element_type=F32)


def _dot_t(a, b):
    return lax.dot_general(a, b, (((1,), (1,)), ((), ())), preferred_element_type=F32)


def _split2(x):
    hi = _mx(x)
    lo = _mx(x - hi.astype(F32))
    return hi, lo


def _split3(x):
    hi = _mx(x)
    r = x - hi.astype(F32)
    mid = _mx(r)
    lo = _mx(r - mid.astype(F32))
    return hi, mid, lo


def _lane_iota(shape):
    return lax.broadcasted_iota(I32, shape, len(shape) - 1)


def _row_iota(shape):
    return lax.broadcasted_iota(I32, shape, len(shape) - 2)


def _lanes(x, width):
    if width <= LANES:
        return x[:, :width]
    return jnp.tile(x, (1, width // LANES))


def _rms_rows(x, g):
    ms = jnp.mean(x * x, axis=-1, keepdims=True)
    return x * lax.rsqrt(ms + EPS) * g


def _seg_rsqrt(y, bd):
    hi, lo = _split2(y * y)
    ss = _dot(hi, bd) + _dot(lo, bd)
    return lax.rsqrt(ss * (1.0 / HEAD_DIM) + EPS)


def _block_diag_ones(n, seg):
    r = np.arange(n)
    return jnp.asarray((r[:, None] // seg) == (r[None, :] // seg), dtype=MXU_DT)


def _lambda(lamv_ref, lam_init):
    lv = lamv_ref[...]
    return (jnp.exp(jnp.sum(lv[0:1] * lv[1:2], axis=-1, keepdims=True))
            - jnp.exp(jnp.sum(lv[2:3] * lv[3:4], axis=-1, keepdims=True)) + lam_init)


def _t5_bucket_np(d):
    n = np.maximum(d, 0)
    exact = REL_BUCKETS // 2
    nf = np.maximum(n, 1).astype(np.float64)
    large = exact + (np.log(nf / exact) / math.log(REL_MAX_DIST / exact) * (REL_BUCKETS - exact)).astype(np.int64)
    large = np.minimum(large, REL_BUCKETS - 1)
    return np.where(n < exact, n, large).astype(np.int32)


def _bias_expand_kernel(relb_ref, idx_ref, o_ref, *, head0):
    h = pl.program_id(0) + head0
    idx = idx_ref[...]
    acc = jnp.zeros(idx.shape, F32)
    for b in range(REL_BUCKETS):
        acc = jnp.where(idx == b, relb_ref[b, h], acc)
    o_ref[...] = acc


def _bias_expand(rel_bias, idx_np, head0, n_heads):
    idx = jnp.asarray(idx_np, I32)
    nd = idx.ndim
    zeros = (0,) * nd
    return pl.pallas_call(
        functools.partial(_bias_expand_kernel, head0=head0),
        grid=(n_heads,),
        in_specs=[pl.BlockSpec(memory_space=pltpu.SMEM),
                  pl.BlockSpec(idx.shape, lambda h: zeros)],
        out_specs=pl.BlockSpec((None,) + idx.shape, lambda h: (h,) + zeros),
        out_shape=jax.ShapeDtypeStruct((n_heads,) + idx.shape, F32),
        compiler_params=_cparams(("arbitrary",)),
        name="bias_expand",
    )(rel_bias, idx)


def _prompt_bucket_tiles(t):
    r = np.arange(t)
    d = r[:, None] - r[None, :]
    return np.stack([_t5_bucket_np(d), _t5_bucket_np(d + t), np.full((t, t), REL_BUCKETS - 1, np.int32)])


def _store_t(y, is_prompt, t_refs, row_ref, c):
    if is_prompt:
        yt = jnp.transpose(y)
        for r in t_refs:
            r[c * 256:(c + 1) * 256, :] = yt.astype(r.dtype)
    elif row_ref is not None:
        row_ref[:, c * 256:(c + 1) * 256] = y


def _per_tile_kind(body, n_prompt_tiles):
    prompt_tile = pl.program_id(0) < n_prompt_tiles

    @pl.when(prompt_tile)
    def _():
        body(True)

    @pl.when(jnp.logical_not(prompt_tile))
    def _():
        body(False)


def _proj0_kernel(*refs, n_prompt_tiles):
    _per_tile_kind(functools.partial(_proj0_body, *refs), n_prompt_tiles)


def _proj0_body(x_ref, g_ref, w_ref, gain_ref, bd_ref,
                qa_ref, ka_ref, kat_ref, katbf_ref, qb_ref, small_ref, smallt_ref, kk_ref, vv_ref, kiki_ref,
                va_ref, vabf_ref, qi_ref, is_prompt):
    h = _mx(_rms_rows(x_ref[...], g_ref[...]))
    bd = bd_ref[...]

    def chunk(c):
        return _dot(h, w_ref[:, c * 256:(c + 1) * 256])

    def normed(c):
        y = chunk(c)
        return y * _seg_rsqrt(y, bd) * gain_ref[:, c * 256:(c + 1) * 256]

    for c in range(2):
        qa_ref[:, c * 256:(c + 1) * 256] = _mx(normed(c) * QK_SCALE)
    for c in range(2):
        _store_t(normed(2 + c), is_prompt, (kat_ref, katbf_ref), ka_ref, c)
    for c in range(2):
        qb_ref[:, c * 256:(c + 1) * 256] = _mx(normed(4 + c) * QK_SCALE)
    y = chunk(6)
    yn = y * _seg_rsqrt(y, bd) * gain_ref[:, 6 * 256:7 * 256]
    lane = _lane_iota(y.shape)
    y = jnp.where(lane < HEAD_DIM, yn, y)
    small_ref[...] = y
    _store_t(y, is_prompt, (smallt_ref,), None, 0)
    t0 = y[:, :LANES]
    t1 = y[:, LANES:]
    lo = _lane_iota(t0.shape) < HEAD_DIM
    r0 = pltpu.roll(t0, HEAD_DIM, 1)
    kk_ref[...] = _mx(jnp.where(lo, t0, r0))
    vv_ref[...] = _mx(jnp.where(lo, r0, t0))
    r1 = pltpu.roll(t1, HEAD_DIM, 1)
    kiki_ref[...] = _mx(jnp.where(lo, t1, r1))
    for c in range(2):
        y = chunk(7 + c)
        va_ref[:, c * 256:(c + 1) * 256] = y
        vabf_ref[:, c * 256:(c + 1) * 256] = _mx(y)
    for c in range(2):
        qi_ref[:, c * 256:(c + 1) * 256] = _mx(chunk(9 + c) * IDX_SCALE)


def _token_specs(n_p, n_s, s, tm):
    npt = n_p // tm
    tps = s // tm

    def t_map(i):
        ip = jnp.minimum(i, npt - 1)
        return (ip // tps, 0, ip % tps)

    def s_map(i):
        return (jnp.maximum(i - npt, 0), 0)

    return npt, t_map, s_map


def _proj0(x, g, w_in, a_qn, a_kn, b_qn, b_kn, n_p, s, tm=256):
    n = x.shape[0]
    n_s = n - n_p
    nb = n_p // s
    tm = _tile(math.gcd(n_p, n_s, s), tm)
    npt, t_map, s_map = _token_specs(n_p, n_s, s, tm)
    sp = np.cumsum([512, 512, 512, 512, 64, 64, 512, 64, 8])[:-1]
    wqa, wka, wva, wqb, wkb, wvb, wqi, wki, wwi = jnp.split(w_in, sp, axis=1)
    w = jnp.concatenate([wqa, wka, wqb, wkb, wvb, wki, wwi, jnp.zeros((D_MODEL, 56), F32), wva, wqi], axis=1)
    w = _mx(w)
    ncol = w.shape[1]
    gain = jnp.concatenate([jnp.tile(a_qn, 8), jnp.tile(a_kn, 8), jnp.tile(b_qn, 8), b_kn,
                            jnp.ones((192,), F32)])[None, :]
    bd = _block_diag_ones(256, HEAD_DIM)
    row = lambda i: (i, 0)
    fixed = lambda i: (0, 0)
    def rows(wd, dt):
        return pl.BlockSpec((tm, wd), row), jax.ShapeDtypeStruct((n, wd), dt)

    def rows_s(wd, dt):
        return pl.BlockSpec((tm, wd), s_map), jax.ShapeDtypeStruct((n_s, wd), dt)

    def cols_p(wd, dt):
        return pl.BlockSpec((None, wd, tm), t_map), jax.ShapeDtypeStruct((nb, wd, s), dt)

    outs = [rows(512, MXU_DT), rows_s(512, F32), cols_p(512, F32), cols_p(512, MXU_DT), rows(512, MXU_DT),
            rows(256, F32), cols_p(256, F32), rows(128, MXU_DT), rows(128, MXU_DT), rows(128, MXU_DT),
            rows(512, F32), rows(512, MXU_DT), rows(512, MXU_DT)]
    return pl.pallas_call(
        functools.partial(_proj0_kernel, n_prompt_tiles=npt),
        grid=(n // tm,),
        in_specs=[pl.BlockSpec((tm, D_MODEL), row), pl.BlockSpec((1, D_MODEL), fixed),
                  pl.BlockSpec((D_MODEL, ncol), fixed), pl.BlockSpec((1, gain.shape[1]), fixed),
                  pl.BlockSpec((256, 256), fixed)],
        out_specs=[o[0] for o in outs],
        out_shape=[o[1] for o in outs],
        compiler_params=_cparams(("arbitrary",)),
        name="proj0",
    )(x, g[None, :], w, gain, bd)


def _proj1_kernel(*refs, n_prompt_tiles):
    _per_tile_kind(functools.partial(_proj1_body, *refs), n_prompt_tiles)


def _proj1_body(x_ref, g_ref, w_ref, gain_ref, bf_ref, bd_ref,
                q_ref, k_ref, kt_ref, ktbf_ref, v_ref, vt_ref, vbf_ref, logf_ref, logft_ref, is_prompt):
    h = _mx(_rms_rows(x_ref[...], g_ref[...]))
    bd = bd_ref[...]

    def chunk(c):
        return _dot(h, w_ref[:, c * 256:(c + 1) * 256])

    def normed(c):
        y = chunk(c)
        return y * _seg_rsqrt(y, bd) * gain_ref[:, c * 256:(c + 1) * 256]

    for c in range(4):
        q_ref[:, c * 256:(c + 1) * 256] = _mx(normed(c) * QK_SCALE)
    for c in range(4):
        _store_t(normed(4 + c), is_prompt, (kt_ref, ktbf_ref), k_ref, c)
    for c in range(4):
        y = chunk(8 + c)
        vbf_ref[:, c * 256:(c + 1) * 256] = _mx(y)
        _store_t(y, is_prompt, (vt_ref,), v_ref, c)
    f = _dot(h, w_ref[:, 12 * 256:12 * 256 + LANES]) + bf_ref[...]
    logf = jnp.minimum(f, 0.0) - jnp.log(1.0 + jnp.exp(-jnp.abs(f)))
    logf_ref[...] = logf
    if is_prompt:
        logft_ref[...] = jnp.transpose(logf)


def _proj1(x, g, w_in, b_f, qn, kn, n_p, s, tm=256):
    n = x.shape[0]
    n_s = n - n_p
    nb = n_p // s
    tm = _tile(math.gcd(n_p, n_s, s), tm)
    npt, t_map, s_map = _token_specs(n_p, n_s, s, tm)
    w = _mx(jnp.concatenate([w_in, jnp.zeros((D_MODEL, LANES - C_HEADS), F32)], axis=1))
    ncol = w.shape[1]
    gain = jnp.concatenate([jnp.tile(qn, C_HEADS), jnp.tile(kn, C_HEADS)])[None, :]
    bf = jnp.concatenate([b_f, jnp.zeros((LANES - C_HEADS,), F32)])[None, :]
    bd = _block_diag_ones(256, HEAD_DIM)
    row = lambda i: (i, 0)
    fixed = lambda i: (0, 0)
    def rows(wd, dt):
        return pl.BlockSpec((tm, wd), row), jax.ShapeDtypeStruct((n, wd), dt)

    def rows_s(wd, dt):
        return pl.BlockSpec((tm, wd), s_map), jax.ShapeDtypeStruct((n_s, wd), dt)

    def cols_p(wd, dt):
        return pl.BlockSpec((None, wd, tm), t_map), jax.ShapeDtypeStruct((nb, wd, s), dt)

    outs = [rows(1024, MXU_DT), rows_s(1024, F32), cols_p(1024, F32), cols_p(1024, MXU_DT),
            rows_s(1024, F32), cols_p(1024, F32), rows(1024, MXU_DT), rows(LANES, F32), cols_p(LANES, F32)]
    return pl.pallas_call(
        functools.partial(_proj1_kernel, n_prompt_tiles=npt),
        grid=(n // tm,),
        in_specs=[pl.BlockSpec((tm, D_MODEL), row), pl.BlockSpec((1, D_MODEL), fixed),
                  pl.BlockSpec((D_MODEL, ncol), fixed), pl.BlockSpec((1, gain.shape[1]), fixed),
                  pl.BlockSpec((1, LANES), fixed), pl.BlockSpec((256, 256), fixed)],
        out_specs=[o[0] for o in outs],
        out_shape=[o[1] for o in outs],
        compiler_params=_cparams(("arbitrary",)),
        name="proj1",
    )(x, g[None, :], w, gain, bf, bd)


def _outproj_kernel(*refs, n_in):
    res_ref = refs[0]
    out_ref = refs[-1]
    acc = res_ref[...]
    for i in range(n_in):
        acc = acc + _dot(refs[1 + 2 * i][...], refs[2 + 2 * i][...])
    out_ref[...] = acc


def _outproj(res, pairs, tm=512):
    n = res.shape[0]
    tm = _tile(n, tm)
    row = lambda i: (i, 0)
    fixed = lambda i: (0, 0)
    in_specs = [pl.BlockSpec((tm, D_MODEL), row)]
    args = [res]
    for a, w in pairs:
        in_specs += [pl.BlockSpec((tm, a.shape[1]), row), pl.BlockSpec(w.shape, fixed)]
        args += [a, _mx(w)]
    return pl.pallas_call(
        functools.partial(_outproj_kernel, n_in=len(pairs)),
        grid=(n // tm,),
        in_specs=in_specs,
        out_specs=pl.BlockSpec((tm, D_MODEL), row),
        out_shape=jax.ShapeDtypeStruct((n, D_MODEL), F32),
        compiler_params=_cparams(("parallel",)),
        name="outproj",
    )(*args)


def _stack_streams(q):
    lane = _lane_iota(q.shape)
    zero = jnp.zeros_like(q)
    return jnp.concatenate([jnp.where(lane < HEAD_DIM, q, zero), jnp.where(lane >= HEAD_DIM, q, zero)], axis=0)


def _flash_stacked(q2, kt_ref, v_ref, i, tq, tk, bias_fn, m_ref, l_ref, acc_ref):
    rows = 2 * tq
    m_ref[...] = jnp.full(m_ref.shape, NEG, F32)
    l_ref[...] = jnp.zeros(l_ref.shape, F32)
    acc_ref[...] = jnp.zeros(acc_ref.shape, F32)

    def tile(j, masked):
        off = pl.multiple_of(j * tk, tk)
        s = _dot(q2, kt_ref[:, pl.ds(off, tk)]) + bias_fn(j, off)
        if masked:
            r = _row_iota((rows, tk))
            s = jnp.where(_lane_iota((rows, tk)) <= jnp.where(r >= tq, r - tq, r), s, NEG)
        alpha, p = _online_update(s, m_ref, l_ref)
        acc_ref[...] = alpha * acc_ref[...] + _dot(_mx(p), v_ref[pl.ds(off, tk), :])

    def body(j, carry):
        tile(j, False)
        return carry

    lax.fori_loop(0, i, body, 0)
    tile(i, True)


def _attn_a_kernel(lamv_ref, gsub_ref, q_ref, k_ref, v_ref, bias_ref, o_ref,
                   m_ref, l_ref, acc_ref, *, tq, tk, lam_init):
    i = pl.program_id(2)

    def bias_fn(j, off):
        b = bias_ref[jnp.minimum(i - j, 2)]
        return jnp.concatenate([b, b], axis=0)

    _flash_stacked(_stack_streams(q_ref[...]), k_ref, v_ref, i, tq, tk, bias_fn, m_ref, l_ref, acc_ref)
    lam = _lambda(lamv_ref, lam_init)
    o = acc_ref[:tq] / l_ref[:tq] - lam * (acc_ref[tq:] / l_ref[tq:])
    o = _rms_rows(o, gsub_ref[...]) * (1.0 - lam_init)
    o_ref[...] = _mx(o)


def _attn_a_prompt(qa, ka, va, lamv, gsub, bias_tiles, nb, s, lam_init, tq=256):
    tk = tq
    nq = s // tq
    kern = functools.partial(_attn_a_kernel, tq=tq, tk=tk, lam_init=lam_init)
    return pl.pallas_call(
        kern,
        grid=(nb, A_HEADS, nq),
        in_specs=[pl.BlockSpec((4, HEAD_DIM), lambda b, h, i: (0, 0)),
                  pl.BlockSpec((1, LANES), lambda b, h, i: (0, 0)),
                  pl.BlockSpec((tq, LANES), lambda b, h, i: (b * nq + i, h)),
                  pl.BlockSpec((None, LANES, s), lambda b, h, i: (b, h, 0)),
                  pl.BlockSpec((s, LANES), lambda b, h, i: (b, h)),
                  pl.BlockSpec((None, 3, tq, tk), lambda b, h, i: (h, 0, 0, 0))],
        out_specs=pl.BlockSpec((tq, LANES), lambda b, h, i: (b * nq + i, h)),
        out_shape=jax.ShapeDtypeStruct((nb * s, A_HEADS * LANES), MXU_DT),
        scratch_shapes=[pltpu.VMEM((2 * tq, LANES), F32), pltpu.VMEM((2 * tq, LANES), F32),
                        pltpu.VMEM((2 * tq, LANES), F32)],
        compiler_params=_cparams(("parallel", "parallel", "arbitrary")),
        name="attn_a_prompt",
    )(lamv, gsub, qa, ka, va, bias_tiles)


def _attn_c_kernel(q_ref, k_ref, v_ref, cq_ref, ckt_ref, o_ref, m_ref, l_ref, acc_ref, *, tq, tk):
    hp = pl.program_id(1)
    i = pl.program_id(2)
    cq_tile = cq_ref[...]
    lane16 = _lane_iota(cq_tile.shape)
    cq2 = jnp.concatenate(
        [jnp.broadcast_to(jnp.sum(jnp.where(lane16 == 2 * hp + c, cq_tile, 0.0), axis=-1, keepdims=True),
                          (tq, LANES)) for c in range(2)], axis=0)

    def bias_fn(j, off):
        ck = ckt_ref[:, pl.ds(off, tk)]
        ck2 = jnp.concatenate([jnp.broadcast_to(ck[0:1], (tq, tk)), jnp.broadcast_to(ck[1:2], (tq, tk))], axis=0)
        return _lanes(cq2, tk) - ck2

    _flash_stacked(_stack_streams(q_ref[...]), k_ref, v_ref, i, tq, tk, bias_fn, m_ref, l_ref, acc_ref)
    lane = _lane_iota((tq, LANES))
    o = jnp.where(lane < HEAD_DIM, acc_ref[:tq] / l_ref[:tq], acc_ref[tq:] / l_ref[tq:])
    o_ref[...] = _mx(o)


def _attn_c_prompt(q, k, v, cq, ckt, nb, s, tq=256):
    tk = tq
    nq = s // tq
    nhp = C_HEADS // 2
    kern = functools.partial(_attn_c_kernel, tq=tq, tk=tk)
    return pl.pallas_call(
        kern,
        grid=(nb, nhp, nq),
        in_specs=[pl.BlockSpec((tq, LANES), lambda b, h, i: (b * nq + i, h)),
                  pl.BlockSpec((None, LANES, s), lambda b, h, i: (b, h, 0)),
                  pl.BlockSpec((s, LANES), lambda b, h, i: (b, h)),
                  pl.BlockSpec((tq, C_HEADS), lambda b, h, i: (b * nq + i, 0)),
                  pl.BlockSpec((None, None, 2, s), lambda b, h, i: (b, h, 0, 0))],
        out_specs=pl.BlockSpec((tq, LANES), lambda b, h, i: (b * nq + i, h)),
        out_shape=jax.ShapeDtypeStruct((nb * s, C_HEADS * HEAD_DIM), MXU_DT),
        scratch_shapes=[pltpu.VMEM((2 * tq, LANES), F32), pltpu.VMEM((2 * tq, LANES), F32),
                        pltpu.VMEM((2 * tq, LANES), F32)],
        compiler_params=_cparams(("parallel", "parallel", "arbitrary")),
        name="attn_c_prompt",
    )(q, k, v, cq, ckt)


def _score_keys(score):
    score = jnp.where(score == 0.0, 0.0, score)
    bits = pltpu.bitcast(score, I32)
    return bits ^ (jnp.right_shift(bits, 31) & 0x7FFFFFFF)


def _topk_select(keys_ref, width, kcount, active, col):
    kf = float(kcount)
    nbits_col = int(width - 1).bit_length()

    def count(pred):
        return jnp.sum(jnp.where(pred, 1.0, 0.0), axis=1, keepdims=True)

    t0 = jnp.where(count(keys_ref[:, :width] >= 0) >= kf, 0, INT_MIN).astype(I32)

    def body(it, t):
        cand = t + jnp.left_shift(jnp.int32(1), 30 - it)
        return jnp.where(count(keys_ref[:, :width] >= cand) >= kf, cand, t)

    t = lax.fori_loop(0, 31, body, t0)
    t = jnp.where(active, t, INT_MIN)
    keys = keys_ref[:, :width]
    gt = keys > t
    eq = keys == t
    need = kf - count(gt)
    excess = jnp.where(active, count(eq) - need, 0.0)

    def tie_break():
        def tb(it, jj):
            cand = jj + jnp.left_shift(jnp.int32(1), nbits_col - 1 - it)
            c = count((keys_ref[:, :width] == t) & (col < cand))
            return jnp.where(c < need, cand, jj)
        return lax.fori_loop(0, nbits_col, tb, jnp.zeros(t.shape, I32))

    jmax = lax.cond(jnp.max(excess) > 0.0, tie_break, lambda: jnp.full(t.shape, width, I32))
    return gt | (eq & (col <= jmax))


KEY_OF_NEG_INF = -2139095041
SELECT_UNIT = 2


def _attn_b_kernel(qb_ref, qi_ref, w_ref, kk_ref, vv_ref, kiki_ref, bias_ref, o_ref,
                   keys_ref, selm_ref, m_ref, l_ref, acc_ref, *, tq, s_len, topk):
    i = pl.program_id(1)
    cw = tq
    nh = B_HEADS
    n_chunks = s_len // cw
    lane = _lane_iota((tq, LANES))
    halves = (lane < HEAD_DIM, lane >= HEAD_DIM)

    def stack_heads(ref):
        parts = []
        for h in range(nh):
            t = ref[:, (h // 2) * LANES:(h // 2 + 1) * LANES]
            parts.append(jnp.where(halves[h % 2], t, jnp.zeros_like(t)))
        return jnp.concatenate(parts, axis=0)

    row = _row_iota((tq, cw)) + i * tq
    colc = _lane_iota((tq, cw))

    qi8 = stack_heads(qi_ref)
    wt = w_ref[...] * IDX_HEAD_SCALE
    wcol = jnp.concatenate([jnp.broadcast_to(wt[:, HEAD_DIM + h:HEAD_DIM + h + 1], (tq, LANES))
                            for h in range(nh)], axis=0)

    def score_chunk(j, carry):
        off = pl.multiple_of(j * cw, cw)
        d = jnp.maximum(_dot_t(qi8, kiki_ref[pl.ds(off, cw), :]), 0.0) * _lanes(wcol, cw)
        sc = d[0:tq]
        for h in range(1, nh):
            sc = sc + d[h * tq:(h + 1) * tq]
        keys_ref[:, pl.ds(off, cw)] = _score_keys(jnp.where((colc + j * cw) <= row, sc, -jnp.inf))
        return carry

    lax.fori_loop(0, i + 1, score_chunk, 0)
    unit = SELECT_UNIT if n_chunks % SELECT_UNIT == 0 else 1
    rem = (i + 1) % unit
    for u in range(1, unit):
        @pl.when((rem != 0) & (u <= unit - rem))
        def _():
            off = pl.multiple_of((i + u) * cw, cw)
            keys_ref[:, pl.ds(off, cw)] = jnp.full((tq, cw), KEY_OF_NEG_INF, I32)
    widths = [w * unit * cw for w in range(1, n_chunks // unit + 1)]

    qpos = _row_iota((tq, 1)) + i * tq

    def select_branch(width):
        def br():
            colw = _lane_iota((tq, width))
            causal = colw <= (_row_iota((tq, width)) + i * tq)
            sel = _topk_select(keys_ref, width, topk, qpos >= topk, colw)
            selm_ref[:, :width] = jnp.where(sel & causal, 0.0, NEG)
        return br

    def causal_only():
        col = _lane_iota((tq, s_len))
        selm_ref[...] = jnp.where(col <= (_row_iota((tq, s_len)) + i * tq), 0.0, NEG)

    branch = jnp.where((i + 1) * tq > topk, 1 + i // unit, 0)
    lax.switch(branch, [causal_only] + [select_branch(w) for w in widths])

    qb8 = stack_heads(qb_ref)
    m_ref[...] = jnp.full(m_ref.shape, NEG, F32)
    l_ref[...] = jnp.zeros(l_ref.shape, F32)
    acc_ref[...] = jnp.zeros(acc_ref.shape, F32)

    def attn_chunk(j, carry):
        off = pl.multiple_of(j * cw, cw)
        bidx = jnp.minimum(i - j, 2)
        bias = jnp.concatenate([bias_ref[h, bidx] for h in range(nh)], axis=0)
        selm = selm_ref[:, pl.ds(off, cw)]
        s = _dot_t(qb8, kk_ref[pl.ds(off, cw), :]) + bias + jnp.concatenate([selm] * nh, axis=0)
        alpha, p = _online_update(s, m_ref, l_ref)
        acc_ref[...] = alpha * acc_ref[...] + _dot(_mx(p), vv_ref[pl.ds(off, cw), :])
        return carry

    lax.fori_loop(0, i + 1, attn_chunk, 0)
    o = acc_ref[...] / l_ref[...]
    for hp in range(nh // 2):
        o_ref[:, hp * LANES:(hp + 1) * LANES] = _mx(
            jnp.where(halves[0], o[2 * hp * tq:(2 * hp + 1) * tq], o[(2 * hp + 1) * tq:(2 * hp + 2) * tq]))


def _attn_b_prompt(qb, qi, small, kk, vv, kiki, bias_tiles, nb, s, topk, tq=256):
    nq = s // tq
    kern = functools.partial(_attn_b_kernel, tq=tq, s_len=s, topk=topk)
    qrow = lambda b, i: (b * nq + i, 0)
    kv = lambda b, i: (b, 0)
    rows = B_HEADS * tq
    return pl.pallas_call(
        kern,
        grid=(nb, nq),
        in_specs=[pl.BlockSpec((tq, B_HEADS * HEAD_DIM), qrow),
                  pl.BlockSpec((tq, IDX_HEADS * IDX_DIM), qrow),
                  pl.BlockSpec((tq, LANES), lambda b, i: (b * nq + i, 1)),
                  pl.BlockSpec((s, LANES), kv), pl.BlockSpec((s, LANES), kv), pl.BlockSpec((s, LANES), kv),
                  pl.BlockSpec(bias_tiles.shape, lambda b, i: (0, 0, 0, 0))],
        out_specs=pl.BlockSpec((tq, B_HEADS * HEAD_DIM), qrow),
        out_shape=jax.ShapeDtypeStruct((nb * s, B_HEADS * HEAD_DIM), MXU_DT),
        scratch_shapes=[pltpu.VMEM((tq, s), I32), pltpu.VMEM((tq, s), F32),
                        pltpu.VMEM((rows, LANES), F32), pltpu.VMEM((rows, LANES), F32),
                        pltpu.VMEM((rows, LANES), F32)],
        compiler_params=_cparams(("parallel", "arbitrary")),
        name="attn_b_prompt",
    )(qb, qi, small, kk, vv, kiki, bias_tiles)


def _cumsum_rows_kernel(x_ref, tri_ref, o_ref, carry_ref):
    j = pl.program_id(1)

    @pl.when(j == 0)
    def _():
        carry_ref[...] = jnp.zeros(carry_ref.shape, F32)

    tri = tri_ref[...]
    hi, mid, lo = _split3(x_ref[...])
    cum = _dot(tri, hi) + _dot(tri, mid) + _dot(tri, lo) + carry_ref[...]
    o_ref[...] = cum
    carry_ref[...] = cum[-1:, :]


def _cumsum_prompt(x, nb, s, blk=128):
    nj = s // blk
    w = x.shape[1]
    r = np.arange(blk)
    tri = jnp.asarray(r[:, None] >= r[None, :], dtype=MXU_DT)
    return pl.pallas_call(
        _cumsum_rows_kernel,
        grid=(nb, nj),
        in_specs=[pl.BlockSpec((blk, w), lambda b, j: (b * nj + j, 0)),
                  pl.BlockSpec((blk, blk), lambda b, j: (0, 0))],
        out_specs=pl.BlockSpec((blk, w), lambda b, j: (b * nj + j, 0)),
        out_shape=jax.ShapeDtypeStruct((nb * s, w), F32),
        scratch_shapes=[pltpu.VMEM((1, w), F32)],
        compiler_params=_cparams(("parallel", "arbitrary")),
        name="cumsum_prompt",
    )(x, tri)


def _cumsum_pages_kernel(*refs, n_pages, ps):
    pt_ref = refs[0]
    page_refs = refs[1:1 + n_pages]
    tri_ref, o_ref = refs[1 + n_pages:]
    tri = tri_ref[...]
    carry = jnp.zeros((page_refs[0].shape[0], 1), F32)
    for j in range(n_pages):
        hi, mid, lo = _split3(page_refs[j][...])
        cum = _dot(hi, tri) + _dot(mid, tri) + _dot(lo, tri) + carry
        o_ref[:, j * ps:(j + 1) * ps] = cum
        carry = cum[:, ps - 1:ps]


def _cumsum_pages(logf_t, layer, page_table):
    db, npg = page_table.shape
    nh, ps = logf_t.shape[2], logf_t.shape[3]
    r = np.arange(ps)
    tri = jnp.asarray(r[:, None] <= r[None, :], dtype=MXU_DT)
    page_specs = [pl.BlockSpec((None, None, nh, ps), lambda b, pt, j=j: (layer, pt[b * npg + j], 0, 0))
                  for j in range(npg)]
    grid_spec = pltpu.PrefetchScalarGridSpec(
        num_scalar_prefetch=1,
        grid=(db,),
        in_specs=page_specs + [pl.BlockSpec((ps, ps), lambda b, pt: (0, 0))],
        out_specs=pl.BlockSpec((None, nh, npg * ps), lambda b, pt: (b, 0, 0)),
    )
    return pl.pallas_call(
        functools.partial(_cumsum_pages_kernel, n_pages=npg, ps=ps),
        grid_spec=grid_spec,
        out_shape=jax.ShapeDtypeStruct((db, nh, npg * ps), F32),
        compiler_params=_cparams(("parallel",)),
        name="cumsum_pages",
    )(page_table.reshape(-1), *([logf_t] * npg), tri)


def _online_update(s, m_ref, l_ref):
    m_old = m_ref[...]
    m_new = jnp.maximum(m_old, jnp.max(s, axis=-1, keepdims=True))
    alpha = jnp.exp(m_old - m_new)
    p = jnp.exp(s - _lanes(m_new, s.shape[1]))
    l_ref[...] = alpha * l_ref[...] + jnp.sum(p, axis=-1, keepdims=True)
    m_ref[...] = m_new
    return alpha, p


def _decode0_kernel(*refs, n_steps, npp, ps, topk, lam_init, group):
    (pt_ref, lamv_ref, gsub_ref, qa_ref, qi_ref, qb_ref, w8_ref, kna_ref, vna_ref, bnew_ref,
     taba_ref, tabb_ref, validb_ref) = refs[:13]
    pages = refs[13:13 + 5 * npp]
    cak, cav, cbk, cbv, cbi = (pages[0:npp], pages[npp:2 * npp], pages[2 * npp:3 * npp],
                               pages[3 * npp:4 * npp], pages[4 * npp:5 * npp])
    oa_ref, ob_ref, m_ref, l_ref, acc_ref, kb_ref, vb_ref, ki_ref, keys_ref = refs[13 + 5 * npp:]
    g = pl.program_id(0) % group
    step = pl.program_id(1)
    past = n_steps * npp * ps
    lk = kb_ref.shape[2]

    @pl.when(step == 0)
    def _():
        m_ref[...] = jnp.full(m_ref.shape, NEG, F32)
        l_ref[...] = jnp.zeros(l_ref.shape, F32)
        acc_ref[...] = jnp.zeros(acc_ref.shape, F32)

    qa = qa_ref[...]
    far = taba_ref[0]
    tail = taba_ref[jnp.where(step == n_steps - 1, 1, 0)]
    s = jnp.concatenate([_dot(qa, _mx(cak[j][...])) + (tail if j == npp - 1 else far) for j in range(npp)], axis=1)
    alpha, p = _online_update(s, m_ref, l_ref)
    p = _mx(p)
    for h in range(A_HEADS):
        rows = slice(h * 2 * T8, (h + 1) * 2 * T8)
        upd = alpha[rows] * acc_ref[rows]
        for j in range(npp):
            vh = _mx(cav[j][pl.ds(h, ps, stride=A_HEADS), :])
            upd = upd + _dot(p[rows, j * ps:(j + 1) * ps], vh)
        acc_ref[rows] = upd

    for j in range(npp):
        off = pl.multiple_of((step * npp + j) * ps, ps)
        kb_ref[g, :, pl.ds(off, ps)] = _mx(cbk[j][...])
        vb_ref[g, :, pl.ds(off, ps)] = _mx(cbv[j][...])
        ki_ref[g, :, pl.ds(off, ps)] = _mx(cbi[j][...])

    @pl.when(step == n_steps - 1)
    def _():
        s_new = _dot_t(qa, _mx(kna_ref[...])) + taba_ref[2][:, :kna_ref.shape[0]]
        alpha2, p2 = _online_update(s_new, m_ref, l_ref)
        vn = _mx(vna_ref[...])
        lam = _lambda(lamv_ref, lam_init)
        for h in range(A_HEADS):
            rows = slice(h * 2 * T8, (h + 1) * 2 * T8)
            o16 = (alpha2[rows] * acc_ref[rows] + _dot(_mx(p2[rows]), vn[:, h * LANES:(h + 1) * LANES])) / l_ref[rows]
            o = o16[:T8] - lam * o16[T8:]
            oa_ref[h] = _rms_rows(o, gsub_ref[...]) * (1.0 - lam_init)

        bn = bnew_ref[g]
        kb_ref[g, :, past:past + LANES] = _mx(bn[0])
        vb_ref[g, :, past:past + LANES] = _mx(bn[1])
        ki_ref[g, :, past:past + LANES] = _mx(bn[2])

    @pl.when((step == n_steps - 1) & (g == group - 1))
    def _():
        valid = validb_ref[...] == 0.0
        for gg in range(group):
            dots = jnp.maximum(_dot(qi_ref[gg], ki_ref[gg]), 0.0)
            w8 = w8_ref[gg] * IDX_HEAD_SCALE
            score = jnp.zeros((T8, lk), F32)
            for h in range(IDX_HEADS):
                score = score + w8[:, h:h + 1] * dots[h * T8:(h + 1) * T8]
            keys_ref[gg * T8:(gg + 1) * T8, :] = _score_keys(jnp.where(valid, score, -jnp.inf))
        rows = group * T8
        sel = _topk_select(keys_ref, lk, topk, jnp.full((rows, 1), True), _lane_iota((rows, lk)))
        selm = jnp.where(sel & jnp.concatenate([valid] * group, axis=0), 0.0, NEG)
        for gg in range(group):
            sg = selm[gg * T8:(gg + 1) * T8]
            sb = _dot(qb_ref[gg], kb_ref[gg]) + tabb_ref[...] + jnp.concatenate([sg] * B_HEADS, axis=0)
            mb = jnp.max(sb, axis=-1, keepdims=True)
            pb = jnp.exp(sb - mb)
            lb = jnp.sum(pb, axis=-1, keepdims=True)
            ob = _dot_t(_mx(pb), vb_ref[gg]) / lb
            for h in range(B_HEADS):
                ob_ref[gg, h] = ob[h * T8:(h + 1) * T8]


def _decode0(page_table, lamv, gsub, qa_bd, qi64, qb64, w8, kna, vna, bnew_t,
             cak_t, cav_r, cbk_t, cbv_t, cbi_t, taba, tabb, validb, layer, topk, lam_init):
    db, npg = page_table.shape
    ps = cak_t.shape[3]
    npp = PAGES_PER_STEP if npg % PAGES_PER_STEP == 0 else 1
    n_steps = npg // npp
    lk = tabb.shape[1]
    group = max(gsz for gsz in (8, 4, 2, 1) if db % gsz == 0)
    per_b = lambda b, p, pt: (b, 0, 0)
    per_g = lambda b, p, pt: (b // group, 0, 0)
    per_g4 = lambda b, p, pt: (b // group, 0, 0, 0)
    fixed2 = lambda b, p, pt: (0, 0)
    fixed3 = lambda b, p, pt: (0, 0, 0)

    def page_specs(arr):
        blk = (None, None) + arr.shape[2:]
        return [pl.BlockSpec(blk, lambda b, p, pt, j=j: (layer, pt[b * npg + p * npp + j], 0, 0)) for j in range(npp)]

    caches = (cak_t, cav_r, cbk_t, cbv_t, cbi_t)
    kern = functools.partial(_decode0_kernel, n_steps=n_steps, npp=npp, ps=ps, topk=topk, lam_init=lam_init,
                             group=group)
    grid_spec = pltpu.PrefetchScalarGridSpec(
        num_scalar_prefetch=1,
        grid=(db, n_steps),
        in_specs=[pl.BlockSpec((4, HEAD_DIM), fixed2), pl.BlockSpec((1, LANES), fixed2),
                  pl.BlockSpec((None,) + qa_bd.shape[1:], per_b),
                  pl.BlockSpec((group,) + qi64.shape[1:], per_g),
                  pl.BlockSpec((group,) + qb64.shape[1:], per_g),
                  pl.BlockSpec((group,) + w8.shape[1:], per_g),
                  pl.BlockSpec((None,) + kna.shape[1:], per_b),
                  pl.BlockSpec((None,) + vna.shape[1:], per_b),
                  pl.BlockSpec((group,) + bnew_t.shape[1:], per_g4),
                  pl.BlockSpec(taba.shape, fixed3), pl.BlockSpec(tabb.shape, fixed2),
                  pl.BlockSpec(validb.shape, fixed2)]
                 + [sp for c in caches for sp in page_specs(c)],
        out_specs=[pl.BlockSpec((None, A_HEADS, T8, LANES), lambda b, p, pt: (b, 0, 0, 0)),
                   pl.BlockSpec((group, B_HEADS, T8, HEAD_DIM), per_g4)],
        scratch_shapes=[pltpu.VMEM((64, LANES), F32), pltpu.VMEM((64, LANES), F32),
                        pltpu.VMEM((64, LANES), F32),
                        pltpu.VMEM((group, HEAD_DIM, lk), MXU_DT), pltpu.VMEM((group, HEAD_DIM, lk), MXU_DT),
                        pltpu.VMEM((group, IDX_DIM, lk), MXU_DT), pltpu.VMEM((group * T8, lk), I32)],
    )
    return pl.pallas_call(
        kern,
        grid_spec=grid_spec,
        out_shape=[jax.ShapeDtypeStruct((db, A_HEADS, T8, LANES), F32),
                   jax.ShapeDtypeStruct((db, B_HEADS, T8, HEAD_DIM), F32)],
        compiler_params=_cparams(("arbitrary", "arbitrary")),
        name="decode0",
    )(page_table.reshape(-1), lamv, gsub, qa_bd, qi64, qb64, w8, kna, vna, bnew_t, taba, tabb, validb,
      *[c for c in caches for _ in range(npp)])


def _expand_rows(x):
    hh, ww = x.shape
    return jnp.broadcast_to(x[:, None, :], (hh, T8, ww)).reshape(hh * T8, ww)


def _decode1_kernel(*refs, n_steps, npp, ps):
    pt_ref, q_ref, cq_ref, kn_ref, vn_ref, cnew_ref, ckt_ref, maskn_ref = refs[:8]
    ck = refs[8:8 + npp]
    cv = refs[8 + npp:8 + 2 * npp]
    o_ref, m_ref, l_ref, acc_ref = refs[8 + 2 * npp:]
    step = pl.program_id(1)

    @pl.when(step == 0)
    def _():
        m_ref[...] = jnp.full(m_ref.shape, NEG, F32)
        l_ref[...] = jnp.zeros(l_ref.shape, F32)
        acc_ref[...] = jnp.zeros(acc_ref.shape, F32)

    q = q_ref[...]
    cq = jnp.broadcast_to(cq_ref[...], (q.shape[0], LANES))
    s = jnp.concatenate([_dot(q, _mx(ck[j][...])) for j in range(npp)], axis=1)
    s = s + _lanes(cq, npp * ps) - _expand_rows(ckt_ref[...])
    alpha, p = _online_update(s, m_ref, l_ref)
    p = _mx(p)
    upd = _lanes(alpha, acc_ref.shape[1]) * acc_ref[...]
    for j in range(npp):
        upd = upd + _dot_t(p[:, j * ps:(j + 1) * ps], _mx(cv[j][...]))
    acc_ref[...] = upd

    @pl.when(step == n_steps - 1)
    def _():
        nn = kn_ref.shape[0]
        s_new = (_dot_t(q, _mx(kn_ref[...])) + cq[:, :nn] - _expand_rows(cnew_ref[...]) + maskn_ref[...])
        alpha2, p2 = _online_update(s_new, m_ref, l_ref)
        o_full = ((_lanes(alpha2, acc_ref.shape[1]) * acc_ref[...] + _dot(_mx(p2), _mx(vn_ref[...])))
                  / _lanes(l_ref[...], acc_ref.shape[1]))
        for h in range(C_HEADS):
            o_ref[h] = o_full[h * T8:(h + 1) * T8, h * HEAD_DIM:(h + 1) * HEAD_DIM]


def _decode1(page_table, q_bd, cq, kn, vn, cnew_t, cck_t, ccv_t, ckt, maskn, layer):
    db, npg = page_table.shape
    ps = cck_t.shape[3]
    npp = PAGES_PER_STEP if npg % PAGES_PER_STEP == 0 else 1
    n_steps = npg // npp
    per_b = lambda b, p, pt: (b, 0, 0)

    def page_specs(arr):
        blk = (None, None) + arr.shape[2:]
        return [pl.BlockSpec(blk, lambda b, p, pt, j=j: (layer, pt[b * npg + p * npp + j], 0, 0)) for j in range(npp)]

    kern = functools.partial(_decode1_kernel, n_steps=n_steps, npp=npp, ps=ps)
    rows = C_HEADS * T8
    grid_spec = pltpu.PrefetchScalarGridSpec(
        num_scalar_prefetch=1,
        grid=(db, n_steps),
        in_specs=[pl.BlockSpec((None,) + q_bd.shape[1:], per_b),
                  pl.BlockSpec((None,) + cq.shape[1:], per_b),
                  pl.BlockSpec((None,) + kn.shape[1:], per_b),
                  pl.BlockSpec((None,) + vn.shape[1:], per_b),
                  pl.BlockSpec((None,) + cnew_t.shape[1:], per_b),
                  pl.BlockSpec((None, C_HEADS, npp * ps), lambda b, p, pt: (b, 0, p)),
                  pl.BlockSpec(maskn.shape, lambda b, p, pt: (0, 0))]
                 + page_specs(cck_t) + page_specs(ccv_t),
        out_specs=pl.BlockSpec((None, C_HEADS, T8, HEAD_DIM), lambda b, p, pt: (b, 0, 0, 0)),
        scratch_shapes=[pltpu.VMEM((rows, LANES), F32), pltpu.VMEM((rows, LANES), F32),
                        pltpu.VMEM((rows, C_HEADS * HEAD_DIM), F32)],
    )
    return pl.pallas_call(
        kern,
        grid_spec=grid_spec,
        out_shape=jax.ShapeDtypeStruct((db, C_HEADS, T8, HEAD_DIM), F32),
        compiler_params=_cparams(("parallel", "arbitrary")),
        name="decode1",
    )(page_table.reshape(-1), q_bd, cq, kn, vn, cnew_t, ckt, maskn, *([cck_t] * npp), *([ccv_t] * npp))


def _router_kernel(x_ref, g_ref, whi_ref, wlo_ref, b_ref, t_ref, route_ref):
    t = _rms_rows(x_ref[...], g_ref[...])
    thi, tlo = _split2(t)
    t_ref[...] = thi
    whi = whi_ref[...]
    logits = _dot(thi, whi) + _dot(tlo, whi) + _dot(thi, wlo_ref[...]) + b_ref[...]
    lane = _lane_iota(logits.shape)
    big = jnp.int32(1 << 20)

    def first_max(v):
        mx = jnp.max(v, axis=-1, keepdims=True)
        idx = jnp.min(jnp.where(v == mx, lane, big), axis=-1, keepdims=True)
        return mx, idx

    glog = jnp.where(lane < N_GROUPS, logits, -jnp.inf)
    gmax, gidx = first_max(glog)
    grp_w = 1.0 / jnp.sum(jnp.exp(glog - gmax), axis=-1, keepdims=True)
    el = lane - N_GROUPS
    in_grp = (el >= 0) & (el < N_EXPERTS) & (jnp.right_shift(el, 3) == gidx)
    v1 = jnp.where(in_grp, logits, -jnp.inf)
    top1, i1 = first_max(v1)
    v2 = jnp.where(lane == i1, -jnp.inf, v1)
    top2, i2 = first_max(v2)
    e2 = jnp.exp(top2 - top1)
    w1 = grp_w / (1.0 + e2)
    w2 = grp_w * e2 / (1.0 + e2)
    route = jnp.where(lane == 0, (i1 - N_GROUPS).astype(F32),
                      jnp.where(lane == 1, (i2 - N_GROUPS).astype(F32),
                                jnp.where(lane == 2, w1, jnp.where(lane == 3, w2, 0.0))))
    route_ref[...] = route


def _router(x, g, w_group, b_group, w_router, b_router, tm=512):
    n = x.shape[0]
    tm = _tile(n, tm)
    pad = LANES - N_GROUPS - N_EXPERTS
    w = jnp.concatenate([w_group, w_router, jnp.zeros((D_MODEL, pad), F32)], axis=1)
    whi = _mx(w)
    wlo = _mx(w - whi.astype(F32))
    b = jnp.concatenate([b_group, b_router, jnp.zeros((pad,), F32)])[None, :]
    row = lambda i: (i, 0)
    fixed = lambda i: (0, 0)
    return pl.pallas_call(
        _router_kernel,
        grid=(n // tm,),
        in_specs=[pl.BlockSpec((tm, D_MODEL), row), pl.BlockSpec((1, D_MODEL), fixed),
                  pl.BlockSpec((D_MODEL, LANES), fixed), pl.BlockSpec((D_MODEL, LANES), fixed),
                  pl.BlockSpec((1, LANES), fixed)],
        out_specs=[pl.BlockSpec((tm, D_MODEL), row), pl.BlockSpec((tm, LANES), row)],
        out_shape=[jax.ShapeDtypeStruct((n, D_MODEL), MXU_DT), jax.ShapeDtypeStruct((n, LANES), F32)],
        compiler_params=_cparams(("parallel",)),
        name="moe_router",
    )(x, g[None, :], whi, wlo, b)


def _moe_dense_kernel(x_ref, t_ref, route_ref, wgu_ref, wd_ref, o_ref):
    e = pl.program_id(1)

    @pl.when(e == 0)
    def _():
        o_ref[...] = x_ref[...]

    au = _dot(t_ref[...], wgu_ref[...])
    a = au[:, :EXPERT_FF]
    u = au[:, EXPERT_FF:]
    r = route_ref[...]
    ef = e.astype(F32)
    gate = jnp.where(r[:, 0:1] == ef, r[:, 2:3], 0.0) + jnp.where(r[:, 1:2] == ef, r[:, 3:4], 0.0)
    hdn = a * (1.0 / (1.0 + jnp.exp(-a))) * u * gate
    o_ref[...] += _dot(_mx(hdn), wd_ref[...])


def _moe_dense(x, t, route, w_gate, w_up, w_down, tm=1536):
    n = x.shape[0]
    tm = _tile(n, tm)
    wgu = _mx(jnp.concatenate([w_gate.reshape(N_EXPERTS, D_MODEL, EXPERT_FF),
                               w_up.reshape(N_EXPERTS, D_MODEL, EXPERT_FF)], axis=-1))
    wd = _mx(w_down.reshape(N_EXPERTS, EXPERT_FF, D_MODEL))
    row = lambda i, e: (i, 0)
    once = pl.Buffered(1)
    return pl.pallas_call(
        _moe_dense_kernel,
        grid=(n // tm, N_EXPERTS),
        in_specs=[pl.BlockSpec((tm, D_MODEL), row, pipeline_mode=once),
                  pl.BlockSpec((tm, D_MODEL), row, pipeline_mode=once),
                  pl.BlockSpec((tm, LANES), row, pipeline_mode=once),
                  pl.BlockSpec((None, D_MODEL, 2 * EXPERT_FF), lambda i, e: (e, 0, 0)),
                  pl.BlockSpec((None, EXPERT_FF, D_MODEL), lambda i, e: (e, 0, 0))],
        out_specs=pl.BlockSpec((tm, D_MODEL), row),
        out_shape=jax.ShapeDtypeStruct((n, D_MODEL), F32),
        compiler_params=_cparams(("parallel", "arbitrary")),
        name="moe_dense",
    )(x, t, route, wgu, wd)


def _hier_moe(x, g, w_group, b_group, w_router, b_router, w_gate, w_up, w_down):
    t, route = _router(x, g, w_group, b_group, w_router, b_router)
    return _moe_dense(x, t, route, w_gate, w_up, w_down)


def _decode_tables(rel_bias, past, t_new, lk):
    t8 = np.arange(T8)
    lane = np.arange(LANES)
    far = np.full((T8, LANES), REL_BUCKETS - 1, np.int32)
    d_last = (past + t8[:, None]) - (past - LANES + lane[None, :])
    d_new = t8[:, None] - lane[None, :]
    ok_new = (d_new >= 0) & (lane[None, :] < t_new)
    ta = _bias_expand(rel_bias, np.stack([far, _t5_bucket_np(d_last), _t5_bucket_np(d_new)]), 0, A_HEADS)
    ta = jnp.where(jnp.asarray(ok_new)[None, None] | (jnp.arange(3) < 2)[None, :, None, None], ta, NEG)
    taba = jnp.broadcast_to(jnp.transpose(ta, (1, 0, 2, 3))[:, :, None], (3, A_HEADS, 2, T8, LANES))
    taba = taba.reshape(3, A_HEADS * 2 * T8, LANES)
    kpos = np.arange(lk)
    d_b = (past + t8[:, None]) - kpos[None, :]
    tabb = _bias_expand(rel_bias, _t5_bucket_np(d_b), A_HEADS, B_HEADS).reshape(B_HEADS * T8, lk)
    valid = (kpos[None, :] < past) | ((d_b >= 0) & (kpos[None, :] < past + t_new))
    validb = jnp.asarray(np.where(valid, 0.0, NEG), F32)
    return taba, tabb, validb


def _pad_rows(x, rows):
    pad = [(0, 0)] * x.ndim
    pad[1] = (0, rows - x.shape[1])
    return jnp.pad(x, pad)


def kernel(x_prompt, x_sample, cache_a_k, cache_a_v, cache_b_k, cache_b_v, cache_b_kidx, cache_c_k, cache_c_v, cache_c_logf, page_table, rel_bias, ab_norm, ab_w_in, a_q_norm, a_k_norm, b_q_norm, b_k_norm, a_lambda_q1, a_lambda_k1, a_lambda_q2, a_lambda_k2, a_sub_norm, ab_w_out, c_norm, c_w_in, c_forget_bias, c_q_norm, c_k_norm, c_w_out, ffn_norm, moe_w_group, moe_b_group, moe_w_router, moe_b_router, moe_w_gate, moe_w_up, moe_w_down):
    nb, s, d = x_prompt.shape
    db, ts, _ = x_sample.shape
    npg = page_table.shape[1]
    pool, ps = cache_a_k.shape[1], cache_a_k.shape[2]
    past = npg * ps
    n_p = nb * s
    n_s = db * ts
    depth = ffn_norm.shape[0]
    topk_p = min(IDX_TOPK_MAX, s // 4)
    topk_s = min(IDX_TOPK_MAX, (past + ts) // 4)
    lk = past + LANES

    cak_t = jnp.transpose(cache_a_k, (0, 1, 3, 4, 5, 2)).reshape(-1, pool, 2 * A_HEADS * HEAD_DIM, ps)
    cav_r = cache_a_v.reshape(-1, pool, ps * A_HEADS, 2 * HEAD_DIM)
    cbk_t = jnp.transpose(cache_b_k, (0, 1, 3, 2))
    cbv_t = jnp.transpose(cache_b_v, (0, 1, 3, 2))
    cbi_t = jnp.transpose(cache_b_kidx, (0, 1, 3, 2))
    cck_t = jnp.transpose(cache_c_k, (0, 1, 3, 4, 2)).reshape(-1, pool, C_HEADS * HEAD_DIM, ps)
    ccv_t = jnp.transpose(cache_c_v, (0, 1, 3, 4, 2)).reshape(-1, pool, C_HEADS * HEAD_DIM, ps)
    ccf_t = jnp.transpose(cache_c_logf, (0, 1, 3, 2))

    x = jnp.concatenate([x_prompt.reshape(n_p, d), x_sample.reshape(n_s, d)], axis=0)
    outs_p = {k: [] for k in ("ak", "av", "bk", "bv", "bi", "ck", "cv", "cf")}
    outs_s = {k: [] for k in ("ak", "av", "bk", "bv", "bi", "ck", "cv", "cf")}

    def smp(a):
        return a[n_p:].reshape(db, ts, a.shape[1])

    for layer in range(depth):
        if layer % 2 == 0:
            e = layer // 2
            lam_init = 0.8 - 0.6 * math.exp(-0.3 * layer)
            lamv = jnp.stack([a_lambda_q1[e], a_lambda_k1[e], a_lambda_q2[e], a_lambda_k2[e]])
            gsub = a_sub_norm[e][None, :]
            (qa, ka_s, kat, katbf, qb, small, smallt, kk, vv, kiki, va, vabf, qi) = _proj0(
                x, ab_norm[e], ab_w_in[e], a_q_norm[e], a_k_norm[e], b_q_norm[e], b_k_norm[e], n_p, s)
            tq_a = 256
            bias_a = _bias_expand(rel_bias, _prompt_bucket_tiles(tq_a), 0, A_HEADS)
            oa_p = _attn_a_prompt(qa, katbf, vabf, lamv, gsub, bias_a, nb, s, lam_init, tq=tq_a)
            tq_b = 256
            bias_b = _bias_expand(rel_bias, _prompt_bucket_tiles(tq_b), A_HEADS, B_HEADS)
            ob_p = _attn_b_prompt(qb, qi, small, kk, vv, kiki, bias_b, nb, s, topk_p, tq=tq_b)
            qa_s = _pad_rows(smp(qa), T8)
            hc = np.arange(2 * A_HEADS)
            colmask = jnp.asarray((np.arange(qa_s.shape[2])[None, :] // HEAD_DIM) == hc[:, None], MXU_DT)
            qa_bd = (qa_s[:, None, :, :] * colmask[None, :, None, :]).reshape(db, 2 * A_HEADS * T8, -1)

            def heads_rows(a, nh):
                a = _pad_rows(a, T8).reshape(db, T8, nh, HEAD_DIM)
                return jnp.transpose(a, (0, 2, 1, 3)).reshape(db, nh * T8, HEAD_DIM)

            qi64 = heads_rows(smp(qi), IDX_HEADS)
            qb64 = heads_rows(smp(qb), B_HEADS)
            small_s = smp(small)
            w8 = _pad_rows(small_s[:, :, 192:200], T8)
            ka_s3 = ka_s.reshape(db, ts, -1)
            va_s3 = smp(va)
            kna = _pad_rows(ka_s3, 16)
            vna = _pad_rows(va_s3, 16)
            bnew_t = jnp.transpose(_pad_rows(small_s[:, :, :192], LANES).reshape(db, LANES, 3, HEAD_DIM), (0, 2, 3, 1))
            taba, tabb, validb = _decode_tables(rel_bias, past, ts, lk)
            oa_d, ob_d = _decode0(page_table, lamv, gsub, qa_bd, qi64, qb64, w8, kna, vna, bnew_t,
                                  cak_t, cav_r, cbk_t, cbv_t, cbi_t, taba, tabb, validb, e, topk_s, lam_init)
            oa_s = jnp.transpose(oa_d[:, :, :ts], (0, 2, 1, 3)).reshape(n_s, A_HEADS * LANES)
            ob_s = jnp.transpose(ob_d[:, :, :ts], (0, 2, 1, 3)).reshape(n_s, B_HEADS * HEAD_DIM)
            oa = jnp.concatenate([oa_p, _mx(oa_s)], axis=0)
            ob = jnp.concatenate([ob_p, _mx(ob_s)], axis=0)
            w_out = ab_w_out[e]
            x = _outproj(x, [(oa, w_out[:A_HEADS * LANES]), (ob, w_out[A_HEADS * LANES:])])
            outs_p["ak"].append(jnp.transpose(kat.reshape(nb, A_HEADS, 2, HEAD_DIM, s), (0, 4, 1, 2, 3)))
            outs_p["av"].append(va[:n_p].reshape(nb, s, A_HEADS, 2 * HEAD_DIM))
            outs_p["bk"].append(jnp.transpose(smallt[:, 0:64], (0, 2, 1)))
            outs_p["bv"].append(jnp.transpose(smallt[:, 64:128], (0, 2, 1)))
            outs_p["bi"].append(jnp.transpose(smallt[:, 128:192], (0, 2, 1)))
            outs_s["ak"].append(ka_s3.reshape(db, ts, A_HEADS, 2, HEAD_DIM))
            outs_s["av"].append(va_s3.reshape(db, ts, A_HEADS, 2 * HEAD_DIM))
            outs_s["bk"].append(small_s[:, :, 0:64])
            outs_s["bv"].append(small_s[:, :, 64:128])
            outs_s["bi"].append(small_s[:, :, 128:192])
        else:
            o = layer // 2
            q, k_s, kt, ktbf, v_s, vt, vbf, logf128, logft = _proj1(
                x, c_norm[o], c_w_in[o], c_forget_bias[o], c_q_norm[o], c_k_norm[o], n_p, s)
            logf = logf128[:, :C_HEADS]
            cum_p = _cumsum_prompt(logf128[:n_p], nb, s)[:, :C_HEADS]
            ckt = jnp.transpose(cum_p.reshape(nb, s, C_HEADS // 2, 2), (0, 2, 3, 1))
            oc_p = _attn_c_prompt(q, ktbf, vbf, cum_p, ckt, nb, s)
            ckt_s = _cumsum_pages(ccf_t, o, page_table)
            logf_s = smp(logf)
            run = ckt_s[:, :, -1]
            c_rows = []
            for t in range(ts):
                run = run + logf_s[:, t]
                c_rows.append(run)
            c_new = jnp.stack(c_rows, axis=1)
            cq = jnp.transpose(_pad_rows(c_new, T8), (0, 2, 1)).reshape(db, C_HEADS * T8, 1)
            cnew_t = jnp.transpose(_pad_rows(c_new, 16), (0, 2, 1))
            q_s = _pad_rows(smp(q), T8)
            hmask = jnp.asarray((np.arange(q_s.shape[2])[None, :] // HEAD_DIM) == np.arange(C_HEADS)[:, None], MXU_DT)
            q_bd = (q_s[:, None, :, :] * hmask[None, :, None, :]).reshape(db, C_HEADS * T8, -1)
            k_s3 = k_s.reshape(db, ts, -1)
            v_s3 = v_s.reshape(db, ts, -1)
            kn = _pad_rows(k_s3, 16)
            vn = _pad_rows(v_s3, 16)
            t8 = np.arange(T8)
            okn = (t8[:, None] >= np.arange(16)[None, :]) & (np.arange(16)[None, :] < ts)
            maskn = jnp.asarray(np.tile(np.where(okn, 0.0, NEG), (C_HEADS, 1)), F32)
            oc_d = _decode1(page_table, q_bd, cq, kn, vn, cnew_t, cck_t, ccv_t, ckt_s, maskn, o)
            oc_s = jnp.transpose(oc_d[:, :, :ts], (0, 2, 1, 3)).reshape(n_s, C_HEADS * HEAD_DIM)
            oc = jnp.concatenate([oc_p, _mx(oc_s)], axis=0)
            x = _outproj(x, [(oc, c_w_out[o])])
            outs_p["ck"].append(jnp.transpose(kt.reshape(nb, C_HEADS, HEAD_DIM, s), (0, 3, 1, 2)))
            outs_p["cv"].append(jnp.transpose(vt.reshape(nb, C_HEADS, HEAD_DIM, s), (0, 3, 1, 2)))
            outs_p["cf"].append(jnp.transpose(logft[:, :C_HEADS], (0, 2, 1)))
            outs_s["ck"].append(k_s3.reshape(db, ts, C_HEADS, HEAD_DIM))
            outs_s["cv"].append(v_s3.reshape(db, ts, C_HEADS, HEAD_DIM))
            outs_s["cf"].append(logf_s)
        x = _hier_moe(x, ffn_norm[layer], moe_w_group[layer], moe_b_group[layer], moe_w_router[layer],
                      moe_b_router[layer], moe_w_gate[layer], moe_w_up[layer], moe_w_down[layer])

    keys = ("ak", "av", "bk", "bv", "bi", "ck", "cv", "cf")
    return ((x[:n_p].reshape(nb, s, d), x[n_p:].reshape(db, ts, d))
            + tuple(jnp.stack(outs_p[k]) for k in keys)
            + tuple(jnp.stack(outs_s[k]) for k in keys))
```

```python
import functools
import math

import numpy as np
import jax
import jax.numpy as jnp
from jax import lax
from jax.experimental import pallas as pl
from jax.experimental.pallas import tpu as pltpu

F32 = jnp.float32
I32 = jnp.int32
MXU_DT = jnp.bfloat16

D_MODEL = 1024
HEAD_DIM = 64
A_HEADS = 4
B_HEADS = 8
IDX_HEADS = 8
IDX_DIM = 64
IDX_TOPK_MAX = 256
C_HEADS = 16
REL_BUCKETS = 32
REL_MAX_DIST = 128
N_GROUPS = 4
EXPERTS_PER_GROUP = 8
N_EXPERTS = N_GROUPS * EXPERTS_PER_GROUP
EXPERT_FF = 256
EPS = 1e-6
NEG = -1e30
INT_MIN = -2 ** 31
QK_SCALE = HEAD_DIM ** -0.5
IDX_SCALE = IDX_DIM ** -0.5
IDX_HEAD_SCALE = IDX_HEADS ** -0.5
LANES = 128
T8 = 8
VMEM_LIMIT = 56 * 1024 * 1024
PAGES_PER_STEP = 8
FLASH_WIDE = 2


def _tile(n, pref):
    best = 16
    for t in range(16, pref + 1, 16):
        if n % t == 0:
            best = t
    assert n % best == 0, (n, pref)
    return best


def _cparams(sem):
    return pltpu.CompilerParams(dimension_semantics=sem, vmem_limit_bytes=VMEM_LIMIT)


def _mx(x):
    return x.astype(MXU_DT)


def _dot(a, b):
    return jnp.dot(a, b, preferred_element_type=F32)


def _dot_t(a, b):
    return lax.dot_general(a, b, (((1,), (1,)), ((), ())), preferred_element_type=F32)


def _split2(x):
    hi = _mx(x)
    lo = _mx(x - hi.astype(F32))
    return hi, lo


def _split3(x):
    hi = _mx(x)
    r = x - hi.astype(F32)
    mid = _mx(r)
    lo = _mx(r - mid.astype(F32))
    return hi, mid, lo


def _lane_iota(shape):
    return lax.broadcasted_iota(I32, shape, len(shape) - 1)


def _row_iota(shape):
    return lax.broadcasted_iota(I32, shape, len(shape) - 2)


def _lanes(x, width):
    if width <= LANES:
        return x[:, :width]
    return jnp.tile(x, (1, width // LANES))


def _rms_rows(x, g):
    ms = jnp.mean(x * x, axis=-1, keepdims=True)
    return x * lax.rsqrt(ms + EPS) * g


def _seg_rsqrt(y, bd):
    hi, lo = _split2(y * y)
    ss = _dot(hi, bd) + _dot(lo, bd)
    return lax.rsqrt(ss * (1.0 / HEAD_DIM) + EPS)


def _block_diag_ones(n, seg):
    r = np.arange(n)
    return jnp.asarray((r[:, None] // seg) == (r[None, :] // seg), dtype=MXU_DT)


def _lambda(lamv_ref, lam_init):
    lv = lamv_ref[...]
    return (jnp.exp(jnp.sum(lv[0:1] * lv[1:2], axis=-1, keepdims=True))
            - jnp.exp(jnp.sum(lv[2:3] * lv[3:4], axis=-1, keepdims=True)) + lam_init)


def _t5_bucket_np(d):
    n = np.maximum(d, 0)
    exact = REL_BUCKETS // 2
    nf = np.maximum(n, 1).astype(np.float64)
    large = exact + (np.log(nf / exact) / math.log(REL_MAX_DIST / exact) * (REL_BUCKETS - exact)).astype(np.int64)
    large = np.minimum(large, REL_BUCKETS - 1)
    return np.where(n < exact, n, large).astype(np.int32)


def _bias_expand_kernel(relb_ref, idx_ref, o_ref, *, head0):
    h = pl.program_id(0) + head0
    idx = idx_ref[...]
    acc = jnp.zeros(idx.shape, F32)
    for b in range(REL_BUCKETS):
        acc = jnp.where(idx == b, relb_ref[b, h], acc)
    o_ref[...] = acc


def _bias_expand(rel_bias, idx_np, head0, n_heads):
    idx = jnp.asarray(idx_np, I32)
    nd = idx.ndim
    zeros = (0,) * nd
    return pl.pallas_call(
        functools.partial(_bias_expand_kernel, head0=head0),
        grid=(n_heads,),
        in_specs=[pl.BlockSpec(memory_space=pltpu.SMEM),
                  pl.BlockSpec(idx.shape, lambda h: zeros)],
        out_specs=pl.BlockSpec((None,) + idx.shape, lambda h: (h,) + zeros),
        out_shape=jax.ShapeDtypeStruct((n_heads,) + idx.shape, F32),
        compiler_params=_cparams(("arbitrary",)),
        name="bias_expand",
    )(rel_bias, idx)


def _prompt_bucket_tiles(t):
    r = np.arange(t)
    d = r[:, None] - r[None, :]
    return np.stack([_t5_bucket_np(d), _t5_bucket_np(d + t), np.full((t, t), REL_BUCKETS - 1, np.int32)])


def _store_t(y, is_prompt, t_refs, row_ref, c):
    if is_prompt:
        yt = jnp.transpose(y)
        for r in t_refs:
            r[c * 256:(c + 1) * 256, :] = yt.astype(r.dtype)
    elif row_ref is not None:
        row_ref[:, c * 256:(c + 1) * 256] = y


def _per_tile_kind(body, n_prompt_tiles):
    prompt_tile = pl.program_id(0) < n_prompt_tiles

    @pl.when(prompt_tile)
    def _():
        body(True)

    @pl.when(jnp.logical_not(prompt_tile))
    def _():
        body(False)


def _proj0_kernel(*refs, n_prompt_tiles):
    _per_tile_kind(functools.partial(_proj0_body, *refs), n_prompt_tiles)


def _proj0_body(x_ref, g_ref, w_ref, gain_ref, bd_ref,
                qa_ref, ka_ref, kat_ref, katbf_ref, qb_ref, small_ref, smallt_ref, kk_ref, vv_ref, kiki_ref,
                va_ref, vabf_ref, qi_ref, is_prompt):
    h = _mx(_rms_rows(x_ref[...], g_ref[...]))
    bd = bd_ref[...]

    def chunk(c):
        return _dot(h, w_ref[:, c * 256:(c + 1) * 256])

    def normed(c):
        y = chunk(c)
        return y * _seg_rsqrt(y, bd) * gain_ref[:, c * 256:(c + 1) * 256]

    for c in range(2):
        qa_ref[:, c * 256:(c + 1) * 256] = _mx(normed(c) * QK_SCALE)
    for c in range(2):
        _store_t(normed(2 + c), is_prompt, (kat_ref, katbf_ref), ka_ref, c)
    for c in range(2):
        qb_ref[:, c * 256:(c + 1) * 256] = _mx(normed(4 + c) * QK_SCALE)
    y = chunk(6)
    yn = y * _seg_rsqrt(y, bd) * gain_ref[:, 6 * 256:7 * 256]
    lane = _lane_iota(y.shape)
    y = jnp.where(lane < HEAD_DIM, yn, y)
    small_ref[...] = y
    _store_t(y, is_prompt, (smallt_ref,), None, 0)
    t0 = y[:, :LANES]
    t1 = y[:, LANES:]
    lo = _lane_iota(t0.shape) < HEAD_DIM
    r0 = pltpu.roll(t0, HEAD_DIM, 1)
    kk_ref[...] = _mx(jnp.where(lo, t0, r0))
    vv_ref[...] = _mx(jnp.where(lo, r0, t0))
    r1 = pltpu.roll(t1, HEAD_DIM, 1)
    kiki_ref[...] = _mx(jnp.where(lo, t1, r1))
    for c in range(2):
        y = chunk(7 + c)
        va_ref[:, c * 256:(c + 1) * 256] = y
        vabf_ref[:, c * 256:(c + 1) * 256] = _mx(y)
    for c in range(2):
        qi_ref[:, c * 256:(c + 1) * 256] = _mx(chunk(9 + c) * IDX_SCALE)


def _token_specs(n_p, n_s, s, tm):
    npt = n_p // tm
    tps = s // tm

    def t_map(i):
        ip = jnp.minimum(i, npt - 1)
        return (ip // tps, 0, ip % tps)

    def s_map(i):
        return (jnp.maximum(i - npt, 0), 0)

    return npt, t_map, s_map


def _proj0(x, g, w_in, a_qn, a_kn, b_qn, b_kn, n_p, s, tm=256):
    n = x.shape[0]
    n_s = n - n_p
    nb = n_p // s
    tm = _tile(math.gcd(n_p, n_s, s), tm)
    npt, t_map, s_map = _token_specs(n_p, n_s, s, tm)
    sp = np.cumsum([512, 512, 512, 512, 64, 64, 512, 64, 8])[:-1]
    wqa, wka, wva, wqb, wkb, wvb, wqi, wki, wwi = jnp.split(w_in, sp, axis=1)
    w = jnp.concatenate([wqa, wka, wqb, wkb, wvb, wki, wwi, jnp.zeros((D_MODEL, 56), F32), wva, wqi], axis=1)
    w = _mx(w)
    ncol = w.shape[1]
    gain = jnp.concatenate([jnp.tile(a_qn, 8), jnp.tile(a_kn, 8), jnp.tile(b_qn, 8), b_kn,
                            jnp.ones((192,), F32)])[None, :]
    bd = _block_diag_ones(256, HEAD_DIM)
    row = lambda i: (i, 0)
    fixed = lambda i: (0, 0)
    def rows(wd, dt):
        return pl.BlockSpec((tm, wd), row), jax.ShapeDtypeStruct((n, wd), dt)

    def rows_s(wd, dt):
        return pl.BlockSpec((tm, wd), s_map), jax.ShapeDtypeStruct((n_s, wd), dt)

    def cols_p(wd, dt):
        return pl.BlockSpec((None, wd, tm), t_map), jax.ShapeDtypeStruct((nb, wd, s), dt)

    outs = [rows(512, MXU_DT), rows_s(512, F32), cols_p(512, F32), cols_p(512, MXU_DT), rows(512, MXU_DT),
            rows(256, F32), cols_p(256, F32), rows(128, MXU_DT), rows(128, MXU_DT), rows(128, MXU_DT),
            rows(512, F32), rows(512, MXU_DT), rows(512, MXU_DT)]
    return pl.pallas_call(
        functools.partial(_proj0_kernel, n_prompt_tiles=npt),
        grid=(n // tm,),
        in_specs=[pl.BlockSpec((tm, D_MODEL), row), pl.BlockSpec((1, D_MODEL), fixed),
                  pl.BlockSpec((D_MODEL, ncol), fixed), pl.BlockSpec((1, gain.shape[1]), fixed),
                  pl.BlockSpec((256, 256), fixed)],
        out_specs=[o[0] for o in outs],
        out_shape=[o[1] for o in outs],
        compiler_params=_cparams(("arbitrary",)),
        name="proj0",
    )(x, g[None, :], w, gain, bd)


def _proj1_kernel(*refs, n_prompt_tiles):
    _per_tile_kind(functools.partial(_proj1_body, *refs), n_prompt_tiles)


def _proj1_body(x_ref, g_ref, w_ref, gain_ref, bf_ref, bd_ref,
                q_ref, k_ref, kt_ref, ktbf_ref, v_ref, vt_ref, vbf_ref, logf_ref, logft_ref, is_prompt):
    h = _mx(_rms_rows(x_ref[...], g_ref[...]))
    bd = bd_ref[...]

    def chunk(c):
        return _dot(h, w_ref[:, c * 256:(c + 1) * 256])

    def normed(c):
        y = chunk(c)
        return y * _seg_rsqrt(y, bd) * gain_ref[:, c * 256:(c + 1) * 256]

    for c in range(4):
        q_ref[:, c * 256:(c + 1) * 256] = _mx(normed(c) * QK_SCALE)
    for c in range(4):
        _store_t(normed(4 + c), is_prompt, (kt_ref, ktbf_ref), k_ref, c)
    for c in range(4):
        y = chunk(8 + c)
        vbf_ref[:, c * 256:(c + 1) * 256] = _mx(y)
        _store_t(y, is_prompt, (vt_ref,), v_ref, c)
    f = _dot(h, w_ref[:, 12 * 256:12 * 256 + LANES]) + bf_ref[...]
    logf = jnp.minimum(f, 0.0) - jnp.log(1.0 + jnp.exp(-jnp.abs(f)))
    logf_ref[...] = logf
    if is_prompt:
        logft_ref[...] = jnp.transpose(logf)


def _proj1(x, g, w_in, b_f, qn, kn, n_p, s, tm=256):
    n = x.shape[0]
    n_s = n - n_p
    nb = n_p // s
    tm = _tile(math.gcd(n_p, n_s, s), tm)
    npt, t_map, s_map = _token_specs(n_p, n_s, s, tm)
    w = _mx(jnp.concatenate([w_in, jnp.zeros((D_MODEL, LANES - C_HEADS), F32)], axis=1))
    ncol = w.shape[1]
    gain = jnp.concatenate([jnp.tile(qn, C_HEADS), jnp.tile(kn, C_HEADS)])[None, :]
    bf = jnp.concatenate([b_f, jnp.zeros((LANES - C_HEADS,), F32)])[None, :]
    bd = _block_diag_ones(256, HEAD_DIM)
    row = lambda i: (i, 0)
    fixed = lambda i: (0, 0)
    def rows(wd, dt):
        return pl.BlockSpec((tm, wd), row), jax.ShapeDtypeStruct((n, wd), dt)

    def rows_s(wd, dt):
        return pl.BlockSpec((tm, wd), s_map), jax.ShapeDtypeStruct((n_s, wd), dt)

    def cols_p(wd, dt):
        return pl.BlockSpec((None, wd, tm), t_map), jax.ShapeDtypeStruct((nb, wd, s), dt)

    outs = [rows(1024, MXU_DT), rows_s(1024, F32), cols_p(1024, F32), cols_p(1024, MXU_DT),
            rows_s(1024, F32), cols_p(1024, F32), rows(1024, MXU_DT), rows(LANES, F32), cols_p(LANES, F32)]
    return pl.pallas_call(
        functools.partial(_proj1_kernel, n_prompt_tiles=npt),
        grid=(n // tm,),
        in_specs=[pl.BlockSpec((tm, D_MODEL), row), pl.BlockSpec((1, D_MODEL), fixed),
                  pl.BlockSpec((D_MODEL, ncol), fixed), pl.BlockSpec((1, gain.shape[1]), fixed),
                  pl.BlockSpec((1, LANES), fixed), pl.BlockSpec((256, 256), fixed)],
        out_specs=[o[0] for o in outs],
        out_shape=[o[1] for o in outs],
        compiler_params=_cparams(("arbitrary",)),
        name="proj1",
    )(x, g[None, :], w, gain, bf, bd)


def _outproj_kernel(*refs, n_in):
    res_ref = refs[0]
    out_ref = refs[-1]
    acc = res_ref[...]
    for i in range(n_in):
        acc = acc + _dot(refs[1 + 2 * i][...], refs[2 + 2 * i][...])
    out_ref[...] = acc


def _outproj(res, pairs, tm=512):
    n = res.shape[0]
    tm = _tile(n, tm)
    row = lambda i: (i, 0)
    fixed = lambda i: (0, 0)
    in_specs = [pl.BlockSpec((tm, D_MODEL), row)]
    args = [res]
    for a, w in pairs:
        in_specs += [pl.BlockSpec((tm, a.shape[1]), row), pl.BlockSpec(w.shape, fixed)]
        args += [a, _mx(w)]
    return pl.pallas_call(
        functools.partial(_outproj_kernel, n_in=len(pairs)),
        grid=(n // tm,),
        in_specs=in_specs,
        out_specs=pl.BlockSpec((tm, D_MODEL), row),
        out_shape=jax.ShapeDtypeStruct((n, D_MODEL), F32),
        compiler_params=_cparams(("parallel",)),
        name="outproj",
    )(*args)


def _stack_streams(q):
    lane = _lane_iota(q.shape)
    zero = jnp.zeros_like(q)
    return jnp.concatenate([jnp.where(lane < HEAD_DIM, q, zero), jnp.where(lane >= HEAD_DIM, q, zero)], axis=0)


def _flash_stacked(q2, kt_ref, v_ref, i, tq, tk, bias_fn, m_ref, l_ref, acc_ref):
    rows = 2 * tq
    m_ref[...] = jnp.full(m_ref.shape, NEG, F32)
    l_ref[...] = jnp.zeros(l_ref.shape, F32)
    acc_ref[...] = jnp.zeros(acc_ref.shape, F32)

    def tile(j, n_sub, masked):
        width = n_sub * tk
        off = pl.multiple_of(j * tk, tk)
        s = _dot(q2, kt_ref[:, pl.ds(off, width)]) + bias_fn(j, off, n_sub)
        if masked:
            r = _row_iota((rows, width))
            s = jnp.where(_lane_iota((rows, width)) <= jnp.where(r >= tq, r - tq, r), s, NEG)
        alpha, p = _online_update(s, m_ref, l_ref)
        acc_ref[...] = alpha * acc_ref[...] + _dot(_mx(p), v_ref[pl.ds(off, width), :])

    def body(jj, carry):
        tile(jj * FLASH_WIDE, FLASH_WIDE, False)
        return carry

    lax.fori_loop(0, i // FLASH_WIDE, body, 0)
    for u in range(1, FLASH_WIDE):
        @pl.when(i % FLASH_WIDE >= u)
        def _():
            tile(i - i % FLASH_WIDE + (u - 1), 1, False)
    tile(i, 1, True)


def _attn_a_kernel(lamv_ref, gsub_ref, q_ref, k_ref, v_ref, bias_ref, o_ref,
                   m_ref, l_ref, acc_ref, *, tq, tk, lam_init):
    i = pl.program_id(2)

    def bias_fn(j, off, n_sub):
        b = jnp.concatenate([bias_ref[jnp.minimum(i - j - u, 2)] for u in range(n_sub)], axis=1)
        return jnp.concatenate([b, b], axis=0)

    _flash_stacked(_stack_streams(q_ref[...]), k_ref, v_ref, i, tq, tk, bias_fn, m_ref, l_ref, acc_ref)
    lam = _lambda(lamv_ref, lam_init)
    o = acc_ref[:tq] / l_ref[:tq] - lam * (acc_ref[tq:] / l_ref[tq:])
    o = _rms_rows(o, gsub_ref[...]) * (1.0 - lam_init)
    o_ref[...] = _mx(o)


def _attn_a_prompt(qa, ka, va, lamv, gsub, bias_tiles, nb, s, lam_init, tq=256):
    tk = tq
    nq = s // tq
    kern = functools.partial(_attn_a_kernel, tq=tq, tk=tk, lam_init=lam_init)
    return pl.pallas_call(
        kern,
        grid=(nb, A_HEADS, nq),
        in_specs=[pl.BlockSpec((4, HEAD_DIM), lambda b, h, i: (0, 0)),
                  pl.BlockSpec((1, LANES), lambda b, h, i: (0, 0)),
                  pl.BlockSpec((tq, LANES), lambda b, h, i: (b * nq + i, h)),
                  pl.BlockSpec((None, LANES, s), lambda b, h, i: (b, h, 0)),
                  pl.BlockSpec((s, LANES), lambda b, h, i: (b, h)),
                  pl.BlockSpec((None, 3, tq, tk), lambda b, h, i: (h, 0, 0, 0))],
        out_specs=pl.BlockSpec((tq, LANES), lambda b, h, i: (b * nq + i, h)),
        out_shape=jax.ShapeDtypeStruct((nb * s, A_HEADS * LANES), MXU_DT),
        scratch_shapes=[pltpu.VMEM((2 * tq, LANES), F32), pltpu.VMEM((2 * tq, LANES), F32),
                        pltpu.VMEM((2 * tq, LANES), F32)],
        compiler_params=_cparams(("parallel", "parallel", "arbitrary")),
        name="attn_a_prompt",
    )(lamv, gsub, qa, ka, va, bias_tiles)


def _attn_c_kernel(q_ref, k_ref, v_ref, cq_ref, ckt_ref, o_ref, m_ref, l_ref, acc_ref, *, tq, tk):
    hp = pl.program_id(1)
    i = pl.program_id(2)
    cq_tile = cq_ref[...]
    lane16 = _lane_iota(cq_tile.shape)
    cq2 = jnp.concatenate(
        [jnp.broadcast_to(jnp.sum(jnp.where(lane16 == 2 * hp + c, cq_tile, 0.0), axis=-1, keepdims=True),
                          (tq, LANES)) for c in range(2)], axis=0)

    def bias_fn(j, off, n_sub):
        width = n_sub * tk
        ck = ckt_ref[:, pl.ds(off, width)]
        ck2 = jnp.concatenate([jnp.broadcast_to(ck[0:1], (tq, width)),
                               jnp.broadcast_to(ck[1:2], (tq, width))], axis=0)
        return _lanes(cq2, width) - ck2

    _flash_stacked(_stack_streams(q_ref[...]), k_ref, v_ref, i, tq, tk, bias_fn, m_ref, l_ref, acc_ref)
    lane = _lane_iota((tq, LANES))
    o = jnp.where(lane < HEAD_DIM, acc_ref[:tq] / l_ref[:tq], acc_ref[tq:] / l_ref[tq:])
    o_ref[...] = _mx(o)


def _attn_c_prompt(q, k, v, cq, ckt, nb, s, tq=256):
    tk = tq
    nq = s // tq
    nhp = C_HEADS // 2
    kern = functools.partial(_attn_c_kernel, tq=tq, tk=tk)
    return pl.pallas_call(
        kern,
        grid=(nb, nhp, nq),
        in_specs=[pl.BlockSpec((tq, LANES), lambda b, h, i: (b * nq + i, h)),
                  pl.BlockSpec((None, LANES, s), lambda b, h, i: (b, h, 0)),
                  pl.BlockSpec((s, LANES), lambda b, h, i: (b, h)),
                  pl.BlockSpec((tq, C_HEADS), lambda b, h, i: (b * nq + i, 0)),
                  pl.BlockSpec((None, None, 2, s), lambda b, h, i: (b, h, 0, 0))],
        out_specs=pl.BlockSpec((tq, LANES), lambda b, h, i: (b * nq + i, h)),
        out_shape=jax.ShapeDtypeStruct((nb * s, C_HEADS * HEAD_DIM), MXU_DT),
        scratch_shapes=[pltpu.VMEM((2 * tq, LANES), F32), pltpu.VMEM((2 * tq, LANES), F32),
                        pltpu.VMEM((2 * tq, LANES), F32)],
        compiler_params=_cparams(("parallel", "parallel", "arbitrary")),
        name="attn_c_prompt",
    )(q, k, v, cq, ckt)


def _score_keys(score):
    score = jnp.where(score == 0.0, 0.0, score)
    bits = pltpu.bitcast(score, I32)
    return bits ^ (jnp.right_shift(bits, 31) & 0x7FFFFFFF)


def _topk_select(keys_ref, width, kcount, active, col):
    kf = float(kcount)
    nbits_col = int(width - 1).bit_length()

    def count(pred):
        return jnp.sum(jnp.where(pred, 1.0, 0.0), axis=1, keepdims=True)

    t0 = jnp.where(count(keys_ref[:, :width] >= 0) >= kf, 0, INT_MIN).astype(I32)

    def body(it, t):
        cand = t + jnp.left_shift(jnp.int32(1), 30 - it)
        return jnp.where(count(keys_ref[:, :width] >= cand) >= kf, cand, t)

    t = lax.fori_loop(0, 31, body, t0)
    t = jnp.where(active, t, INT_MIN)
    keys = keys_ref[:, :width]
    gt = keys > t
    eq = keys == t
    need = kf - count(gt)
    excess = jnp.where(active, count(eq) - need, 0.0)

    def tie_break():
        def tb(it, jj):
            cand = jj + jnp.left_shift(jnp.int32(1), nbits_col - 1 - it)
            c = count((keys_ref[:, :width] == t) & (col < cand))
            return jnp.where(c < need, cand, jj)
        return lax.fori_loop(0, nbits_col, tb, jnp.zeros(t.shape, I32))

    jmax = lax.cond(jnp.max(excess) > 0.0, tie_break, lambda: jnp.full(t.shape, width, I32))
    return gt | (eq & (col <= jmax))


KEY_OF_NEG_INF = -2139095041
SELECT_UNIT = 2


def _attn_b_kernel(qb_ref, qi_ref, w_ref, kk_ref, vv_ref, kiki_ref, bias_ref, o_ref,
                   keys_ref, selm_ref, m_ref, l_ref, acc_ref, *, tq, s_len, topk):
    i = pl.program_id(1)
    cw = tq
    nh = B_HEADS
    n_chunks = s_len // cw
    lane = _lane_iota((tq, LANES))
    halves = (lane < HEAD_DIM, lane >= HEAD_DIM)

    def stack_heads(ref):
        parts = []
        for h in range(nh):
            t = ref[:, (h // 2) * LANES:(h // 2 + 1) * LANES]
            parts.append(jnp.where(halves[h % 2], t, jnp.zeros_like(t)))
        return jnp.concatenate(parts, axis=0)

    row = _row_iota((tq, cw)) + i * tq
    colc = _lane_iota((tq, cw))

    qi8 = stack_heads(qi_ref)
    wt = w_ref[...] * IDX_HEAD_SCALE
    wcol = jnp.concatenate([jnp.broadcast_to(wt[:, HEAD_DIM + h:HEAD_DIM + h + 1], (tq, LANES))
                            for h in range(nh)], axis=0)

    def score_chunk(j, carry):
        off = pl.multiple_of(j * cw, cw)
        d = jnp.maximum(_dot_t(qi8, kiki_ref[pl.ds(off, cw), :]), 0.0) * _lanes(wcol, cw)
        sc = d[0:tq]
        for h in range(1, nh):
            sc = sc + d[h * tq:(h + 1) * tq]
        keys_ref[:, pl.ds(off, cw)] = _score_keys(jnp.where((colc + j * cw) <= row, sc, -jnp.inf))
        return carry

    lax.fori_loop(0, i + 1, score_chunk, 0)
    unit = SELECT_UNIT if n_chunks % SELECT_UNIT == 0 else 1
    rem = (i + 1) % unit
    for u in range(1, unit):
        @pl.when((rem != 0) & (u <= unit - rem))
        def _():
            off = pl.multiple_of((i + u) * cw, cw)
            keys_ref[:, pl.ds(off, cw)] = jnp.full((tq, cw), KEY_OF_NEG_INF, I32)
    widths = [w * unit * cw for w in range(1, n_chunks // unit + 1)]

    qpos = _row_iota((tq, 1)) + i * tq

    def select_branch(width):
        def br():
            colw = _lane_iota((tq, width))
            causal = colw <= (_row_iota((tq, width)) + i * tq)
            sel = _topk_select(keys_ref, width, topk, qpos >= topk, colw)
            selm_ref[:, :width] = jnp.where(sel & causal, 0.0, NEG)
        return br

    def causal_only():
        col = _lane_iota((tq, s_len))
        selm_ref[...] = jnp.where(col <= (_row_iota((tq, s_len)) + i * tq), 0.0, NEG)

    branch = jnp.where((i + 1) * tq > topk, 1 + i // unit, 0)
    lax.switch(branch, [causal_only] + [select_branch(w) for w in widths])

    qb8 = stack_heads(qb_ref)
    m_ref[...] = jnp.full(m_ref.shape, NEG, F32)
    l_ref[...] = jnp.zeros(l_ref.shape, F32)
    acc_ref[...] = jnp.zeros(acc_ref.shape, F32)

    def attn_chunk(j, carry):
        off = pl.multiple_of(j * cw, cw)
        bidx = jnp.minimum(i - j, 2)
        bias = jnp.concatenate([bias_ref[h, bidx] for h in range(nh)], axis=0)
        selm = selm_ref[:, pl.ds(off, cw)]
        s = _dot_t(qb8, kk_ref[pl.ds(off, cw), :]) + bias + jnp.concatenate([selm] * nh, axis=0)
        alpha, p = _online_update(s, m_ref, l_ref)
        acc_ref[...] = alpha * acc_ref[...] + _dot(_mx(p), vv_ref[pl.ds(off, cw), :])
        return carry

    lax.fori_loop(0, i + 1, attn_chunk, 0)
    o = acc_ref[...] / l_ref[...]
    for hp in range(nh // 2):
        o_ref[:, hp * LANES:(hp + 1) * LANES] = _mx(
            jnp.where(halves[0], o[2 * hp * tq:(2 * hp + 1) * tq], o[(2 * hp + 1) * tq:(2 * hp + 2) * tq]))


def _attn_b_prompt(qb, qi, small, kk, vv, kiki, bias_tiles, nb, s, topk, tq=256):
    nq = s // tq
    kern = functools.partial(_attn_b_kernel, tq=tq, s_len=s, topk=topk)
    qrow = lambda b, i: (b * nq + i, 0)
    kv = lambda b, i: (b, 0)
    rows = B_HEADS * tq
    return pl.pallas_call(
        kern,
        grid=(nb, nq),
        in_specs=[pl.BlockSpec((tq, B_HEADS * HEAD_DIM), qrow),
                  pl.BlockSpec((tq, IDX_HEADS * IDX_DIM), qrow),
                  pl.BlockSpec((tq, LANES), lambda b, i: (b * nq + i, 1)),
                  pl.BlockSpec((s, LANES), kv), pl.BlockSpec((s, LANES), kv), pl.BlockSpec((s, LANES), kv),
                  pl.BlockSpec(bias_tiles.shape, lambda b, i: (0, 0, 0, 0))],
        out_specs=pl.BlockSpec((tq, B_HEADS * HEAD_DIM), qrow),
        out_shape=jax.ShapeDtypeStruct((nb * s, B_HEADS * HEAD_DIM), MXU_DT),
        scratch_shapes=[pltpu.VMEM((tq, s), I32), pltpu.VMEM((tq, s), F32),
                        pltpu.VMEM((rows, LANES), F32), pltpu.VMEM((rows, LANES), F32),
                        pltpu.VMEM((rows, LANES), F32)],
        compiler_params=_cparams(("parallel", "arbitrary")),
        name="attn_b_prompt",
    )(qb, qi, small, kk, vv, kiki, bias_tiles)


def _cumsum_rows_kernel(x_ref, tri_ref, o_ref, carry_ref):
    j = pl.program_id(1)

    @pl.when(j == 0)
    def _():
        carry_ref[...] = jnp.zeros(carry_ref.shape, F32)

    tri = tri_ref[...]
    hi, mid, lo = _split3(x_ref[...])
    cum = _dot(tri, hi) + _dot(tri, mid) + _dot(tri, lo) + carry_ref[...]
    o_ref[...] = cum
    carry_ref[...] = cum[-1:, :]


def _cumsum_prompt(x, nb, s, blk=128):
    nj = s // blk
    w = x.shape[1]
    r = np.arange(blk)
    tri = jnp.asarray(r[:, None] >= r[None, :], dtype=MXU_DT)
    return pl.pallas_call(
        _cumsum_rows_kernel,
        grid=(nb, nj),
        in_specs=[pl.BlockSpec((blk, w), lambda b, j: (b * nj + j, 0)),
                  pl.BlockSpec((blk, blk), lambda b, j: (0, 0))],
        out_specs=pl.BlockSpec((blk, w), lambda b, j: (b * nj + j, 0)),
        out_shape=jax.ShapeDtypeStruct((nb * s, w), F32),
        scratch_shapes=[pltpu.VMEM((1, w), F32)],
        compiler_params=_cparams(("parallel", "arbitrary")),
        name="cumsum_prompt",
    )(x, tri)


def _online_update(s, m_ref, l_ref):
    m_old = m_ref[...]
    m_new = jnp.maximum(m_old, jnp.max(s, axis=-1, keepdims=True))
    alpha = jnp.exp(m_old - m_new)
    p = jnp.exp(s - _lanes(m_new, s.shape[1]))
    l_ref[...] = alpha * l_ref[...] + jnp.sum(p, axis=-1, keepdims=True)
    m_ref[...] = m_new
    return alpha, p


def _decode0_kernel(*refs, n_steps, npp, ps, topk, lam_init, group):
    (pt_ref, lamv_ref, gsub_ref, qa_ref, qi_ref, qb_ref, w8_ref, kna_ref, vna_ref, bnew_ref,
     taba_ref, tabb_ref, validb_ref) = refs[:13]
    pages = refs[13:13 + 5 * npp]
    cak, cav, cbk, cbv, cbi = (pages[0:npp], pages[npp:2 * npp], pages[2 * npp:3 * npp],
                               pages[3 * npp:4 * npp], pages[4 * npp:5 * npp])
    oa_ref, ob_ref, m_ref, l_ref, acc_ref, kb_ref, vb_ref, ki_ref, keys_ref = refs[13 + 5 * npp:]
    g = pl.program_id(0) % group
    step = pl.program_id(1)
    past = n_steps * npp * ps
    lk = kb_ref.shape[2]

    @pl.when(step == 0)
    def _():
        m_ref[...] = jnp.full(m_ref.shape, NEG, F32)
        l_ref[...] = jnp.zeros(l_ref.shape, F32)
        acc_ref[...] = jnp.zeros(acc_ref.shape, F32)

    qa = qa_ref[...]
    far = taba_ref[0]
    tail = taba_ref[jnp.where(step == n_steps - 1, 1, 0)]
    s = jnp.concatenate([_dot(qa, _mx(cak[j][...])) + (tail if j == npp - 1 else far) for j in range(npp)], axis=1)
    alpha, p = _online_update(s, m_ref, l_ref)
    p = _mx(p)
    for h in range(A_HEADS):
        rows = slice(h * 2 * T8, (h + 1) * 2 * T8)
        upd = alpha[rows] * acc_ref[rows]
        for j in range(npp):
            vh = _mx(cav[j][pl.ds(h, ps, stride=A_HEADS), :])
            upd = upd + _dot(p[rows, j * ps:(j + 1) * ps], vh)
        acc_ref[rows] = upd

    for j in range(npp):
        off = pl.multiple_of((step * npp + j) * ps, ps)
        kb_ref[g, :, pl.ds(off, ps)] = _mx(cbk[j][...])
        vb_ref[g, :, pl.ds(off, ps)] = _mx(cbv[j][...])
        ki_ref[g, :, pl.ds(off, ps)] = _mx(cbi[j][...])

    @pl.when(step == n_steps - 1)
    def _():
        s_new = _dot_t(qa, _mx(kna_ref[...])) + taba_ref[2][:, :kna_ref.shape[0]]
        alpha2, p2 = _online_update(s_new, m_ref, l_ref)
        vn = _mx(vna_ref[...])
        lam = _lambda(lamv_ref, lam_init)
        for h in range(A_HEADS):
            rows = slice(h * 2 * T8, (h + 1) * 2 * T8)
            o16 = (alpha2[rows] * acc_ref[rows] + _dot(_mx(p2[rows]), vn[:, h * LANES:(h + 1) * LANES])) / l_ref[rows]
            o = o16[:T8] - lam * o16[T8:]
            oa_ref[h] = _rms_rows(o, gsub_ref[...]) * (1.0 - lam_init)

        bn = bnew_ref[g]
        kb_ref[g, :, past:past + LANES] = _mx(bn[0])
        vb_ref[g, :, past:past + LANES] = _mx(bn[1])
        ki_ref[g, :, past:past + LANES] = _mx(bn[2])

    @pl.when((step == n_steps - 1) & (g == group - 1))
    def _():
        valid = validb_ref[...] == 0.0
        for gg in range(group):
            dots = jnp.maximum(_dot(qi_ref[gg], ki_ref[gg]), 0.0)
            w8 = w8_ref[gg] * IDX_HEAD_SCALE
            score = jnp.zeros((T8, lk), F32)
            for h in range(IDX_HEADS):
                score = score + w8[:, h:h + 1] * dots[h * T8:(h + 1) * T8]
            keys_ref[gg * T8:(gg + 1) * T8, :] = _score_keys(jnp.where(valid, score, -jnp.inf))
        rows = group * T8
        sel = _topk_select(keys_ref, lk, topk, jnp.full((rows, 1), True), _lane_iota((rows, lk)))
        selm = jnp.where(sel & jnp.concatenate([valid] * group, axis=0), 0.0, NEG)
        for gg in range(group):
            sg = selm[gg * T8:(gg + 1) * T8]
            sb = _dot(qb_ref[gg], kb_ref[gg]) + tabb_ref[...] + jnp.concatenate([sg] * B_HEADS, axis=0)
            mb = jnp.max(sb, axis=-1, keepdims=True)
            pb = jnp.exp(sb - mb)
            lb = jnp.sum(pb, axis=-1, keepdims=True)
            ob = _dot_t(_mx(pb), vb_ref[gg]) / lb
            for h in range(B_HEADS):
                ob_ref[gg, h] = ob[h * T8:(h + 1) * T8]


def _decode0(page_table, lamv, gsub, qa_bd, qi64, qb64, w8, kna, vna, bnew_t,
             cak_t, cav_r, cbk_t, cbv_t, cbi_t, taba, tabb, validb, layer, topk, lam_init):
    db, npg = page_table.shape
    ps = cak_t.shape[3]
    npp = PAGES_PER_STEP if npg % PAGES_PER_STEP == 0 else 1
    n_steps = npg // npp
    lk = tabb.shape[1]
    group = max(gsz for gsz in (8, 4, 2, 1) if db % gsz == 0)
    per_b = lambda b, p, pt: (b, 0, 0)
    per_g = lambda b, p, pt: (b // group, 0, 0)
    per_g4 = lambda b, p, pt: (b // group, 0, 0, 0)
    fixed2 = lambda b, p, pt: (0, 0)
    fixed3 = lambda b, p, pt: (0, 0, 0)

    def page_specs(arr):
        blk = (None, None) + arr.shape[2:]
        return [pl.BlockSpec(blk, lambda b, p, pt, j=j: (layer, pt[b * npg + p * npp + j], 0, 0)) for j in range(npp)]

    caches = (cak_t, cav_r, cbk_t, cbv_t, cbi_t)
    kern = functools.partial(_decode0_kernel, n_steps=n_steps, npp=npp, ps=ps, topk=topk, lam_init=lam_init,
                             group=group)
    grid_spec = pltpu.PrefetchScalarGridSpec(
        num_scalar_prefetch=1,
        grid=(db, n_steps),
        in_specs=[pl.BlockSpec((4, HEAD_DIM), fixed2), pl.BlockSpec((1, LANES), fixed2),
                  pl.BlockSpec((None,) + qa_bd.shape[1:], per_b),
                  pl.BlockSpec((group,) + qi64.shape[1:], per_g),
                  pl.BlockSpec((group,) + qb64.shape[1:], per_g),
                  pl.BlockSpec((group,) + w8.shape[1:], per_g),
                  pl.BlockSpec((None,) + kna.shape[1:], per_b),
                  pl.BlockSpec((None,) + vna.shape[1:], per_b),
                  pl.BlockSpec((group,) + bnew_t.shape[1:], per_g4),
                  pl.BlockSpec(taba.shape, fixed3), pl.BlockSpec(tabb.shape, fixed2),
                  pl.BlockSpec(validb.shape, fixed2)]
                 + [sp for c in caches for sp in page_specs(c)],
        out_specs=[pl.BlockSpec((None, A_HEADS, T8, LANES), lambda b, p, pt: (b, 0, 0, 0)),
                   pl.BlockSpec((group, B_HEADS, T8, HEAD_DIM), per_g4)],
        scratch_shapes=[pltpu.VMEM((64, LANES), F32), pltpu.VMEM((64, LANES), F32),
                        pltpu.VMEM((64, LANES), F32),
                        pltpu.VMEM((group, HEAD_DIM, lk), MXU_DT), pltpu.VMEM((group, HEAD_DIM, lk), MXU_DT),
                        pltpu.VMEM((group, IDX_DIM, lk), MXU_DT), pltpu.VMEM((group * T8, lk), I32)],
    )
    return pl.pallas_call(
        kern,
        grid_spec=grid_spec,
        out_shape=[jax.ShapeDtypeStruct((db, A_HEADS, T8, LANES), F32),
                   jax.ShapeDtypeStruct((db, B_HEADS, T8, HEAD_DIM), F32)],
        compiler_params=_cparams(("arbitrary", "arbitrary")),
        name="decode0",
    )(page_table.reshape(-1), lamv, gsub, qa_bd, qi64, qb64, w8, kna, vna, bnew_t, taba, tabb, validb,
      *[c for c in caches for _ in range(npp)])


def _expand_rows(x):
    hh, ww = x.shape
    return jnp.broadcast_to(x[:, None, :], (hh, T8, ww)).reshape(hh * T8, ww)


def _decode1_kernel(*refs, n_steps, npp, ps):
    pt_ref, q_ref, cq_ref, kn_ref, vn_ref, cnew_ref, maskn_ref, tri_ref = refs[:8]
    ck = refs[8:8 + npp]
    cv = refs[8 + npp:8 + 2 * npp]
    cf = refs[8 + 2 * npp:8 + 3 * npp]
    o_ref, m_ref, l_ref, acc_ref, suf_ref = refs[8 + 3 * npp:]
    step = pl.program_id(1)

    @pl.when(step == 0)
    def _():
        m_ref[...] = jnp.full(m_ref.shape, NEG, F32)
        l_ref[...] = jnp.zeros(l_ref.shape, F32)
        acc_ref[...] = jnp.zeros(acc_ref.shape, F32)
        suf_ref[...] = jnp.zeros(suf_ref.shape, F32)

    q = q_ref[...]
    cq = jnp.broadcast_to(cq_ref[...], (q.shape[0], LANES))
    tri = tri_ref[...]
    running = suf_ref[...]
    sufs = [None] * npp
    for j in reversed(range(npp)):
        hi, mid, lo = _split3(cf[j][...])
        cum = _dot(hi, tri) + _dot(mid, tri) + _dot(lo, tri)
        tot = jnp.broadcast_to(cum[:, ps - 1:ps], cum.shape)
        sufs[j] = running + tot - cum
        running = running + tot
    suf_ref[...] = running
    s = jnp.concatenate([_dot(q, _mx(ck[j][...])) for j in range(npp)], axis=1)
    s = s + _lanes(cq, npp * ps) + _expand_rows(jnp.concatenate(sufs, axis=1))
    alpha, p = _online_update(s, m_ref, l_ref)
    p = _mx(p)
    upd = _lanes(alpha, acc_ref.shape[1]) * acc_ref[...]
    for j in range(npp):
        upd = upd + _dot_t(p[:, j * ps:(j + 1) * ps], _mx(cv[j][...]))
    acc_ref[...] = upd

    @pl.when(step == n_steps - 1)
    def _():
        nn = kn_ref.shape[0]
        s_new = (_dot_t(q, _mx(kn_ref[...])) + cq[:, :nn] - _expand_rows(cnew_ref[...]) + maskn_ref[...])
        alpha2, p2 = _online_update(s_new, m_ref, l_ref)
        o_full = ((_lanes(alpha2, acc_ref.shape[1]) * acc_ref[...] + _dot(_mx(p2), _mx(vn_ref[...])))
                  / _lanes(l_ref[...], acc_ref.shape[1]))
        for h in range(C_HEADS):
            o_ref[h] = o_full[h * T8:(h + 1) * T8, h * HEAD_DIM:(h + 1) * HEAD_DIM]


def _decode1(page_table, q_bd, cq, kn, vn, cnew_t, cck_t, ccv_t, ccf_t, maskn, layer):
    db, npg = page_table.shape
    ps = cck_t.shape[3]
    assert ps == LANES, ps
    npp = PAGES_PER_STEP if npg % PAGES_PER_STEP == 0 else 1
    n_steps = npg // npp
    per_b = lambda b, p, pt: (b, 0, 0)
    fixed = lambda b, p, pt: (0, 0)
    r = np.arange(ps)
    tri = jnp.asarray(r[:, None] <= r[None, :], dtype=MXU_DT)

    def page_specs(arr):
        blk = (None, None) + arr.shape[2:]
        return [pl.BlockSpec(blk, lambda b, p, pt, j=j: (layer, pt[b * npg + (n_steps - 1 - p) * npp + j], 0, 0))
                for j in range(npp)]

    kern = functools.partial(_decode1_kernel, n_steps=n_steps, npp=npp, ps=ps)
    rows = C_HEADS * T8
    grid_spec = pltpu.PrefetchScalarGridSpec(
        num_scalar_prefetch=1,
        grid=(db, n_steps),
        in_specs=[pl.BlockSpec((None,) + q_bd.shape[1:], per_b),
                  pl.BlockSpec((None,) + cq.shape[1:], per_b),
                  pl.BlockSpec((None,) + kn.shape[1:], per_b),
                  pl.BlockSpec((None,) + vn.shape[1:], per_b),
                  pl.BlockSpec((None,) + cnew_t.shape[1:], per_b),
                  pl.BlockSpec(maskn.shape, fixed), pl.BlockSpec(tri.shape, fixed)]
                 + page_specs(cck_t) + page_specs(ccv_t) + page_specs(ccf_t),
        out_specs=pl.BlockSpec((None, C_HEADS, T8, HEAD_DIM), lambda b, p, pt: (b, 0, 0, 0)),
        scratch_shapes=[pltpu.VMEM((rows, LANES), F32), pltpu.VMEM((rows, LANES), F32),
                        pltpu.VMEM((rows, C_HEADS * HEAD_DIM), F32), pltpu.VMEM((C_HEADS, LANES), F32)],
    )
    return pl.pallas_call(
        kern,
        grid_spec=grid_spec,
        out_shape=jax.ShapeDtypeStruct((db, C_HEADS, T8, HEAD_DIM), F32),
        compiler_params=_cparams(("parallel", "arbitrary")),
        name="decode1",
    )(page_table.reshape(-1), q_bd, cq, kn, vn, cnew_t, maskn, tri,
      *([cck_t] * npp), *([ccv_t] * npp), *([ccf_t] * npp))


def _router_kernel(x_ref, g_ref, whi_ref, wlo_ref, b_ref, t_ref, route_ref):
    t = _rms_rows(x_ref[...], g_ref[...])
    thi, tlo = _split2(t)
    t_ref[...] = thi
    whi = whi_ref[...]
    logits = _dot(thi, whi) + _dot(tlo, whi) + _dot(thi, wlo_ref[...]) + b_ref[...]
    lane = _lane_iota(logits.shape)
    big = jnp.int32(1 << 20)

    def first_max(v):
        mx = jnp.max(v, axis=-1, keepdims=True)
        idx = jnp.min(jnp.where(v == mx, lane, big), axis=-1, keepdims=True)
        return mx, idx

    glog = jnp.where(lane < N_GROUPS, logits, -jnp.inf)
    gmax, gidx = first_max(glog)
    grp_w = 1.0 / jnp.sum(jnp.exp(glog - gmax), axis=-1, keepdims=True)
    el = lane - N_GROUPS
    in_grp = (el >= 0) & (el < N_EXPERTS) & (jnp.right_shift(el, 3) == gidx)
    v1 = jnp.where(in_grp, logits, -jnp.inf)
    top1, i1 = first_max(v1)
    v2 = jnp.where(lane == i1, -jnp.inf, v1)
    top2, i2 = first_max(v2)
    e2 = jnp.exp(top2 - top1)
    w1 = grp_w / (1.0 + e2)
    w2 = grp_w * e2 / (1.0 + e2)
    route = jnp.where(lane == 0, (i1 - N_GROUPS).astype(F32),
                      jnp.where(lane == 1, (i2 - N_GROUPS).astype(F32),
                                jnp.where(lane == 2, w1, jnp.where(lane == 3, w2, 0.0))))
    route_ref[...] = route


def _router(x, g, w_group, b_group, w_router, b_router, tm=512):
    n = x.shape[0]
    tm = _tile(n, tm)
    pad = LANES - N_GROUPS - N_EXPERTS
    w = jnp.concatenate([w_group, w_router, jnp.zeros((D_MODEL, pad), F32)], axis=1)
    whi = _mx(w)
    wlo = _mx(w - whi.astype(F32))
    b = jnp.concatenate([b_group, b_router, jnp.zeros((pad,), F32)])[None, :]
    row = lambda i: (i, 0)
    fixed = lambda i: (0, 0)
    return pl.pallas_call(
        _router_kernel,
        grid=(n // tm,),
        in_specs=[pl.BlockSpec((tm, D_MODEL), row), pl.BlockSpec((1, D_MODEL), fixed),
                  pl.BlockSpec((D_MODEL, LANES), fixed), pl.BlockSpec((D_MODEL, LANES), fixed),
                  pl.BlockSpec((1, LANES), fixed)],
        out_specs=[pl.BlockSpec((tm, D_MODEL), row), pl.BlockSpec((tm, LANES), row)],
        out_shape=[jax.ShapeDtypeStruct((n, D_MODEL), MXU_DT), jax.ShapeDtypeStruct((n, LANES), F32)],
        compiler_params=_cparams(("parallel",)),
        name="moe_router",
    )(x, g[None, :], whi, wlo, b)


def _moe_dense_kernel(x_ref, t_ref, route_ref, wgu_ref, wd_ref, o_ref):
    e = pl.program_id(1)

    @pl.when(e == 0)
    def _():
        o_ref[...] = x_ref[...]

    au = _dot(t_ref[...], wgu_ref[...])
    a = au[:, :EXPERT_FF]
    u = au[:, EXPERT_FF:]
    r = route_ref[...]
    ef = e.astype(F32)
    gate = jnp.where(r[:, 0:1] == ef, r[:, 2:3], 0.0) + jnp.where(r[:, 1:2] == ef, r[:, 3:4], 0.0)
    hdn = a * (1.0 / (1.0 + jnp.exp(-a))) * u * gate
    o_ref[...] += _dot(_mx(hdn), wd_ref[...])


def _moe_dense(x, t, route, w_gate, w_up, w_down, tm=1536):
    n = x.shape[0]
    tm = _tile(n, tm)
    wgu = _mx(jnp.concatenate([w_gate.reshape(N_EXPERTS, D_MODEL, EXPERT_FF),
                               w_up.reshape(N_EXPERTS, D_MODEL, EXPERT_FF)], axis=-1))
    wd = _mx(w_down.reshape(N_EXPERTS, EXPERT_FF, D_MODEL))
    row = lambda i, e: (i, 0)
    once = pl.Buffered(1)
    return pl.pallas_call(
        _moe_dense_kernel,
        grid=(n // tm, N_EXPERTS),
        in_specs=[pl.BlockSpec((tm, D_MODEL), row, pipeline_mode=once),
                  pl.BlockSpec((tm, D_MODEL), row, pipeline_mode=once),
                  pl.BlockSpec((tm, LANES), row, pipeline_mode=once),
                  pl.BlockSpec((None, D_MODEL, 2 * EXPERT_FF), lambda i, e: (e, 0, 0)),
                  pl.BlockSpec((None, EXPERT_FF, D_MODEL), lambda i, e: (e, 0, 0))],
        out_specs=pl.BlockSpec((tm, D_MODEL), row),
        out_shape=jax.ShapeDtypeStruct((n, D_MODEL), F32),
        compiler_params=_cparams(("parallel", "arbitrary")),
        name="moe_dense",
    )(x, t, route, wgu, wd)


def _hier_moe(x, g, w_group, b_group, w_router, b_router, w_gate, w_up, w_down):
    t, route = _router(x, g, w_group, b_group, w_router, b_router)
    return _moe_dense(x, t, route, w_gate, w_up, w_down)


def _decode_tables(rel_bias, past, t_new, lk):
    t8 = np.arange(T8)
    lane = np.arange(LANES)
    far = np.full((T8, LANES), REL_BUCKETS - 1, np.int32)
    d_last = (past + t8[:, None]) - (past - LANES + lane[None, :])
    d_new = t8[:, None] - lane[None, :]
    ok_new = (d_new >= 0) & (lane[None, :] < t_new)
    ta = _bias_expand(rel_bias, np.stack([far, _t5_bucket_np(d_last), _t5_bucket_np(d_new)]), 0, A_HEADS)
    ta = jnp.where(jnp.asarray(ok_new)[None, None] | (jnp.arange(3) < 2)[None, :, None, None], ta, NEG)
    taba = jnp.broadcast_to(jnp.transpose(ta, (1, 0, 2, 3))[:, :, None], (3, A_HEADS, 2, T8, LANES))
    taba = taba.reshape(3, A_HEADS * 2 * T8, LANES)
    kpos = np.arange(lk)
    d_b = (past + t8[:, None]) - kpos[None, :]
    tabb = _bias_expand(rel_bias, _t5_bucket_np(d_b), A_HEADS, B_HEADS).reshape(B_HEADS * T8, lk)
    valid = (kpos[None, :] < past) | ((d_b >= 0) & (kpos[None, :] < past + t_new))
    validb = jnp.asarray(np.where(valid, 0.0, NEG), F32)
    return taba, tabb, validb


def _pad_rows(x, rows):
    pad = [(0, 0)] * x.ndim
    pad[1] = (0, rows - x.shape[1])
    return jnp.pad(x, pad)


def kernel(x_prompt, x_sample, cache_a_k, cache_a_v, cache_b_k, cache_b_v, cache_b_kidx, cache_c_k, cache_c_v, cache_c_logf, page_table, rel_bias, ab_norm, ab_w_in, a_q_norm, a_k_norm, b_q_norm, b_k_norm, a_lambda_q1, a_lambda_k1, a_lambda_q2, a_lambda_k2, a_sub_norm, ab_w_out, c_norm, c_w_in, c_forget_bias, c_q_norm, c_k_norm, c_w_out, ffn_norm, moe_w_group, moe_b_group, moe_w_router, moe_b_router, moe_w_gate, moe_w_up, moe_w_down):
    nb, s, d = x_prompt.shape
    db, ts, _ = x_sample.shape
    npg = page_table.shape[1]
    pool, ps = cache_a_k.shape[1], cache_a_k.shape[2]
    past = npg * ps
    n_p = nb * s
    n_s = db * ts
    depth = ffn_norm.shape[0]
    topk_p = min(IDX_TOPK_MAX, s // 4)
    topk_s = min(IDX_TOPK_MAX, (past + ts) // 4)
    lk = past + LANES

    cak_t = jnp.transpose(cache_a_k, (0, 1, 3, 4, 5, 2)).reshape(-1, pool, 2 * A_HEADS * HEAD_DIM, ps)
    cav_r = cache_a_v.reshape(-1, pool, ps * A_HEADS, 2 * HEAD_DIM)
    cbk_t = jnp.transpose(cache_b_k, (0, 1, 3, 2))
    cbv_t = jnp.transpose(cache_b_v, (0, 1, 3, 2))
    cbi_t = jnp.transpose(cache_b_kidx, (0, 1, 3, 2))
    cck_t = jnp.transpose(cache_c_k, (0, 1, 3, 4, 2)).reshape(-1, pool, C_HEADS * HEAD_DIM, ps)
    ccv_t = jnp.transpose(cache_c_v, (0, 1, 3, 4, 2)).reshape(-1, pool, C_HEADS * HEAD_DIM, ps)
    ccf_t = jnp.transpose(cache_c_logf, (0, 1, 3, 2))

    x = jnp.concatenate([x_prompt.reshape(n_p, d), x_sample.reshape(n_s, d)], axis=0)
    outs_p = {k: [] for k in ("ak", "av", "bk", "bv", "bi", "ck", "cv", "cf")}
    outs_s = {k: [] for k in ("ak", "av", "bk", "bv", "bi", "ck", "cv", "cf")}

    def smp(a):
        return a[n_p:].reshape(db, ts, a.shape[1])

    for layer in range(depth):
        if layer % 2 == 0:
            e = layer // 2
            lam_init = 0.8 - 0.6 * math.exp(-0.3 * layer)
            lamv = jnp.stack([a_lambda_q1[e], a_lambda_k1[e], a_lambda_q2[e], a_lambda_k2[e]])
            gsub = a_sub_norm[e][None, :]
            (qa, ka_s, kat, katbf, qb, small, smallt, kk, vv, kiki, va, vabf, qi) = _proj0(
                x, ab_norm[e], ab_w_in[e], a_q_norm[e], a_k_norm[e], b_q_norm[e], b_k_norm[e], n_p, s)
            tq_a = 256
            bias_a = _bias_expand(rel_bias, _prompt_bucket_tiles(tq_a), 0, A_HEADS)
            oa_p = _attn_a_prompt(qa, katbf, vabf, lamv, gsub, bias_a, nb, s, lam_init, tq=tq_a)
            tq_b = 256
            bias_b = _bias_expand(rel_bias, _prompt_bucket_tiles(tq_b), A_HEADS, B_HEADS)
            ob_p = _attn_b_prompt(qb, qi, small, kk, vv, kiki, bias_b, nb, s, topk_p, tq=tq_b)
            qa_s = _pad_rows(smp(qa), T8)
            hc = np.arange(2 * A_HEADS)
            colmask = jnp.asarray((np.arange(qa_s.shape[2])[None, :] // HEAD_DIM) == hc[:, None], MXU_DT)
            qa_bd = (qa_s[:, None, :, :] * colmask[None, :, None, :]).reshape(db, 2 * A_HEADS * T8, -1)

            def heads_rows(a, nh):
                a = _pad_rows(a, T8).reshape(db, T8, nh, HEAD_DIM)
                return jnp.transpose(a, (0, 2, 1, 3)).reshape(db, nh * T8, HEAD_DIM)

            qi64 = heads_rows(smp(qi), IDX_HEADS)
            qb64 = heads_rows(smp(qb), B_HEADS)
            small_s = smp(small)
            w8 = _pad_rows(small_s[:, :, 192:200], T8)
            ka_s3 = ka_s.reshape(db, ts, -1)
            va_s3 = smp(va)
            kna = _pad_rows(ka_s3, 16)
            vna = _pad_rows(va_s3, 16)
            bnew_t = jnp.transpose(_pad_rows(small_s[:, :, :192], LANES).reshape(db, LANES, 3, HEAD_DIM), (0, 2, 3, 1))
            taba, tabb, validb = _decode_tables(rel_bias, past, ts, lk)
            oa_d, ob_d = _decode0(page_table, lamv, gsub, qa_bd, qi64, qb64, w8, kna, vna, bnew_t,
                                  cak_t, cav_r, cbk_t, cbv_t, cbi_t, taba, tabb, validb, e, topk_s, lam_init)
            oa_s = jnp.transpose(oa_d[:, :, :ts], (0, 2, 1, 3)).reshape(n_s, A_HEADS * LANES)
            ob_s = jnp.transpose(ob_d[:, :, :ts], (0, 2, 1, 3)).reshape(n_s, B_HEADS * HEAD_DIM)
            oa = jnp.concatenate([oa_p, _mx(oa_s)], axis=0)
            ob = jnp.concatenate([ob_p, _mx(ob_s)], axis=0)
            w_out = ab_w_out[e]
            x = _outproj(x, [(oa, w_out[:A_HEADS * LANES]), (ob, w_out[A_HEADS * LANES:])])
            outs_p["ak"].append(jnp.transpose(kat.reshape(nb, A_HEADS, 2, HEAD_DIM, s), (0, 4, 1, 2, 3)))
            outs_p["av"].append(va[:n_p].reshape(nb, s, A_HEADS, 2 * HEAD_DIM))
            outs_p["bk"].append(jnp.transpose(smallt[:, 0:64], (0, 2, 1)))
            outs_p["bv"].append(jnp.transpose(smallt[:, 64:128], (0, 2, 1)))
            outs_p["bi"].append(jnp.transpose(smallt[:, 128:192], (0, 2, 1)))
            outs_s["ak"].append(ka_s3.reshape(db, ts, A_HEADS, 2, HEAD_DIM))
            outs_s["av"].append(va_s3.reshape(db, ts, A_HEADS, 2 * HEAD_DIM))
            outs_s["bk"].append(small_s[:, :, 0:64])
            outs_s["bv"].append(small_s[:, :, 64:128])
            outs_s["bi"].append(small_s[:, :, 128:192])
        else:
            o = layer // 2
            q, k_s, kt, ktbf, v_s, vt, vbf, logf128, logft = _proj1(
                x, c_norm[o], c_w_in[o], c_forget_bias[o], c_q_norm[o], c_k_norm[o], n_p, s)
            logf = logf128[:, :C_HEADS]
            cum_p = _cumsum_prompt(logf128[:n_p], nb, s)[:, :C_HEADS]
            ckt = jnp.transpose(cum_p.reshape(nb, s, C_HEADS // 2, 2), (0, 2, 3, 1))
            oc_p = _attn_c_prompt(q, ktbf, vbf, cum_p, ckt, nb, s)
            logf_s = smp(logf)
            run = jnp.zeros_like(logf_s[:, 0])
            c_rows = []
            for t in range(ts):
                run = run + logf_s[:, t]
                c_rows.append(run)
            c_new = jnp.stack(c_rows, axis=1)
            cq = jnp.transpose(_pad_rows(c_new, T8), (0, 2, 1)).reshape(db, C_HEADS * T8, 1)
            cnew_t = jnp.transpose(_pad_rows(c_new, 16), (0, 2, 1))
            q_s = _pad_rows(smp(q), T8)
            hmask = jnp.asarray((np.arange(q_s.shape[2])[None, :] // HEAD_DIM) == np.arange(C_HEADS)[:, None], MXU_DT)
            q_bd = (q_s[:, None, :, :] * hmask[None, :, None, :]).reshape(db, C_HEADS * T8, -1)
            k_s3 = k_s.reshape(db, ts, -1)
            v_s3 = v_s.reshape(db, ts, -1)
            kn = _pad_rows(k_s3, 16)
            vn = _pad_rows(v_s3, 16)
            t8 = np.arange(T8)
            okn = (t8[:, None] >= np.arange(16)[None, :]) & (np.arange(16)[None, :] < ts)
            maskn = jnp.asarray(np.tile(np.where(okn, 0.0, NEG), (C_HEADS, 1)), F32)
            oc_d = _decode1(page_table, q_bd, cq, kn, vn, cnew_t, cck_t, ccv_t, ccf_t, maskn, o)
            oc_s = jnp.transpose(oc_d[:, :, :ts], (0, 2, 1, 3)).reshape(n_s, C_HEADS * HEAD_DIM)
            oc = jnp.concatenate([oc_p, _mx(oc_s)], axis=0)
            x = _outproj(x, [(oc, c_w_out[o])])
            outs_p["ck"].append(jnp.transpose(kt.reshape(nb, C_HEADS, HEAD_DIM, s), (0, 3, 1, 2)))
            outs_p["cv"].append(jnp.transpose(vt.reshape(nb, C_HEADS, HEAD_DIM, s), (0, 3, 1, 2)))
            outs_p["cf"].append(jnp.transpose(logft[:, :C_HEADS], (0, 2, 1)))
            outs_s["ck"].append(k_s3.reshape(db, ts, C_HEADS, HEAD_DIM))
            outs_s["cv"].append(v_s3.reshape(db, ts, C_HEADS, HEAD_DIM))
            outs_s["cf"].append(logf_s)
        x = _hier_moe(x, ffn_norm[layer], moe_w_group[layer], moe_b_group[layer], moe_w_router[layer],
                      moe_b_router[layer], moe_w_gate[layer], moe_w_up[layer], moe_w_down[layer])

    keys = ("ak", "av", "bk", "bv", "bi", "ck", "cv", "cf")
    return ((x[:n_p].reshape(nb, s, d), x[n_p:].reshape(db, ts, d))
            + tuple(jnp.stack(outs_p[k]) for k in keys)
            + tuple(jnp.stack(outs_s[k]) for k in keys))
```

```python
import functools
import math

import numpy as np
import jax
import jax.numpy as jnp
from jax import lax
from jax.experimental import pallas as pl
from jax.experimental.pallas import tpu as pltpu

F32 = jnp.float32
I32 = jnp.int32
MXU_DT = jnp.bfloat16

D_MODEL = 1024
HEAD_DIM = 64
A_HEADS = 4
B_HEADS = 8
IDX_HEADS = 8
IDX_DIM = 64
IDX_TOPK_MAX = 256
C_HEADS = 16
REL_BUCKETS = 32
REL_MAX_DIST = 128
N_GROUPS = 4
EXPERTS_PER_GROUP = 8
N_EXPERTS = N_GROUPS * EXPERTS_PER_GROUP
EXPERT_FF = 256
EPS = 1e-6
NEG = -1e30
INT_MIN = -2 ** 31
QK_SCALE = HEAD_DIM ** -0.5
IDX_SCALE = IDX_DIM ** -0.5
IDX_HEAD_SCALE = IDX_HEADS ** -0.5
LANES = 128
T8 = 8
VMEM_LIMIT = 56 * 1024 * 1024
PAGES_PER_STEP = 8
FLASH_WIDE = 2


def _tile(n, pref):
    best = 16
    for t in range(16, pref + 1, 16):
        if n % t == 0:
            best = t
    assert n % best == 0, (n, pref)
    return best


def _cparams(sem):
    return pltpu.CompilerParams(dimension_semantics=sem, vmem_limit_bytes=VMEM_LIMIT)


def _mx(x):
    return x.astype(MXU_DT)


def _dot(a, b):
    return jnp.dot(a, b, preferred_element_type=F32)


def _dot_t(a, b):
    return lax.dot_general(a, b, (((1,), (1,)), ((), ())), preferred_element_type=F32)


def _split2(x):
    hi = _mx(x)
    lo = _mx(x - hi.astype(F32))
    return hi, lo


def _split3(x):
    hi = _mx(x)
    r = x - hi.astype(F32)
    mid = _mx(r)
    lo = _mx(r - mid.astype(F32))
    return hi, mid, lo


def _lane_iota(shape):
    return lax.broadcasted_iota(I32, shape, len(shape) - 1)


def _row_iota(shape):
    return lax.broadcasted_iota(I32, shape, len(shape) - 2)


def _lanes(x, width):
    if width <= LANES:
        return x[:, :width]
    return jnp.tile(x, (1, width // LANES))


def _rms_rows(x, g):
    ms = jnp.mean(x * x, axis=-1, keepdims=True)
    return x * lax.rsqrt(ms + EPS) * g


def _seg_rsqrt(y, bd):
    hi, lo = _split2(y * y)
    ss = _dot(hi, bd) + _dot(lo, bd)
    return lax.rsqrt(ss * (1.0 / HEAD_DIM) + EPS)


def _block_diag_ones(n, seg):
    r = np.arange(n)
    return jnp.asarray((r[:, None] // seg) == (r[None, :] // seg), dtype=MXU_DT)


def _lambda(lamv_ref, lam_init):
    lv = lamv_ref[...]
    return (jnp.exp(jnp.sum(lv[0:1] * lv[1:2], axis=-1, keepdims=True))
            - jnp.exp(jnp.sum(lv[2:3] * lv[3:4], axis=-1, keepdims=True)) + lam_init)


def _t5_bucket_np(d):
    n = np.maximum(d, 0)
    exact = REL_BUCKETS // 2
    nf = np.maximum(n, 1).astype(np.float64)
    large = exact + (np.log(nf / exact) / math.log(REL_MAX_DIST / exact) * (REL_BUCKETS - exact)).astype(np.int64)
    large = np.minimum(large, REL_BUCKETS - 1)
    return np.where(n < exact, n, large).astype(np.int32)


def _bias_expand_kernel(relb_ref, idx_ref, o_ref, *, head0):
    h = pl.program_id(0) + head0
    idx = idx_ref[...]
    acc = jnp.zeros(idx.shape, F32)
    for b in range(REL_BUCKETS):
        acc = jnp.where(idx == b, relb_ref[b, h], acc)
    o_ref[...] = acc


def _bias_expand(rel_bias, idx_np, head0, n_heads):
    idx = jnp.asarray(idx_np, I32)
    nd = idx.ndim
    zeros = (0,) * nd
    return pl.pallas_call(
        functools.partial(_bias_expand_kernel, head0=head0),
        grid=(n_heads,),
        in_specs=[pl.BlockSpec(memory_space=pltpu.SMEM),
                  pl.BlockSpec(idx.shape, lambda h: zeros)],
        out_specs=pl.BlockSpec((None,) + idx.shape, lambda h: (h,) + zeros),
        out_shape=jax.ShapeDtypeStruct((n_heads,) + idx.shape, F32),
        compiler_params=_cparams(("arbitrary",)),
        name="bias_expand",
    )(rel_bias, idx)


def _prompt_bucket_tiles(t):
    r = np.arange(t)
    d = r[:, None] - r[None, :]
    return np.stack([_t5_bucket_np(d), _t5_bucket_np(d + t), np.full((t, t), REL_BUCKETS - 1, np.int32)])


def _store_t(y, is_prompt, t_refs, row_ref, c):
    if is_prompt:
        yt = jnp.transpose(y)
        for r in t_refs:
            r[c * 256:(c + 1) * 256, :] = yt.astype(r.dtype)
    elif row_ref is not None:
        row_ref[:, c * 256:(c + 1) * 256] = y


def _per_tile_kind(body, n_prompt_tiles):
    prompt_tile = pl.program_id(0) < n_prompt_tiles

    @pl.when(prompt_tile)
    def _():
        body(True)

    @pl.when(jnp.logical_not(prompt_tile))
    def _():
        body(False)


def _proj0_kernel(*refs, n_prompt_tiles):
    _per_tile_kind(functools.partial(_proj0_body, *refs), n_prompt_tiles)


def _proj0_body(x_ref, g_ref, w_ref, gain_ref, bd_ref,
                qa_ref, ka_ref, kat_ref, katbf_ref, qb_ref, small_ref, smallt_ref, kk_ref, vv_ref, kiki_ref,
                va_ref, vabf_ref, qi_ref, is_prompt):
    h = _mx(_rms_rows(x_ref[...], g_ref[...]))
    bd = bd_ref[...]

    def chunk(c):
        return _dot(h, w_ref[:, c * 256:(c + 1) * 256])

    def normed(c):
        y = chunk(c)
        return y * _seg_rsqrt(y, bd) * gain_ref[:, c * 256:(c + 1) * 256]

    for c in range(2):
        qa_ref[:, c * 256:(c + 1) * 256] = _mx(normed(c) * QK_SCALE)
    for c in range(2):
        _store_t(normed(2 + c), is_prompt, (kat_ref, katbf_ref), ka_ref, c)
    for c in range(2):
        qb_ref[:, c * 256:(c + 1) * 256] = _mx(normed(4 + c) * QK_SCALE)
    y = chunk(6)
    yn = y * _seg_rsqrt(y, bd) * gain_ref[:, 6 * 256:7 * 256]
    lane = _lane_iota(y.shape)
    y = jnp.where(lane < HEAD_DIM, yn, y)
    small_ref[...] = y
    _store_t(y, is_prompt, (smallt_ref,), None, 0)
    t0 = y[:, :LANES]
    t1 = y[:, LANES:]
    lo = _lane_iota(t0.shape) < HEAD_DIM
    r0 = pltpu.roll(t0, HEAD_DIM, 1)
    kk_ref[...] = _mx(jnp.where(lo, t0, r0))
    vv_ref[...] = _mx(jnp.where(lo, r0, t0))
    r1 = pltpu.roll(t1, HEAD_DIM, 1)
    kiki_ref[...] = _mx(jnp.where(lo, t1, r1))
    for c in range(2):
        y = chunk(7 + c)
        va_ref[:, c * 256:(c + 1) * 256] = y
        vabf_ref[:, c * 256:(c + 1) * 256] = _mx(y)
    for c in range(2):
        qi_ref[:, c * 256:(c + 1) * 256] = _mx(chunk(9 + c) * IDX_SCALE)


def _token_specs(n_p, n_s, s, tm):
    npt = n_p // tm
    tps = s // tm

    def t_map(i):
        ip = jnp.minimum(i, npt - 1)
        return (ip // tps, 0, ip % tps)

    def s_map(i):
        return (jnp.maximum(i - npt, 0), 0)

    return npt, t_map, s_map


def _proj0(x, g, w_in, a_qn, a_kn, b_qn, b_kn, n_p, s, tm=256):
    n = x.shape[0]
    n_s = n - n_p
    nb = n_p // s
    tm = _tile(math.gcd(n_p, n_s, s), tm)
    npt, t_map, s_map = _token_specs(n_p, n_s, s, tm)
    sp = np.cumsum([512, 512, 512, 512, 64, 64, 512, 64, 8])[:-1]
    wqa, wka, wva, wqb, wkb, wvb, wqi, wki, wwi = jnp.split(w_in, sp, axis=1)
    w = jnp.concatenate([wqa, wka, wqb, wkb, wvb, wki, wwi, jnp.zeros((D_MODEL, 56), F32), wva, wqi], axis=1)
    w = _mx(w)
    ncol = w.shape[1]
    gain = jnp.concatenate([jnp.tile(a_qn, 8), jnp.tile(a_kn, 8), jnp.tile(b_qn, 8), b_kn,
                            jnp.ones((192,), F32)])[None, :]
    bd = _block_diag_ones(256, HEAD_DIM)
    row = lambda i: (i, 0)
    fixed = lambda i: (0, 0)
    def rows(wd, dt):
        return pl.BlockSpec((tm, wd), row), jax.ShapeDtypeStruct((n, wd), dt)

    def rows_s(wd, dt):
        return pl.BlockSpec((tm, wd), s_map), jax.ShapeDtypeStruct((n_s, wd), dt)

    def cols_p(wd, dt):
        return pl.BlockSpec((None, wd, tm), t_map), jax.ShapeDtypeStruct((nb, wd, s), dt)

    outs = [rows(512, MXU_DT), rows_s(512, F32), cols_p(512, F32), cols_p(512, MXU_DT), rows(512, MXU_DT),
            rows(256, F32), cols_p(256, F32), rows(128, MXU_DT), rows(128, MXU_DT), rows(128, MXU_DT),
            rows(512, F32), rows(512, MXU_DT), rows(512, MXU_DT)]
    return pl.pallas_call(
        functools.partial(_proj0_kernel, n_prompt_tiles=npt),
        grid=(n // tm,),
        in_specs=[pl.BlockSpec((tm, D_MODEL), row), pl.BlockSpec((1, D_MODEL), fixed),
                  pl.BlockSpec((D_MODEL, ncol), fixed), pl.BlockSpec((1, gain.shape[1]), fixed),
                  pl.BlockSpec((256, 256), fixed)],
        out_specs=[o[0] for o in outs],
        out_shape=[o[1] for o in outs],
        compiler_params=_cparams(("arbitrary",)),
        name="proj0",
    )(x, g[None, :], w, gain, bd)


def _proj1_kernel(*refs, n_prompt_tiles):
    _per_tile_kind(functools.partial(_proj1_body, *refs), n_prompt_tiles)


def _proj1_body(x_ref, g_ref, w_ref, gain_ref, bf_ref, bd_ref,
                q_ref, k_ref, kt_ref, ktbf_ref, v_ref, vt_ref, vbf_ref, logf_ref, logft_ref, is_prompt):
    h = _mx(_rms_rows(x_ref[...], g_ref[...]))
    bd = bd_ref[...]

    def chunk(c):
        return _dot(h, w_ref[:, c * 256:(c + 1) * 256])

    def normed(c):
        y = chunk(c)
        return y * _seg_rsqrt(y, bd) * gain_ref[:, c * 256:(c + 1) * 256]

    for c in range(4):
        q_ref[:, c * 256:(c + 1) * 256] = _mx(normed(c) * QK_SCALE)
    for c in range(4):
        _store_t(normed(4 + c), is_prompt, (kt_ref, ktbf_ref), k_ref, c)
    for c in range(4):
        y = chunk(8 + c)
        vbf_ref[:, c * 256:(c + 1) * 256] = _mx(y)
        _store_t(y, is_prompt, (vt_ref,), v_ref, c)
    f = _dot(h, w_ref[:, 12 * 256:12 * 256 + LANES]) + bf_ref[...]
    logf = jnp.minimum(f, 0.0) - jnp.log(1.0 + jnp.exp(-jnp.abs(f)))
    logf_ref[...] = logf
    if is_prompt:
        logft_ref[...] = jnp.transpose(logf)


def _proj1(x, g, w_in, b_f, qn, kn, n_p, s, tm=256):
    n = x.shape[0]
    n_s = n - n_p
    nb = n_p // s
    tm = _tile(math.gcd(n_p, n_s, s), tm)
    npt, t_map, s_map = _token_specs(n_p, n_s, s, tm)
    w = _mx(jnp.concatenate([w_in, jnp.zeros((D_MODEL, LANES - C_HEADS), F32)], axis=1))
    ncol = w.shape[1]
    gain = jnp.concatenate([jnp.tile(qn, C_HEADS), jnp.tile(kn, C_HEADS)])[None, :]
    bf = jnp.concatenate([b_f, jnp.zeros((LANES - C_HEADS,), F32)])[None, :]
    bd = _block_diag_ones(256, HEAD_DIM)
    row = lambda i: (i, 0)
    fixed = lambda i: (0, 0)
    def rows(wd, dt):
        return pl.BlockSpec((tm, wd), row), jax.ShapeDtypeStruct((n, wd), dt)

    def rows_s(wd, dt):
        return pl.BlockSpec((tm, wd), s_map), jax.ShapeDtypeStruct((n_s, wd), dt)

    def cols_p(wd, dt):
        return pl.BlockSpec((None, wd, tm), t_map), jax.ShapeDtypeStruct((nb, wd, s), dt)

    outs = [rows(1024, MXU_DT), rows_s(1024, F32), cols_p(1024, F32), cols_p(1024, MXU_DT),
            rows_s(1024, F32), cols_p(1024, F32), rows(1024, MXU_DT), rows(LANES, F32), cols_p(LANES, F32)]
    return pl.pallas_call(
        functools.partial(_proj1_kernel, n_prompt_tiles=npt),
        grid=(n // tm,),
        in_specs=[pl.BlockSpec((tm, D_MODEL), row), pl.BlockSpec((1, D_MODEL), fixed),
                  pl.BlockSpec((D_MODEL, ncol), fixed), pl.BlockSpec((1, gain.shape[1]), fixed),
                  pl.BlockSpec((1, LANES), fixed), pl.BlockSpec((256, 256), fixed)],
        out_specs=[o[0] for o in outs],
        out_shape=[o[1] for o in outs],
        compiler_params=_cparams(("arbitrary",)),
        name="proj1",
    )(x, g[None, :], w, gain, bf, bd)


def _outproj_kernel(*refs, n_in):
    res_ref = refs[0]
    out_ref = refs[-1]
    acc = res_ref[...]
    for i in range(n_in):
        acc = acc + _dot(refs[1 + 2 * i][...], refs[2 + 2 * i][...])
    out_ref[...] = acc


def _outproj(res, pairs, tm=512):
    n = res.shape[0]
    tm = _tile(n, tm)
    row = lambda i: (i, 0)
    fixed = lambda i: (0, 0)
    in_specs = [pl.BlockSpec((tm, D_MODEL), row)]
    args = [res]
    for a, w in pairs:
        in_specs += [pl.BlockSpec((tm, a.shape[1]), row), pl.BlockSpec(w.shape, fixed)]
        args += [a, _mx(w)]
    return pl.pallas_call(
        functools.partial(_outproj_kernel, n_in=len(pairs)),
        grid=(n // tm,),
        in_specs=in_specs,
        out_specs=pl.BlockSpec((tm, D_MODEL), row),
        out_shape=jax.ShapeDtypeStruct((n, D_MODEL), F32),
        compiler_params=_cparams(("parallel",)),
        name="outproj",
    )(*args)


def _stack_streams(q):
    lane = _lane_iota(q.shape)
    zero = jnp.zeros_like(q)
    return jnp.concatenate([jnp.where(lane < HEAD_DIM, q, zero), jnp.where(lane >= HEAD_DIM, q, zero)], axis=0)


def _flash_stacked(q2, kt_ref, v_ref, i, tq, tk, bias_fn, m_ref, l_ref, acc_ref):
    rows = 2 * tq
    m_ref[...] = jnp.full(m_ref.shape, NEG, F32)
    l_ref[...] = jnp.zeros(l_ref.shape, F32)
    acc_ref[...] = jnp.zeros(acc_ref.shape, F32)

    def tile(j, n_sub, masked):
        width = n_sub * tk
        off = pl.multiple_of(j * tk, tk)
        s = _dot(q2, kt_ref[:, pl.ds(off, width)]) + bias_fn(j, off, n_sub)
        if masked:
            r = _row_iota((rows, width))
            s = jnp.where(_lane_iota((rows, width)) <= jnp.where(r >= tq, r - tq, r), s, NEG)
        alpha, p = _online_update(s, m_ref, l_ref)
        acc_ref[...] = alpha * acc_ref[...] + _dot(_mx(p), v_ref[pl.ds(off, width), :])

    def body(jj, carry):
        tile(jj * FLASH_WIDE, FLASH_WIDE, False)
        return carry

    lax.fori_loop(0, i // FLASH_WIDE, body, 0)
    for u in range(1, FLASH_WIDE):
        @pl.when(i % FLASH_WIDE >= u)
        def _():
            tile(i - i % FLASH_WIDE + (u - 1), 1, False)
    tile(i, 1, True)


def _attn_a_kernel(lamv_ref, gsub_ref, q_ref, k_ref, v_ref, bias_ref, o_ref,
                   m_ref, l_ref, acc_ref, *, tq, tk, lam_init):
    i = pl.program_id(2)

    def bias_fn(j, off, n_sub):
        b = jnp.concatenate([bias_ref[jnp.minimum(i - j - u, 2)] for u in range(n_sub)], axis=1)
        return jnp.concatenate([b, b], axis=0)

    _flash_stacked(_stack_streams(q_ref[...]), k_ref, v_ref, i, tq, tk, bias_fn, m_ref, l_ref, acc_ref)
    lam = _lambda(lamv_ref, lam_init)
    o = acc_ref[:tq] / l_ref[:tq] - lam * (acc_ref[tq:] / l_ref[tq:])
    o = _rms_rows(o, gsub_ref[...]) * (1.0 - lam_init)
    o_ref[...] = _mx(o)


def _attn_a_prompt(qa, ka, va, lamv, gsub, bias_tiles, nb, s, lam_init, tq=256):
    tk = tq
    nq = s // tq
    kern = functools.partial(_attn_a_kernel, tq=tq, tk=tk, lam_init=lam_init)
    return pl.pallas_call(
        kern,
        grid=(nb, A_HEADS, nq),
        in_specs=[pl.BlockSpec((4, HEAD_DIM), lambda b, h, i: (0, 0)),
                  pl.BlockSpec((1, LANES), lambda b, h, i: (0, 0)),
                  pl.BlockSpec((tq, LANES), lambda b, h, i: (b * nq + i, h)),
                  pl.BlockSpec((None, LANES, s), lambda b, h, i: (b, h, 0)),
                  pl.BlockSpec((s, LANES), lambda b, h, i: (b, h)),
                  pl.BlockSpec((None, 3, tq, tk), lambda b, h, i: (h, 0, 0, 0))],
        out_specs=pl.BlockSpec((tq, LANES), lambda b, h, i: (b * nq + i, h)),
        out_shape=jax.ShapeDtypeStruct((nb * s, A_HEADS * LANES), MXU_DT),
        scratch_shapes=[pltpu.VMEM((2 * tq, LANES), F32), pltpu.VMEM((2 * tq, LANES), F32),
                        pltpu.VMEM((2 * tq, LANES), F32)],
        compiler_params=_cparams(("parallel", "parallel", "arbitrary")),
        name="attn_a_prompt",
    )(lamv, gsub, qa, ka, va, bias_tiles)


def _attn_c_kernel(q_ref, k_ref, v_ref, cq_ref, ckt_ref, o_ref, m_ref, l_ref, acc_ref, *, tq, tk):
    hp = pl.program_id(1)
    i = pl.program_id(2)
    cq_tile = cq_ref[...]
    lane16 = _lane_iota(cq_tile.shape)
    cq2 = jnp.concatenate(
        [jnp.broadcast_to(jnp.sum(jnp.where(lane16 == 2 * hp + c, cq_tile, 0.0), axis=-1, keepdims=True),
                          (tq, LANES)) for c in range(2)], axis=0)

    def bias_fn(j, off, n_sub):
        width = n_sub * tk
        ck = ckt_ref[:, pl.ds(off, width)]
        ck2 = jnp.concatenate([jnp.broadcast_to(ck[0:1], (tq, width)),
                               jnp.broadcast_to(ck[1:2], (tq, width))], axis=0)
        return _lanes(cq2, width) - ck2

    _flash_stacked(_stack_streams(q_ref[...]), k_ref, v_ref, i, tq, tk, bias_fn, m_ref, l_ref, acc_ref)
    lane = _lane_iota((tq, LANES))
    o = jnp.where(lane < HEAD_DIM, acc_ref[:tq] / l_ref[:tq], acc_ref[tq:] / l_ref[tq:])
    o_ref[...] = _mx(o)


def _attn_c_prompt(q, k, v, cq, ckt, nb, s, tq=256):
    tk = tq
    nq = s // tq
    nhp = C_HEADS // 2
    kern = functools.partial(_attn_c_kernel, tq=tq, tk=tk)
    return pl.pallas_call(
        kern,
        grid=(nb, nhp, nq),
        in_specs=[pl.BlockSpec((tq, LANES), lambda b, h, i: (b * nq + i, h)),
                  pl.BlockSpec((None, LANES, s), lambda b, h, i: (b, h, 0)),
                  pl.BlockSpec((s, LANES), lambda b, h, i: (b, h)),
                  pl.BlockSpec((tq, C_HEADS), lambda b, h, i: (b * nq + i, 0)),
                  pl.BlockSpec((None, None, 2, s), lambda b, h, i: (b, h, 0, 0))],
        out_specs=pl.BlockSpec((tq, LANES), lambda b, h, i: (b * nq + i, h)),
        out_shape=jax.ShapeDtypeStruct((nb * s, C_HEADS * HEAD_DIM), MXU_DT),
        scratch_shapes=[pltpu.VMEM((2 * tq, LANES), F32), pltpu.VMEM((2 * tq, LANES), F32),
                        pltpu.VMEM((2 * tq, LANES), F32)],
        compiler_params=_cparams(("parallel", "parallel", "arbitrary")),
        name="attn_c_prompt",
    )(q, k, v, cq, ckt)


def _score_keys(score):
    score = jnp.where(score == 0.0, 0.0, score)
    bits = pltpu.bitcast(score, I32)
    return bits ^ (jnp.right_shift(bits, 31) & 0x7FFFFFFF)


def _topk_select(keys_ref, width, kcount, active, col):
    kf = float(kcount)
    nbits_col = int(width - 1).bit_length()

    def count(pred):
        return jnp.sum(jnp.where(pred, 1.0, 0.0), axis=1, keepdims=True)

    t0 = jnp.where(count(keys_ref[:, :width] >= 0) >= kf, 0, INT_MIN).astype(I32)

    def body(it, t):
        cand = t + jnp.left_shift(jnp.int32(1), 30 - it)
        return jnp.where(count(keys_ref[:, :width] >= cand) >= kf, cand, t)

    t = lax.fori_loop(0, 31, body, t0)
    t = jnp.where(active, t, INT_MIN)
    keys = keys_ref[:, :width]
    gt = keys > t
    eq = keys == t
    need = kf - count(gt)
    excess = jnp.where(active, count(eq) - need, 0.0)

    def tie_break():
        def tb(it, jj):
            cand = jj + jnp.left_shift(jnp.int32(1), nbits_col - 1 - it)
            c = count((keys_ref[:, :width] == t) & (col < cand))
            return jnp.where(c < need, cand, jj)
        return lax.fori_loop(0, nbits_col, tb, jnp.zeros(t.shape, I32))

    jmax = lax.cond(jnp.max(excess) > 0.0, tie_break, lambda: jnp.full(t.shape, width, I32))
    return gt | (eq & (col <= jmax))


KEY_OF_NEG_INF = -2139095041
SELECT_UNIT = 2


def _attn_b_kernel(qb_ref, qi_ref, w_ref, kk_ref, vv_ref, kiki_ref, bias_ref, o_ref,
                   keys_ref, selm_ref, m_ref, l_ref, acc_ref, *, tq, s_len, topk):
    i = pl.program_id(1)
    cw = tq
    nh = B_HEADS
    n_chunks = s_len // cw
    lane = _lane_iota((tq, LANES))
    halves = (lane < HEAD_DIM, lane >= HEAD_DIM)

    def stack_heads(ref):
        parts = []
        for h in range(nh):
            t = ref[:, (h // 2) * LANES:(h // 2 + 1) * LANES]
            parts.append(jnp.where(halves[h % 2], t, jnp.zeros_like(t)))
        return jnp.concatenate(parts, axis=0)

    row = _row_iota((tq, cw)) + i * tq
    colc = _lane_iota((tq, cw))

    qi8 = stack_heads(qi_ref)
    wt = w_ref[...] * IDX_HEAD_SCALE
    wcol = jnp.concatenate([jnp.broadcast_to(wt[:, HEAD_DIM + h:HEAD_DIM + h + 1], (tq, LANES))
                            for h in range(nh)], axis=0)

    def score_chunk(j, carry):
        off = pl.multiple_of(j * cw, cw)
        d = jnp.maximum(_dot_t(qi8, kiki_ref[pl.ds(off, cw), :]), 0.0) * _lanes(wcol, cw)
        sc = d[0:tq]
        for h in range(1, nh):
            sc = sc + d[h * tq:(h + 1) * tq]
        keys_ref[:, pl.ds(off, cw)] = _score_keys(jnp.where((colc + j * cw) <= row, sc, -jnp.inf))
        return carry

    lax.fori_loop(0, i + 1, score_chunk, 0)
    unit = SELECT_UNIT if n_chunks % SELECT_UNIT == 0 else 1
    rem = (i + 1) % unit
    for u in range(1, unit):
        @pl.when((rem != 0) & (u <= unit - rem))
        def _():
            off = pl.multiple_of((i + u) * cw, cw)
            keys_ref[:, pl.ds(off, cw)] = jnp.full((tq, cw), KEY_OF_NEG_INF, I32)
    widths = [w * unit * cw for w in range(1, n_chunks // unit + 1)]

    qpos = _row_iota((tq, 1)) + i * tq

    def select_branch(width):
        def br():
            colw = _lane_iota((tq, width))
            causal = colw <= (_row_iota((tq, width)) + i * tq)
            sel = _topk_select(keys_ref, width, topk, qpos >= topk, colw)
            selm_ref[:, :width] = jnp.where(sel & causal, 0.0, NEG)
        return br

    def causal_only():
        col = _lane_iota((tq, s_len))
        selm_ref[...] = jnp.where(col <= (_row_iota((tq, s_len)) + i * tq), 0.0, NEG)

    branch = jnp.where((i + 1) * tq > topk, 1 + i // unit, 0)
    lax.switch(branch, [causal_only] + [select_branch(w) for w in widths])

    qb8 = stack_heads(qb_ref)
    m_ref[...] = jnp.full(m_ref.shape, NEG, F32)
    l_ref[...] = jnp.zeros(l_ref.shape, F32)
    acc_ref[...] = jnp.zeros(acc_ref.shape, F32)

    def attn_chunk(j, carry):
        off = pl.multiple_of(j * cw, cw)
        bidx = jnp.minimum(i - j, 2)
        bias = jnp.concatenate([bias_ref[h, bidx] for h in range(nh)], axis=0)
        selm = selm_ref[:, pl.ds(off, cw)]
        s = _dot_t(qb8, kk_ref[pl.ds(off, cw), :]) + bias + jnp.concatenate([selm] * nh, axis=0)
        alpha, p = _online_update(s, m_ref, l_ref)
        acc_ref[...] = alpha * acc_ref[...] + _dot(_mx(p), vv_ref[pl.ds(off, cw), :])
        return carry

    lax.fori_loop(0, i + 1, attn_chunk, 0)
    o = acc_ref[...] / l_ref[...]
    for hp in range(nh // 2):
        o_ref[:, hp * LANES:(hp + 1) * LANES] = _mx(
            jnp.where(halves[0], o[2 * hp * tq:(2 * hp + 1) * tq], o[(2 * hp + 1) * tq:(2 * hp + 2) * tq]))


def _attn_b_prompt(qb, qi, small, kk, vv, kiki, bias_tiles, nb, s, topk, tq=256):
    nq = s // tq
    kern = functools.partial(_attn_b_kernel, tq=tq, s_len=s, topk=topk)
    qrow = lambda b, i: (b * nq + i, 0)
    kv = lambda b, i: (b, 0)
    rows = B_HEADS * tq
    return pl.pallas_call(
        kern,
        grid=(nb, nq),
        in_specs=[pl.BlockSpec((tq, B_HEADS * HEAD_DIM), qrow),
                  pl.BlockSpec((tq, IDX_HEADS * IDX_DIM), qrow),
                  pl.BlockSpec((tq, LANES), lambda b, i: (b * nq + i, 1)),
                  pl.BlockSpec((s, LANES), kv), pl.BlockSpec((s, LANES), kv), pl.BlockSpec((s, LANES), kv),
                  pl.BlockSpec(bias_tiles.shape, lambda b, i: (0, 0, 0, 0))],
        out_specs=pl.BlockSpec((tq, B_HEADS * HEAD_DIM), qrow),
        out_shape=jax.ShapeDtypeStruct((nb * s, B_HEADS * HEAD_DIM), MXU_DT),
        scratch_shapes=[pltpu.VMEM((tq, s), I32), pltpu.VMEM((tq, s), F32),
                        pltpu.VMEM((rows, LANES), F32), pltpu.VMEM((rows, LANES), F32),
                        pltpu.VMEM((rows, LANES), F32)],
        compiler_params=_cparams(("parallel", "arbitrary")),
        name="attn_b_prompt",
    )(qb, qi, small, kk, vv, kiki, bias_tiles)


def _cumsum_rows_kernel(x_ref, tri_ref, o_ref, carry_ref):
    j = pl.program_id(1)

    @pl.when(j == 0)
    def _():
        carry_ref[...] = jnp.zeros(carry_ref.shape, F32)

    tri = tri_ref[...]
    hi, mid, lo = _split3(x_ref[...])
    cum = _dot(tri, hi) + _dot(tri, mid) + _dot(tri, lo) + carry_ref[...]
    o_ref[...] = cum
    carry_ref[...] = cum[-1:, :]


def _cumsum_prompt(x, nb, s, blk=128):
    nj = s // blk
    w = x.shape[1]
    r = np.arange(blk)
    tri = jnp.asarray(r[:, None] >= r[None, :], dtype=MXU_DT)
    return pl.pallas_call(
        _cumsum_rows_kernel,
        grid=(nb, nj),
        in_specs=[pl.BlockSpec((blk, w), lambda b, j: (b * nj + j, 0)),
                  pl.BlockSpec((blk, blk), lambda b, j: (0, 0))],
        out_specs=pl.BlockSpec((blk, w), lambda b, j: (b * nj + j, 0)),
        out_shape=jax.ShapeDtypeStruct((nb * s, w), F32),
        scratch_shapes=[pltpu.VMEM((1, w), F32)],
        compiler_params=_cparams(("parallel", "arbitrary")),
        name="cumsum_prompt",
    )(x, tri)


def _online_update(s, m_ref, l_ref):
    m_old = m_ref[...]
    m_new = jnp.maximum(m_old, jnp.max(s, axis=-1, keepdims=True))
    alpha = jnp.exp(m_old - m_new)
    p = jnp.exp(s - _lanes(m_new, s.shape[1]))
    l_ref[...] = alpha * l_ref[...] + jnp.sum(p, axis=-1, keepdims=True)
    m_ref[...] = m_new
    return alpha, p


def _decode0_kernel(*refs, n_steps, npp, ps, topk, lam_init, group):
    (pt_ref, lamv_ref, gsub_ref, qa_ref, qi_ref, qb_ref, w8_ref, kna_ref, vna_ref, bnew_ref,
     taba_ref, tabb_ref, validb_ref) = refs[:13]
    pages = refs[13:13 + 5 * npp]
    cak, cav, cbk, cbv, cbi = (pages[0:npp], pages[npp:2 * npp], pages[2 * npp:3 * npp],
                               pages[3 * npp:4 * npp], pages[4 * npp:5 * npp])
    oa_ref, ob_ref, m_ref, l_ref, acc_ref, kb_ref, vb_ref, ki_ref, keys_ref = refs[13 + 5 * npp:]
    g = pl.program_id(0) % group
    step = pl.program_id(1)
    past = n_steps * npp * ps
    lk = kb_ref.shape[2]

    @pl.when(step == 0)
    def _():
        m_ref[...] = jnp.full(m_ref.shape, NEG, F32)
        l_ref[...] = jnp.zeros(l_ref.shape, F32)
        acc_ref[...] = jnp.zeros(acc_ref.shape, F32)

    qa = qa_ref[...]
    far = taba_ref[0]
    tail = taba_ref[jnp.where(step == n_steps - 1, 1, 0)]
    s = jnp.concatenate([_dot(qa, _mx(cak[j][...])) + (tail if j == npp - 1 else far) for j in range(npp)], axis=1)
    alpha, p = _online_update(s, m_ref, l_ref)
    p = _mx(p)
    for h in range(A_HEADS):
        rows = slice(h * 2 * T8, (h + 1) * 2 * T8)
        upd = alpha[rows] * acc_ref[rows]
        for j in range(npp):
            vh = _mx(cav[j][pl.ds(h, ps, stride=A_HEADS), :])
            upd = upd + _dot(p[rows, j * ps:(j + 1) * ps], vh)
        acc_ref[rows] = upd

    for j in range(npp):
        off = pl.multiple_of((step * npp + j) * ps, ps)
        kb_ref[g, :, pl.ds(off, ps)] = _mx(cbk[j][...])
        vb_ref[g, :, pl.ds(off, ps)] = _mx(cbv[j][...])
        ki_ref[g, :, pl.ds(off, ps)] = _mx(cbi[j][...])

    @pl.when(step == n_steps - 1)
    def _():
        s_new = _dot_t(qa, _mx(kna_ref[...])) + taba_ref[2][:, :kna_ref.shape[0]]
        alpha2, p2 = _online_update(s_new, m_ref, l_ref)
        vn = _mx(vna_ref[...])
        lam = _lambda(lamv_ref, lam_init)
        for h in range(A_HEADS):
            rows = slice(h * 2 * T8, (h + 1) * 2 * T8)
            o16 = (alpha2[rows] * acc_ref[rows] + _dot(_mx(p2[rows]), vn[:, h * LANES:(h + 1) * LANES])) / l_ref[rows]
            o = o16[:T8] - lam * o16[T8:]
            oa_ref[h] = _rms_rows(o, gsub_ref[...]) * (1.0 - lam_init)

        bn = bnew_ref[g]
        kb_ref[g, :, past:past + LANES] = _mx(bn[0])
        vb_ref[g, :, past:past + LANES] = _mx(bn[1])
        ki_ref[g, :, past:past + LANES] = _mx(bn[2])

    @pl.when((step == n_steps - 1) & (g == group - 1))
    def _():
        valid = validb_ref[...] == 0.0
        for gg in range(group):
            dots = jnp.maximum(_dot(qi_ref[gg], ki_ref[gg]), 0.0)
            w8 = w8_ref[gg] * IDX_HEAD_SCALE
            score = jnp.zeros((T8, lk), F32)
            for h in range(IDX_HEADS):
                score = score + w8[:, h:h + 1] * dots[h * T8:(h + 1) * T8]
            keys_ref[gg * T8:(gg + 1) * T8, :] = _score_keys(jnp.where(valid, score, -jnp.inf))
        rows = group * T8
        sel = _topk_select(keys_ref, lk, topk, jnp.full((rows, 1), True), _lane_iota((rows, lk)))
        selm = jnp.where(sel & jnp.concatenate([valid] * group, axis=0), 0.0, NEG)
        for gg in range(group):
            sg = selm[gg * T8:(gg + 1) * T8]
            sb = _dot(qb_ref[gg], kb_ref[gg]) + tabb_ref[...] + jnp.concatenate([sg] * B_HEADS, axis=0)
            mb = jnp.max(sb, axis=-1, keepdims=True)
            pb = jnp.exp(sb - mb)
            lb = jnp.sum(pb, axis=-1, keepdims=True)
            ob = _dot_t(_mx(pb), vb_ref[gg]) / lb
            for h in range(B_HEADS):
                ob_ref[gg, h] = ob[h * T8:(h + 1) * T8]


def _decode0(page_table, lamv, gsub, qa_bd, qi64, qb64, w8, kna, vna, bnew_t,
             cak_t, cav_r, cbk_t, cbv_t, cbi_t, taba, tabb, validb, layer, topk, lam_init):
    db, npg = page_table.shape
    ps = cak_t.shape[3]
    npp = PAGES_PER_STEP if npg % PAGES_PER_STEP == 0 else 1
    n_steps = npg // npp
    lk = tabb.shape[1]
    group = max(gsz for gsz in (8, 4, 2, 1) if db % gsz == 0)
    per_b = lambda b, p, pt: (b, 0, 0)
    per_g = lambda b, p, pt: (b // group, 0, 0)
    per_g4 = lambda b, p, pt: (b // group, 0, 0, 0)
    fixed2 = lambda b, p, pt: (0, 0)
    fixed3 = lambda b, p, pt: (0, 0, 0)

    def page_specs(arr):
        blk = (None, None) + arr.shape[2:]
        return [pl.BlockSpec(blk, lambda b, p, pt, j=j: (layer, pt[b * npg + p * npp + j], 0, 0)) for j in range(npp)]

    caches = (cak_t, cav_r, cbk_t, cbv_t, cbi_t)
    kern = functools.partial(_decode0_kernel, n_steps=n_steps, npp=npp, ps=ps, topk=topk, lam_init=lam_init,
                             group=group)
    grid_spec = pltpu.PrefetchScalarGridSpec(
        num_scalar_prefetch=1,
        grid=(db, n_steps),
        in_specs=[pl.BlockSpec((4, HEAD_DIM), fixed2), pl.BlockSpec((1, LANES), fixed2),
                  pl.BlockSpec((None,) + qa_bd.shape[1:], per_b),
                  pl.BlockSpec((group,) + qi64.shape[1:], per_g),
                  pl.BlockSpec((group,) + qb64.shape[1:], per_g),
                  pl.BlockSpec((group,) + w8.shape[1:], per_g),
                  pl.BlockSpec((None,) + kna.shape[1:], per_b),
                  pl.BlockSpec((None,) + vna.shape[1:], per_b),
                  pl.BlockSpec((group,) + bnew_t.shape[1:], per_g4),
                  pl.BlockSpec(taba.shape, fixed3), pl.BlockSpec(tabb.shape, fixed2),
                  pl.BlockSpec(validb.shape, fixed2)]
                 + [sp for c in caches for sp in page_specs(c)],
        out_specs=[pl.BlockSpec((None, A_HEADS, T8, LANES), lambda b, p, pt: (b, 0, 0, 0)),
                   pl.BlockSpec((group, B_HEADS, T8, HEAD_DIM), per_g4)],
        scratch_shapes=[pltpu.VMEM((64, LANES), F32), pltpu.VMEM((64, LANES), F32),
                        pltpu.VMEM((64, LANES), F32),
                        pltpu.VMEM((group, HEAD_DIM, lk), MXU_DT), pltpu.VMEM((group, HEAD_DIM, lk), MXU_DT),
                        pltpu.VMEM((group, IDX_DIM, lk), MXU_DT), pltpu.VMEM((group * T8, lk), I32)],
    )
    return pl.pallas_call(
        kern,
        grid_spec=grid_spec,
        out_shape=[jax.ShapeDtypeStruct((db, A_HEADS, T8, LANES), F32),
                   jax.ShapeDtypeStruct((db, B_HEADS, T8, HEAD_DIM), F32)],
        compiler_params=_cparams(("arbitrary", "arbitrary")),
        name="decode0",
    )(page_table.reshape(-1), lamv, gsub, qa_bd, qi64, qb64, w8, kna, vna, bnew_t, taba, tabb, validb,
      *[c for c in caches for _ in range(npp)])


def _expand_rows(x):
    hh, ww = x.shape
    return jnp.broadcast_to(x[:, None, :], (hh, T8, ww)).reshape(hh * T8, ww)


def _decode1_kernel(*refs, n_steps, npp, ps):
    pt_ref, q_ref, cq_ref, kn_ref, vn_ref, cnew_ref, maskn_ref, tri_ref = refs[:8]
    ck = refs[8:8 + npp]
    cv = refs[8 + npp:8 + 2 * npp]
    cf = refs[8 + 2 * npp:8 + 3 * npp]
    o_ref, m_ref, l_ref, acc_ref, suf_ref = refs[8 + 3 * npp:]
    step = pl.program_id(1)

    @pl.when(step == 0)
    def _():
        m_ref[...] = jnp.full(m_ref.shape, NEG, F32)
        l_ref[...] = jnp.zeros(l_ref.shape, F32)
        acc_ref[...] = jnp.zeros(acc_ref.shape, F32)
        suf_ref[...] = jnp.zeros(suf_ref.shape, F32)

    q = q_ref[...]
    cq = jnp.broadcast_to(cq_ref[...], (q.shape[0], LANES))
    tri = tri_ref[...]
    running = suf_ref[...]
    sufs = [None] * npp
    for j in reversed(range(npp)):
        hi, mid, lo = _split3(cf[j][...])
        cum = _dot(hi, tri) + _dot(mid, tri) + _dot(lo, tri)
        tot = jnp.broadcast_to(cum[:, ps - 1:ps], cum.shape)
        sufs[j] = running + tot - cum
        running = running + tot
    suf_ref[...] = running
    s = jnp.concatenate([_dot(q, _mx(ck[j][...])) for j in range(npp)], axis=1)
    s = s + _lanes(cq, npp * ps) + _expand_rows(jnp.concatenate(sufs, axis=1))
    alpha, p = _online_update(s, m_ref, l_ref)
    p = _mx(p)
    upd = _lanes(alpha, acc_ref.shape[1]) * acc_ref[...]
    for j in range(npp):
        upd = upd + _dot_t(p[:, j * ps:(j + 1) * ps], _mx(cv[j][...]))
    acc_ref[...] = upd

    @pl.when(step == n_steps - 1)
    def _():
        nn = kn_ref.shape[0]
        s_new = (_dot_t(q, _mx(kn_ref[...])) + cq[:, :nn] - _expand_rows(cnew_ref[...]) + maskn_ref[...])
        alpha2, p2 = _online_update(s_new, m_ref, l_ref)
        o_full = ((_lanes(alpha2, acc_ref.shape[1]) * acc_ref[...] + _dot(_mx(p2), _mx(vn_ref[...])))
                  / _lanes(l_ref[...], acc_ref.shape[1]))
        for h in range(C_HEADS):
            o_ref[h] = o_full[h * T8:(h + 1) * T8, h * HEAD_DIM:(h + 1) * HEAD_DIM]


def _decode1(page_table, q_bd, cq, kn, vn, cnew_t, cck_t, ccv_t, ccf_t, maskn, layer):
    db, npg = page_table.shape
    ps = cck_t.shape[3]
    assert ps == LANES, ps
    npp = PAGES_PER_STEP if npg % PAGES_PER_STEP == 0 else 1
    n_steps = npg // npp
    per_b = lambda b, p, pt: (b, 0, 0)
    fixed = lambda b, p, pt: (0, 0)
    r = np.arange(ps)
    tri = jnp.asarray(r[:, None] <= r[None, :], dtype=MXU_DT)

    def page_specs(arr):
        blk = (None, None) + arr.shape[2:]
        return [pl.BlockSpec(blk, lambda b, p, pt, j=j: (layer, pt[b * npg + (n_steps - 1 - p) * npp + j], 0, 0))
                for j in range(npp)]

    kern = functools.partial(_decode1_kernel, n_steps=n_steps, npp=npp, ps=ps)
    rows = C_HEADS * T8
    grid_spec = pltpu.PrefetchScalarGridSpec(
        num_scalar_prefetch=1,
        grid=(db, n_steps),
        in_specs=[pl.BlockSpec((None,) + q_bd.shape[1:], per_b),
                  pl.BlockSpec((None,) + cq.shape[1:], per_b),
                  pl.BlockSpec((None,) + kn.shape[1:], per_b),
                  pl.BlockSpec((None,) + vn.shape[1:], per_b),
                  pl.BlockSpec((None,) + cnew_t.shape[1:], per_b),
                  pl.BlockSpec(maskn.shape, fixed), pl.BlockSpec(tri.shape, fixed)]
                 + page_specs(cck_t) + page_specs(ccv_t) + page_specs(ccf_t),
        out_specs=pl.BlockSpec((None, C_HEADS, T8, HEAD_DIM), lambda b, p, pt: (b, 0, 0, 0)),
        scratch_shapes=[pltpu.VMEM((rows, LANES), F32), pltpu.VMEM((rows, LANES), F32),
                        pltpu.VMEM((rows, C_HEADS * HEAD_DIM), F32), pltpu.VMEM((C_HEADS, LANES), F32)],
    )
    return pl.pallas_call(
        kern,
        grid_spec=grid_spec,
        out_shape=jax.ShapeDtypeStruct((db, C_HEADS, T8, HEAD_DIM), F32),
        compiler_params=_cparams(("parallel", "arbitrary")),
        name="decode1",
    )(page_table.reshape(-1), q_bd, cq, kn, vn, cnew_t, maskn, tri,
      *([cck_t] * npp), *([ccv_t] * npp), *([ccf_t] * npp))


def _router_kernel(x_ref, g_ref, whi_ref, wlo_ref, b_ref, t_ref, route_ref):
    t = _rms_rows(x_ref[...], g_ref[...])
    thi, tlo = _split2(t)
    t_ref[...] = thi
    whi = whi_ref[...]
    logits = _dot(thi, whi) + _dot(tlo, whi) + _dot(thi, wlo_ref[...]) + b_ref[...]
    lane = _lane_iota(logits.shape)
    big = jnp.int32(1 << 20)

    def first_max(v):
        mx = jnp.max(v, axis=-1, keepdims=True)
        idx = jnp.min(jnp.where(v == mx, lane, big), axis=-1, keepdims=True)
        return mx, idx

    glog = jnp.where(lane < N_GROUPS, logits, -jnp.inf)
    gmax, gidx = first_max(glog)
    grp_w = 1.0 / jnp.sum(jnp.exp(glog - gmax), axis=-1, keepdims=True)
    el = lane - N_GROUPS
    in_grp = (el >= 0) & (el < N_EXPERTS) & (jnp.right_shift(el, 3) == gidx)
    v1 = jnp.where(in_grp, logits, -jnp.inf)
    top1, i1 = first_max(v1)
    v2 = jnp.where(lane == i1, -jnp.inf, v1)
    top2, i2 = first_max(v2)
    e2 = jnp.exp(top2 - top1)
    w1 = grp_w / (1.0 + e2)
    w2 = grp_w * e2 / (1.0 + e2)
    route = jnp.where(lane == 0, (i1 - N_GROUPS).astype(F32),
                      jnp.where(lane == 1, (i2 - N_GROUPS).astype(F32),
                                jnp.where(lane == 2, w1, jnp.where(lane == 3, w2, 0.0))))
    route_ref[...] = route


def _router(x, g, w_group, b_group, w_router, b_router, tm=512):
    n = x.shape[0]
    tm = _tile(n, tm)
    pad = LANES - N_GROUPS - N_EXPERTS
    w = jnp.concatenate([w_group, w_router, jnp.zeros((D_MODEL, pad), F32)], axis=1)
    whi = _mx(w)
    wlo = _mx(w - whi.astype(F32))
    b = jnp.concatenate([b_group, b_router, jnp.zeros((pad,), F32)])[None, :]
    row = lambda i: (i, 0)
    fixed = lambda i: (0, 0)
    return pl.pallas_call(
        _router_kernel,
        grid=(n // tm,),
        in_specs=[pl.BlockSpec((tm, D_MODEL), row), pl.BlockSpec((1, D_MODEL), fixed),
                  pl.BlockSpec((D_MODEL, LANES), fixed), pl.BlockSpec((D_MODEL, LANES), fixed),
                  pl.BlockSpec((1, LANES), fixed)],
        out_specs=[pl.BlockSpec((tm, D_MODEL), row), pl.BlockSpec((tm, LANES), row)],
        out_shape=[jax.ShapeDtypeStruct((n, D_MODEL), MXU_DT), jax.ShapeDtypeStruct((n, LANES), F32)],
        compiler_params=_cparams(("parallel",)),
        name="moe_router",
    )(x, g[None, :], whi, wlo, b)


def _moe_dense_kernel(x_ref, t_ref, route_ref, wgu_ref, wd_ref, o_ref):
    e = pl.program_id(1)

    @pl.when(e == 0)
    def _():
        o_ref[...] = x_ref[...]

    au = _dot(t_ref[...], wgu_ref[...])
    a = au[:, :EXPERT_FF]
    u = au[:, EXPERT_FF:]
    r = route_ref[...]
    ef = e.astype(F32)
    gate = jnp.where(r[:, 0:1] == ef, r[:, 2:3], 0.0) + jnp.where(r[:, 1:2] == ef, r[:, 3:4], 0.0)
    hdn = a * (1.0 / (1.0 + jnp.exp(-a))) * u * gate
    o_ref[...] += _dot(_mx(hdn), wd_ref[...])


def _moe_dense(x, t, route, w_gate, w_up, w_down, tm=1536):
    n = x.shape[0]
    tm = _tile(n, tm)
    wgu = _mx(jnp.concatenate([w_gate.reshape(N_EXPERTS, D_MODEL, EXPERT_FF),
                               w_up.reshape(N_EXPERTS, D_MODEL, EXPERT_FF)], axis=-1))
    wd = _mx(w_down.reshape(N_EXPERTS, EXPERT_FF, D_MODEL))
    row = lambda i, e: (i, 0)
    once = pl.Buffered(1)
    return pl.pallas_call(
        _moe_dense_kernel,
        grid=(n // tm, N_EXPERTS),
        in_specs=[pl.BlockSpec((tm, D_MODEL), row, pipeline_mode=once),
                  pl.BlockSpec((tm, D_MODEL), row, pipeline_mode=once),
                  pl.BlockSpec((tm, LANES), row, pipeline_mode=once),
                  pl.BlockSpec((None, D_MODEL, 2 * EXPERT_FF), lambda i, e: (e, 0, 0)),
                  pl.BlockSpec((None, EXPERT_FF, D_MODEL), lambda i, e: (e, 0, 0))],
        out_specs=pl.BlockSpec((tm, D_MODEL), row),
        out_shape=jax.ShapeDtypeStruct((n, D_MODEL), F32),
        compiler_params=_cparams(("parallel", "arbitrary")),
        name="moe_dense",
    )(x, t, route, wgu, wd)


MOE_TILE = 512
DMA_CHUNK = 512


def _route(x, g, whi, wlo, b):
    t = _rms_rows(x, g)
    thi, tlo = _split2(t)
    logits = _dot(thi, whi) + _dot(tlo, whi) + _dot(thi, wlo) + b
    lane = _lane_iota(logits.shape)
    big = jnp.int32(1 << 20)

    def first_max(v):
        mx = jnp.max(v, axis=-1, keepdims=True)
        idx = jnp.min(jnp.where(v == mx, lane, big), axis=-1, keepdims=True)
        return mx, idx

    glog = jnp.where(lane < N_GROUPS, logits, -jnp.inf)
    gmax, gidx = first_max(glog)
    grp_w = 1.0 / jnp.sum(jnp.exp(glog - gmax), axis=-1, keepdims=True)
    el = lane - N_GROUPS
    in_grp = (el >= 0) & (el < N_EXPERTS) & (jnp.right_shift(el, 3) == gidx)
    v1 = jnp.where(in_grp, logits, -jnp.inf)
    top1, i1 = first_max(v1)
    v2 = jnp.where(lane == i1, -jnp.inf, v1)
    top2, i2 = first_max(v2)
    e2 = jnp.exp(top2 - top1)
    return thi, gidx, i1 - N_GROUPS, i2 - N_GROUPS, grp_w / (1.0 + e2), grp_w * e2 / (1.0 + e2)


def _group_rank_kernel(x_ref, g_ref, whi_ref, wlo_ref, b_ref, tri_ref, info_ref, cnt_ref, carry_ref):
    @pl.when(pl.program_id(0) == 0)
    def _():
        carry_ref[...] = jnp.zeros(carry_ref.shape, F32)

    gidx = _route(x_ref[...], g_ref[...], whi_ref[...], wlo_ref[...], b_ref[...])[1]
    lane = _lane_iota(info_ref.shape)
    onehot = jnp.where(lane == gidx, 1.0, 0.0)
    before = _dot(tri_ref[...], _mx(onehot)) + carry_ref[...]
    rank = jnp.sum(jnp.where(lane == gidx, before, 0.0), axis=-1, keepdims=True)
    info_ref[...] = jnp.where(lane == 0, gidx.astype(F32), jnp.where(lane == 1, rank, 0.0))
    carry = carry_ref[...] + jnp.sum(onehot, axis=0, keepdims=True)
    carry_ref[...] = carry
    cnt_ref[...] = jnp.broadcast_to(carry, cnt_ref.shape)


def _gather_rows_kernel(idx_ref, src_ref, dst_ref, sem, *, n_rows, chunk):
    def wait_chunk():
        def body(k, c):
            pltpu.make_async_copy(src_ref.at[pl.ds(0, 1)], dst_ref.at[pl.ds(0, 1)], sem).wait()
            return c
        lax.fori_loop(0, chunk, body, 0)

    def chunk_body(c, carry):
        def issue(k, cc):
            r = c * chunk + k
            pltpu.make_async_copy(src_ref.at[pl.ds(idx_ref[r], 1)], dst_ref.at[pl.ds(r, 1)], sem).start()
            return cc
        lax.fori_loop(0, chunk, issue, 0, unroll=8)

        @pl.when(c > 0)
        def _():
            wait_chunk()
        return carry

    lax.fori_loop(0, n_rows // chunk, chunk_body, 0)
    wait_chunk()


def _gather_rows(src, idx):
    n_rows = idx.shape[0]
    chunk = _tile(n_rows, DMA_CHUNK)
    grid_spec = pltpu.PrefetchScalarGridSpec(
        num_scalar_prefetch=1, grid=(1,),
        in_specs=[pl.BlockSpec(memory_space=pl.ANY)],
        out_specs=pl.BlockSpec(memory_space=pl.ANY),
        scratch_shapes=[pltpu.SemaphoreType.DMA(())],
    )
    return pl.pallas_call(
        functools.partial(_gather_rows_kernel, n_rows=n_rows, chunk=chunk),
        grid_spec=grid_spec,
        out_shape=jax.ShapeDtypeStruct((n_rows, src.shape[1]), src.dtype),
        compiler_params=_cparams(("arbitrary",)),
        name="gather_rows",
    )(idx, src)


def _moe_group_kernel(tg_ref, xs_ref, g_ref, whi_ref, wlo_ref, b_ref, wgu_ref, wd_ref, o_ref, t_ref, gate_ref):
    e8 = pl.program_id(1)
    lane = _lane_iota(gate_ref.shape)

    @pl.when(e8 == 0)
    def _():
        x = xs_ref[...]
        thi, _, e1, e2, w1, w2 = _route(x, g_ref[...], whi_ref[...], wlo_ref[...], b_ref[...])
        t_ref[...] = thi
        eid = lane + tg_ref[pl.program_id(0)] * EXPERTS_PER_GROUP
        gates = jnp.where(e1 == eid, w1, 0.0) + jnp.where(e2 == eid, w2, 0.0)
        gate_ref[...] = jnp.where(lane < EXPERTS_PER_GROUP, gates, 0.0)
        o_ref[...] = x

    au = _dot(t_ref[...], wgu_ref[...])
    a = au[:, :EXPERT_FF]
    u = au[:, EXPERT_FF:]
    gate = jnp.sum(jnp.where(lane == e8, gate_ref[...], 0.0), axis=-1, keepdims=True)
    hdn = a * (1.0 / (1.0 + jnp.exp(-a))) * u * gate
    o_ref[...] += _dot(_mx(hdn), wd_ref[...])


def _hier_moe(x, g, w_group, b_group, w_router, b_router, w_gate, w_up, w_down):
    n = x.shape[0]
    tm = _tile(n, MOE_TILE)
    n_tiles = n // tm + N_GROUPS
    pad = LANES - N_GROUPS - N_EXPERTS
    w = jnp.concatenate([w_group, w_router, jnp.zeros((D_MODEL, pad), F32)], axis=1)
    whi = _mx(w)
    wlo = _mx(w - whi.astype(F32))
    b = jnp.concatenate([b_group, b_router, jnp.zeros((pad,), F32)])[None, :]
    r = np.arange(tm)
    tri = jnp.asarray(r[:, None] > r[None, :], dtype=MXU_DT)
    row = lambda i: (i, 0)
    fixed = lambda i: (0, 0)
    router_specs = [pl.BlockSpec((1, D_MODEL), fixed), pl.BlockSpec((D_MODEL, LANES), fixed),
                    pl.BlockSpec((D_MODEL, LANES), fixed), pl.BlockSpec((1, LANES), fixed)]
    info, cnt = pl.pallas_call(
        _group_rank_kernel,
        grid=(n // tm,),
        in_specs=[pl.BlockSpec((tm, D_MODEL), row)] + router_specs + [pl.BlockSpec((tm, tm), fixed)],
        out_specs=[pl.BlockSpec((tm, LANES), row), pl.BlockSpec((T8, LANES), fixed)],
        out_shape=[jax.ShapeDtypeStruct((n, LANES), F32), jax.ShapeDtypeStruct((T8, LANES), F32)],
        scratch_shapes=[pltpu.VMEM((1, LANES), F32)],
        compiler_params=_cparams(("arbitrary",)),
        name="moe_group_rank",
    )(x, g[None, :], whi, wlo, b, tri)
    gid = info[:, 0].astype(I32)
    rank = info[:, 1].astype(I32)
    counts = cnt[0, :N_GROUPS].astype(I32)
    padded = (counts + tm - 1) // tm * tm
    gend = jnp.cumsum(padded)
    goff = gend - padded
    pos = rank + sum(jnp.where(gid == k, goff[k], 0) for k in range(N_GROUPS))
    tile_start = jnp.arange(n_tiles, dtype=I32) * tm
    tile_group = jnp.minimum(sum((tile_start >= gend[k]).astype(I32) for k in range(N_GROUPS)), N_GROUPS - 1)
    inv = jnp.zeros((n_tiles * tm,), I32).at[pos].set(jnp.arange(n, dtype=I32))
    xs = _gather_rows(x, inv)

    wgu = _mx(jnp.concatenate([w_gate.reshape(N_EXPERTS, D_MODEL, EXPERT_FF),
                               w_up.reshape(N_EXPERTS, D_MODEL, EXPERT_FF)], axis=-1))
    wd = _mx(w_down.reshape(N_EXPERTS, EXPERT_FF, D_MODEL))
    trow = lambda t, e, tg: (t, 0)
    tfixed = lambda t, e, tg: (0, 0)
    expert = lambda t, e, tg: (tg[t] * EXPERTS_PER_GROUP + e, 0, 0)
    grid_spec = pltpu.PrefetchScalarGridSpec(
        num_scalar_prefetch=1,
        grid=(n_tiles, EXPERTS_PER_GROUP),
        in_specs=[pl.BlockSpec((tm, D_MODEL), trow),
                  pl.BlockSpec((1, D_MODEL), tfixed), pl.BlockSpec((D_MODEL, LANES), tfixed),
                  pl.BlockSpec((D_MODEL, LANES), tfixed), pl.BlockSpec((1, LANES), tfixed),
                  pl.BlockSpec((None, D_MODEL, 2 * EXPERT_FF), expert),
                  pl.BlockSpec((None, EXPERT_FF, D_MODEL), expert)],
        out_specs=pl.BlockSpec((tm, D_MODEL), trow),
        scratch_shapes=[pltpu.VMEM((tm, D_MODEL), MXU_DT), pltpu.VMEM((tm, LANES), F32)],
    )
    ys = pl.pallas_call(
        _moe_group_kernel,
        grid_spec=grid_spec,
        out_shape=jax.ShapeDtypeStruct((n_tiles * tm, D_MODEL), F32),
        compiler_params=_cparams(("parallel", "arbitrary")),
        name="moe_group",
    )(tile_group, xs, g[None, :], whi, wlo, b, wgu, wd)
    return _gather_rows(ys, pos)


def _decode_tables(rel_bias, past, t_new, lk):
    t8 = np.arange(T8)
    lane = np.arange(LANES)
    far = np.full((T8, LANES), REL_BUCKETS - 1, np.int32)
    d_last = (past + t8[:, None]) - (past - LANES + lane[None, :])
    d_new = t8[:, None] - lane[None, :]
    ok_new = (d_new >= 0) & (lane[None, :] < t_new)
    ta = _bias_expand(rel_bias, np.stack([far, _t5_bucket_np(d_last), _t5_bucket_np(d_new)]), 0, A_HEADS)
    ta = jnp.where(jnp.asarray(ok_new)[None, None] | (jnp.arange(3) < 2)[None, :, None, None], ta, NEG)
    taba = jnp.broadcast_to(jnp.transpose(ta, (1, 0, 2, 3))[:, :, None], (3, A_HEADS, 2, T8, LANES))
    taba = taba.reshape(3, A_HEADS * 2 * T8, LANES)
    kpos = np.arange(lk)
    d_b = (past + t8[:, None]) - kpos[None, :]
    tabb = _bias_expand(rel_bias, _t5_bucket_np(d_b), A_HEADS, B_HEADS).reshape(B_HEADS * T8, lk)
    valid = (kpos[None, :] < past) | ((d_b >= 0) & (kpos[None, :] < past + t_new))
    validb = jnp.asarray(np.where(valid, 0.0, NEG), F32)
    return taba, tabb, validb


def _pad_rows(x, rows):
    pad = [(0, 0)] * x.ndim
    pad[1] = (0, rows - x.shape[1])
    return jnp.pad(x, pad)


def kernel(x_prompt, x_sample, cache_a_k, cache_a_v, cache_b_k, cache_b_v, cache_b_kidx, cache_c_k, cache_c_v, cache_c_logf, page_table, rel_bias, ab_norm, ab_w_in, a_q_norm, a_k_norm, b_q_norm, b_k_norm, a_lambda_q1, a_lambda_k1, a_lambda_q2, a_lambda_k2, a_sub_norm, ab_w_out, c_norm, c_w_in, c_forget_bias, c_q_norm, c_k_norm, c_w_out, ffn_norm, moe_w_group, moe_b_group, moe_w_router, moe_b_router, moe_w_gate, moe_w_up, moe_w_down):
    nb, s, d = x_prompt.shape
    db, ts, _ = x_sample.shape
    npg = page_table.shape[1]
    pool, ps = cache_a_k.shape[1], cache_a_k.shape[2]
    past = npg * ps
    n_p = nb * s
    n_s = db * ts
    depth = ffn_norm.shape[0]
    topk_p = min(IDX_TOPK_MAX, s // 4)
    topk_s = min(IDX_TOPK_MAX, (past + ts) // 4)
    lk = past + LANES

    cak_t = jnp.transpose(cache_a_k, (0, 1, 3, 4, 5, 2)).reshape(-1, pool, 2 * A_HEADS * HEAD_DIM, ps)
    cav_r = cache_a_v.reshape(-1, pool, ps * A_HEADS, 2 * HEAD_DIM)
    cbk_t = jnp.transpose(cache_b_k, (0, 1, 3, 2))
    cbv_t = jnp.transpose(cache_b_v, (0, 1, 3, 2))
    cbi_t = jnp.transpose(cache_b_kidx, (0, 1, 3, 2))
    cck_t = jnp.transpose(cache_c_k, (0, 1, 3, 4, 2)).reshape(-1, pool, C_HEADS * HEAD_DIM, ps)
    ccv_t = jnp.transpose(cache_c_v, (0, 1, 3, 4, 2)).reshape(-1, pool, C_HEADS * HEAD_DIM, ps)
    ccf_t = jnp.transpose(cache_c_logf, (0, 1, 3, 2))

    x = jnp.concatenate([x_prompt.reshape(n_p, d), x_sample.reshape(n_s, d)], axis=0)
    outs_p = {k: [] for k in ("ak", "av", "bk", "bv", "bi", "ck", "cv", "cf")}
    outs_s = {k: [] for k in ("ak", "av", "bk", "bv", "bi", "ck", "cv", "cf")}

    def smp(a):
        return a[n_p:].reshape(db, ts, a.shape[1])

    for layer in range(depth):
        if layer % 2 == 0:
            e = layer // 2
            lam_init = 0.8 - 0.6 * math.exp(-0.3 * layer)
            lamv = jnp.stack([a_lambda_q1[e], a_lambda_k1[e], a_lambda_q2[e], a_lambda_k2[e]])
            gsub = a_sub_norm[e][None, :]
            (qa, ka_s, kat, katbf, qb, small, smallt, kk, vv, kiki, va, vabf, qi) = _proj0(
                x, ab_norm[e], ab_w_in[e], a_q_norm[e], a_k_norm[e], b_q_norm[e], b_k_norm[e], n_p, s)
            tq_a = 256
            bias_a = _bias_expand(rel_bias, _prompt_bucket_tiles(tq_a), 0, A_HEADS)
            oa_p = _attn_a_prompt(qa, katbf, vabf, lamv, gsub, bias_a, nb, s, lam_init, tq=tq_a)
            tq_b = 256
            bias_b = _bias_expand(rel_bias, _prompt_bucket_tiles(tq_b), A_HEADS, B_HEADS)
            ob_p = _attn_b_prompt(qb, qi, small, kk, vv, kiki, bias_b, nb, s, topk_p, tq=tq_b)
            qa_s = _pad_rows(smp(qa), T8)
            hc = np.arange(2 * A_HEADS)
            colmask = jnp.asarray((np.arange(qa_s.shape[2])[None, :] // HEAD_DIM) == hc[:, None], MXU_DT)
            qa_bd = (qa_s[:, None, :, :] * colmask[None, :, None, :]).reshape(db, 2 * A_HEADS * T8, -1)

            def heads_rows(a, nh):
                a = _pad_rows(a, T8).reshape(db, T8, nh, HEAD_DIM)
                return jnp.transpose(a, (0, 2, 1, 3)).reshape(db, nh * T8, HEAD_DIM)

            qi64 = heads_rows(smp(qi), IDX_HEADS)
            qb64 = heads_rows(smp(qb), B_HEADS)
            small_s = smp(small)
            w8 = _pad_rows(small_s[:, :, 192:200], T8)
            ka_s3 = ka_s.reshape(db, ts, -1)
            va_s3 = smp(va)
            kna = _pad_rows(ka_s3, 16)
            vna = _pad_rows(va_s3, 16)
            bnew_t = jnp.transpose(_pad_rows(small_s[:, :, :192], LANES).reshape(db, LANES, 3, HEAD_DIM), (0, 2, 3, 1))
            taba, tabb, validb = _decode_tables(rel_bias, past, ts, lk)
            oa_d, ob_d = _decode0(page_table, lamv, gsub, qa_bd, qi64, qb64, w8, kna, vna, bnew_t,
                                  cak_t, cav_r, cbk_t, cbv_t, cbi_t, taba, tabb, validb, e, topk_s, lam_init)
            oa_s = jnp.transpose(oa_d[:, :, :ts], (0, 2, 1, 3)).reshape(n_s, A_HEADS * LANES)
            ob_s = jnp.transpose(ob_d[:, :, :ts], (0, 2, 1, 3)).reshape(n_s, B_HEADS * HEAD_DIM)
            oa = jnp.concatenate([oa_p, _mx(oa_s)], axis=0)
            ob = jnp.concatenate([ob_p, _mx(ob_s)], axis=0)
            w_out = ab_w_out[e]
            x = _outproj(x, [(oa, w_out[:A_HEADS * LANES]), (ob, w_out[A_HEADS * LANES:])])
            outs_p["ak"].append(jnp.transpose(kat.reshape(nb, A_HEADS, 2, HEAD_DIM, s), (0, 4, 1, 2, 3)))
            outs_p["av"].append(va[:n_p].reshape(nb, s, A_HEADS, 2 * HEAD_DIM))
            outs_p["bk"].append(jnp.transpose(smallt[:, 0:64], (0, 2, 1)))
            outs_p["bv"].append(jnp.transpose(smallt[:, 64:128], (0, 2, 1)))
            outs_p["bi"].append(jnp.transpose(smallt[:, 128:192], (0, 2, 1)))
            outs_s["ak"].append(ka_s3.reshape(db, ts, A_HEADS, 2, HEAD_DIM))
            outs_s["av"].append(va_s3.reshape(db, ts, A_HEADS, 2 * HEAD_DIM))
            outs_s["bk"].append(small_s[:, :, 0:64])
            outs_s["bv"].append(small_s[:, :, 64:128])
            outs_s["bi"].append(small_s[:, :, 128:192])
        else:
            o = layer // 2
            q, k_s, kt, ktbf, v_s, vt, vbf, logf128, logft = _proj1(
                x, c_norm[o], c_w_in[o], c_forget_bias[o], c_q_norm[o], c_k_norm[o], n_p, s)
            logf = logf128[:, :C_HEADS]
            cum_p = _cumsum_prompt(logf128[:n_p], nb, s)[:, :C_HEADS]
            ckt = jnp.transpose(cum_p.reshape(nb, s, C_HEADS // 2, 2), (0, 2, 3, 1))
            oc_p = _attn_c_prompt(q, ktbf, vbf, cum_p, ckt, nb, s)
            logf_s = smp(logf)
            run = jnp.zeros_like(logf_s[:, 0])
            c_rows = []
            for t in range(ts):
                run = run + logf_s[:, t]
                c_rows.append(run)
            c_new = jnp.stack(c_rows, axis=1)
            cq = jnp.transpose(_pad_rows(c_new, T8), (0, 2, 1)).reshape(db, C_HEADS * T8, 1)
            cnew_t = jnp.transpose(_pad_rows(c_new, 16), (0, 2, 1))
            q_s = _pad_rows(smp(q), T8)
            hmask = jnp.asarray((np.arange(q_s.shape[2])[None, :] // HEAD_DIM) == np.arange(C_HEADS)[:, None], MXU_DT)
            q_bd = (q_s[:, None, :, :] * hmask[None, :, None, :]).reshape(db, C_HEADS * T8, -1)
            k_s3 = k_s.reshape(db, ts, -1)
            v_s3 = v_s.reshape(db, ts, -1)
            kn = _pad_rows(k_s3, 16)
            vn = _pad_rows(v_s3, 16)
            t8 = np.arange(T8)
            okn = (t8[:, None] >= np.arange(16)[None, :]) & (np.arange(16)[None, :] < ts)
            maskn = jnp.asarray(np.tile(np.where(okn, 0.0, NEG), (C_HEADS, 1)), F32)
            oc_d = _decode1(page_table, q_bd, cq, kn, vn, cnew_t, cck_t, ccv_t, ccf_t, maskn, o)
            oc_s = jnp.transpose(oc_d[:, :, :ts], (0, 2, 1, 3)).reshape(n_s, C_HEADS * HEAD_DIM)
            oc = jnp.concatenate([oc_p, _mx(oc_s)], axis=0)
            x = _outproj(x, [(oc, c_w_out[o])])
            outs_p["ck"].append(jnp.transpose(kt.reshape(nb, C_HEADS, HEAD_DIM, s), (0, 3, 1, 2)))
            outs_p["cv"].append(jnp.transpose(vt.reshape(nb, C_HEADS, HEAD_DIM, s), (0, 3, 1, 2)))
            outs_p["cf"].append(jnp.transpose(logft[:, :C_HEADS], (0, 2, 1)))
            outs_s["ck"].append(k_s3.reshape(db, ts, C_HEADS, HEAD_DIM))
            outs_s["cv"].append(v_s3.reshape(db, ts, C_HEADS, HEAD_DIM))
            outs_s["cf"].append(logf_s)
        x = _hier_moe(x, ffn_norm[layer], moe_w_group[layer], moe_b_group[layer], moe_w_router[layer],
                      moe_b_router[layer], moe_w_gate[layer], moe_w_up[layer], moe_w_down[layer])

    keys = ("ak", "av", "bk", "bv", "bi", "ck", "cv", "cf")
    return ((x[:n_p].reshape(nb, s, d), x[n_p:].reshape(db, ts, d))
            + tuple(jnp.stack(outs_p[k]) for k in keys)
            + tuple(jnp.stack(outs_s[k]) for k in keys))
```

```python
import functools
import math

import numpy as np
import jax
import jax.numpy as jnp
from jax import lax
from jax.experimental import pallas as pl
from jax.experimental.pallas import tpu as pltpu

F32 = jnp.float32
I32 = jnp.int32
MXU_DT = jnp.bfloat16

D_MODEL = 1024
HEAD_DIM = 64
A_HEADS = 4
B_HEADS = 8
IDX_HEADS = 8
IDX_DIM = 64
IDX_TOPK_MAX = 256
C_HEADS = 16
REL_BUCKETS = 32
REL_MAX_DIST = 128
N_GROUPS = 4
EXPERTS_PER_GROUP = 8
N_EXPERTS = N_GROUPS * EXPERTS_PER_GROUP
EXPERT_FF = 256
EPS = 1e-6
NEG = -1e30
INT_MIN = -2 ** 31
QK_SCALE = HEAD_DIM ** -0.5
IDX_SCALE = IDX_DIM ** -0.5
IDX_HEAD_SCALE = IDX_HEADS ** -0.5
LANES = 128
T8 = 8
VMEM_LIMIT = 56 * 1024 * 1024
PAGES_PER_STEP = 8
FLASH_WIDE = 2
EXPERTS_PER_STEP = 2

def _tile(n, pref):
    best = 16
    for t in range(16, pref + 1, 16):
        if n % t == 0:
            best = t
    assert n % best == 0, (n, pref)
    return best


def _cparams(sem):
    return pltpu.CompilerParams(dimension_semantics=sem, vmem_limit_bytes=VMEM_LIMIT)


def _mx(x):
    return x.astype(MXU_DT)


def _dot(a, b):
    return jnp.dot(a, b, preferred_element_type=F32)


def _dot_t(a, b):
    return lax.dot_general(a, b, (((1,), (1,)), ((), ())), preferred_element_type=F32)


def _split2(x):
    hi = _mx(x)
    lo = _mx(x - hi.astype(F32))
    return hi, lo


def _split3(x):
    hi = _mx(x)
    r = x - hi.astype(F32)
    mid = _mx(r)
    lo = _mx(r - mid.astype(F32))
    return hi, mid, lo


def _lane_iota(shape):
    return lax.broadcasted_iota(I32, shape, len(shape) - 1)


def _row_iota(shape):
    return lax.broadcasted_iota(I32, shape, len(shape) - 2)


def _lanes(x, width):
    if width <= LANES:
        return x[:, :width]
    return jnp.tile(x, (1, width // LANES))


def _rms_rows(x, g):
    ms = jnp.mean(x * x, axis=-1, keepdims=True)
    return x * lax.rsqrt(ms + EPS) * g


def _seg_rsqrt(y, bd):
    hi, lo = _split2(y * y)
    ss = _dot(hi, bd) + _dot(lo, bd)
    return lax.rsqrt(ss * (1.0 / HEAD_DIM) + EPS)


def _block_diag_ones(n, seg):
    r = np.arange(n)
    return jnp.asarray((r[:, None] // seg) == (r[None, :] // seg), dtype=MXU_DT)


def _lambda(lamv_ref, lam_init):
    lv = lamv_ref[...]
    return (jnp.exp(jnp.sum(lv[0:1] * lv[1:2], axis=-1, keepdims=True))
            - jnp.exp(jnp.sum(lv[2:3] * lv[3:4], axis=-1, keepdims=True)) + lam_init)


def _t5_bucket_np(d):
    n = np.maximum(d, 0)
    exact = REL_BUCKETS // 2
    nf = np.maximum(n, 1).astype(np.float64)
    large = exact + (np.log(nf / exact) / math.log(REL_MAX_DIST / exact) * (REL_BUCKETS - exact)).astype(np.int64)
    large = np.minimum(large, REL_BUCKETS - 1)
    return np.where(n < exact, n, large).astype(np.int32)


def _bias_expand_kernel(relb_ref, idx_ref, o_ref, *, head0):
    h = pl.program_id(0) + head0
    idx = idx_ref[...]
    acc = jnp.zeros(idx.shape, F32)
    for b in range(REL_BUCKETS):
        acc = jnp.where(idx == b, relb_ref[b, h], acc)
    o_ref[...] = acc


def _bias_expand(rel_bias, idx_np, head0, n_heads):
    idx = jnp.asarray(idx_np, I32)
    nd = idx.ndim
    zeros = (0,) * nd
    return pl.pallas_call(
        functools.partial(_bias_expand_kernel, head0=head0),
        grid=(n_heads,),
        in_specs=[pl.BlockSpec(memory_space=pltpu.SMEM),
                  pl.BlockSpec(idx.shape, lambda h: zeros)],
        out_specs=pl.BlockSpec((None,) + idx.shape, lambda h: (h,) + zeros),
        out_shape=jax.ShapeDtypeStruct((n_heads,) + idx.shape, F32),
        compiler_params=_cparams(("arbitrary",)),
        name="bias_expand",
    )(rel_bias, idx)


def _prompt_bucket_tiles(t):
    r = np.arange(t)
    d = r[:, None] - r[None, :]
    return np.stack([_t5_bucket_np(d), _t5_bucket_np(d + t), np.full((t, t), REL_BUCKETS - 1, np.int32)])


def _store_t(y, is_prompt, t_refs, row_ref, c):
    if is_prompt:
        yt = jnp.transpose(y)
        for r in t_refs:
            r[c * 256:(c + 1) * 256, :] = yt.astype(r.dtype)
    elif row_ref is not None:
        row_ref[:, c * 256:(c + 1) * 256] = y


def _per_tile_kind(body, n_prompt_tiles):
    prompt_tile = pl.program_id(0) < n_prompt_tiles

    @pl.when(prompt_tile)
    def _():
        body(True)

    @pl.when(jnp.logical_not(prompt_tile))
    def _():
        body(False)


def _proj0_kernel(*refs, n_prompt_tiles):
    _per_tile_kind(functools.partial(_proj0_body, *refs), n_prompt_tiles)


def _proj0_body(x_ref, g_ref, w_ref, gain_ref, bd_ref,
                qa_ref, ka_ref, kat_ref, katbf_ref, qb_ref, small_ref, smallt_ref, kk_ref, vv_ref, kiki_ref,
                va_ref, vabf_ref, qi_ref, is_prompt):
    h = _mx(_rms_rows(x_ref[...], g_ref[...]))
    bd = bd_ref[...]

    def chunk(c):
        return _dot(h, w_ref[:, c * 256:(c + 1) * 256])

    def normed(c):
        y = chunk(c)
        return y * _seg_rsqrt(y, bd) * gain_ref[:, c * 256:(c + 1) * 256]

    for c in range(2):
        qa_ref[:, c * 256:(c + 1) * 256] = _mx(normed(c) * QK_SCALE)
    for c in range(2):
        _store_t(normed(2 + c), is_prompt, (kat_ref, katbf_ref), ka_ref, c)
    for c in range(2):
        qb_ref[:, c * 256:(c + 1) * 256] = _mx(normed(4 + c) * QK_SCALE)
    y = chunk(6)
    yn = y * _seg_rsqrt(y, bd) * gain_ref[:, 6 * 256:7 * 256]
    lane = _lane_iota(y.shape)
    y = jnp.where(lane < HEAD_DIM, yn, y)
    small_ref[...] = y
    _store_t(y, is_prompt, (smallt_ref,), None, 0)
    t0 = y[:, :LANES]
    t1 = y[:, LANES:]
    lo = _lane_iota(t0.shape) < HEAD_DIM
    r0 = pltpu.roll(t0, HEAD_DIM, 1)
    kk_ref[...] = _mx(jnp.where(lo, t0, r0))
    vv_ref[...] = _mx(jnp.where(lo, r0, t0))
    r1 = pltpu.roll(t1, HEAD_DIM, 1)
    kiki_ref[...] = _mx(jnp.where(lo, t1, r1))
    for c in range(2):
        y = chunk(7 + c)
        va_ref[:, c * 256:(c + 1) * 256] = y
        vabf_ref[:, c * 256:(c + 1) * 256] = _mx(y)
    for c in range(2):
        qi_ref[:, c * 256:(c + 1) * 256] = _mx(chunk(9 + c) * IDX_SCALE)


def _token_specs(n_p, n_s, s, tm):
    npt = n_p // tm
    tps = s // tm

    def t_map(i):
        ip = jnp.minimum(i, npt - 1)
        return (ip // tps, 0, ip % tps)

    def s_map(i):
        return (jnp.maximum(i - npt, 0), 0)

    return npt, t_map, s_map


def _proj0(x, g, w_in, a_qn, a_kn, b_qn, b_kn, n_p, s, tm=256):
    n = x.shape[0]
    n_s = n - n_p
    nb = n_p // s
    tm = _tile(math.gcd(n_p, n_s, s), tm)
    npt, t_map, s_map = _token_specs(n_p, n_s, s, tm)
    sp = np.cumsum([512, 512, 512, 512, 64, 64, 512, 64, 8])[:-1]
    wqa, wka, wva, wqb, wkb, wvb, wqi, wki, wwi = jnp.split(w_in, sp, axis=1)
    w = jnp.concatenate([wqa, wka, wqb, wkb, wvb, wki, wwi, jnp.zeros((D_MODEL, 56), F32), wva, wqi], axis=1)
    w = _mx(w)
    ncol = w.shape[1]
    gain = jnp.concatenate([jnp.tile(a_qn, 8), jnp.tile(a_kn, 8), jnp.tile(b_qn, 8), b_kn,
                            jnp.ones((192,), F32)])[None, :]
    bd = _block_diag_ones(256, HEAD_DIM)
    row = lambda i: (i, 0)
    fixed = lambda i: (0, 0)
    def rows(wd, dt):
        return pl.BlockSpec((tm, wd), row), jax.ShapeDtypeStruct((n, wd), dt)

    def rows_s(wd, dt):
        return pl.BlockSpec((tm, wd), s_map), jax.ShapeDtypeStruct((n_s, wd), dt)

    def cols_p(wd, dt):
        return pl.BlockSpec((None, wd, tm), t_map), jax.ShapeDtypeStruct((nb, wd, s), dt)

    outs = [rows(512, MXU_DT), rows_s(512, F32), cols_p(512, F32), cols_p(512, MXU_DT), rows(512, MXU_DT),
            rows(256, F32), cols_p(256, F32), rows(128, MXU_DT), rows(128, MXU_DT), rows(128, MXU_DT),
            rows(512, F32), rows(512, MXU_DT), rows(512, MXU_DT)]
    return pl.pallas_call(
        functools.partial(_proj0_kernel, n_prompt_tiles=npt),
        grid=(n // tm,),
        in_specs=[pl.BlockSpec((tm, D_MODEL), row), pl.BlockSpec((1, D_MODEL), fixed),
                  pl.BlockSpec((D_MODEL, ncol), fixed), pl.BlockSpec((1, gain.shape[1]), fixed),
                  pl.BlockSpec((256, 256), fixed)],
        out_specs=[o[0] for o in outs],
        out_shape=[o[1] for o in outs],
        compiler_params=_cparams(("arbitrary",)),
        name="proj0",
    )(x, g[None, :], w, gain, bd)


def _proj1_kernel(*refs, n_prompt_tiles):
    _per_tile_kind(functools.partial(_proj1_body, *refs), n_prompt_tiles)


def _proj1_body(x_ref, g_ref, w_ref, gain_ref, bf_ref, bd_ref,
                q_ref, k_ref, kt_ref, ktbf_ref, v_ref, vt_ref, vbf_ref, logf_ref, logft_ref, is_prompt):
    h = _mx(_rms_rows(x_ref[...], g_ref[...]))
    bd = bd_ref[...]

    def chunk(c):
        return _dot(h, w_ref[:, c * 256:(c + 1) * 256])

    def normed(c):
        y = chunk(c)
        return y * _seg_rsqrt(y, bd) * gain_ref[:, c * 256:(c + 1) * 256]

    for c in range(4):
        q_ref[:, c * 256:(c + 1) * 256] = _mx(normed(c) * QK_SCALE)
    for c in range(4):
        _store_t(normed(4 + c), is_prompt, (kt_ref, ktbf_ref), k_ref, c)
    for c in range(4):
        y = chunk(8 + c)
        vbf_ref[:, c * 256:(c + 1) * 256] = _mx(y)
        _store_t(y, is_prompt, (vt_ref,), v_ref, c)
    f = _dot(h, w_ref[:, 12 * 256:12 * 256 + LANES]) + bf_ref[...]
    logf = jnp.minimum(f, 0.0) - jnp.log(1.0 + jnp.exp(-jnp.abs(f)))
    logf_ref[...] = logf
    if is_prompt:
        logft_ref[...] = jnp.transpose(logf)


def _proj1(x, g, w_in, b_f, qn, kn, n_p, s, tm=256):
    n = x.shape[0]
    n_s = n - n_p
    nb = n_p // s
    tm = _tile(math.gcd(n_p, n_s, s), tm)
    npt, t_map, s_map = _token_specs(n_p, n_s, s, tm)
    w = _mx(jnp.concatenate([w_in, jnp.zeros((D_MODEL, LANES - C_HEADS), F32)], axis=1))
    ncol = w.shape[1]
    gain = jnp.concatenate([jnp.tile(qn, C_HEADS), jnp.tile(kn, C_HEADS)])[None, :]
    bf = jnp.concatenate([b_f, jnp.zeros((LANES - C_HEADS,), F32)])[None, :]
    bd = _block_diag_ones(256, HEAD_DIM)
    row = lambda i: (i, 0)
    fixed = lambda i: (0, 0)
    def rows(wd, dt):
        return pl.BlockSpec((tm, wd), row), jax.ShapeDtypeStruct((n, wd), dt)

    def rows_s(wd, dt):
        return pl.BlockSpec((tm, wd), s_map), jax.ShapeDtypeStruct((n_s, wd), dt)

    def cols_p(wd, dt):
        return pl.BlockSpec((None, wd, tm), t_map), jax.ShapeDtypeStruct((nb, wd, s), dt)

    outs = [rows(1024, MXU_DT), rows_s(1024, F32), cols_p(1024, F32), cols_p(1024, MXU_DT),
            rows_s(1024, F32), cols_p(1024, F32), rows(1024, MXU_DT), rows(LANES, F32), cols_p(LANES, F32)]
    return pl.pallas_call(
        functools.partial(_proj1_kernel, n_prompt_tiles=npt),
        grid=(n // tm,),
        in_specs=[pl.BlockSpec((tm, D_MODEL), row), pl.BlockSpec((1, D_MODEL), fixed),
                  pl.BlockSpec((D_MODEL, ncol), fixed), pl.BlockSpec((1, gain.shape[1]), fixed),
                  pl.BlockSpec((1, LANES), fixed), pl.BlockSpec((256, 256), fixed)],
        out_specs=[o[0] for o in outs],
        out_shape=[o[1] for o in outs],
        compiler_params=_cparams(("arbitrary",)),
        name="proj1",
    )(x, g[None, :], w, gain, bf, bd)


def _outproj_kernel(*refs, n_in, n_prompt_tiles):
    res_ref = refs[0]
    out_ref = refs[-1]

    def body(is_prompt):
        acc = res_ref[...]
        for i in range(n_in):
            a_ref = refs[1 + 3 * i] if is_prompt else refs[2 + 3 * i]
            acc = acc + _dot(a_ref[...], refs[3 + 3 * i][...])
        out_ref[...] = acc

    _per_tile_kind(body, n_prompt_tiles)


def _outproj(res, terms, n_p, tm=512):
    n = res.shape[0]
    n_s = n - n_p
    tm = _tile(math.gcd(n_p, n_s), tm)
    npt = n_p // tm
    row = lambda i: (i, 0)
    fixed = lambda i: (0, 0)
    p_map = lambda i: (jnp.minimum(i, npt - 1), 0)
    s_map = lambda i: (jnp.maximum(i - npt, 0), 0)
    in_specs = [pl.BlockSpec((tm, D_MODEL), row)]
    args = [res]
    for a_p, a_s, w in terms:
        in_specs += [pl.BlockSpec((tm, a_p.shape[1]), p_map), pl.BlockSpec((tm, a_s.shape[1]), s_map),
                     pl.BlockSpec(w.shape, fixed)]
        args += [a_p, a_s, _mx(w)]
    return pl.pallas_call(
        functools.partial(_outproj_kernel, n_in=len(terms), n_prompt_tiles=npt),
        grid=(n // tm,),
        in_specs=in_specs,
        out_specs=pl.BlockSpec((tm, D_MODEL), row),
        out_shape=jax.ShapeDtypeStruct((n, D_MODEL), F32),
        compiler_params=_cparams(("parallel",)),
        name="outproj",
    )(*args)


def _stack_streams(q):
    lane = _lane_iota(q.shape)
    zero = jnp.zeros_like(q)
    return jnp.concatenate([jnp.where(lane < HEAD_DIM, q, zero), jnp.where(lane >= HEAD_DIM, q, zero)], axis=0)


def _flash_stacked(q2, kt_ref, v_ref, i, tq, tk, bias_fn, m_ref, l_ref, acc_ref):
    rows = 2 * tq
    m_ref[...] = jnp.full(m_ref.shape, NEG, F32)
    l_ref[...] = jnp.zeros(l_ref.shape, F32)
    acc_ref[...] = jnp.zeros(acc_ref.shape, F32)

    def tile(j, n_sub, masked):
        width = n_sub * tk
        off = pl.multiple_of(j * tk, tk)
        s = _dot(q2, kt_ref[:, pl.ds(off, width)]) + bias_fn(j, off, n_sub)
        if masked:
            r = _row_iota((rows, width))
            s = jnp.where(_lane_iota((rows, width)) <= jnp.where(r >= tq, r - tq, r), s, NEG)
        alpha, p = _online_update(s, m_ref, l_ref)
        acc_ref[...] = alpha * acc_ref[...] + _dot(_mx(p), v_ref[pl.ds(off, width), :])

    def body(jj, carry):
        tile(jj * FLASH_WIDE, FLASH_WIDE, False)
        return carry

    lax.fori_loop(0, i // FLASH_WIDE, body, 0)
    for u in range(1, FLASH_WIDE):
        @pl.when(i % FLASH_WIDE >= u)
        def _():
            tile(i - i % FLASH_WIDE + (u - 1), 1, False)
    tile(i, 1, True)


def _attn_a_kernel(lamv_ref, gsub_ref, q_ref, k_ref, v_ref, bias_ref, o_ref,
                   m_ref, l_ref, acc_ref, *, tq, tk, lam_init):
    i = pl.program_id(2)

    def bias_fn(j, off, n_sub):
        b = jnp.concatenate([bias_ref[jnp.minimum(i - j - u, 2)] for u in range(n_sub)], axis=1)
        return jnp.concatenate([b, b], axis=0)

    _flash_stacked(_stack_streams(q_ref[...]), k_ref, v_ref, i, tq, tk, bias_fn, m_ref, l_ref, acc_ref)
    lam = _lambda(lamv_ref, lam_init)
    o = acc_ref[:tq] / l_ref[:tq] - lam * (acc_ref[tq:] / l_ref[tq:])
    o = _rms_rows(o, gsub_ref[...]) * (1.0 - lam_init)
    o_ref[...] = _mx(o)


def _attn_a_prompt(qa, ka, va, lamv, gsub, bias_tiles, nb, s, lam_init, tq=256):
    tk = tq
    nq = s // tq
    kern = functools.partial(_attn_a_kernel, tq=tq, tk=tk, lam_init=lam_init)
    return pl.pallas_call(
        kern,
        grid=(nb, A_HEADS, nq),
        in_specs=[pl.BlockSpec((4, HEAD_DIM), lambda b, h, i: (0, 0)),
                  pl.BlockSpec((1, LANES), lambda b, h, i: (0, 0)),
                  pl.BlockSpec((tq, LANES), lambda b, h, i: (b * nq + i, h)),
                  pl.BlockSpec((None, LANES, s), lambda b, h, i: (b, h, 0)),
                  pl.BlockSpec((s, LANES), lambda b, h, i: (b, h)),
                  pl.BlockSpec((None, 3, tq, tk), lambda b, h, i: (h, 0, 0, 0))],
        out_specs=pl.BlockSpec((tq, LANES), lambda b, h, i: (b * nq + i, h)),
        out_shape=jax.ShapeDtypeStruct((nb * s, A_HEADS * LANES), MXU_DT),
        scratch_shapes=[pltpu.VMEM((2 * tq, LANES), F32), pltpu.VMEM((2 * tq, LANES), F32),
                        pltpu.VMEM((2 * tq, LANES), F32)],
        compiler_params=_cparams(("parallel", "parallel", "arbitrary")),
        name="attn_a_prompt",
    )(lamv, gsub, qa, ka, va, bias_tiles)


def _attn_c_kernel(q_ref, k_ref, v_ref, cq_ref, ckt_ref, o_ref, m_ref, l_ref, acc_ref, *, tq, tk):
    hp = pl.program_id(1)
    i = pl.program_id(2)
    cq_tile = cq_ref[...]
    lane16 = _lane_iota(cq_tile.shape)
    cq2 = jnp.concatenate(
        [jnp.broadcast_to(jnp.sum(jnp.where(lane16 == 2 * hp + c, cq_tile, 0.0), axis=-1, keepdims=True),
                          (tq, LANES)) for c in range(2)], axis=0)

    def bias_fn(j, off, n_sub):
        width = n_sub * tk
        ck = ckt_ref[:, pl.ds(off, width)]
        ck2 = jnp.concatenate([jnp.broadcast_to(ck[0:1], (tq, width)),
                               jnp.broadcast_to(ck[1:2], (tq, width))], axis=0)
        return _lanes(cq2, width) - ck2

    _flash_stacked(_stack_streams(q_ref[...]), k_ref, v_ref, i, tq, tk, bias_fn, m_ref, l_ref, acc_ref)
    lane = _lane_iota((tq, LANES))
    o = jnp.where(lane < HEAD_DIM, acc_ref[:tq] / l_ref[:tq], acc_ref[tq:] / l_ref[tq:])
    o_ref[...] = _mx(o)


def _attn_c_prompt(q, k, v, cq, ckt, nb, s, tq=256):
    tk = tq
    nq = s // tq
    nhp = C_HEADS // 2
    kern = functools.partial(_attn_c_kernel, tq=tq, tk=tk)
    return pl.pallas_call(
        kern,
        grid=(nb, nhp, nq),
        in_specs=[pl.BlockSpec((tq, LANES), lambda b, h, i: (b * nq + i, h)),
                  pl.BlockSpec((None, LANES, s), lambda b, h, i: (b, h, 0)),
                  pl.BlockSpec((s, LANES), lambda b, h, i: (b, h)),
                  pl.BlockSpec((tq, C_HEADS), lambda b, h, i: (b * nq + i, 0)),
                  pl.BlockSpec((None, None, 2, s), lambda b, h, i: (b, h, 0, 0))],
        out_specs=pl.BlockSpec((tq, LANES), lambda b, h, i: (b * nq + i, h)),
        out_shape=jax.ShapeDtypeStruct((nb * s, C_HEADS * HEAD_DIM), MXU_DT),
        scratch_shapes=[pltpu.VMEM((2 * tq, LANES), F32), pltpu.VMEM((2 * tq, LANES), F32),
                        pltpu.VMEM((2 * tq, LANES), F32)],
        compiler_params=_cparams(("parallel", "parallel", "arbitrary")),
        name="attn_c_prompt",
    )(q, k, v, cq, ckt)


def _score_keys(score):
    score = jnp.where(score == 0.0, 0.0, score)
    bits = pltpu.bitcast(score, I32)
    return bits ^ (jnp.right_shift(bits, 31) & 0x7FFFFFFF)


def _topk_select(keys_ref, width, kcount, active, col):
    kf = float(kcount)
    nbits_col = int(width - 1).bit_length()

    def count(pred):
        return jnp.sum(jnp.where(pred, 1.0, 0.0), axis=1, keepdims=True)

    t0 = jnp.where(count(keys_ref[:, :width] >= 0) >= kf, 0, INT_MIN).astype(I32)

    def body(it, t):
        cand = t + jnp.left_shift(jnp.int32(1), 30 - it)
        return jnp.where(count(keys_ref[:, :width] >= cand) >= kf, cand, t)

    t = lax.fori_loop(0, 31, body, t0)
    t = jnp.where(active, t, INT_MIN)
    keys = keys_ref[:, :width]
    gt = keys > t
    eq = keys == t
    need = kf - count(gt)
    excess = jnp.where(active, count(eq) - need, 0.0)

    def tie_break():
        def tb(it, jj):
            cand = jj + jnp.left_shift(jnp.int32(1), nbits_col - 1 - it)
            c = count((keys_ref[:, :width] == t) & (col < cand))
            return jnp.where(c < need, cand, jj)
        return lax.fori_loop(0, nbits_col, tb, jnp.zeros(t.shape, I32))

    jmax = lax.cond(jnp.max(excess) > 0.0, tie_break, lambda: jnp.full(t.shape, width, I32))
    return gt | (eq & (col <= jmax))


KEY_OF_NEG_INF = -2139095041
SELECT_UNIT = 2


def _attn_b_kernel(qb_ref, qi_ref, w_ref, kk_ref, vv_ref, kiki_ref, bias_ref, o_ref,
                   keys_ref, selm_ref, m_ref, l_ref, acc_ref, *, tq, s_len, topk):
    i = pl.program_id(1)
    cw = tq
    nh = B_HEADS
    n_chunks = s_len // cw
    lane = _lane_iota((tq, LANES))
    halves = (lane < HEAD_DIM, lane >= HEAD_DIM)

    def stack_heads(ref):
        parts = []
        for h in range(nh):
            t = ref[:, (h // 2) * LANES:(h // 2 + 1) * LANES]
            parts.append(jnp.where(halves[h % 2], t, jnp.zeros_like(t)))
        return jnp.concatenate(parts, axis=0)

    row = _row_iota((tq, cw)) + i * tq
    colc = _lane_iota((tq, cw))

    qi8 = stack_heads(qi_ref)
    wt = w_ref[...] * IDX_HEAD_SCALE
    wcol = jnp.concatenate([jnp.broadcast_to(wt[:, HEAD_DIM + h:HEAD_DIM + h + 1], (tq, LANES))
                            for h in range(nh)], axis=0)

    def score_chunk(j, carry):
        off = pl.multiple_of(j * cw, cw)
        d = jnp.maximum(_dot_t(qi8, kiki_ref[pl.ds(off, cw), :]), 0.0) * _lanes(wcol, cw)
        sc = d[0:tq]
        for h in range(1, nh):
            sc = sc + d[h * tq:(h + 1) * tq]
        keys_ref[:, pl.ds(off, cw)] = _score_keys(jnp.where((colc + j * cw) <= row, sc, -jnp.inf))
        return carry

    lax.fori_loop(0, i + 1, score_chunk, 0)
    unit = SELECT_UNIT if n_chunks % SELECT_UNIT == 0 else 1
    rem = (i + 1) % unit
    for u in range(1, unit):
        @pl.when((rem != 0) & (u <= unit - rem))
        def _():
            off = pl.multiple_of((i + u) * cw, cw)
            keys_ref[:, pl.ds(off, cw)] = jnp.full((tq, cw), KEY_OF_NEG_INF, I32)
    widths = [w * unit * cw for w in range(1, n_chunks // unit + 1)]

    qpos = _row_iota((tq, 1)) + i * tq

    def select_branch(width):
        def br():
            colw = _lane_iota((tq, width))
            causal = colw <= (_row_iota((tq, width)) + i * tq)
            sel = _topk_select(keys_ref, width, topk, qpos >= topk, colw)
            selm_ref[:, :width] = jnp.where(sel & causal, 0.0, NEG)
        return br

    def causal_only():
        col = _lane_iota((tq, s_len))
        selm_ref[...] = jnp.where(col <= (_row_iota((tq, s_len)) + i * tq), 0.0, NEG)

    branch = jnp.where((i + 1) * tq > topk, 1 + i // unit, 0)
    lax.switch(branch, [causal_only] + [select_branch(w) for w in widths])

    qb8 = stack_heads(qb_ref)
    m_ref[...] = jnp.full(m_ref.shape, NEG, F32)
    l_ref[...] = jnp.zeros(l_ref.shape, F32)
    acc_ref[...] = jnp.zeros(acc_ref.shape, F32)

    def attn_chunks(j, n_sub):
        width = n_sub * cw
        off = pl.multiple_of(j * cw, cw)
        bias = jnp.concatenate(
            [jnp.concatenate([bias_ref[h, jnp.minimum(i - j - u, 2)] for h in range(nh)], axis=0)
             for u in range(n_sub)], axis=1)
        selm = selm_ref[:, pl.ds(off, width)]
        s = _dot_t(qb8, kk_ref[pl.ds(off, width), :]) + bias + jnp.concatenate([selm] * nh, axis=0)
        alpha, p = _online_update(s, m_ref, l_ref)
        acc_ref[...] = alpha * acc_ref[...] + _dot(_mx(p), vv_ref[pl.ds(off, width), :])

    def wide_body(jj, carry):
        attn_chunks(jj * FLASH_WIDE, FLASH_WIDE)
        return carry

    n_causal = i + 1
    lax.fori_loop(0, n_causal // FLASH_WIDE, wide_body, 0)
    for u in range(1, FLASH_WIDE):
        @pl.when(n_causal % FLASH_WIDE >= u)
        def _():
            attn_chunks(n_causal - n_causal % FLASH_WIDE + (u - 1), 1)
    o = acc_ref[...] / l_ref[...]
    for hp in range(nh // 2):
        o_ref[:, hp * LANES:(hp + 1) * LANES] = _mx(
            jnp.where(halves[0], o[2 * hp * tq:(2 * hp + 1) * tq], o[(2 * hp + 1) * tq:(2 * hp + 2) * tq]))


def _attn_b_prompt(qb, qi, small, kk, vv, kiki, bias_tiles, nb, s, topk, tq=256):
    nq = s // tq
    kern = functools.partial(_attn_b_kernel, tq=tq, s_len=s, topk=topk)
    qrow = lambda b, i: (b * nq + i, 0)
    kv = lambda b, i: (b, 0)
    rows = B_HEADS * tq
    return pl.pallas_call(
        kern,
        grid=(nb, nq),
        in_specs=[pl.BlockSpec((tq, B_HEADS * HEAD_DIM), qrow),
                  pl.BlockSpec((tq, IDX_HEADS * IDX_DIM), qrow),
                  pl.BlockSpec((tq, LANES), lambda b, i: (b * nq + i, 1)),
                  pl.BlockSpec((s, LANES), kv), pl.BlockSpec((s, LANES), kv), pl.BlockSpec((s, LANES), kv),
                  pl.BlockSpec(bias_tiles.shape, lambda b, i: (0, 0, 0, 0))],
        out_specs=pl.BlockSpec((tq, B_HEADS * HEAD_DIM), qrow),
        out_shape=jax.ShapeDtypeStruct((nb * s, B_HEADS * HEAD_DIM), MXU_DT),
        scratch_shapes=[pltpu.VMEM((tq, s), I32), pltpu.VMEM((tq, s), F32),
                        pltpu.VMEM((rows, LANES), F32), pltpu.VMEM((rows, LANES), F32),
                        pltpu.VMEM((rows, LANES), F32)],
        compiler_params=_cparams(("parallel", "arbitrary")),
        name="attn_b_prompt",
    )(qb, qi, small, kk, vv, kiki, bias_tiles)


def _cumsum_rows_kernel(x_ref, tri_ref, o_ref, carry_ref):
    j = pl.program_id(1)

    @pl.when(j == 0)
    def _():
        carry_ref[...] = jnp.zeros(carry_ref.shape, F32)

    tri = tri_ref[...]
    hi, mid, lo = _split3(x_ref[...])
    cum = _dot(tri, hi) + _dot(tri, mid) + _dot(tri, lo) + carry_ref[...]
    o_ref[...] = cum
    carry_ref[...] = cum[-1:, :]


def _cumsum_prompt(x, nb, s, blk=128):
    nj = s // blk
    w = x.shape[1]
    r = np.arange(blk)
    tri = jnp.asarray(r[:, None] >= r[None, :], dtype=MXU_DT)
    return pl.pallas_call(
        _cumsum_rows_kernel,
        grid=(nb, nj),
        in_specs=[pl.BlockSpec((blk, w), lambda b, j: (b * nj + j, 0)),
                  pl.BlockSpec((blk, blk), lambda b, j: (0, 0))],
        out_specs=pl.BlockSpec((blk, w), lambda b, j: (b * nj + j, 0)),
        out_shape=jax.ShapeDtypeStruct((nb * s, w), F32),
        scratch_shapes=[pltpu.VMEM((1, w), F32)],
        compiler_params=_cparams(("parallel", "arbitrary")),
        name="cumsum_prompt",
    )(x, tri)


def _online_update(s, m_ref, l_ref):
    m_old = m_ref[...]
    m_new = jnp.maximum(m_old, jnp.max(s, axis=-1, keepdims=True))
    alpha = jnp.exp(m_old - m_new)
    p = jnp.exp(s - _lanes(m_new, s.shape[1]))
    l_ref[...] = alpha * l_ref[...] + jnp.sum(p, axis=-1, keepdims=True)
    m_ref[...] = m_new
    return alpha, p


def _decode0_kernel(*refs, n_steps, npp, ps, topk, lam_init, group):
    (pt_ref, lamv_ref, gsub_ref, qa_ref, qi_ref, qb_ref, w8_ref, kna_ref, vna_ref, bnew_ref,
     taba_ref, tabb_ref, validb_ref) = refs[:13]
    pages = refs[13:13 + 5 * npp]
    cak, cav, cbk, cbv, cbi = (pages[0:npp], pages[npp:2 * npp], pages[2 * npp:3 * npp],
                               pages[3 * npp:4 * npp], pages[4 * npp:5 * npp])
    oa_ref, ob_ref, m_ref, l_ref, acc_ref, kb_ref, vb_ref, ki_ref, keys_ref = refs[13 + 5 * npp:]
    g = pl.program_id(0) % group
    step = pl.program_id(1)
    past = n_steps * npp * ps
    lk = kb_ref.shape[2]

    @pl.when(step == 0)
    def _():
        m_ref[...] = jnp.full(m_ref.shape, NEG, F32)
        l_ref[...] = jnp.zeros(l_ref.shape, F32)
        acc_ref[...] = jnp.zeros(acc_ref.shape, F32)

    qa = qa_ref[...]
    far = taba_ref[0]
    tail = taba_ref[jnp.where(step == n_steps - 1, 1, 0)]
    s = jnp.concatenate([_dot(qa, _mx(cak[j][...])) + (tail if j == npp - 1 else far) for j in range(npp)], axis=1)
    alpha, p = _online_update(s, m_ref, l_ref)
    p = _mx(p)
    for h in range(A_HEADS):
        rows = slice(h * 2 * T8, (h + 1) * 2 * T8)
        upd = alpha[rows] * acc_ref[rows]
        for j in range(npp):
            vh = _mx(cav[j][pl.ds(h, ps, stride=A_HEADS), :])
            upd = upd + _dot(p[rows, j * ps:(j + 1) * ps], vh)
        acc_ref[rows] = upd

    for j in range(npp):
        off = pl.multiple_of((step * npp + j) * ps, ps)
        kb_ref[g, :, pl.ds(off, ps)] = _mx(cbk[j][...])
        vb_ref[g, :, pl.ds(off, ps)] = _mx(cbv[j][...])
        ki_ref[g, :, pl.ds(off, ps)] = _mx(cbi[j][...])

    @pl.when(step == n_steps - 1)
    def _():
        s_new = _dot_t(qa, _mx(kna_ref[...])) + taba_ref[2][:, :kna_ref.shape[0]]
        alpha2, p2 = _online_update(s_new, m_ref, l_ref)
        vn = _mx(vna_ref[...])
        lam = _lambda(lamv_ref, lam_init)
        for h in range(A_HEADS):
            rows = slice(h * 2 * T8, (h + 1) * 2 * T8)
            o16 = (alpha2[rows] * acc_ref[rows] + _dot(_mx(p2[rows]), vn[:, h * LANES:(h + 1) * LANES])) / l_ref[rows]
            o = o16[:T8] - lam * o16[T8:]
            oa_ref[h] = _rms_rows(o, gsub_ref[...]) * (1.0 - lam_init)

        bn = bnew_ref[g]
        kb_ref[g, :, past:past + LANES] = _mx(bn[0])
        vb_ref[g, :, past:past + LANES] = _mx(bn[1])
        ki_ref[g, :, past:past + LANES] = _mx(bn[2])

    @pl.when((step == n_steps - 1) & (g == group - 1))
    def _():
        valid = validb_ref[...] == 0.0
        for gg in range(group):
            dots = jnp.maximum(_dot(qi_ref[gg], ki_ref[gg]), 0.0)
            w8 = w8_ref[gg] * IDX_HEAD_SCALE
            score = jnp.zeros((T8, lk), F32)
            for h in range(IDX_HEADS):
                score = score + w8[:, h:h + 1] * dots[h * T8:(h + 1) * T8]
            keys_ref[gg * T8:(gg + 1) * T8, :] = _score_keys(jnp.where(valid, score, -jnp.inf))
        rows = group * T8
        sel = _topk_select(keys_ref, lk, topk, jnp.full((rows, 1), True), _lane_iota((rows, lk)))
        selm = jnp.where(sel & jnp.concatenate([valid] * group, axis=0), 0.0, NEG)
        for gg in range(group):
            sg = selm[gg * T8:(gg + 1) * T8]
            sb = _dot(qb_ref[gg], kb_ref[gg]) + tabb_ref[...] + jnp.concatenate([sg] * B_HEADS, axis=0)
            mb = jnp.max(sb, axis=-1, keepdims=True)
            pb = jnp.exp(sb - mb)
            lb = jnp.sum(pb, axis=-1, keepdims=True)
            ob = _dot_t(_mx(pb), vb_ref[gg]) / lb
            for h in range(B_HEADS):
                ob_ref[gg, h] = ob[h * T8:(h + 1) * T8]


def _decode0(page_table, lamv, gsub, qa_bd, qi64, qb64, w8, kna, vna, bnew_t,
             cak_t, cav_r, cbk_t, cbv_t, cbi_t, taba, tabb, validb, layer, topk, lam_init):
    db, npg = page_table.shape
    ps = cak_t.shape[3]
    npp = PAGES_PER_STEP if npg % PAGES_PER_STEP == 0 else 1
    n_steps = npg // npp
    lk = tabb.shape[1]
    group = max(gsz for gsz in (8, 4, 2, 1) if db % gsz == 0)
    per_b = lambda b, p, pt: (b, 0, 0)
    per_g = lambda b, p, pt: (b // group, 0, 0)
    per_g4 = lambda b, p, pt: (b // group, 0, 0, 0)
    fixed2 = lambda b, p, pt: (0, 0)
    fixed3 = lambda b, p, pt: (0, 0, 0)

    def page_specs(arr):
        blk = (None, None) + arr.shape[2:]
        return [pl.BlockSpec(blk, lambda b, p, pt, j=j: (layer, pt[b * npg + p * npp + j], 0, 0)) for j in range(npp)]

    caches = (cak_t, cav_r, cbk_t, cbv_t, cbi_t)
    kern = functools.partial(_decode0_kernel, n_steps=n_steps, npp=npp, ps=ps, topk=topk, lam_init=lam_init,
                             group=group)
    grid_spec = pltpu.PrefetchScalarGridSpec(
        num_scalar_prefetch=1,
        grid=(db, n_steps),
        in_specs=[pl.BlockSpec((4, HEAD_DIM), fixed2), pl.BlockSpec((1, LANES), fixed2),
                  pl.BlockSpec((None,) + qa_bd.shape[1:], per_b),
                  pl.BlockSpec((group,) + qi64.shape[1:], per_g),
                  pl.BlockSpec((group,) + qb64.shape[1:], per_g),
                  pl.BlockSpec((group,) + w8.shape[1:], per_g),
                  pl.BlockSpec((None,) + kna.shape[1:], per_b),
                  pl.BlockSpec((None,) + vna.shape[1:], per_b),
                  pl.BlockSpec((group,) + bnew_t.shape[1:], per_g4),
                  pl.BlockSpec(taba.shape, fixed3), pl.BlockSpec(tabb.shape, fixed2),
                  pl.BlockSpec(validb.shape, fixed2)]
                 + [sp for c in caches for sp in page_specs(c)],
        out_specs=[pl.BlockSpec((None, A_HEADS, T8, LANES), lambda b, p, pt: (b, 0, 0, 0)),
                   pl.BlockSpec((group, B_HEADS, T8, HEAD_DIM), per_g4)],
        scratch_shapes=[pltpu.VMEM((64, LANES), F32), pltpu.VMEM((64, LANES), F32),
                        pltpu.VMEM((64, LANES), F32),
                        pltpu.VMEM((group, HEAD_DIM, lk), MXU_DT), pltpu.VMEM((group, HEAD_DIM, lk), MXU_DT),
                        pltpu.VMEM((group, IDX_DIM, lk), MXU_DT), pltpu.VMEM((group * T8, lk), I32)],
    )
    return pl.pallas_call(
        kern,
        grid_spec=grid_spec,
        out_shape=[jax.ShapeDtypeStruct((db, A_HEADS, T8, LANES), F32),
                   jax.ShapeDtypeStruct((db, B_HEADS, T8, HEAD_DIM), F32)],
        compiler_params=_cparams(("arbitrary", "arbitrary")),
        name="decode0",
    )(page_table.reshape(-1), lamv, gsub, qa_bd, qi64, qb64, w8, kna, vna, bnew_t, taba, tabb, validb,
      *[c for c in caches for _ in range(npp)])


def _expand_rows(x):
    hh, ww = x.shape
    return jnp.broadcast_to(x[:, None, :], (hh, T8, ww)).reshape(hh * T8, ww)


def _decode1_kernel(*refs, n_steps, npp, ps):
    pt_ref, q_ref, cq_ref, kn_ref, vn_ref, cnew_ref, maskn_ref, tri_ref = refs[:8]
    ck = refs[8:8 + npp]
    cv = refs[8 + npp:8 + 2 * npp]
    cf = refs[8 + 2 * npp:8 + 3 * npp]
    o_ref, m_ref, l_ref, acc_ref, suf_ref = refs[8 + 3 * npp:]
    step = pl.program_id(1)

    @pl.when(step == 0)
    def _():
        m_ref[...] = jnp.full(m_ref.shape, NEG, F32)
        l_ref[...] = jnp.zeros(l_ref.shape, F32)
        acc_ref[...] = jnp.zeros(acc_ref.shape, F32)
        suf_ref[...] = jnp.zeros(suf_ref.shape, F32)

    q = q_ref[...]
    cq = jnp.broadcast_to(cq_ref[...], (q.shape[0], LANES))
    tri = tri_ref[...]
    running = suf_ref[...]
    sufs = [None] * npp
    for j in reversed(range(npp)):
        hi, mid, lo = _split3(cf[j][...])
        cum = _dot(hi, tri) + _dot(mid, tri) + _dot(lo, tri)
        tot = jnp.broadcast_to(cum[:, ps - 1:ps], cum.shape)
        sufs[j] = running + tot - cum
        running = running + tot
    suf_ref[...] = running
    s = jnp.concatenate([_dot(q, _mx(ck[j][...])) for j in range(npp)], axis=1)
    s = s + _lanes(cq, npp * ps) + _expand_rows(jnp.concatenate(sufs, axis=1))
    alpha, p = _online_update(s, m_ref, l_ref)
    p = _mx(p)
    upd = _lanes(alpha, acc_ref.shape[1]) * acc_ref[...]
    for j in range(npp):
        upd = upd + _dot_t(p[:, j * ps:(j + 1) * ps], _mx(cv[j][...]))
    acc_ref[...] = upd

    @pl.when(step == n_steps - 1)
    def _():
        nn = kn_ref.shape[0]
        s_new = (_dot_t(q, _mx(kn_ref[...])) + cq[:, :nn] - _expand_rows(cnew_ref[...]) + maskn_ref[...])
        alpha2, p2 = _online_update(s_new, m_ref, l_ref)
        o_full = ((_lanes(alpha2, acc_ref.shape[1]) * acc_ref[...] + _dot(_mx(p2), _mx(vn_ref[...])))
                  / _lanes(l_ref[...], acc_ref.shape[1]))
        for h in range(C_HEADS):
            o_ref[h] = o_full[h * T8:(h + 1) * T8, h * HEAD_DIM:(h + 1) * HEAD_DIM]


def _decode1(page_table, q_bd, cq, kn, vn, cnew_t, cck_t, ccv_t, ccf_t, maskn, layer):
    db, npg = page_table.shape
    ps = cck_t.shape[3]
    assert ps == LANES, ps
    npp = PAGES_PER_STEP if npg % PAGES_PER_STEP == 0 else 1
    n_steps = npg // npp
    per_b = lambda b, p, pt: (b, 0, 0)
    fixed = lambda b, p, pt: (0, 0)
    r = np.arange(ps)
    tri = jnp.asarray(r[:, None] <= r[None, :], dtype=MXU_DT)

    def page_specs(arr):
        blk = (None, None) + arr.shape[2:]
        return [pl.BlockSpec(blk, lambda b, p, pt, j=j: (layer, pt[b * npg + (n_steps - 1 - p) * npp + j], 0, 0))
                for j in range(npp)]

    kern = functools.partial(_decode1_kernel, n_steps=n_steps, npp=npp, ps=ps)
    rows = C_HEADS * T8
    grid_spec = pltpu.PrefetchScalarGridSpec(
        num_scalar_prefetch=1,
        grid=(db, n_steps),
        in_specs=[pl.BlockSpec((None,) + q_bd.shape[1:], per_b),
                  pl.BlockSpec((None,) + cq.shape[1:], per_b),
                  pl.BlockSpec((None,) + kn.shape[1:], per_b),
                  pl.BlockSpec((None,) + vn.shape[1:], per_b),
                  pl.BlockSpec((None,) + cnew_t.shape[1:], per_b),
                  pl.BlockSpec(maskn.shape, fixed), pl.BlockSpec(tri.shape, fixed)]
                 + page_specs(cck_t) + page_specs(ccv_t) + page_specs(ccf_t),
        out_specs=pl.BlockSpec((None, C_HEADS, T8, HEAD_DIM), lambda b, p, pt: (b, 0, 0, 0)),
        scratch_shapes=[pltpu.VMEM((rows, LANES), F32), pltpu.VMEM((rows, LANES), F32),
                        pltpu.VMEM((rows, C_HEADS * HEAD_DIM), F32), pltpu.VMEM((C_HEADS, LANES), F32)],
    )
    return pl.pallas_call(
        kern,
        grid_spec=grid_spec,
        out_shape=jax.ShapeDtypeStruct((db, C_HEADS, T8, HEAD_DIM), F32),
        compiler_params=_cparams(("parallel", "arbitrary")),
        name="decode1",
    )(page_table.reshape(-1), q_bd, cq, kn, vn, cnew_t, maskn, tri,
      *([cck_t] * npp), *([ccv_t] * npp), *([ccf_t] * npp))


def _router_kernel(x_ref, g_ref, whi_ref, wlo_ref, b_ref, t_ref, route_ref):
    t = _rms_rows(x_ref[...], g_ref[...])
    thi, tlo = _split2(t)
    t_ref[...] = thi
    whi = whi_ref[...]
    logits = _dot(thi, whi) + _dot(tlo, whi) + _dot(thi, wlo_ref[...]) + b_ref[...]
    lane = _lane_iota(logits.shape)
    big = jnp.int32(1 << 20)

    def first_max(v):
        mx = jnp.max(v, axis=-1, keepdims=True)
        idx = jnp.min(jnp.where(v == mx, lane, big), axis=-1, keepdims=True)
        return mx, idx

    glog = jnp.where(lane < N_GROUPS, logits, -jnp.inf)
    gmax, gidx = first_max(glog)
    grp_w = 1.0 / jnp.sum(jnp.exp(glog - gmax), axis=-1, keepdims=True)
    el = lane - N_GROUPS
    in_grp = (el >= 0) & (el < N_EXPERTS) & (jnp.right_shift(el, 3) == gidx)
    v1 = jnp.where(in_grp, logits, -jnp.inf)
    top1, i1 = first_max(v1)
    v2 = jnp.where(lane == i1, -jnp.inf, v1)
    top2, i2 = first_max(v2)
    e2 = jnp.exp(top2 - top1)
    w1 = grp_w / (1.0 + e2)
    w2 = grp_w * e2 / (1.0 + e2)
    route = jnp.where(lane == 0, (i1 - N_GROUPS).astype(F32),
                      jnp.where(lane == 1, (i2 - N_GROUPS).astype(F32),
                                jnp.where(lane == 2, w1, jnp.where(lane == 3, w2, 0.0))))
    route_ref[...] = route


def _router(x, g, w_group, b_group, w_router, b_router, tm=512):
    n = x.shape[0]
    tm = _tile(n, tm)
    pad = LANES - N_GROUPS - N_EXPERTS
    w = jnp.concatenate([w_group, w_router, jnp.zeros((D_MODEL, pad), F32)], axis=1)
    whi = _mx(w)
    wlo = _mx(w - whi.astype(F32))
    b = jnp.concatenate([b_group, b_router, jnp.zeros((pad,), F32)])[None, :]
    row = lambda i: (i, 0)
    fixed = lambda i: (0, 0)
    return pl.pallas_call(
        _router_kernel,
        grid=(n // tm,),
        in_specs=[pl.BlockSpec((tm, D_MODEL), row), pl.BlockSpec((1, D_MODEL), fixed),
                  pl.BlockSpec((D_MODEL, LANES), fixed), pl.BlockSpec((D_MODEL, LANES), fixed),
                  pl.BlockSpec((1, LANES), fixed)],
        out_specs=[pl.BlockSpec((tm, D_MODEL), row), pl.BlockSpec((tm, LANES), row)],
        out_shape=[jax.ShapeDtypeStruct((n, D_MODEL), MXU_DT), jax.ShapeDtypeStruct((n, LANES), F32)],
        compiler_params=_cparams(("parallel",)),
        name="moe_router",
    )(x, g[None, :], whi, wlo, b)


def _moe_dense_kernel(x_ref, t_ref, route_ref, wgu_ref, wd_ref, o_ref):
    e = pl.program_id(1)

    @pl.when(e == 0)
    def _():
        o_ref[...] = x_ref[...]

    t = t_ref[...]
    r = route_ref[...]
    upd = None
    for k in range(EXPERTS_PER_STEP):
        au = _dot(t, wgu_ref[k])
        a = au[:, :EXPERT_FF]
        u = au[:, EXPERT_FF:]
        ef = (e * EXPERTS_PER_STEP + k).astype(F32)
        gate = jnp.where(r[:, 0:1] == ef, r[:, 2:3], 0.0) + jnp.where(r[:, 1:2] == ef, r[:, 3:4], 0.0)
        hdn = a * (1.0 / (1.0 + jnp.exp(-a))) * u * gate
        y = _dot(_mx(hdn), wd_ref[k])
        upd = y if upd is None else upd + y
    o_ref[...] += upd


def _moe_dense(x, t, route, w_gate, w_up, w_down, tm=1536):
    n = x.shape[0]
    tm = _tile(n, tm)
    wgu = _mx(jnp.concatenate([w_gate.reshape(N_EXPERTS, D_MODEL, EXPERT_FF),
                               w_up.reshape(N_EXPERTS, D_MODEL, EXPERT_FF)], axis=-1))
    wd = _mx(w_down.reshape(N_EXPERTS, EXPERT_FF, D_MODEL))
    row = lambda i, e: (i, 0)
    once = pl.Buffered(1)
    return pl.pallas_call(
        _moe_dense_kernel,
        grid=(n // tm, N_EXPERTS // EXPERTS_PER_STEP),
        in_specs=[pl.BlockSpec((tm, D_MODEL), row, pipeline_mode=once),
                  pl.BlockSpec((tm, D_MODEL), row, pipeline_mode=once),
                  pl.BlockSpec((tm, LANES), row, pipeline_mode=once),
                  pl.BlockSpec((EXPERTS_PER_STEP, D_MODEL, 2 * EXPERT_FF), lambda i, e: (e, 0, 0)),
                  pl.BlockSpec((EXPERTS_PER_STEP, EXPERT_FF, D_MODEL), lambda i, e: (e, 0, 0))],
        out_specs=pl.BlockSpec((tm, D_MODEL), row),
        out_shape=jax.ShapeDtypeStruct((n, D_MODEL), F32),
        compiler_params=_cparams(("parallel", "arbitrary")),
        name="moe_dense",
    )(x, t, route, wgu, wd)


def _hier_moe(x, g, w_group, b_group, w_router, b_router, w_gate, w_up, w_down):
    t, route = _router(x, g, w_group, b_group, w_router, b_router)
    return _moe_dense(x, t, route, w_gate, w_up, w_down)


def _decode_tables(rel_bias, past, t_new, lk):
    t8 = np.arange(T8)
    lane = np.arange(LANES)
    far = np.full((T8, LANES), REL_BUCKETS - 1, np.int32)
    d_last = (past + t8[:, None]) - (past - LANES + lane[None, :])
    d_new = t8[:, None] - lane[None, :]
    ok_new = (d_new >= 0) & (lane[None, :] < t_new)
    ta = _bias_expand(rel_bias, np.stack([far, _t5_bucket_np(d_last), _t5_bucket_np(d_new)]), 0, A_HEADS)
    ta = jnp.where(jnp.asarray(ok_new)[None, None] | (jnp.arange(3) < 2)[None, :, None, None], ta, NEG)
    taba = jnp.broadcast_to(jnp.transpose(ta, (1, 0, 2, 3))[:, :, None], (3, A_HEADS, 2, T8, LANES))
    taba = taba.reshape(3, A_HEADS * 2 * T8, LANES)
    kpos = np.arange(lk)
    d_b = (past + t8[:, None]) - kpos[None, :]
    tabb = _bias_expand(rel_bias, _t5_bucket_np(d_b), A_HEADS, B_HEADS).reshape(B_HEADS * T8, lk)
    valid = (kpos[None, :] < past) | ((d_b >= 0) & (kpos[None, :] < past + t_new))
    validb = jnp.asarray(np.where(valid, 0.0, NEG), F32)
    return taba, tabb, validb


def _pad_rows(x, rows):
    pad = [(0, 0)] * x.ndim
    pad[1] = (0, rows - x.shape[1])
    return jnp.pad(x, pad)


def kernel(x_prompt, x_sample, cache_a_k, cache_a_v, cache_b_k, cache_b_v, cache_b_kidx, cache_c_k, cache_c_v, cache_c_logf, page_table, rel_bias, ab_norm, ab_w_in, a_q_norm, a_k_norm, b_q_norm, b_k_norm, a_lambda_q1, a_lambda_k1, a_lambda_q2, a_lambda_k2, a_sub_norm, ab_w_out, c_norm, c_w_in, c_forget_bias, c_q_norm, c_k_norm, c_w_out, ffn_norm, moe_w_group, moe_b_group, moe_w_router, moe_b_router, moe_w_gate, moe_w_up, moe_w_down):
    nb, s, d = x_prompt.shape
    db, ts, _ = x_sample.shape
    npg = page_table.shape[1]
    pool, ps = cache_a_k.shape[1], cache_a_k.shape[2]
    past = npg * ps
    n_p = nb * s
    n_s = db * ts
    depth = ffn_norm.shape[0]
    topk_p = min(IDX_TOPK_MAX, s // 4)
    topk_s = min(IDX_TOPK_MAX, (past + ts) // 4)
    lk = past + LANES

    cak_t = jnp.transpose(cache_a_k, (0, 1, 3, 4, 5, 2)).reshape(-1, pool, 2 * A_HEADS * HEAD_DIM, ps)
    cav_r = cache_a_v.reshape(-1, pool, ps * A_HEADS, 2 * HEAD_DIM)
    cbk_t = jnp.transpose(cache_b_k, (0, 1, 3, 2))
    cbv_t = jnp.transpose(cache_b_v, (0, 1, 3, 2))
    cbi_t = jnp.transpose(cache_b_kidx, (0, 1, 3, 2))
    cck_t = jnp.transpose(cache_c_k, (0, 1, 3, 4, 2)).reshape(-1, pool, C_HEADS * HEAD_DIM, ps)
    ccv_t = jnp.transpose(cache_c_v, (0, 1, 3, 4, 2)).reshape(-1, pool, C_HEADS * HEAD_DIM, ps)
    ccf_t = jnp.transpose(cache_c_logf, (0, 1, 3, 2))

    x = jnp.concatenate([x_prompt.reshape(n_p, d), x_sample.reshape(n_s, d)], axis=0)
    outs_p = {k: [] for k in ("ak", "av", "bk", "bv", "bi", "ck", "cv", "cf")}
    outs_s = {k: [] for k in ("ak", "av", "bk", "bv", "bi", "ck", "cv", "cf")}

    def smp(a):
        return a[n_p:].reshape(db, ts, a.shape[1])

    for layer in range(depth):
        if layer % 2 == 0:
            e = layer // 2
            lam_init = 0.8 - 0.6 * math.exp(-0.3 * layer)
            lamv = jnp.stack([a_lambda_q1[e], a_lambda_k1[e], a_lambda_q2[e], a_lambda_k2[e]])
            gsub = a_sub_norm[e][None, :]
            (qa, ka_s, kat, katbf, qb, small, smallt, kk, vv, kiki, va, vabf, qi) = _proj0(
                x, ab_norm[e], ab_w_in[e], a_q_norm[e], a_k_norm[e], b_q_norm[e], b_k_norm[e], n_p, s)
            tq_a = 256
            bias_a = _bias_expand(rel_bias, _prompt_bucket_tiles(tq_a), 0, A_HEADS)
            oa_p = _attn_a_prompt(qa, katbf, vabf, lamv, gsub, bias_a, nb, s, lam_init, tq=tq_a)
            tq_b = 256
            bias_b = _bias_expand(rel_bias, _prompt_bucket_tiles(tq_b), A_HEADS, B_HEADS)
            ob_p = _attn_b_prompt(qb, qi, small, kk, vv, kiki, bias_b, nb, s, topk_p, tq=tq_b)
            qa_s = _pad_rows(smp(qa), T8)
            hc = np.arange(2 * A_HEADS)
            colmask = jnp.asarray((np.arange(qa_s.shape[2])[None, :] // HEAD_DIM) == hc[:, None], MXU_DT)
            qa_bd = (qa_s[:, None, :, :] * colmask[None, :, None, :]).reshape(db, 2 * A_HEADS * T8, -1)

            def heads_rows(a, nh):
                a = _pad_rows(a, T8).reshape(db, T8, nh, HEAD_DIM)
                return jnp.transpose(a, (0, 2, 1, 3)).reshape(db, nh * T8, HEAD_DIM)

            qi64 = heads_rows(smp(qi), IDX_HEADS)
            qb64 = heads_rows(smp(qb), B_HEADS)
            small_s = smp(small)
            w8 = _pad_rows(small_s[:, :, 192:200], T8)
            ka_s3 = ka_s.reshape(db, ts, -1)
            va_s3 = smp(va)
            kna = _pad_rows(ka_s3, 16)
            vna = _pad_rows(va_s3, 16)
            bnew_t = jnp.transpose(_pad_rows(small_s[:, :, :192], LANES).reshape(db, LANES, 3, HEAD_DIM), (0, 2, 3, 1))
            taba, tabb, validb = _decode_tables(rel_bias, past, ts, lk)
            oa_d, ob_d = _decode0(page_table, lamv, gsub, qa_bd, qi64, qb64, w8, kna, vna, bnew_t,
                                  cak_t, cav_r, cbk_t, cbv_t, cbi_t, taba, tabb, validb, e, topk_s, lam_init)
            oa_s = jnp.transpose(oa_d[:, :, :ts], (0, 2, 1, 3)).reshape(n_s, A_HEADS * LANES)
            ob_s = jnp.transpose(ob_d[:, :, :ts], (0, 2, 1, 3)).reshape(n_s, B_HEADS * HEAD_DIM)
            w_out = ab_w_out[e]
            x = _outproj(x, [(oa_p, _mx(oa_s), w_out[:A_HEADS * LANES]),
                             (ob_p, _mx(ob_s), w_out[A_HEADS * LANES:])], n_p)
            outs_p["ak"].append(jnp.transpose(kat.reshape(nb, A_HEADS, 2, HEAD_DIM, s), (0, 4, 1, 2, 3)))
            outs_p["av"].append(va[:n_p].reshape(nb, s, A_HEADS, 2 * HEAD_DIM))
            outs_p["bk"].append(jnp.transpose(smallt[:, 0:64], (0, 2, 1)))
            outs_p["bv"].append(jnp.transpose(smallt[:, 64:128], (0, 2, 1)))
            outs_p["bi"].append(jnp.transpose(smallt[:, 128:192], (0, 2, 1)))
            outs_s["ak"].append(ka_s3.reshape(db, ts, A_HEADS, 2, HEAD_DIM))
            outs_s["av"].append(va_s3.reshape(db, ts, A_HEADS, 2 * HEAD_DIM))
            outs_s["bk"].append(small_s[:, :, 0:64])
            outs_s["bv"].append(small_s[:, :, 64:128])
            outs_s["bi"].append(small_s[:, :, 128:192])
        else:
            o = layer // 2
            q, k_s, kt, ktbf, v_s, vt, vbf, logf128, logft = _proj1(
                x, c_norm[o], c_w_in[o], c_forget_bias[o], c_q_norm[o], c_k_norm[o], n_p, s)
            logf = logf128[:, :C_HEADS]
            cum_p = _cumsum_prompt(logf128[:n_p], nb, s)[:, :C_HEADS]
            ckt = jnp.transpose(cum_p.reshape(nb, s, C_HEADS // 2, 2), (0, 2, 3, 1))
            oc_p = _attn_c_prompt(q, ktbf, vbf, cum_p, ckt, nb, s)
            logf_s = smp(logf)
            run = jnp.zeros_like(logf_s[:, 0])
            c_rows = []
            for t in range(ts):
                run = run + logf_s[:, t]
                c_rows.append(run)
            c_new = jnp.stack(c_rows, axis=1)
            cq = jnp.transpose(_pad_rows(c_new, T8), (0, 2, 1)).reshape(db, C_HEADS * T8, 1)
            cnew_t = jnp.transpose(_pad_rows(c_new, 16), (0, 2, 1))
            q_s = _pad_rows(smp(q), T8)
            hmask = jnp.asarray((np.arange(q_s.shape[2])[None, :] // HEAD_DIM) == np.arange(C_HEADS)[:, None], MXU_DT)
            q_bd = (q_s[:, None, :, :] * hmask[None, :, None, :]).reshape(db, C_HEADS * T8, -1)
            k_s3 = k_s.reshape(db, ts, -1)
            v_s3 = v_s.reshape(db, ts, -1)
            kn = _pad_rows(k_s3, 16)
            vn = _pad_rows(v_s3, 16)
            t8 = np.arange(T8)
            okn = (t8[:, None] >= np.arange(16)[None, :]) & (np.arange(16)[None, :] < ts)
            maskn = jnp.asarray(np.tile(np.where(okn, 0.0, NEG), (C_HEADS, 1)), F32)
            oc_d = _decode1(page_table, q_bd, cq, kn, vn, cnew_t, cck_t, ccv_t, ccf_t, maskn, o)
            oc_s = jnp.transpose(oc_d[:, :, :ts], (0, 2, 1, 3)).reshape(n_s, C_HEADS * HEAD_DIM)
            x = _outproj(x, [(oc_p, _mx(oc_s), c_w_out[o])], n_p)
            outs_p["ck"].append(jnp.transpose(kt.reshape(nb, C_HEADS, HEAD_DIM, s), (0, 3, 1, 2)))
            outs_p["cv"].append(jnp.transpose(vt.reshape(nb, C_HEADS, HEAD_DIM, s), (0, 3, 1, 2)))
            outs_p["cf"].append(jnp.transpose(logft[:, :C_HEADS], (0, 2, 1)))
            outs_s["ck"].append(k_s3.reshape(db, ts, C_HEADS, HEAD_DIM))
            outs_s["cv"].append(v_s3.reshape(db, ts, C_HEADS, HEAD_DIM))
            outs_s["cf"].append(logf_s)
        x = _hier_moe(x, ffn_norm[layer], moe_w_group[layer], moe_b_group[layer], moe_w_router[layer],
                      moe_b_router[layer], moe_w_gate[layer], moe_w_up[layer], moe_w_down[layer])

    keys = ("ak", "av", "bk", "bv", "bi", "ck", "cv", "cf")
    return ((x[:n_p].reshape(nb, s, d), x[n_p:].reshape(db, ts, d))
            + tuple(jnp.stack(outs_p[k]) for k in keys)
            + tuple(jnp.stack(outs_s[k]) for k in keys))
```

```python
import functools
import math

import numpy as np
import jax
import jax.numpy as jnp
from jax import lax
from jax.experimental import pallas as pl
from jax.experimental.pallas import tpu as pltpu

F32 = jnp.float32
I32 = jnp.int32
MXU_DT = jnp.bfloat16

D_MODEL = 1024
HEAD_DIM = 64
A_HEADS = 4
B_HEADS = 8
IDX_HEADS = 8
IDX_DIM = 64
IDX_TOPK_MAX = 256
C_HEADS = 16
REL_BUCKETS = 32
REL_MAX_DIST = 128
N_GROUPS = 4
EXPERTS_PER_GROUP = 8
N_EXPERTS = N_GROUPS * EXPERTS_PER_GROUP
EXPERT_FF = 256
EPS = 1e-6
NEG = -1e30
INT_MIN = -2 ** 31
QK_SCALE = HEAD_DIM ** -0.5
IDX_SCALE = IDX_DIM ** -0.5
IDX_HEAD_SCALE = IDX_HEADS ** -0.5
LANES = 128
T8 = 8
VMEM_LIMIT = 56 * 1024 * 1024
PAGES_PER_STEP = 16
FLASH_WIDE = 2
EXPERTS_PER_STEP = 4

def _tile(n, pref):
    best = 16
    for t in range(16, pref + 1, 16):
        if n % t == 0:
            best = t
    assert n % best == 0, (n, pref)
    return best


def _cparams(sem):
    return pltpu.CompilerParams(dimension_semantics=sem, vmem_limit_bytes=VMEM_LIMIT)


def _mx(x):
    return x.astype(MXU_DT)


def _dot(a, b):
    return jnp.dot(a, b, preferred_element_type=F32)


def _dot_t(a, b):
    return lax.dot_general(a, b, (((1,), (1,)), ((), ())), preferred_element_type=F32)


def _split2(x):
    hi = _mx(x)
    lo = _mx(x - hi.astype(F32))
    return hi, lo


def _split3(x):
    hi = _mx(x)
    r = x - hi.astype(F32)
    mid = _mx(r)
    lo = _mx(r - mid.astype(F32))
    return hi, mid, lo


def _lane_iota(shape):
    return lax.broadcasted_iota(I32, shape, len(shape) - 1)


def _row_iota(shape):
    return lax.broadcasted_iota(I32, shape, len(shape) - 2)


def _lanes(x, width):
    if width <= LANES:
        return x[:, :width]
    return jnp.tile(x, (1, width // LANES))


def _rms_rows(x, g):
    ms = jnp.mean(x * x, axis=-1, keepdims=True)
    return x * lax.rsqrt(ms + EPS) * g


def _seg_rsqrt(y, bd):
    hi, lo = _split2(y * y)
    ss = _dot(hi, bd) + _dot(lo, bd)
    return lax.rsqrt(ss * (1.0 / HEAD_DIM) + EPS)


def _block_diag_ones(n, seg):
    r = np.arange(n)
    return jnp.asarray((r[:, None] // seg) == (r[None, :] // seg), dtype=MXU_DT)


def _lambda(lamv_ref, lam_init):
    lv = lamv_ref[...]
    return (jnp.exp(jnp.sum(lv[0:1] * lv[1:2], axis=-1, keepdims=True))
            - jnp.exp(jnp.sum(lv[2:3] * lv[3:4], axis=-1, keepdims=True)) + lam_init)


def _t5_bucket_np(d):
    n = np.maximum(d, 0)
    exact = REL_BUCKETS // 2
    nf = np.maximum(n, 1).astype(np.float64)
    large = exact + (np.log(nf / exact) / math.log(REL_MAX_DIST / exact) * (REL_BUCKETS - exact)).astype(np.int64)
    large = np.minimum(large, REL_BUCKETS - 1)
    return np.where(n < exact, n, large).astype(np.int32)


def _bias_expand_kernel(relb_ref, idx_ref, o_ref, *, head0):
    h = pl.program_id(0) + head0
    idx = idx_ref[...]
    acc = jnp.zeros(idx.shape, F32)
    for b in range(REL_BUCKETS):
        acc = jnp.where(idx == b, relb_ref[b, h], acc)
    o_ref[...] = acc


def _bias_expand(rel_bias, idx_np, head0, n_heads):
    idx = jnp.asarray(idx_np, I32)
    nd = idx.ndim
    zeros = (0,) * nd
    return pl.pallas_call(
        functools.partial(_bias_expand_kernel, head0=head0),
        grid=(n_heads,),
        in_specs=[pl.BlockSpec(memory_space=pltpu.SMEM),
                  pl.BlockSpec(idx.shape, lambda h: zeros)],
        out_specs=pl.BlockSpec((None,) + idx.shape, lambda h: (h,) + zeros),
        out_shape=jax.ShapeDtypeStruct((n_heads,) + idx.shape, F32),
        compiler_params=_cparams(("arbitrary",)),
        name="bias_expand",
    )(rel_bias, idx)


def _prompt_bucket_tiles(t):
    r = np.arange(t)
    d = r[:, None] - r[None, :]
    return np.stack([_t5_bucket_np(d), _t5_bucket_np(d + t), np.full((t, t), REL_BUCKETS - 1, np.int32)])


def _store_t(y, is_prompt, t_refs, row_ref, c):
    if is_prompt:
        yt = jnp.transpose(y)
        for r in t_refs:
            r[c * 256:(c + 1) * 256, :] = yt.astype(r.dtype)
    elif row_ref is not None:
        row_ref[:, c * 256:(c + 1) * 256] = y


def _per_tile_kind(body, n_prompt_tiles):
    prompt_tile = pl.program_id(0) < n_prompt_tiles

    @pl.when(prompt_tile)
    def _():
        body(True)

    @pl.when(jnp.logical_not(prompt_tile))
    def _():
        body(False)


def _proj0_kernel(*refs, n_prompt_tiles):
    _per_tile_kind(functools.partial(_proj0_body, *refs), n_prompt_tiles)


def _proj0_body(x_ref, g_ref, w_ref, gain_ref, bd_ref,
                qa_ref, ka_ref, kat_ref, katbf_ref, qb_ref, small_ref, smallt_ref, kk_ref, vv_ref, kiki_ref,
                va_ref, vabf_ref, qi_ref, is_prompt):
    h = _mx(_rms_rows(x_ref[...], g_ref[...]))
    bd = bd_ref[...]

    def chunk(c):
        return _dot(h, w_ref[:, c * 256:(c + 1) * 256])

    def normed(c):
        y = chunk(c)
        return y * _seg_rsqrt(y, bd) * gain_ref[:, c * 256:(c + 1) * 256]

    for c in range(2):
        qa_ref[:, c * 256:(c + 1) * 256] = _mx(normed(c) * QK_SCALE)
    for c in range(2):
        _store_t(normed(2 + c), is_prompt, (kat_ref, katbf_ref), ka_ref, c)
    for c in range(2):
        qb_ref[:, c * 256:(c + 1) * 256] = _mx(normed(4 + c) * QK_SCALE)
    y = chunk(6)
    yn = y * _seg_rsqrt(y, bd) * gain_ref[:, 6 * 256:7 * 256]
    lane = _lane_iota(y.shape)
    y = jnp.where(lane < HEAD_DIM, yn, y)
    small_ref[...] = y
    _store_t(y, is_prompt, (smallt_ref,), None, 0)
    t0 = y[:, :LANES]
    t1 = y[:, LANES:]
    lo = _lane_iota(t0.shape) < HEAD_DIM
    r0 = pltpu.roll(t0, HEAD_DIM, 1)
    kk_ref[...] = _mx(jnp.where(lo, t0, r0))
    vv_ref[...] = _mx(jnp.where(lo, r0, t0))
    r1 = pltpu.roll(t1, HEAD_DIM, 1)
    kiki_ref[...] = _mx(jnp.where(lo, t1, r1))
    for c in range(2):
        y = chunk(7 + c)
        va_ref[:, c * 256:(c + 1) * 256] = y
        vabf_ref[:, c * 256:(c + 1) * 256] = _mx(y)
    for c in range(2):
        qi_ref[:, c * 256:(c + 1) * 256] = _mx(chunk(9 + c) * IDX_SCALE)


def _token_specs(n_p, n_s, s, tm):
    npt = n_p // tm
    tps = s // tm

    def t_map(i):
        ip = jnp.minimum(i, npt - 1)
        return (ip // tps, 0, ip % tps)

    def s_map(i):
        return (jnp.maximum(i - npt, 0), 0)

    return npt, t_map, s_map


def _proj0(x, g, w_in, a_qn, a_kn, b_qn, b_kn, n_p, s, tm=512):
    n = x.shape[0]
    n_s = n - n_p
    nb = n_p // s
    tm = _tile(math.gcd(n_p, n_s, s), tm)
    npt, t_map, s_map = _token_specs(n_p, n_s, s, tm)
    sp = np.cumsum([512, 512, 512, 512, 64, 64, 512, 64, 8])[:-1]
    wqa, wka, wva, wqb, wkb, wvb, wqi, wki, wwi = jnp.split(w_in, sp, axis=1)
    w = jnp.concatenate([wqa, wka, wqb, wkb, wvb, wki, wwi, jnp.zeros((D_MODEL, 56), F32), wva, wqi], axis=1)
    w = _mx(w)
    ncol = w.shape[1]
    gain = jnp.concatenate([jnp.tile(a_qn, 8), jnp.tile(a_kn, 8), jnp.tile(b_qn, 8), b_kn,
                            jnp.ones((192,), F32)])[None, :]
    bd = _block_diag_ones(256, HEAD_DIM)
    row = lambda i: (i, 0)
    fixed = lambda i: (0, 0)
    def rows(wd, dt):
        return pl.BlockSpec((tm, wd), row), jax.ShapeDtypeStruct((n, wd), dt)

    def rows_s(wd, dt):
        return pl.BlockSpec((tm, wd), s_map), jax.ShapeDtypeStruct((n_s, wd), dt)

    def cols_p(wd, dt):
        return pl.BlockSpec((None, wd, tm), t_map), jax.ShapeDtypeStruct((nb, wd, s), dt)

    outs = [rows(512, MXU_DT), rows_s(512, F32), cols_p(512, F32), cols_p(512, MXU_DT), rows(512, MXU_DT),
            rows(256, F32), cols_p(256, F32), rows(128, MXU_DT), rows(128, MXU_DT), rows(128, MXU_DT),
            rows(512, F32), rows(512, MXU_DT), rows(512, MXU_DT)]
    return pl.pallas_call(
        functools.partial(_proj0_kernel, n_prompt_tiles=npt),
        grid=(n // tm,),
        in_specs=[pl.BlockSpec((tm, D_MODEL), row), pl.BlockSpec((1, D_MODEL), fixed),
                  pl.BlockSpec((D_MODEL, ncol), fixed), pl.BlockSpec((1, gain.shape[1]), fixed),
                  pl.BlockSpec((256, 256), fixed)],
        out_specs=[o[0] for o in outs],
        out_shape=[o[1] for o in outs],
        compiler_params=_cparams(("arbitrary",)),
        name="proj0",
    )(x, g[None, :], w, gain, bd)


def _proj1_kernel(*refs, n_prompt_tiles):
    _per_tile_kind(functools.partial(_proj1_body, *refs), n_prompt_tiles)


def _proj1_body(x_ref, g_ref, w_ref, gain_ref, bf_ref, bd_ref,
                q_ref, k_ref, kt_ref, ktbf_ref, v_ref, vt_ref, vbf_ref, logf_ref, logft_ref, is_prompt):
    h = _mx(_rms_rows(x_ref[...], g_ref[...]))
    bd = bd_ref[...]

    def chunk(c):
        return _dot(h, w_ref[:, c * 256:(c + 1) * 256])

    def normed(c):
        y = chunk(c)
        return y * _seg_rsqrt(y, bd) * gain_ref[:, c * 256:(c + 1) * 256]

    for c in range(4):
        q_ref[:, c * 256:(c + 1) * 256] = _mx(normed(c) * QK_SCALE)
    for c in range(4):
        _store_t(normed(4 + c), is_prompt, (kt_ref, ktbf_ref), k_ref, c)
    for c in range(4):
        y = chunk(8 + c)
        vbf_ref[:, c * 256:(c + 1) * 256] = _mx(y)
        _store_t(y, is_prompt, (vt_ref,), v_ref, c)
    f = _dot(h, w_ref[:, 12 * 256:12 * 256 + LANES]) + bf_ref[...]
    logf = jnp.minimum(f, 0.0) - jnp.log(1.0 + jnp.exp(-jnp.abs(f)))
    logf_ref[...] = logf
    if is_prompt:
        logft_ref[...] = jnp.transpose(logf)


def _proj1(x, g, w_in, b_f, qn, kn, n_p, s, tm=512):
    n = x.shape[0]
    n_s = n - n_p
    nb = n_p // s
    tm = _tile(math.gcd(n_p, n_s, s), tm)
    npt, t_map, s_map = _token_specs(n_p, n_s, s, tm)
    w = _mx(jnp.concatenate([w_in, jnp.zeros((D_MODEL, LANES - C_HEADS), F32)], axis=1))
    ncol = w.shape[1]
    gain = jnp.concatenate([jnp.tile(qn, C_HEADS), jnp.tile(kn, C_HEADS)])[None, :]
    bf = jnp.concatenate([b_f, jnp.zeros((LANES - C_HEADS,), F32)])[None, :]
    bd = _block_diag_ones(256, HEAD_DIM)
    row = lambda i: (i, 0)
    fixed = lambda i: (0, 0)
    def rows(wd, dt):
        return pl.BlockSpec((tm, wd), row), jax.ShapeDtypeStruct((n, wd), dt)

    def rows_s(wd, dt):
        return pl.BlockSpec((tm, wd), s_map), jax.ShapeDtypeStruct((n_s, wd), dt)

    def cols_p(wd, dt):
        return pl.BlockSpec((None, wd, tm), t_map), jax.ShapeDtypeStruct((nb, wd, s), dt)

    outs = [rows(1024, MXU_DT), rows_s(1024, F32), cols_p(1024, F32), cols_p(1024, MXU_DT),
            rows_s(1024, F32), cols_p(1024, F32), rows(1024, MXU_DT), rows(LANES, F32), cols_p(LANES, F32)]
    return pl.pallas_call(
        functools.partial(_proj1_kernel, n_prompt_tiles=npt),
        grid=(n // tm,),
        in_specs=[pl.BlockSpec((tm, D_MODEL), row), pl.BlockSpec((1, D_MODEL), fixed),
                  pl.BlockSpec((D_MODEL, ncol), fixed), pl.BlockSpec((1, gain.shape[1]), fixed),
                  pl.BlockSpec((1, LANES), fixed), pl.BlockSpec((256, 256), fixed)],
        out_specs=[o[0] for o in outs],
        out_shape=[o[1] for o in outs],
        compiler_params=_cparams(("arbitrary",)),
        name="proj1",
    )(x, g[None, :], w, gain, bf, bd)


def _outproj_kernel(*refs, n_in, n_prompt_tiles):
    res_ref = refs[0]
    out_ref = refs[-1]

    def body(is_prompt):
        acc = res_ref[...]
        for i in range(n_in):
            a_ref = refs[1 + 3 * i] if is_prompt else refs[2 + 3 * i]
            acc = acc + _dot(a_ref[...], refs[3 + 3 * i][...])
        out_ref[...] = acc

    _per_tile_kind(body, n_prompt_tiles)


def _outproj(res, terms, n_p, tm=512):
    n = res.shape[0]
    n_s = n - n_p
    tm = _tile(math.gcd(n_p, n_s), tm)
    npt = n_p // tm
    row = lambda i: (i, 0)
    fixed = lambda i: (0, 0)
    p_map = lambda i: (jnp.minimum(i, npt - 1), 0)
    s_map = lambda i: (jnp.maximum(i - npt, 0), 0)
    in_specs = [pl.BlockSpec((tm, D_MODEL), row)]
    args = [res]
    for a_p, a_s, w in terms:
        in_specs += [pl.BlockSpec((tm, a_p.shape[1]), p_map), pl.BlockSpec((tm, a_s.shape[1]), s_map),
                     pl.BlockSpec(w.shape, fixed)]
        args += [a_p, a_s, _mx(w)]
    return pl.pallas_call(
        functools.partial(_outproj_kernel, n_in=len(terms), n_prompt_tiles=npt),
        grid=(n // tm,),
        in_specs=in_specs,
        out_specs=pl.BlockSpec((tm, D_MODEL), row),
        out_shape=jax.ShapeDtypeStruct((n, D_MODEL), F32),
        compiler_params=_cparams(("parallel",)),
        name="outproj",
    )(*args)


def _stack_streams(q):
    lane = _lane_iota(q.shape)
    zero = jnp.zeros_like(q)
    return jnp.concatenate([jnp.where(lane < HEAD_DIM, q, zero), jnp.where(lane >= HEAD_DIM, q, zero)], axis=0)


def _flash_stacked(q2, kt_ref, v_ref, i, tq, tk, bias_fn, m_ref, l_ref, acc_ref):
    rows = 2 * tq
    m_ref[...] = jnp.full(m_ref.shape, NEG, F32)
    l_ref[...] = jnp.zeros(l_ref.shape, F32)
    acc_ref[...] = jnp.zeros(acc_ref.shape, F32)

    def tile(j, n_sub, masked):
        width = n_sub * tk
        off = pl.multiple_of(j * tk, tk)
        s = _dot(q2, kt_ref[:, pl.ds(off, width)]) + bias_fn(j, off, n_sub)
        if masked:
            r = _row_iota((rows, width))
            s = jnp.where(_lane_iota((rows, width)) <= jnp.where(r >= tq, r - tq, r), s, NEG)
        alpha, p = _online_update(s, m_ref, l_ref)
        acc_ref[...] = alpha * acc_ref[...] + _dot(_mx(p), v_ref[pl.ds(off, width), :])

    def body(jj, carry):
        tile(jj * FLASH_WIDE, FLASH_WIDE, False)
        return carry

    lax.fori_loop(0, i // FLASH_WIDE, body, 0)
    for u in range(1, FLASH_WIDE):
        @pl.when(i % FLASH_WIDE >= u)
        def _():
            tile(i - i % FLASH_WIDE + (u - 1), 1, False)
    tile(i, 1, True)


def _attn_a_kernel(lamv_ref, gsub_ref, q_ref, k_ref, v_ref, bias_ref, o_ref,
                   m_ref, l_ref, acc_ref, *, tq, tk, lam_init):
    i = pl.program_id(2)

    def bias_fn(j, off, n_sub):
        b = jnp.concatenate([bias_ref[jnp.minimum(i - j - u, 2)] for u in range(n_sub)], axis=1)
        return jnp.concatenate([b, b], axis=0)

    _flash_stacked(_stack_streams(q_ref[...]), k_ref, v_ref, i, tq, tk, bias_fn, m_ref, l_ref, acc_ref)
    lam = _lambda(lamv_ref, lam_init)
    o = acc_ref[:tq] / l_ref[:tq] - lam * (acc_ref[tq:] / l_ref[tq:])
    o = _rms_rows(o, gsub_ref[...]) * (1.0 - lam_init)
    o_ref[...] = _mx(o)


def _attn_a_prompt(qa, ka, va, lamv, gsub, bias_tiles, nb, s, lam_init, tq=256):
    tk = tq
    nq = s // tq
    kern = functools.partial(_attn_a_kernel, tq=tq, tk=tk, lam_init=lam_init)
    return pl.pallas_call(
        kern,
        grid=(nb, A_HEADS, nq),
        in_specs=[pl.BlockSpec((4, HEAD_DIM), lambda b, h, i: (0, 0)),
                  pl.BlockSpec((1, LANES), lambda b, h, i: (0, 0)),
                  pl.BlockSpec((tq, LANES), lambda b, h, i: (b * nq + i, h)),
                  pl.BlockSpec((None, LANES, s), lambda b, h, i: (b, h, 0)),
                  pl.BlockSpec((s, LANES), lambda b, h, i: (b, h)),
                  pl.BlockSpec((None, 3, tq, tk), lambda b, h, i: (h, 0, 0, 0))],
        out_specs=pl.BlockSpec((tq, LANES), lambda b, h, i: (b * nq + i, h)),
        out_shape=jax.ShapeDtypeStruct((nb * s, A_HEADS * LANES), MXU_DT),
        scratch_shapes=[pltpu.VMEM((2 * tq, LANES), F32), pltpu.VMEM((2 * tq, LANES), F32),
                        pltpu.VMEM((2 * tq, LANES), F32)],
        compiler_params=_cparams(("parallel", "parallel", "arbitrary")),
        name="attn_a_prompt",
    )(lamv, gsub, qa, ka, va, bias_tiles)


def _attn_c_kernel(q_ref, k_ref, v_ref, cq_ref, ckt_ref, o_ref, m_ref, l_ref, acc_ref, *, tq, tk):
    hp = pl.program_id(1)
    i = pl.program_id(2)
    cq_tile = cq_ref[...]
    lane16 = _lane_iota(cq_tile.shape)
    cq2 = jnp.concatenate(
        [jnp.broadcast_to(jnp.sum(jnp.where(lane16 == 2 * hp + c, cq_tile, 0.0), axis=-1, keepdims=True),
                          (tq, LANES)) for c in range(2)], axis=0)

    def bias_fn(j, off, n_sub):
        width = n_sub * tk
        ck = ckt_ref[:, pl.ds(off, width)]
        ck2 = jnp.concatenate([jnp.broadcast_to(ck[0:1], (tq, width)),
                               jnp.broadcast_to(ck[1:2], (tq, width))], axis=0)
        return _lanes(cq2, width) - ck2

    _flash_stacked(_stack_streams(q_ref[...]), k_ref, v_ref, i, tq, tk, bias_fn, m_ref, l_ref, acc_ref)
    lane = _lane_iota((tq, LANES))
    o = jnp.where(lane < HEAD_DIM, acc_ref[:tq] / l_ref[:tq], acc_ref[tq:] / l_ref[tq:])
    o_ref[...] = _mx(o)


def _attn_c_prompt(q, k, v, cq, ckt, nb, s, tq=256):
    tk = tq
    nq = s // tq
    nhp = C_HEADS // 2
    kern = functools.partial(_attn_c_kernel, tq=tq, tk=tk)
    return pl.pallas_call(
        kern,
        grid=(nb, nhp, nq),
        in_specs=[pl.BlockSpec((tq, LANES), lambda b, h, i: (b * nq + i, h)),
                  pl.BlockSpec((None, LANES, s), lambda b, h, i: (b, h, 0)),
                  pl.BlockSpec((s, LANES), lambda b, h, i: (b, h)),
                  pl.BlockSpec((tq, C_HEADS), lambda b, h, i: (b * nq + i, 0)),
                  pl.BlockSpec((None, None, 2, s), lambda b, h, i: (b, h, 0, 0))],
        out_specs=pl.BlockSpec((tq, LANES), lambda b, h, i: (b * nq + i, h)),
        out_shape=jax.ShapeDtypeStruct((nb * s, C_HEADS * HEAD_DIM), MXU_DT),
        scratch_shapes=[pltpu.VMEM((2 * tq, LANES), F32), pltpu.VMEM((2 * tq, LANES), F32),
                        pltpu.VMEM((2 * tq, LANES), F32)],
        compiler_params=_cparams(("parallel", "parallel", "arbitrary")),
        name="attn_c_prompt",
    )(q, k, v, cq, ckt)


def _score_keys(score):
    score = jnp.where(score == 0.0, 0.0, score)
    bits = pltpu.bitcast(score, I32)
    return bits ^ (jnp.right_shift(bits, 31) & 0x7FFFFFFF)


def _topk_select(keys_ref, width, kcount, active, col):
    kf = float(kcount)
    nbits_col = int(width - 1).bit_length()

    def count(pred):
        return jnp.sum(jnp.where(pred, 1.0, 0.0), axis=1, keepdims=True)

    t0 = jnp.where(count(keys_ref[:, :width] >= 0) >= kf, 0, INT_MIN).astype(I32)

    def body(it, t):
        cand = t + jnp.left_shift(jnp.int32(1), 30 - it)
        return jnp.where(count(keys_ref[:, :width] >= cand) >= kf, cand, t)

    t = lax.fori_loop(0, 31, body, t0)
    t = jnp.where(active, t, INT_MIN)
    keys = keys_ref[:, :width]
    gt = keys > t
    eq = keys == t
    need = kf - count(gt)
    excess = jnp.where(active, count(eq) - need, 0.0)

    def tie_break():
        def tb(it, jj):
            cand = jj + jnp.left_shift(jnp.int32(1), nbits_col - 1 - it)
            c = count((keys_ref[:, :width] == t) & (col < cand))
            return jnp.where(c < need, cand, jj)
        return lax.fori_loop(0, nbits_col, tb, jnp.zeros(t.shape, I32))

    jmax = lax.cond(jnp.max(excess) > 0.0, tie_break, lambda: jnp.full(t.shape, width, I32))
    return gt | (eq & (col <= jmax))


KEY_OF_NEG_INF = -2139095041
SELECT_UNIT = 2


def _attn_b_kernel(qb_ref, qi_ref, w_ref, kk_ref, vv_ref, kiki_ref, bias_ref, o_ref,
                   keys_ref, selm_ref, m_ref, l_ref, acc_ref, *, tq, s_len, topk):
    i = pl.program_id(1)
    cw = tq
    nh = B_HEADS
    n_chunks = s_len // cw
    lane = _lane_iota((tq, LANES))
    halves = (lane < HEAD_DIM, lane >= HEAD_DIM)

    def stack_heads(ref):
        parts = []
        for h in range(nh):
            t = ref[:, (h // 2) * LANES:(h // 2 + 1) * LANES]
            parts.append(jnp.where(halves[h % 2], t, jnp.zeros_like(t)))
        return jnp.concatenate(parts, axis=0)

    row = _row_iota((tq, cw)) + i * tq
    colc = _lane_iota((tq, cw))

    qi8 = stack_heads(qi_ref)
    wt = w_ref[...] * IDX_HEAD_SCALE
    wcol = jnp.concatenate([jnp.broadcast_to(wt[:, HEAD_DIM + h:HEAD_DIM + h + 1], (tq, LANES))
                            for h in range(nh)], axis=0)

    def score_chunk(j, carry):
        off = pl.multiple_of(j * cw, cw)
        d = jnp.maximum(_dot_t(qi8, kiki_ref[pl.ds(off, cw), :]), 0.0) * _lanes(wcol, cw)
        sc = d[0:tq]
        for h in range(1, nh):
            sc = sc + d[h * tq:(h + 1) * tq]
        keys_ref[:, pl.ds(off, cw)] = _score_keys(jnp.where((colc + j * cw) <= row, sc, -jnp.inf))
        return carry

    lax.fori_loop(0, i + 1, score_chunk, 0)
    unit = SELECT_UNIT if n_chunks % SELECT_UNIT == 0 else 1
    rem = (i + 1) % unit
    for u in range(1, unit):
        @pl.when((rem != 0) & (u <= unit - rem))
        def _():
            off = pl.multiple_of((i + u) * cw, cw)
            keys_ref[:, pl.ds(off, cw)] = jnp.full((tq, cw), KEY_OF_NEG_INF, I32)
    widths = [w * unit * cw for w in range(1, n_chunks // unit + 1)]

    qpos = _row_iota((tq, 1)) + i * tq

    def select_branch(width):
        def br():
            colw = _lane_iota((tq, width))
            causal = colw <= (_row_iota((tq, width)) + i * tq)
            sel = _topk_select(keys_ref, width, topk, qpos >= topk, colw)
            selm_ref[:, :width] = jnp.where(sel & causal, 0.0, NEG)
        return br

    def causal_only():
        col = _lane_iota((tq, s_len))
        selm_ref[...] = jnp.where(col <= (_row_iota((tq, s_len)) + i * tq), 0.0, NEG)

    branch = jnp.where((i + 1) * tq > topk, 1 + i // unit, 0)
    lax.switch(branch, [causal_only] + [select_branch(w) for w in widths])

    qb8 = stack_heads(qb_ref)
    m_ref[...] = jnp.full(m_ref.shape, NEG, F32)
    l_ref[...] = jnp.zeros(l_ref.shape, F32)
    acc_ref[...] = jnp.zeros(acc_ref.shape, F32)

    def attn_chunks(j, n_sub):
        width = n_sub * cw
        off = pl.multiple_of(j * cw, cw)
        bias = jnp.concatenate(
            [jnp.concatenate([bias_ref[h, jnp.minimum(i - j - u, 2)] for h in range(nh)], axis=0)
             for u in range(n_sub)], axis=1)
        selm = selm_ref[:, pl.ds(off, width)]
        s = _dot_t(qb8, kk_ref[pl.ds(off, width), :]) + bias + jnp.concatenate([selm] * nh, axis=0)
        alpha, p = _online_update(s, m_ref, l_ref)
        acc_ref[...] = alpha * acc_ref[...] + _dot(_mx(p), vv_ref[pl.ds(off, width), :])

    def wide_body(jj, carry):
        attn_chunks(jj * FLASH_WIDE, FLASH_WIDE)
        return carry

    n_causal = i + 1
    lax.fori_loop(0, n_causal // FLASH_WIDE, wide_body, 0)
    for u in range(1, FLASH_WIDE):
        @pl.when(n_causal % FLASH_WIDE >= u)
        def _():
            attn_chunks(n_causal - n_causal % FLASH_WIDE + (u - 1), 1)
    o = acc_ref[...] / l_ref[...]
    for hp in range(nh // 2):
        o_ref[:, hp * LANES:(hp + 1) * LANES] = _mx(
            jnp.where(halves[0], o[2 * hp * tq:(2 * hp + 1) * tq], o[(2 * hp + 1) * tq:(2 * hp + 2) * tq]))


def _attn_b_prompt(qb, qi, small, kk, vv, kiki, bias_tiles, nb, s, topk, tq=256):
    nq = s // tq
    kern = functools.partial(_attn_b_kernel, tq=tq, s_len=s, topk=topk)
    qrow = lambda b, i: (b * nq + i, 0)
    kv = lambda b, i: (b, 0)
    rows = B_HEADS * tq
    return pl.pallas_call(
        kern,
        grid=(nb, nq),
        in_specs=[pl.BlockSpec((tq, B_HEADS * HEAD_DIM), qrow),
                  pl.BlockSpec((tq, IDX_HEADS * IDX_DIM), qrow),
                  pl.BlockSpec((tq, LANES), lambda b, i: (b * nq + i, 1)),
                  pl.BlockSpec((s, LANES), kv), pl.BlockSpec((s, LANES), kv), pl.BlockSpec((s, LANES), kv),
                  pl.BlockSpec(bias_tiles.shape, lambda b, i: (0, 0, 0, 0))],
        out_specs=pl.BlockSpec((tq, B_HEADS * HEAD_DIM), qrow),
        out_shape=jax.ShapeDtypeStruct((nb * s, B_HEADS * HEAD_DIM), MXU_DT),
        scratch_shapes=[pltpu.VMEM((tq, s), I32), pltpu.VMEM((tq, s), F32),
                        pltpu.VMEM((rows, LANES), F32), pltpu.VMEM((rows, LANES), F32),
                        pltpu.VMEM((rows, LANES), F32)],
        compiler_params=_cparams(("parallel", "arbitrary")),
        name="attn_b_prompt",
    )(qb, qi, small, kk, vv, kiki, bias_tiles)


def _cumsum_rows_kernel(x_ref, tri_ref, o_ref, carry_ref):
    j = pl.program_id(1)

    @pl.when(j == 0)
    def _():
        carry_ref[...] = jnp.zeros(carry_ref.shape, F32)

    tri = tri_ref[...]
    hi, mid, lo = _split3(x_ref[...])
    cum = _dot(tri, hi) + _dot(tri, mid) + _dot(tri, lo) + carry_ref[...]
    o_ref[...] = cum
    carry_ref[...] = cum[-1:, :]


def _cumsum_prompt(x, nb, s, blk=128):
    nj = s // blk
    w = x.shape[1]
    r = np.arange(blk)
    tri = jnp.asarray(r[:, None] >= r[None, :], dtype=MXU_DT)
    return pl.pallas_call(
        _cumsum_rows_kernel,
        grid=(nb, nj),
        in_specs=[pl.BlockSpec((blk, w), lambda b, j: (b * nj + j, 0)),
                  pl.BlockSpec((blk, blk), lambda b, j: (0, 0))],
        out_specs=pl.BlockSpec((blk, w), lambda b, j: (b * nj + j, 0)),
        out_shape=jax.ShapeDtypeStruct((nb * s, w), F32),
        scratch_shapes=[pltpu.VMEM((1, w), F32)],
        compiler_params=_cparams(("parallel", "arbitrary")),
        name="cumsum_prompt",
    )(x, tri)


def _online_update(s, m_ref, l_ref):
    m_old = m_ref[...]
    m_new = jnp.maximum(m_old, jnp.max(s, axis=-1, keepdims=True))
    alpha = jnp.exp(m_old - m_new)
    p = jnp.exp(s - _lanes(m_new, s.shape[1]))
    l_ref[...] = alpha * l_ref[...] + jnp.sum(p, axis=-1, keepdims=True)
    m_ref[...] = m_new
    return alpha, p


def _decode0_kernel(*refs, n_steps, npp, ps, topk, lam_init, group):
    (pt_ref, lamv_ref, gsub_ref, qa_ref, qi_ref, qb_ref, w8_ref, kna_ref, vna_ref, bnew_ref,
     taba_ref, tabb_ref, validb_ref) = refs[:13]
    pages = refs[13:13 + 5 * npp]
    cak, cav, cbk, cbv, cbi = (pages[0:npp], pages[npp:2 * npp], pages[2 * npp:3 * npp],
                               pages[3 * npp:4 * npp], pages[4 * npp:5 * npp])
    oa_ref, ob_ref, m_ref, l_ref, acc_ref, kb_ref, vb_ref, ki_ref, keys_ref = refs[13 + 5 * npp:]
    g = pl.program_id(0) % group
    step = pl.program_id(1)
    past = n_steps * npp * ps
    lk = kb_ref.shape[2]

    @pl.when(step == 0)
    def _():
        m_ref[...] = jnp.full(m_ref.shape, NEG, F32)
        l_ref[...] = jnp.zeros(l_ref.shape, F32)
        acc_ref[...] = jnp.zeros(acc_ref.shape, F32)

    qa = qa_ref[...]
    far = taba_ref[0]
    tail = taba_ref[jnp.where(step == n_steps - 1, 1, 0)]
    s = jnp.concatenate([_dot(qa, _mx(cak[j][...])) + (tail if j == npp - 1 else far) for j in range(npp)], axis=1)
    alpha, p = _online_update(s, m_ref, l_ref)
    p = _mx(p)
    for h in range(A_HEADS):
        rows = slice(h * 2 * T8, (h + 1) * 2 * T8)
        upd = alpha[rows] * acc_ref[rows]
        for j in range(npp):
            vh = _mx(cav[j][pl.ds(h, ps, stride=A_HEADS), :])
            upd = upd + _dot(p[rows, j * ps:(j + 1) * ps], vh)
        acc_ref[rows] = upd

    for j in range(npp):
        off = pl.multiple_of((step * npp + j) * ps, ps)
        kb_ref[g, :, pl.ds(off, ps)] = _mx(cbk[j][...])
        vb_ref[g, :, pl.ds(off, ps)] = _mx(cbv[j][...])
        ki_ref[g, :, pl.ds(off, ps)] = _mx(cbi[j][...])

    @pl.when(step == n_steps - 1)
    def _():
        s_new = _dot_t(qa, _mx(kna_ref[...])) + taba_ref[2][:, :kna_ref.shape[0]]
        alpha2, p2 = _online_update(s_new, m_ref, l_ref)
        vn = _mx(vna_ref[...])
        lam = _lambda(lamv_ref, lam_init)
        for h in range(A_HEADS):
            rows = slice(h * 2 * T8, (h + 1) * 2 * T8)
            o16 = (alpha2[rows] * acc_ref[rows] + _dot(_mx(p2[rows]), vn[:, h * LANES:(h + 1) * LANES])) / l_ref[rows]
            o = o16[:T8] - lam * o16[T8:]
            oa_ref[h] = _rms_rows(o, gsub_ref[...]) * (1.0 - lam_init)

        bn = bnew_ref[g]
        kb_ref[g, :, past:past + LANES] = _mx(bn[0])
        vb_ref[g, :, past:past + LANES] = _mx(bn[1])
        ki_ref[g, :, past:past + LANES] = _mx(bn[2])

    @pl.when((step == n_steps - 1) & (g == group - 1))
    def _():
        valid = validb_ref[...] == 0.0
        for gg in range(group):
            dots = jnp.maximum(_dot(qi_ref[gg], ki_ref[gg]), 0.0)
            w8 = w8_ref[gg] * IDX_HEAD_SCALE
            score = jnp.zeros((T8, lk), F32)
            for h in range(IDX_HEADS):
                score = score + w8[:, h:h + 1] * dots[h * T8:(h + 1) * T8]
            keys_ref[gg * T8:(gg + 1) * T8, :] = _score_keys(jnp.where(valid, score, -jnp.inf))
        rows = group * T8
        sel = _topk_select(keys_ref, lk, topk, jnp.full((rows, 1), True), _lane_iota((rows, lk)))
        selm = jnp.where(sel & jnp.concatenate([valid] * group, axis=0), 0.0, NEG)
        for gg in range(group):
            sg = selm[gg * T8:(gg + 1) * T8]
            sb = _dot(qb_ref[gg], kb_ref[gg]) + tabb_ref[...] + jnp.concatenate([sg] * B_HEADS, axis=0)
            mb = jnp.max(sb, axis=-1, keepdims=True)
            pb = jnp.exp(sb - mb)
            lb = jnp.sum(pb, axis=-1, keepdims=True)
            ob = _dot_t(_mx(pb), vb_ref[gg]) / lb
            for h in range(B_HEADS):
                ob_ref[gg, h] = ob[h * T8:(h + 1) * T8]


def _decode0(page_table, lamv, gsub, qa_bd, qi64, qb64, w8, kna, vna, bnew_t,
             cak_t, cav_r, cbk_t, cbv_t, cbi_t, taba, tabb, validb, layer, topk, lam_init):
    db, npg = page_table.shape
    ps = cak_t.shape[3]
    npp = PAGES_PER_STEP if npg % PAGES_PER_STEP == 0 else 1
    n_steps = npg // npp
    lk = tabb.shape[1]
    group = max(gsz for gsz in (8, 4, 2, 1) if db % gsz == 0)
    per_b = lambda b, p, pt: (b, 0, 0)
    per_g = lambda b, p, pt: (b // group, 0, 0)
    per_g4 = lambda b, p, pt: (b // group, 0, 0, 0)
    fixed2 = lambda b, p, pt: (0, 0)
    fixed3 = lambda b, p, pt: (0, 0, 0)

    def page_specs(arr):
        blk = (None, None) + arr.shape[2:]
        return [pl.BlockSpec(blk, lambda b, p, pt, j=j: (layer, pt[b * npg + p * npp + j], 0, 0)) for j in range(npp)]

    caches = (cak_t, cav_r, cbk_t, cbv_t, cbi_t)
    kern = functools.partial(_decode0_kernel, n_steps=n_steps, npp=npp, ps=ps, topk=topk, lam_init=lam_init,
                             group=group)
    grid_spec = pltpu.PrefetchScalarGridSpec(
        num_scalar_prefetch=1,
        grid=(db, n_steps),
        in_specs=[pl.BlockSpec((4, HEAD_DIM), fixed2), pl.BlockSpec((1, LANES), fixed2),
                  pl.BlockSpec((None,) + qa_bd.shape[1:], per_b),
                  pl.BlockSpec((group,) + qi64.shape[1:], per_g),
                  pl.BlockSpec((group,) + qb64.shape[1:], per_g),
                  pl.BlockSpec((group,) + w8.shape[1:], per_g),
                  pl.BlockSpec((None,) + kna.shape[1:], per_b),
                  pl.BlockSpec((None,) + vna.shape[1:], per_b),
                  pl.BlockSpec((group,) + bnew_t.shape[1:], per_g4),
                  pl.BlockSpec(taba.shape, fixed3), pl.BlockSpec(tabb.shape, fixed2),
                  pl.BlockSpec(validb.shape, fixed2)]
                 + [sp for c in caches for sp in page_specs(c)],
        out_specs=[pl.BlockSpec((None, A_HEADS, T8, LANES), lambda b, p, pt: (b, 0, 0, 0)),
                   pl.BlockSpec((group, B_HEADS, T8, HEAD_DIM), per_g4)],
        scratch_shapes=[pltpu.VMEM((64, LANES), F32), pltpu.VMEM((64, LANES), F32),
                        pltpu.VMEM((64, LANES), F32),
                        pltpu.VMEM((group, HEAD_DIM, lk), MXU_DT), pltpu.VMEM((group, HEAD_DIM, lk), MXU_DT),
                        pltpu.VMEM((group, IDX_DIM, lk), MXU_DT), pltpu.VMEM((group * T8, lk), I32)],
    )
    return pl.pallas_call(
        kern,
        grid_spec=grid_spec,
        out_shape=[jax.ShapeDtypeStruct((db, A_HEADS, T8, LANES), F32),
                   jax.ShapeDtypeStruct((db, B_HEADS, T8, HEAD_DIM), F32)],
        compiler_params=_cparams(("arbitrary", "arbitrary")),
        name="decode0",
    )(page_table.reshape(-1), lamv, gsub, qa_bd, qi64, qb64, w8, kna, vna, bnew_t, taba, tabb, validb,
      *[c for c in caches for _ in range(npp)])


def _expand_rows(x):
    hh, ww = x.shape
    return jnp.broadcast_to(x[:, None, :], (hh, T8, ww)).reshape(hh * T8, ww)


def _decode1_kernel(*refs, n_steps, npp, ps):
    pt_ref, q_ref, cq_ref, kn_ref, vn_ref, cnew_ref, maskn_ref, tri_ref = refs[:8]
    ck = refs[8:8 + npp]
    cv = refs[8 + npp:8 + 2 * npp]
    cf = refs[8 + 2 * npp:8 + 3 * npp]
    o_ref, m_ref, l_ref, acc_ref, suf_ref = refs[8 + 3 * npp:]
    step = pl.program_id(1)

    @pl.when(step == 0)
    def _():
        m_ref[...] = jnp.full(m_ref.shape, NEG, F32)
        l_ref[...] = jnp.zeros(l_ref.shape, F32)
        acc_ref[...] = jnp.zeros(acc_ref.shape, F32)
        suf_ref[...] = jnp.zeros(suf_ref.shape, F32)

    q = q_ref[...]
    cq = jnp.broadcast_to(cq_ref[...], (q.shape[0], LANES))
    tri = tri_ref[...]
    running = suf_ref[...]
    sufs = [None] * npp
    for j in reversed(range(npp)):
        hi, mid, lo = _split3(cf[j][...])
        cum = _dot(hi, tri) + _dot(mid, tri) + _dot(lo, tri)
        tot = jnp.broadcast_to(cum[:, ps - 1:ps], cum.shape)
        sufs[j] = running + tot - cum
        running = running + tot
    suf_ref[...] = running
    s = jnp.concatenate([_dot(q, _mx(ck[j][...])) for j in range(npp)], axis=1)
    s = s + _lanes(cq, npp * ps) + _expand_rows(jnp.concatenate(sufs, axis=1))
    alpha, p = _online_update(s, m_ref, l_ref)
    p = _mx(p)
    upd = _lanes(alpha, acc_ref.shape[1]) * acc_ref[...]
    for j in range(npp):
        upd = upd + _dot_t(p[:, j * ps:(j + 1) * ps], _mx(cv[j][...]))
    acc_ref[...] = upd

    @pl.when(step == n_steps - 1)
    def _():
        nn = kn_ref.shape[0]
        s_new = (_dot_t(q, _mx(kn_ref[...])) + cq[:, :nn] - _expand_rows(cnew_ref[...]) + maskn_ref[...])
        alpha2, p2 = _online_update(s_new, m_ref, l_ref)
        o_full = ((_lanes(alpha2, acc_ref.shape[1]) * acc_ref[...] + _dot(_mx(p2), _mx(vn_ref[...])))
                  / _lanes(l_ref[...], acc_ref.shape[1]))
        for h in range(C_HEADS):
            o_ref[h] = o_full[h * T8:(h + 1) * T8, h * HEAD_DIM:(h + 1) * HEAD_DIM]


def _decode1(page_table, q_bd, cq, kn, vn, cnew_t, cck_t, ccv_t, ccf_t, maskn, layer):
    db, npg = page_table.shape
    ps = cck_t.shape[3]
    assert ps == LANES, ps
    npp = PAGES_PER_STEP if npg % PAGES_PER_STEP == 0 else 1
    n_steps = npg // npp
    per_b = lambda b, p, pt: (b, 0, 0)
    fixed = lambda b, p, pt: (0, 0)
    r = np.arange(ps)
    tri = jnp.asarray(r[:, None] <= r[None, :], dtype=MXU_DT)

    def page_specs(arr):
        blk = (None, None) + arr.shape[2:]
        return [pl.BlockSpec(blk, lambda b, p, pt, j=j: (layer, pt[b * npg + (n_steps - 1 - p) * npp + j], 0, 0))
                for j in range(npp)]

    kern = functools.partial(_decode1_kernel, n_steps=n_steps, npp=npp, ps=ps)
    rows = C_HEADS * T8
    grid_spec = pltpu.PrefetchScalarGridSpec(
        num_scalar_prefetch=1,
        grid=(db, n_steps),
        in_specs=[pl.BlockSpec((None,) + q_bd.shape[1:], per_b),
                  pl.BlockSpec((None,) + cq.shape[1:], per_b),
                  pl.BlockSpec((None,) + kn.shape[1:], per_b),
                  pl.BlockSpec((None,) + vn.shape[1:], per_b),
                  pl.BlockSpec((None,) + cnew_t.shape[1:], per_b),
                  pl.BlockSpec(maskn.shape, fixed), pl.BlockSpec(tri.shape, fixed)]
                 + page_specs(cck_t) + page_specs(ccv_t) + page_specs(ccf_t),
        out_specs=pl.BlockSpec((None, C_HEADS, T8, HEAD_DIM), lambda b, p, pt: (b, 0, 0, 0)),
        scratch_shapes=[pltpu.VMEM((rows, LANES), F32), pltpu.VMEM((rows, LANES), F32),
                        pltpu.VMEM((rows, C_HEADS * HEAD_DIM), F32), pltpu.VMEM((C_HEADS, LANES), F32)],
    )
    return pl.pallas_call(
        kern,
        grid_spec=grid_spec,
        out_shape=jax.ShapeDtypeStruct((db, C_HEADS, T8, HEAD_DIM), F32),
        compiler_params=_cparams(("parallel", "arbitrary")),
        name="decode1",
    )(page_table.reshape(-1), q_bd, cq, kn, vn, cnew_t, maskn, tri,
      *([cck_t] * npp), *([ccv_t] * npp), *([ccf_t] * npp))


def _router_kernel(x_ref, g_ref, whi_ref, wlo_ref, b_ref, t_ref, route_ref):
    t = _rms_rows(x_ref[...], g_ref[...])
    thi, tlo = _split2(t)
    t_ref[...] = thi
    whi = whi_ref[...]
    logits = _dot(thi, whi) + _dot(tlo, whi) + _dot(thi, wlo_ref[...]) + b_ref[...]
    lane = _lane_iota(logits.shape)
    big = jnp.int32(1 << 20)

    def first_max(v):
        mx = jnp.max(v, axis=-1, keepdims=True)
        idx = jnp.min(jnp.where(v == mx, lane, big), axis=-1, keepdims=True)
        return mx, idx

    glog = jnp.where(lane < N_GROUPS, logits, -jnp.inf)
    gmax, gidx = first_max(glog)
    grp_w = 1.0 / jnp.sum(jnp.exp(glog - gmax), axis=-1, keepdims=True)
    el = lane - N_GROUPS
    in_grp = (el >= 0) & (el < N_EXPERTS) & (jnp.right_shift(el, 3) == gidx)
    v1 = jnp.where(in_grp, logits, -jnp.inf)
    top1, i1 = first_max(v1)
    v2 = jnp.where(lane == i1, -jnp.inf, v1)
    top2, i2 = first_max(v2)
    e2 = jnp.exp(top2 - top1)
    w1 = grp_w / (1.0 + e2)
    w2 = grp_w * e2 / (1.0 + e2)
    route = jnp.where(lane == 0, (i1 - N_GROUPS).astype(F32),
                      jnp.where(lane == 1, (i2 - N_GROUPS).astype(F32),
                                jnp.where(lane == 2, w1, jnp.where(lane == 3, w2, 0.0))))
    route_ref[...] = route


def _router(x, g, w_group, b_group, w_router, b_router, tm=512):
    n = x.shape[0]
    tm = _tile(n, tm)
    pad = LANES - N_GROUPS - N_EXPERTS
    w = jnp.concatenate([w_group, w_router, jnp.zeros((D_MODEL, pad), F32)], axis=1)
    whi = _mx(w)
    wlo = _mx(w - whi.astype(F32))
    b = jnp.concatenate([b_group, b_router, jnp.zeros((pad,), F32)])[None, :]
    row = lambda i: (i, 0)
    fixed = lambda i: (0, 0)
    return pl.pallas_call(
        _router_kernel,
        grid=(n // tm,),
        in_specs=[pl.BlockSpec((tm, D_MODEL), row), pl.BlockSpec((1, D_MODEL), fixed),
                  pl.BlockSpec((D_MODEL, LANES), fixed), pl.BlockSpec((D_MODEL, LANES), fixed),
                  pl.BlockSpec((1, LANES), fixed)],
        out_specs=[pl.BlockSpec((tm, D_MODEL), row), pl.BlockSpec((tm, LANES), row)],
        out_shape=[jax.ShapeDtypeStruct((n, D_MODEL), MXU_DT), jax.ShapeDtypeStruct((n, LANES), F32)],
        compiler_params=_cparams(("parallel",)),
        name="moe_router",
    )(x, g[None, :], whi, wlo, b)


def _moe_dense_kernel(x_ref, t_ref, route_ref, wgu_ref, wd_ref, o_ref):
    e = pl.program_id(1)

    @pl.when(e == 0)
    def _():
        o_ref[...] = x_ref[...]

    t = t_ref[...]
    r = route_ref[...]
    upd = None
    for k in range(EXPERTS_PER_STEP):
        au = _dot(t, wgu_ref[k])
        a = au[:, :EXPERT_FF]
        u = au[:, EXPERT_FF:]
        ef = (e * EXPERTS_PER_STEP + k).astype(F32)
        gate = jnp.where(r[:, 0:1] == ef, r[:, 2:3], 0.0) + jnp.where(r[:, 1:2] == ef, r[:, 3:4], 0.0)
        hdn = a * (1.0 / (1.0 + jnp.exp(-a))) * u * gate
        y = _dot(_mx(hdn), wd_ref[k])
        upd = y if upd is None else upd + y
    o_ref[...] += upd


def _moe_dense(x, t, route, w_gate, w_up, w_down, tm=1536):
    n = x.shape[0]
    tm = _tile(n, tm)
    wgu = _mx(jnp.concatenate([w_gate.reshape(N_EXPERTS, D_MODEL, EXPERT_FF),
                               w_up.reshape(N_EXPERTS, D_MODEL, EXPERT_FF)], axis=-1))
    wd = _mx(w_down.reshape(N_EXPERTS, EXPERT_FF, D_MODEL))
    row = lambda i, e: (i, 0)
    once = pl.Buffered(1)
    return pl.pallas_call(
        _moe_dense_kernel,
        grid=(n // tm, N_EXPERTS // EXPERTS_PER_STEP),
        in_specs=[pl.BlockSpec((tm, D_MODEL), row, pipeline_mode=once),
                  pl.BlockSpec((tm, D_MODEL), row, pipeline_mode=once),
                  pl.BlockSpec((tm, LANES), row, pipeline_mode=once),
                  pl.BlockSpec((EXPERTS_PER_STEP, D_MODEL, 2 * EXPERT_FF), lambda i, e: (e, 0, 0)),
                  pl.BlockSpec((EXPERTS_PER_STEP, EXPERT_FF, D_MODEL), lambda i, e: (e, 0, 0))],
        out_specs=pl.BlockSpec((tm, D_MODEL), row),
        out_shape=jax.ShapeDtypeStruct((n, D_MODEL), F32),
        compiler_params=_cparams(("parallel", "arbitrary")),
        name="moe_dense",
    )(x, t, route, wgu, wd)


def _hier_moe(x, g, w_group, b_group, w_router, b_router, w_gate, w_up, w_down):
    t, route = _router(x, g, w_group, b_group, w_router, b_router)
    return _moe_dense(x, t, route, w_gate, w_up, w_down)


def _decode_tables(rel_bias, past, t_new, lk):
    t8 = np.arange(T8)
    lane = np.arange(LANES)
    far = np.full((T8, LANES), REL_BUCKETS - 1, np.int32)
    d_last = (past + t8[:, None]) - (past - LANES + lane[None, :])
    d_new = t8[:, None] - lane[None, :]
    ok_new = (d_new >= 0) & (lane[None, :] < t_new)
    ta = _bias_expand(rel_bias, np.stack([far, _t5_bucket_np(d_last), _t5_bucket_np(d_new)]), 0, A_HEADS)
    ta = jnp.where(jnp.asarray(ok_new)[None, None] | (jnp.arange(3) < 2)[None, :, None, None], ta, NEG)
    taba = jnp.broadcast_to(jnp.transpose(ta, (1, 0, 2, 3))[:, :, None], (3, A_HEADS, 2, T8, LANES))
    taba = taba.reshape(3, A_HEADS * 2 * T8, LANES)
    kpos = np.arange(lk)
    d_b = (past + t8[:, None]) - kpos[None, :]
    tabb = _bias_expand(rel_bias, _t5_bucket_np(d_b), A_HEADS, B_HEADS).reshape(B_HEADS * T8, lk)
    valid = (kpos[None, :] < past) | ((d_b >= 0) & (kpos[None, :] < past + t_new))
    validb = jnp.asarray(np.where(valid, 0.0, NEG), F32)
    return taba, tabb, validb


def _pad_rows(x, rows):
    pad = [(0, 0)] * x.ndim
    pad[1] = (0, rows - x.shape[1])
    return jnp.pad(x, pad)


def kernel(x_prompt, x_sample, cache_a_k, cache_a_v, cache_b_k, cache_b_v, cache_b_kidx, cache_c_k, cache_c_v, cache_c_logf, page_table, rel_bias, ab_norm, ab_w_in, a_q_norm, a_k_norm, b_q_norm, b_k_norm, a_lambda_q1, a_lambda_k1, a_lambda_q2, a_lambda_k2, a_sub_norm, ab_w_out, c_norm, c_w_in, c_forget_bias, c_q_norm, c_k_norm, c_w_out, ffn_norm, moe_w_group, moe_b_group, moe_w_router, moe_b_router, moe_w_gate, moe_w_up, moe_w_down):
    nb, s, d = x_prompt.shape
    db, ts, _ = x_sample.shape
    npg = page_table.shape[1]
    pool, ps = cache_a_k.shape[1], cache_a_k.shape[2]
    past = npg * ps
    n_p = nb * s
    n_s = db * ts
    depth = ffn_norm.shape[0]
    topk_p = min(IDX_TOPK_MAX, s // 4)
    topk_s = min(IDX_TOPK_MAX, (past + ts) // 4)
    lk = past + LANES

    cak_t = jnp.transpose(cache_a_k, (0, 1, 3, 4, 5, 2)).reshape(-1, pool, 2 * A_HEADS * HEAD_DIM, ps)
    cav_r = cache_a_v.reshape(-1, pool, ps * A_HEADS, 2 * HEAD_DIM)
    cbk_t = jnp.transpose(cache_b_k, (0, 1, 3, 2))
    cbv_t = jnp.transpose(cache_b_v, (0, 1, 3, 2))
    cbi_t = jnp.transpose(cache_b_kidx, (0, 1, 3, 2))
    cck_t = jnp.transpose(cache_c_k, (0, 1, 3, 4, 2)).reshape(-1, pool, C_HEADS * HEAD_DIM, ps)
    ccv_t = jnp.transpose(cache_c_v, (0, 1, 3, 4, 2)).reshape(-1, pool, C_HEADS * HEAD_DIM, ps)
    ccf_t = jnp.transpose(cache_c_logf, (0, 1, 3, 2))

    x = jnp.concatenate([x_prompt.reshape(n_p, d), x_sample.reshape(n_s, d)], axis=0)
    outs_p = {k: [] for k in ("ak", "av", "bk", "bv", "bi", "ck", "cv", "cf")}
    outs_s = {k: [] for k in ("ak", "av", "bk", "bv", "bi", "ck", "cv", "cf")}

    def smp(a):
        return a[n_p:].reshape(db, ts, a.shape[1])

    for layer in range(depth):
        if layer % 2 == 0:
            e = layer // 2
            lam_init = 0.8 - 0.6 * math.exp(-0.3 * layer)
            lamv = jnp.stack([a_lambda_q1[e], a_lambda_k1[e], a_lambda_q2[e], a_lambda_k2[e]])
            gsub = a_sub_norm[e][None, :]
            (qa, ka_s, kat, katbf, qb, small, smallt, kk, vv, kiki, va, vabf, qi) = _proj0(
                x, ab_norm[e], ab_w_in[e], a_q_norm[e], a_k_norm[e], b_q_norm[e], b_k_norm[e], n_p, s)
            tq_a = 256
            bias_a = _bias_expand(rel_bias, _prompt_bucket_tiles(tq_a), 0, A_HEADS)
            oa_p = _attn_a_prompt(qa, katbf, vabf, lamv, gsub, bias_a, nb, s, lam_init, tq=tq_a)
            tq_b = 256
            bias_b = _bias_expand(rel_bias, _prompt_bucket_tiles(tq_b), A_HEADS, B_HEADS)
            ob_p = _attn_b_prompt(qb, qi, small, kk, vv, kiki, bias_b, nb, s, topk_p, tq=tq_b)
            qa_s = _pad_rows(smp(qa), T8)
            hc = np.arange(2 * A_HEADS)
            colmask = jnp.asarray((np.arange(qa_s.shape[2])[None, :] // HEAD_DIM) == hc[:, None], MXU_DT)
            qa_bd = (qa_s[:, None, :, :] * colmask[None, :, None, :]).reshape(db, 2 * A_HEADS * T8, -1)

            def heads_rows(a, nh):
                a = _pad_rows(a, T8).reshape(db, T8, nh, HEAD_DIM)
                return jnp.transpose(a, (0, 2, 1, 3)).reshape(db, nh * T8, HEAD_DIM)

            qi64 = heads_rows(smp(qi), IDX_HEADS)
            qb64 = heads_rows(smp(qb), B_HEADS)
            small_s = smp(small)
            w8 = _pad_rows(small_s[:, :, 192:200], T8)
            ka_s3 = ka_s.reshape(db, ts, -1)
            va_s3 = smp(va)
            kna = _pad_rows(ka_s3, 16)
            vna = _pad_rows(va_s3, 16)
            bnew_t = jnp.transpose(_pad_rows(small_s[:, :, :192], LANES).reshape(db, LANES, 3, HEAD_DIM), (0, 2, 3, 1))
            taba, tabb, validb = _decode_tables(rel_bias, past, ts, lk)
            oa_d, ob_d = _decode0(page_table, lamv, gsub, qa_bd, qi64, qb64, w8, kna, vna, bnew_t,
                                  cak_t, cav_r, cbk_t, cbv_t, cbi_t, taba, tabb, validb, e, topk_s, lam_init)
            oa_s = jnp.transpose(oa_d[:, :, :ts], (0, 2, 1, 3)).reshape(n_s, A_HEADS * LANES)
            ob_s = jnp.transpose(ob_d[:, :, :ts], (0, 2, 1, 3)).reshape(n_s, B_HEADS * HEAD_DIM)
            w_out = ab_w_out[e]
            x = _outproj(x, [(oa_p, _mx(oa_s), w_out[:A_HEADS * LANES]),
                             (ob_p, _mx(ob_s), w_out[A_HEADS * LANES:])], n_p)
            outs_p["ak"].append(jnp.transpose(kat.reshape(nb, A_HEADS, 2, HEAD_DIM, s), (0, 4, 1, 2, 3)))
            outs_p["av"].append(va[:n_p].reshape(nb, s, A_HEADS, 2 * HEAD_DIM))
            outs_p["bk"].append(jnp.transpose(smallt[:, 0:64], (0, 2, 1)))
            outs_p["bv"].append(jnp.transpose(smallt[:, 64:128], (0, 2, 1)))
            outs_p["bi"].append(jnp.transpose(smallt[:, 128:192], (0, 2, 1)))
            outs_s["ak"].append(ka_s3.reshape(db, ts, A_HEADS, 2, HEAD_DIM))
            outs_s["av"].append(va_s3.reshape(db, ts, A_HEADS, 2 * HEAD_DIM))
            outs_s["bk"].append(small_s[:, :, 0:64])
            outs_s["bv"].append(small_s[:, :, 64:128])
            outs_s["bi"].append(small_s[:, :, 128:192])
        else:
            o = layer // 2
            q, k_s, kt, ktbf, v_s, vt, vbf, logf128, logft = _proj1(
                x, c_norm[o], c_w_in[o], c_forget_bias[o], c_q_norm[o], c_k_norm[o], n_p, s)
            logf = logf128[:, :C_HEADS]
            cum_p = _cumsum_prompt(logf128[:n_p], nb, s)[:, :C_HEADS]
            ckt = jnp.transpose(cum_p.reshape(nb, s, C_HEADS // 2, 2), (0, 2, 3, 1))
            oc_p = _attn_c_prompt(q, ktbf, vbf, cum_p, ckt, nb, s)
            logf_s = smp(logf)
            run = jnp.zeros_like(logf_s[:, 0])
            c_rows = []
            for t in range(ts):
                run = run + logf_s[:, t]
                c_rows.append(run)
            c_new = jnp.stack(c_rows, axis=1)
            cq = jnp.transpose(_pad_rows(c_new, T8), (0, 2, 1)).reshape(db, C_HEADS * T8, 1)
            cnew_t = jnp.transpose(_pad_rows(c_new, 16), (0, 2, 1))
            q_s = _pad_rows(smp(q), T8)
            hmask = jnp.asarray((np.arange(q_s.shape[2])[None, :] // HEAD_DIM) == np.arange(C_HEADS)[:, None], MXU_DT)
            q_bd = (q_s[:, None, :, :] * hmask[None, :, None, :]).reshape(db, C_HEADS * T8, -1)
            k_s3 = k_s.reshape(db, ts, -1)
            v_s3 = v_s.reshape(db, ts, -1)
            kn = _pad_rows(k_s3, 16)
            vn = _pad_rows(v_s3, 16)
            t8 = np.arange(T8)
            okn = (t8[:, None] >= np.arange(16)[None, :]) & (np.arange(16)[None, :] < ts)
            maskn = jnp.asarray(np.tile(np.where(okn, 0.0, NEG), (C_HEADS, 1)), F32)
            oc_d = _decode1(page_table, q_bd, cq, kn, vn, cnew_t, cck_t, ccv_t, ccf_t, maskn, o)
            oc_s = jnp.transpose(oc_d[:, :, :ts], (0, 2, 1, 3)).reshape(n_s, C_HEADS * HEAD_DIM)
            x = _outproj(x, [(oc_p, _mx(oc_s), c_w_out[o])], n_p)
            outs_p["ck"].append(jnp.transpose(kt.reshape(nb, C_HEADS, HEAD_DIM, s), (0, 3, 1, 2)))
            outs_p["cv"].append(jnp.transpose(vt.reshape(nb, C_HEADS, HEAD_DIM, s), (0, 3, 1, 2)))
            outs_p["cf"].append(jnp.transpose(logft[:, :C_HEADS], (0, 2, 1)))
            outs_s["ck"].append(k_s3.reshape(db, ts, C_HEADS, HEAD_DIM))
            outs_s["cv"].append(v_s3.reshape(db, ts, C_HEADS, HEAD_DIM))
            outs_s["cf"].append(logf_s)
        x = _hier_moe(x, ffn_norm[layer], moe_w_group[layer], moe_b_group[layer], moe_w_router[layer],
                      moe_b_router[layer], moe_w_gate[layer], moe_w_up[layer], moe_w_down[layer])

    keys = ("ak", "av", "bk", "bv", "bi", "ck", "cv", "cf")
    return ((x[:n_p].reshape(nb, s, d), x[n_p:].reshape(db, ts, d))
            + tuple(jnp.stack(outs_p[k]) for k in keys)
            + tuple(jnp.stack(outs_s[k]) for k in keys))
```

```python
import functools
import math

import numpy as np
import jax
import jax.numpy as jnp
from jax import lax
from jax.experimental import pallas as pl
from jax.experimental.pallas import tpu as pltpu

F32 = jnp.float32
I32 = jnp.int32
MXU_DT = jnp.bfloat16

D_MODEL = 1024
HEAD_DIM = 64
A_HEADS = 4
B_HEADS = 8
IDX_HEADS = 8
IDX_DIM = 64
IDX_TOPK_MAX = 256
C_HEADS = 16
REL_BUCKETS = 32
REL_MAX_DIST = 128
N_GROUPS = 4
EXPERTS_PER_GROUP = 8
N_EXPERTS = N_GROUPS * EXPERTS_PER_GROUP
EXPERT_FF = 256
EPS = 1e-6
NEG = -1e30
INT_MIN = -2 ** 31
QK_SCALE = HEAD_DIM ** -0.5
IDX_SCALE = IDX_DIM ** -0.5
IDX_HEAD_SCALE = IDX_HEADS ** -0.5
LANES = 128
T8 = 8
VMEM_LIMIT = 56 * 1024 * 1024
PAGES_PER_STEP = 16
FLASH_TQ = 512
FLASH_WIDE = 2
EXPERTS_PER_STEP = 4

def _tile(n, pref):
    best = 16
    for t in range(16, pref + 1, 16):
        if n % t == 0:
            best = t
    assert n % best == 0, (n, pref)
    return best


def _cparams(sem):
    return pltpu.CompilerParams(dimension_semantics=sem, vmem_limit_bytes=VMEM_LIMIT)


def _mx(x):
    return x.astype(MXU_DT)


def _dot(a, b):
    return jnp.dot(a, b, preferred_element_type=F32)


def _dot_t(a, b):
    return lax.dot_general(a, b, (((1,), (1,)), ((), ())), preferred_element_type=F32)


def _split2(x):
    hi = _mx(x)
    lo = _mx(x - hi.astype(F32))
    return hi, lo


def _split3(x):
    hi = _mx(x)
    r = x - hi.astype(F32)
    mid = _mx(r)
    lo = _mx(r - mid.astype(F32))
    return hi, mid, lo


def _lane_iota(shape):
    return lax.broadcasted_iota(I32, shape, len(shape) - 1)


def _row_iota(shape):
    return lax.broadcasted_iota(I32, shape, len(shape) - 2)


def _lanes(x, width):
    if width <= LANES:
        return x[:, :width]
    return jnp.tile(x, (1, width // LANES))


def _rms_rows(x, g):
    ms = jnp.mean(x * x, axis=-1, keepdims=True)
    return x * lax.rsqrt(ms + EPS) * g


def _seg_rsqrt(y, bd):
    hi, lo = _split2(y * y)
    ss = _dot(hi, bd) + _dot(lo, bd)
    return lax.rsqrt(ss * (1.0 / HEAD_DIM) + EPS)


def _block_diag_ones(n, seg):
    r = np.arange(n)
    return jnp.asarray((r[:, None] // seg) == (r[None, :] // seg), dtype=MXU_DT)


def _lambda(lamv_ref, lam_init):
    lv = lamv_ref[...]
    return (jnp.exp(jnp.sum(lv[0:1] * lv[1:2], axis=-1, keepdims=True))
            - jnp.exp(jnp.sum(lv[2:3] * lv[3:4], axis=-1, keepdims=True)) + lam_init)


def _t5_bucket_np(d):
    n = np.maximum(d, 0)
    exact = REL_BUCKETS // 2
    nf = np.maximum(n, 1).astype(np.float64)
    large = exact + (np.log(nf / exact) / math.log(REL_MAX_DIST / exact) * (REL_BUCKETS - exact)).astype(np.int64)
    large = np.minimum(large, REL_BUCKETS - 1)
    return np.where(n < exact, n, large).astype(np.int32)


def _bias_expand_kernel(relb_ref, idx_ref, o_ref, *, head0):
    h = pl.program_id(0) + head0
    idx = idx_ref[...]
    acc = jnp.zeros(idx.shape, F32)
    for b in range(REL_BUCKETS):
        acc = jnp.where(idx == b, relb_ref[b, h], acc)
    o_ref[...] = acc


def _bias_expand(rel_bias, idx_np, head0, n_heads):
    idx = jnp.asarray(idx_np, I32)
    nd = idx.ndim
    zeros = (0,) * nd
    return pl.pallas_call(
        functools.partial(_bias_expand_kernel, head0=head0),
        grid=(n_heads,),
        in_specs=[pl.BlockSpec(memory_space=pltpu.SMEM),
                  pl.BlockSpec(idx.shape, lambda h: zeros)],
        out_specs=pl.BlockSpec((None,) + idx.shape, lambda h: (h,) + zeros),
        out_shape=jax.ShapeDtypeStruct((n_heads,) + idx.shape, F32),
        compiler_params=_cparams(("arbitrary",)),
        name="bias_expand",
    )(rel_bias, idx)


def _prompt_bucket_tiles(t):
    r = np.arange(t)
    d = r[:, None] - r[None, :]
    return np.stack([_t5_bucket_np(d), _t5_bucket_np(d + t), np.full((t, t), REL_BUCKETS - 1, np.int32)])


def _store_t(y, is_prompt, t_refs, row_ref, c):
    if is_prompt:
        yt = jnp.transpose(y)
        for r in t_refs:
            r[c * 256:(c + 1) * 256, :] = yt.astype(r.dtype)
    elif row_ref is not None:
        row_ref[:, c * 256:(c + 1) * 256] = y


def _per_tile_kind(body, n_prompt_tiles):
    prompt_tile = pl.program_id(0) < n_prompt_tiles

    @pl.when(prompt_tile)
    def _():
        body(True)

    @pl.when(jnp.logical_not(prompt_tile))
    def _():
        body(False)


def _proj0_kernel(*refs, n_prompt_tiles):
    _per_tile_kind(functools.partial(_proj0_body, *refs), n_prompt_tiles)


def _proj0_body(x_ref, g_ref, w_ref, gain_ref, bd_ref,
                qa_ref, ka_ref, kat_ref, katbf_ref, qb_ref, small_ref, smallt_ref, kk_ref, vv_ref, kiki_ref,
                va_ref, vabf_ref, qi_ref, is_prompt):
    h = _mx(_rms_rows(x_ref[...], g_ref[...]))
    bd = bd_ref[...]

    def chunk(c):
        return _dot(h, w_ref[:, c * 256:(c + 1) * 256])

    def normed(c):
        y = chunk(c)
        return y * _seg_rsqrt(y, bd) * gain_ref[:, c * 256:(c + 1) * 256]

    for c in range(2):
        qa_ref[:, c * 256:(c + 1) * 256] = _mx(normed(c) * QK_SCALE)
    for c in range(2):
        _store_t(normed(2 + c), is_prompt, (kat_ref, katbf_ref), ka_ref, c)
    for c in range(2):
        qb_ref[:, c * 256:(c + 1) * 256] = _mx(normed(4 + c) * QK_SCALE)
    y = chunk(6)
    yn = y * _seg_rsqrt(y, bd) * gain_ref[:, 6 * 256:7 * 256]
    lane = _lane_iota(y.shape)
    y = jnp.where(lane < HEAD_DIM, yn, y)
    small_ref[...] = y
    _store_t(y, is_prompt, (smallt_ref,), None, 0)
    t0 = y[:, :LANES]
    t1 = y[:, LANES:]
    lo = _lane_iota(t0.shape) < HEAD_DIM
    r0 = pltpu.roll(t0, HEAD_DIM, 1)
    kk_ref[...] = _mx(jnp.where(lo, t0, r0))
    vv_ref[...] = _mx(jnp.where(lo, r0, t0))
    r1 = pltpu.roll(t1, HEAD_DIM, 1)
    kiki_ref[...] = _mx(jnp.where(lo, t1, r1))
    for c in range(2):
        y = chunk(7 + c)
        va_ref[:, c * 256:(c + 1) * 256] = y
        vabf_ref[:, c * 256:(c + 1) * 256] = _mx(y)
    for c in range(2):
        qi_ref[:, c * 256:(c + 1) * 256] = _mx(chunk(9 + c) * IDX_SCALE)


def _token_specs(n_p, n_s, s, tm):
    npt = n_p // tm
    tps = s // tm

    def t_map(i):
        ip = jnp.minimum(i, npt - 1)
        return (ip // tps, 0, ip % tps)

    def s_map(i):
        return (jnp.maximum(i - npt, 0), 0)

    return npt, t_map, s_map


def _proj0(x, g, w_in, a_qn, a_kn, b_qn, b_kn, n_p, s, tm=512):
    n = x.shape[0]
    n_s = n - n_p
    nb = n_p // s
    tm = _tile(math.gcd(n_p, n_s, s), tm)
    npt, t_map, s_map = _token_specs(n_p, n_s, s, tm)
    sp = np.cumsum([512, 512, 512, 512, 64, 64, 512, 64, 8])[:-1]
    wqa, wka, wva, wqb, wkb, wvb, wqi, wki, wwi = jnp.split(w_in, sp, axis=1)
    w = jnp.concatenate([wqa, wka, wqb, wkb, wvb, wki, wwi, jnp.zeros((D_MODEL, 56), F32), wva, wqi], axis=1)
    w = _mx(w)
    ncol = w.shape[1]
    gain = jnp.concatenate([jnp.tile(a_qn, 8), jnp.tile(a_kn, 8), jnp.tile(b_qn, 8), b_kn,
                            jnp.ones((192,), F32)])[None, :]
    bd = _block_diag_ones(256, HEAD_DIM)
    row = lambda i: (i, 0)
    fixed = lambda i: (0, 0)
    def rows(wd, dt):
        return pl.BlockSpec((tm, wd), row), jax.ShapeDtypeStruct((n, wd), dt)

    def rows_s(wd, dt):
        return pl.BlockSpec((tm, wd), s_map), jax.ShapeDtypeStruct((n_s, wd), dt)

    def cols_p(wd, dt):
        return pl.BlockSpec((None, wd, tm), t_map), jax.ShapeDtypeStruct((nb, wd, s), dt)

    outs = [rows(512, MXU_DT), rows_s(512, F32), cols_p(512, F32), cols_p(512, MXU_DT), rows(512, MXU_DT),
            rows(256, F32), cols_p(256, F32), rows(128, MXU_DT), rows(128, MXU_DT), rows(128, MXU_DT),
            rows(512, F32), rows(512, MXU_DT), rows(512, MXU_DT)]
    return pl.pallas_call(
        functools.partial(_proj0_kernel, n_prompt_tiles=npt),
        grid=(n // tm,),
        in_specs=[pl.BlockSpec((tm, D_MODEL), row), pl.BlockSpec((1, D_MODEL), fixed),
                  pl.BlockSpec((D_MODEL, ncol), fixed), pl.BlockSpec((1, gain.shape[1]), fixed),
                  pl.BlockSpec((256, 256), fixed)],
        out_specs=[o[0] for o in outs],
        out_shape=[o[1] for o in outs],
        compiler_params=_cparams(("arbitrary",)),
        name="proj0",
    )(x, g[None, :], w, gain, bd)


def _proj1_kernel(*refs, n_prompt_tiles):
    _per_tile_kind(functools.partial(_proj1_body, *refs), n_prompt_tiles)


def _proj1_body(x_ref, g_ref, w_ref, gain_ref, bf_ref, bd_ref,
                q_ref, k_ref, kt_ref, ktbf_ref, v_ref, vt_ref, vbf_ref, logf_ref, logft_ref, is_prompt):
    h = _mx(_rms_rows(x_ref[...], g_ref[...]))
    bd = bd_ref[...]

    def chunk(c):
        return _dot(h, w_ref[:, c * 256:(c + 1) * 256])

    def normed(c):
        y = chunk(c)
        return y * _seg_rsqrt(y, bd) * gain_ref[:, c * 256:(c + 1) * 256]

    for c in range(4):
        q_ref[:, c * 256:(c + 1) * 256] = _mx(normed(c) * QK_SCALE)
    for c in range(4):
        _store_t(normed(4 + c), is_prompt, (kt_ref, ktbf_ref), k_ref, c)
    for c in range(4):
        y = chunk(8 + c)
        vbf_ref[:, c * 256:(c + 1) * 256] = _mx(y)
        _store_t(y, is_prompt, (vt_ref,), v_ref, c)
    f = _dot(h, w_ref[:, 12 * 256:12 * 256 + LANES]) + bf_ref[...]
    logf = jnp.minimum(f, 0.0) - jnp.log(1.0 + jnp.exp(-jnp.abs(f)))
    logf_ref[...] = logf
    if is_prompt:
        logft_ref[...] = jnp.transpose(logf)


def _proj1(x, g, w_in, b_f, qn, kn, n_p, s, tm=512):
    n = x.shape[0]
    n_s = n - n_p
    nb = n_p // s
    tm = _tile(math.gcd(n_p, n_s, s), tm)
    npt, t_map, s_map = _token_specs(n_p, n_s, s, tm)
    w = _mx(jnp.concatenate([w_in, jnp.zeros((D_MODEL, LANES - C_HEADS), F32)], axis=1))
    ncol = w.shape[1]
    gain = jnp.concatenate([jnp.tile(qn, C_HEADS), jnp.tile(kn, C_HEADS)])[None, :]
    bf = jnp.concatenate([b_f, jnp.zeros((LANES - C_HEADS,), F32)])[None, :]
    bd = _block_diag_ones(256, HEAD_DIM)
    row = lambda i: (i, 0)
    fixed = lambda i: (0, 0)
    def rows(wd, dt):
        return pl.BlockSpec((tm, wd), row), jax.ShapeDtypeStruct((n, wd), dt)

    def rows_s(wd, dt):
        return pl.BlockSpec((tm, wd), s_map), jax.ShapeDtypeStruct((n_s, wd), dt)

    def cols_p(wd, dt):
        return pl.BlockSpec((None, wd, tm), t_map), jax.ShapeDtypeStruct((nb, wd, s), dt)

    outs = [rows(1024, MXU_DT), rows_s(1024, F32), cols_p(1024, F32), cols_p(1024, MXU_DT),
            rows_s(1024, F32), cols_p(1024, F32), rows(1024, MXU_DT), rows(LANES, F32), cols_p(LANES, F32)]
    return pl.pallas_call(
        functools.partial(_proj1_kernel, n_prompt_tiles=npt),
        grid=(n // tm,),
        in_specs=[pl.BlockSpec((tm, D_MODEL), row), pl.BlockSpec((1, D_MODEL), fixed),
                  pl.BlockSpec((D_MODEL, ncol), fixed), pl.BlockSpec((1, gain.shape[1]), fixed),
                  pl.BlockSpec((1, LANES), fixed), pl.BlockSpec((256, 256), fixed)],
        out_specs=[o[0] for o in outs],
        out_shape=[o[1] for o in outs],
        compiler_params=_cparams(("arbitrary",)),
        name="proj1",
    )(x, g[None, :], w, gain, bf, bd)


def _outproj_kernel(*refs, n_in, n_prompt_tiles):
    res_ref = refs[0]
    out_ref = refs[-1]

    def body(is_prompt):
        acc = res_ref[...]
        for i in range(n_in):
            a_ref = refs[1 + 3 * i] if is_prompt else refs[2 + 3 * i]
            acc = acc + _dot(a_ref[...], refs[3 + 3 * i][...])
        out_ref[...] = acc

    _per_tile_kind(body, n_prompt_tiles)


def _outproj(res, terms, n_p, tm=512):
    n = res.shape[0]
    n_s = n - n_p
    tm = _tile(math.gcd(n_p, n_s), tm)
    npt = n_p // tm
    row = lambda i: (i, 0)
    fixed = lambda i: (0, 0)
    p_map = lambda i: (jnp.minimum(i, npt - 1), 0)
    s_map = lambda i: (jnp.maximum(i - npt, 0), 0)
    in_specs = [pl.BlockSpec((tm, D_MODEL), row)]
    args = [res]
    for a_p, a_s, w in terms:
        in_specs += [pl.BlockSpec((tm, a_p.shape[1]), p_map), pl.BlockSpec((tm, a_s.shape[1]), s_map),
                     pl.BlockSpec(w.shape, fixed)]
        args += [a_p, a_s, _mx(w)]
    return pl.pallas_call(
        functools.partial(_outproj_kernel, n_in=len(terms), n_prompt_tiles=npt),
        grid=(n // tm,),
        in_specs=in_specs,
        out_specs=pl.BlockSpec((tm, D_MODEL), row),
        out_shape=jax.ShapeDtypeStruct((n, D_MODEL), F32),
        compiler_params=_cparams(("parallel",)),
        name="outproj",
    )(*args)


def _stack_streams(q):
    lane = _lane_iota(q.shape)
    zero = jnp.zeros_like(q)
    return jnp.concatenate([jnp.where(lane < HEAD_DIM, q, zero), jnp.where(lane >= HEAD_DIM, q, zero)], axis=0)


def _flash_stacked(q2, kt_ref, v_ref, i, tq, tk, bias_fn, m_ref, l_ref, acc_ref):
    rows = 2 * tq
    m_ref[...] = jnp.full(m_ref.shape, NEG, F32)
    l_ref[...] = jnp.zeros(l_ref.shape, F32)
    acc_ref[...] = jnp.zeros(acc_ref.shape, F32)

    def tile(j, n_sub, masked):
        width = n_sub * tk
        off = pl.multiple_of(j * tk, tk)
        s = _dot(q2, kt_ref[:, pl.ds(off, width)]) + bias_fn(j, off, n_sub)
        if masked:
            r = _row_iota((rows, width))
            s = jnp.where(_lane_iota((rows, width)) <= jnp.where(r >= tq, r - tq, r), s, NEG)
        alpha, p = _online_update(s, m_ref, l_ref)
        acc_ref[...] = alpha * acc_ref[...] + _dot(_mx(p), v_ref[pl.ds(off, width), :])

    def body(jj, carry):
        tile(jj * FLASH_WIDE, FLASH_WIDE, False)
        return carry

    lax.fori_loop(0, i // FLASH_WIDE, body, 0)
    for u in range(1, FLASH_WIDE):
        @pl.when(i % FLASH_WIDE >= u)
        def _():
            tile(i - i % FLASH_WIDE + (u - 1), 1, False)
    tile(i, 1, True)


def _attn_a_kernel(lamv_ref, gsub_ref, q_ref, k_ref, v_ref, bias_ref, o_ref,
                   m_ref, l_ref, acc_ref, *, tq, tk, lam_init):
    i = pl.program_id(2)

    def bias_fn(j, off, n_sub):
        b = jnp.concatenate([bias_ref[jnp.minimum(i - j - u, 2)] for u in range(n_sub)], axis=1)
        return jnp.concatenate([b, b], axis=0)

    _flash_stacked(_stack_streams(q_ref[...]), k_ref, v_ref, i, tq, tk, bias_fn, m_ref, l_ref, acc_ref)
    lam = _lambda(lamv_ref, lam_init)
    o = acc_ref[:tq] / l_ref[:tq] - lam * (acc_ref[tq:] / l_ref[tq:])
    o = _rms_rows(o, gsub_ref[...]) * (1.0 - lam_init)
    o_ref[...] = _mx(o)


def _attn_a_prompt(qa, ka, va, lamv, gsub, bias_tiles, nb, s, lam_init, tq=256):
    tk = tq
    nq = s // tq
    kern = functools.partial(_attn_a_kernel, tq=tq, tk=tk, lam_init=lam_init)
    return pl.pallas_call(
        kern,
        grid=(nb, A_HEADS, nq),
        in_specs=[pl.BlockSpec((4, HEAD_DIM), lambda b, h, i: (0, 0)),
                  pl.BlockSpec((1, LANES), lambda b, h, i: (0, 0)),
                  pl.BlockSpec((tq, LANES), lambda b, h, i: (b * nq + i, h)),
                  pl.BlockSpec((None, LANES, s), lambda b, h, i: (b, h, 0)),
                  pl.BlockSpec((s, LANES), lambda b, h, i: (b, h)),
                  pl.BlockSpec((None, 3, tq, tk), lambda b, h, i: (h, 0, 0, 0))],
        out_specs=pl.BlockSpec((tq, LANES), lambda b, h, i: (b * nq + i, h)),
        out_shape=jax.ShapeDtypeStruct((nb * s, A_HEADS * LANES), MXU_DT),
        scratch_shapes=[pltpu.VMEM((2 * tq, LANES), F32), pltpu.VMEM((2 * tq, LANES), F32),
                        pltpu.VMEM((2 * tq, LANES), F32)],
        compiler_params=_cparams(("parallel", "parallel", "arbitrary")),
        name="attn_a_prompt",
    )(lamv, gsub, qa, ka, va, bias_tiles)


def _attn_c_kernel(q_ref, k_ref, v_ref, cq_ref, ckt_ref, o_ref, m_ref, l_ref, acc_ref, *, tq, tk):
    hp = pl.program_id(1)
    i = pl.program_id(2)
    cq_tile = cq_ref[...]
    lane16 = _lane_iota(cq_tile.shape)
    cq2 = jnp.concatenate(
        [jnp.broadcast_to(jnp.sum(jnp.where(lane16 == 2 * hp + c, cq_tile, 0.0), axis=-1, keepdims=True),
                          (tq, LANES)) for c in range(2)], axis=0)

    def bias_fn(j, off, n_sub):
        width = n_sub * tk
        ck = ckt_ref[:, pl.ds(off, width)]
        ck2 = jnp.concatenate([jnp.broadcast_to(ck[0:1], (tq, width)),
                               jnp.broadcast_to(ck[1:2], (tq, width))], axis=0)
        return _lanes(cq2, width) - ck2

    _flash_stacked(_stack_streams(q_ref[...]), k_ref, v_ref, i, tq, tk, bias_fn, m_ref, l_ref, acc_ref)
    lane = _lane_iota((tq, LANES))
    o = jnp.where(lane < HEAD_DIM, acc_ref[:tq] / l_ref[:tq], acc_ref[tq:] / l_ref[tq:])
    o_ref[...] = _mx(o)


def _attn_c_prompt(q, k, v, cq, ckt, nb, s, tq=256):
    tk = tq
    nq = s // tq
    nhp = C_HEADS // 2
    kern = functools.partial(_attn_c_kernel, tq=tq, tk=tk)
    return pl.pallas_call(
        kern,
        grid=(nb, nhp, nq),
        in_specs=[pl.BlockSpec((tq, LANES), lambda b, h, i: (b * nq + i, h)),
                  pl.BlockSpec((None, LANES, s), lambda b, h, i: (b, h, 0)),
                  pl.BlockSpec((s, LANES), lambda b, h, i: (b, h)),
                  pl.BlockSpec((tq, C_HEADS), lambda b, h, i: (b * nq + i, 0)),
                  pl.BlockSpec((None, None, 2, s), lambda b, h, i: (b, h, 0, 0))],
        out_specs=pl.BlockSpec((tq, LANES), lambda b, h, i: (b * nq + i, h)),
        out_shape=jax.ShapeDtypeStruct((nb * s, C_HEADS * HEAD_DIM), MXU_DT),
        scratch_shapes=[pltpu.VMEM((2 * tq, LANES), F32), pltpu.VMEM((2 * tq, LANES), F32),
                        pltpu.VMEM((2 * tq, LANES), F32)],
        compiler_params=_cparams(("parallel", "parallel", "arbitrary")),
        name="attn_c_prompt",
    )(q, k, v, cq, ckt)


def _score_keys(score):
    score = jnp.where(score == 0.0, 0.0, score)
    bits = pltpu.bitcast(score, I32)
    return bits ^ (jnp.right_shift(bits, 31) & 0x7FFFFFFF)


def _topk_select(keys_ref, width, kcount, active, col):
    kf = float(kcount)
    nbits_col = int(width - 1).bit_length()

    def count(pred):
        return jnp.sum(jnp.where(pred, 1.0, 0.0), axis=1, keepdims=True)

    t0 = jnp.where(count(keys_ref[:, :width] >= 0) >= kf, 0, INT_MIN).astype(I32)

    def body(it, t):
        cand = t + jnp.left_shift(jnp.int32(1), 30 - it)
        return jnp.where(count(keys_ref[:, :width] >= cand) >= kf, cand, t)

    t = lax.fori_loop(0, 31, body, t0)
    t = jnp.where(active, t, INT_MIN)
    keys = keys_ref[:, :width]
    gt = keys > t
    eq = keys == t
    need = kf - count(gt)
    excess = jnp.where(active, count(eq) - need, 0.0)

    def tie_break():
        def tb(it, jj):
            cand = jj + jnp.left_shift(jnp.int32(1), nbits_col - 1 - it)
            c = count((keys_ref[:, :width] == t) & (col < cand))
            return jnp.where(c < need, cand, jj)
        return lax.fori_loop(0, nbits_col, tb, jnp.zeros(t.shape, I32))

    jmax = lax.cond(jnp.max(excess) > 0.0, tie_break, lambda: jnp.full(t.shape, width, I32))
    return gt | (eq & (col <= jmax))


KEY_OF_NEG_INF = -2139095041
SELECT_UNIT = 2


def _attn_b_kernel(qb_ref, qi_ref, w_ref, kk_ref, vv_ref, kiki_ref, bias_ref, o_ref,
                   keys_ref, selm_ref, m_ref, l_ref, acc_ref, *, tq, s_len, topk):
    i = pl.program_id(1)
    cw = tq
    nh = B_HEADS
    n_chunks = s_len // cw
    lane = _lane_iota((tq, LANES))
    halves = (lane < HEAD_DIM, lane >= HEAD_DIM)

    def stack_heads(ref):
        parts = []
        for h in range(nh):
            t = ref[:, (h // 2) * LANES:(h // 2 + 1) * LANES]
            parts.append(jnp.where(halves[h % 2], t, jnp.zeros_like(t)))
        return jnp.concatenate(parts, axis=0)

    row = _row_iota((tq, cw)) + i * tq
    colc = _lane_iota((tq, cw))

    qi8 = stack_heads(qi_ref)
    wt = w_ref[...] * IDX_HEAD_SCALE
    wcol = jnp.concatenate([jnp.broadcast_to(wt[:, HEAD_DIM + h:HEAD_DIM + h + 1], (tq, LANES))
                            for h in range(nh)], axis=0)

    def score_chunk(j, carry):
        off = pl.multiple_of(j * cw, cw)
        d = jnp.maximum(_dot_t(qi8, kiki_ref[pl.ds(off, cw), :]), 0.0) * _lanes(wcol, cw)
        sc = d[0:tq]
        for h in range(1, nh):
            sc = sc + d[h * tq:(h + 1) * tq]
        keys_ref[:, pl.ds(off, cw)] = _score_keys(jnp.where((colc + j * cw) <= row, sc, -jnp.inf))
        return carry

    lax.fori_loop(0, i + 1, score_chunk, 0)
    unit = SELECT_UNIT if n_chunks % SELECT_UNIT == 0 else 1
    rem = (i + 1) % unit
    for u in range(1, unit):
        @pl.when((rem != 0) & (u <= unit - rem))
        def _():
            off = pl.multiple_of((i + u) * cw, cw)
            keys_ref[:, pl.ds(off, cw)] = jnp.full((tq, cw), KEY_OF_NEG_INF, I32)
    widths = [w * unit * cw for w in range(1, n_chunks // unit + 1)]

    qpos = _row_iota((tq, 1)) + i * tq

    def select_branch(width):
        def br():
            colw = _lane_iota((tq, width))
            causal = colw <= (_row_iota((tq, width)) + i * tq)
            sel = _topk_select(keys_ref, width, topk, qpos >= topk, colw)
            selm_ref[:, :width] = jnp.where(sel & causal, 0.0, NEG)
        return br

    def causal_only():
        col = _lane_iota((tq, s_len))
        selm_ref[...] = jnp.where(col <= (_row_iota((tq, s_len)) + i * tq), 0.0, NEG)

    branch = jnp.where((i + 1) * tq > topk, 1 + i // unit, 0)
    lax.switch(branch, [causal_only] + [select_branch(w) for w in widths])

    qb8 = stack_heads(qb_ref)
    m_ref[...] = jnp.full(m_ref.shape, NEG, F32)
    l_ref[...] = jnp.zeros(l_ref.shape, F32)
    acc_ref[...] = jnp.zeros(acc_ref.shape, F32)

    def attn_chunks(j, n_sub):
        width = n_sub * cw
        off = pl.multiple_of(j * cw, cw)
        bias = jnp.concatenate(
            [jnp.concatenate([bias_ref[h, jnp.minimum(i - j - u, 2)] for h in range(nh)], axis=0)
             for u in range(n_sub)], axis=1)
        selm = selm_ref[:, pl.ds(off, width)]
        s = _dot_t(qb8, kk_ref[pl.ds(off, width), :]) + bias + jnp.concatenate([selm] * nh, axis=0)
        alpha, p = _online_update(s, m_ref, l_ref)
        acc_ref[...] = alpha * acc_ref[...] + _dot(_mx(p), vv_ref[pl.ds(off, width), :])

    def wide_body(jj, carry):
        attn_chunks(jj * FLASH_WIDE, FLASH_WIDE)
        return carry

    n_causal = i + 1
    lax.fori_loop(0, n_causal // FLASH_WIDE, wide_body, 0)
    for u in range(1, FLASH_WIDE):
        @pl.when(n_causal % FLASH_WIDE >= u)
        def _():
            attn_chunks(n_causal - n_causal % FLASH_WIDE + (u - 1), 1)
    o = acc_ref[...] / l_ref[...]
    for hp in range(nh // 2):
        o_ref[:, hp * LANES:(hp + 1) * LANES] = _mx(
            jnp.where(halves[0], o[2 * hp * tq:(2 * hp + 1) * tq], o[(2 * hp + 1) * tq:(2 * hp + 2) * tq]))


def _attn_b_prompt(qb, qi, small, kk, vv, kiki, bias_tiles, nb, s, topk, tq=256):
    nq = s // tq
    kern = functools.partial(_attn_b_kernel, tq=tq, s_len=s, topk=topk)
    qrow = lambda b, i: (b * nq + i, 0)
    kv = lambda b, i: (b, 0)
    rows = B_HEADS * tq
    return pl.pallas_call(
        kern,
        grid=(nb, nq),
        in_specs=[pl.BlockSpec((tq, B_HEADS * HEAD_DIM), qrow),
                  pl.BlockSpec((tq, IDX_HEADS * IDX_DIM), qrow),
                  pl.BlockSpec((tq, LANES), lambda b, i: (b * nq + i, 1)),
                  pl.BlockSpec((s, LANES), kv), pl.BlockSpec((s, LANES), kv), pl.BlockSpec((s, LANES), kv),
                  pl.BlockSpec(bias_tiles.shape, lambda b, i: (0, 0, 0, 0))],
        out_specs=pl.BlockSpec((tq, B_HEADS * HEAD_DIM), qrow),
        out_shape=jax.ShapeDtypeStruct((nb * s, B_HEADS * HEAD_DIM), MXU_DT),
        scratch_shapes=[pltpu.VMEM((tq, s), I32), pltpu.VMEM((tq, s), F32),
                        pltpu.VMEM((rows, LANES), F32), pltpu.VMEM((rows, LANES), F32),
                        pltpu.VMEM((rows, LANES), F32)],
        compiler_params=_cparams(("parallel", "arbitrary")),
        name="attn_b_prompt",
    )(qb, qi, small, kk, vv, kiki, bias_tiles)


def _cumsum_rows_kernel(x_ref, tri_ref, o_ref, carry_ref):
    j = pl.program_id(1)

    @pl.when(j == 0)
    def _():
        carry_ref[...] = jnp.zeros(carry_ref.shape, F32)

    tri = tri_ref[...]
    hi, mid, lo = _split3(x_ref[...])
    cum = _dot(tri, hi) + _dot(tri, mid) + _dot(tri, lo) + carry_ref[...]
    o_ref[...] = cum
    carry_ref[...] = cum[-1:, :]


def _cumsum_prompt(x, nb, s, blk=128):
    nj = s // blk
    w = x.shape[1]
    r = np.arange(blk)
    tri = jnp.asarray(r[:, None] >= r[None, :], dtype=MXU_DT)
    return pl.pallas_call(
        _cumsum_rows_kernel,
        grid=(nb, nj),
        in_specs=[pl.BlockSpec((blk, w), lambda b, j: (b * nj + j, 0)),
                  pl.BlockSpec((blk, blk), lambda b, j: (0, 0))],
        out_specs=pl.BlockSpec((blk, w), lambda b, j: (b * nj + j, 0)),
        out_shape=jax.ShapeDtypeStruct((nb * s, w), F32),
        scratch_shapes=[pltpu.VMEM((1, w), F32)],
        compiler_params=_cparams(("parallel", "arbitrary")),
        name="cumsum_prompt",
    )(x, tri)


def _online_update(s, m_ref, l_ref):
    m_old = m_ref[...]
    m_new = jnp.maximum(m_old, jnp.max(s, axis=-1, keepdims=True))
    alpha = jnp.exp(m_old - m_new)
    p = jnp.exp(s - _lanes(m_new, s.shape[1]))
    l_ref[...] = alpha * l_ref[...] + jnp.sum(p, axis=-1, keepdims=True)
    m_ref[...] = m_new
    return alpha, p


def _decode0_kernel(*refs, n_steps, npp, ps, topk, lam_init, group):
    (pt_ref, lamv_ref, gsub_ref, qa_ref, qi_ref, qb_ref, w8_ref, kna_ref, vna_ref, bnew_ref,
     taba_ref, tabb_ref, validb_ref) = refs[:13]
    pages = refs[13:13 + 5 * npp]
    cak, cav, cbk, cbv, cbi = (pages[0:npp], pages[npp:2 * npp], pages[2 * npp:3 * npp],
                               pages[3 * npp:4 * npp], pages[4 * npp:5 * npp])
    oa_ref, ob_ref, m_ref, l_ref, acc_ref, kb_ref, vb_ref, ki_ref, keys_ref = refs[13 + 5 * npp:]
    g = pl.program_id(0) % group
    step = pl.program_id(1)
    past = n_steps * npp * ps
    lk = kb_ref.shape[2]

    @pl.when(step == 0)
    def _():
        m_ref[...] = jnp.full(m_ref.shape, NEG, F32)
        l_ref[...] = jnp.zeros(l_ref.shape, F32)
        acc_ref[...] = jnp.zeros(acc_ref.shape, F32)

    qa = qa_ref[...]
    far = taba_ref[0]
    tail = taba_ref[jnp.where(step == n_steps - 1, 1, 0)]
    s = jnp.concatenate([_dot(qa, _mx(cak[j][...])) + (tail if j == npp - 1 else far) for j in range(npp)], axis=1)
    alpha, p = _online_update(s, m_ref, l_ref)
    p = _mx(p)
    for h in range(A_HEADS):
        rows = slice(h * 2 * T8, (h + 1) * 2 * T8)
        upd = alpha[rows] * acc_ref[rows]
        for j in range(npp):
            vh = _mx(cav[j][pl.ds(h, ps, stride=A_HEADS), :])
            upd = upd + _dot(p[rows, j * ps:(j + 1) * ps], vh)
        acc_ref[rows] = upd

    for j in range(npp):
        off = pl.multiple_of((step * npp + j) * ps, ps)
        kb_ref[g, :, pl.ds(off, ps)] = _mx(cbk[j][...])
        vb_ref[g, :, pl.ds(off, ps)] = _mx(cbv[j][...])
        ki_ref[g, :, pl.ds(off, ps)] = _mx(cbi[j][...])

    @pl.when(step == n_steps - 1)
    def _():
        s_new = _dot_t(qa, _mx(kna_ref[...])) + taba_ref[2][:, :kna_ref.shape[0]]
        alpha2, p2 = _online_update(s_new, m_ref, l_ref)
        vn = _mx(vna_ref[...])
        lam = _lambda(lamv_ref, lam_init)
        for h in range(A_HEADS):
            rows = slice(h * 2 * T8, (h + 1) * 2 * T8)
            o16 = (alpha2[rows] * acc_ref[rows] + _dot(_mx(p2[rows]), vn[:, h * LANES:(h + 1) * LANES])) / l_ref[rows]
            o = o16[:T8] - lam * o16[T8:]
            oa_ref[h] = _rms_rows(o, gsub_ref[...]) * (1.0 - lam_init)

        bn = bnew_ref[g]
        kb_ref[g, :, past:past + LANES] = _mx(bn[0])
        vb_ref[g, :, past:past + LANES] = _mx(bn[1])
        ki_ref[g, :, past:past + LANES] = _mx(bn[2])

    @pl.when((step == n_steps - 1) & (g == group - 1))
    def _():
        valid = validb_ref[...] == 0.0
        for gg in range(group):
            dots = jnp.maximum(_dot(qi_ref[gg], ki_ref[gg]), 0.0)
            w8 = w8_ref[gg] * IDX_HEAD_SCALE
            score = jnp.zeros((T8, lk), F32)
            for h in range(IDX_HEADS):
                score = score + w8[:, h:h + 1] * dots[h * T8:(h + 1) * T8]
            keys_ref[gg * T8:(gg + 1) * T8, :] = _score_keys(jnp.where(valid, score, -jnp.inf))
        rows = group * T8
        sel = _topk_select(keys_ref, lk, topk, jnp.full((rows, 1), True), _lane_iota((rows, lk)))
        selm = jnp.where(sel & jnp.concatenate([valid] * group, axis=0), 0.0, NEG)
        for gg in range(group):
            sg = selm[gg * T8:(gg + 1) * T8]
            sb = _dot(qb_ref[gg], kb_ref[gg]) + tabb_ref[...] + jnp.concatenate([sg] * B_HEADS, axis=0)
            mb = jnp.max(sb, axis=-1, keepdims=True)
            pb = jnp.exp(sb - mb)
            lb = jnp.sum(pb, axis=-1, keepdims=True)
            ob = _dot_t(_mx(pb), vb_ref[gg]) / lb
            for h in range(B_HEADS):
                ob_ref[gg, h] = ob[h * T8:(h + 1) * T8]


def _decode0(page_table, lamv, gsub, qa_bd, qi64, qb64, w8, kna, vna, bnew_t,
             cak_t, cav_r, cbk_t, cbv_t, cbi_t, taba, tabb, validb, layer, topk, lam_init):
    db, npg = page_table.shape
    ps = cak_t.shape[3]
    npp = PAGES_PER_STEP if npg % PAGES_PER_STEP == 0 else 1
    n_steps = npg // npp
    lk = tabb.shape[1]
    group = max(gsz for gsz in (8, 4, 2, 1) if db % gsz == 0)
    per_b = lambda b, p, pt: (b, 0, 0)
    per_g = lambda b, p, pt: (b // group, 0, 0)
    per_g4 = lambda b, p, pt: (b // group, 0, 0, 0)
    fixed2 = lambda b, p, pt: (0, 0)
    fixed3 = lambda b, p, pt: (0, 0, 0)

    def page_specs(arr):
        blk = (None, None) + arr.shape[2:]
        return [pl.BlockSpec(blk, lambda b, p, pt, j=j: (layer, pt[b * npg + p * npp + j], 0, 0)) for j in range(npp)]

    caches = (cak_t, cav_r, cbk_t, cbv_t, cbi_t)
    kern = functools.partial(_decode0_kernel, n_steps=n_steps, npp=npp, ps=ps, topk=topk, lam_init=lam_init,
                             group=group)
    grid_spec = pltpu.PrefetchScalarGridSpec(
        num_scalar_prefetch=1,
        grid=(db, n_steps),
        in_specs=[pl.BlockSpec((4, HEAD_DIM), fixed2), pl.BlockSpec((1, LANES), fixed2),
                  pl.BlockSpec((None,) + qa_bd.shape[1:], per_b),
                  pl.BlockSpec((group,) + qi64.shape[1:], per_g),
                  pl.BlockSpec((group,) + qb64.shape[1:], per_g),
                  pl.BlockSpec((group,) + w8.shape[1:], per_g),
                  pl.BlockSpec((None,) + kna.shape[1:], per_b),
                  pl.BlockSpec((None,) + vna.shape[1:], per_b),
                  pl.BlockSpec((group,) + bnew_t.shape[1:], per_g4),
                  pl.BlockSpec(taba.shape, fixed3), pl.BlockSpec(tabb.shape, fixed2),
                  pl.BlockSpec(validb.shape, fixed2)]
                 + [sp for c in caches for sp in page_specs(c)],
        out_specs=[pl.BlockSpec((None, A_HEADS, T8, LANES), lambda b, p, pt: (b, 0, 0, 0)),
                   pl.BlockSpec((group, B_HEADS, T8, HEAD_DIM), per_g4)],
        scratch_shapes=[pltpu.VMEM((64, LANES), F32), pltpu.VMEM((64, LANES), F32),
                        pltpu.VMEM((64, LANES), F32),
                        pltpu.VMEM((group, HEAD_DIM, lk), MXU_DT), pltpu.VMEM((group, HEAD_DIM, lk), MXU_DT),
                        pltpu.VMEM((group, IDX_DIM, lk), MXU_DT), pltpu.VMEM((group * T8, lk), I32)],
    )
    return pl.pallas_call(
        kern,
        grid_spec=grid_spec,
        out_shape=[jax.ShapeDtypeStruct((db, A_HEADS, T8, LANES), F32),
                   jax.ShapeDtypeStruct((db, B_HEADS, T8, HEAD_DIM), F32)],
        compiler_params=_cparams(("arbitrary", "arbitrary")),
        name="decode0",
    )(page_table.reshape(-1), lamv, gsub, qa_bd, qi64, qb64, w8, kna, vna, bnew_t, taba, tabb, validb,
      *[c for c in caches for _ in range(npp)])


def _expand_rows(x):
    hh, ww = x.shape
    return jnp.broadcast_to(x[:, None, :], (hh, T8, ww)).reshape(hh * T8, ww)


def _decode1_kernel(*refs, n_steps, npp, ps):
    pt_ref, q_ref, cq_ref, kn_ref, vn_ref, cnew_ref, maskn_ref, tri_ref = refs[:8]
    ck = refs[8:8 + npp]
    cv = refs[8 + npp:8 + 2 * npp]
    cf = refs[8 + 2 * npp:8 + 3 * npp]
    o_ref, m_ref, l_ref, acc_ref, suf_ref = refs[8 + 3 * npp:]
    step = pl.program_id(1)

    @pl.when(step == 0)
    def _():
        m_ref[...] = jnp.full(m_ref.shape, NEG, F32)
        l_ref[...] = jnp.zeros(l_ref.shape, F32)
        acc_ref[...] = jnp.zeros(acc_ref.shape, F32)
        suf_ref[...] = jnp.zeros(suf_ref.shape, F32)

    q = q_ref[...]
    cq = jnp.broadcast_to(cq_ref[...], (q.shape[0], LANES))
    tri = tri_ref[...]
    running = suf_ref[...]
    sufs = [None] * npp
    for j in reversed(range(npp)):
        hi, mid, lo = _split3(cf[j][...])
        cum = _dot(hi, tri) + _dot(mid, tri) + _dot(lo, tri)
        tot = jnp.broadcast_to(cum[:, ps - 1:ps], cum.shape)
        sufs[j] = running + tot - cum
        running = running + tot
    suf_ref[...] = running
    s = jnp.concatenate([_dot(q, _mx(ck[j][...])) for j in range(npp)], axis=1)
    s = s + _lanes(cq, npp * ps) + _expand_rows(jnp.concatenate(sufs, axis=1))
    alpha, p = _online_update(s, m_ref, l_ref)
    p = _mx(p)
    upd = _lanes(alpha, acc_ref.shape[1]) * acc_ref[...]
    for j in range(npp):
        upd = upd + _dot_t(p[:, j * ps:(j + 1) * ps], _mx(cv[j][...]))
    acc_ref[...] = upd

    @pl.when(step == n_steps - 1)
    def _():
        nn = kn_ref.shape[0]
        s_new = (_dot_t(q, _mx(kn_ref[...])) + cq[:, :nn] - _expand_rows(cnew_ref[...]) + maskn_ref[...])
        alpha2, p2 = _online_update(s_new, m_ref, l_ref)
        o_full = ((_lanes(alpha2, acc_ref.shape[1]) * acc_ref[...] + _dot(_mx(p2), _mx(vn_ref[...])))
                  / _lanes(l_ref[...], acc_ref.shape[1]))
        for h in range(C_HEADS):
            o_ref[h] = o_full[h * T8:(h + 1) * T8, h * HEAD_DIM:(h + 1) * HEAD_DIM]


def _decode1(page_table, q_bd, cq, kn, vn, cnew_t, cck_t, ccv_t, ccf_t, maskn, layer):
    db, npg = page_table.shape
    ps = cck_t.shape[3]
    assert ps == LANES, ps
    npp = PAGES_PER_STEP if npg % PAGES_PER_STEP == 0 else 1
    n_steps = npg // npp
    per_b = lambda b, p, pt: (b, 0, 0)
    fixed = lambda b, p, pt: (0, 0)
    r = np.arange(ps)
    tri = jnp.asarray(r[:, None] <= r[None, :], dtype=MXU_DT)

    def page_specs(arr):
        blk = (None, None) + arr.shape[2:]
        return [pl.BlockSpec(blk, lambda b, p, pt, j=j: (layer, pt[b * npg + (n_steps - 1 - p) * npp + j], 0, 0))
                for j in range(npp)]

    kern = functools.partial(_decode1_kernel, n_steps=n_steps, npp=npp, ps=ps)
    rows = C_HEADS * T8
    grid_spec = pltpu.PrefetchScalarGridSpec(
        num_scalar_prefetch=1,
        grid=(db, n_steps),
        in_specs=[pl.BlockSpec((None,) + q_bd.shape[1:], per_b),
                  pl.BlockSpec((None,) + cq.shape[1:], per_b),
                  pl.BlockSpec((None,) + kn.shape[1:], per_b),
                  pl.BlockSpec((None,) + vn.shape[1:], per_b),
                  pl.BlockSpec((None,) + cnew_t.shape[1:], per_b),
                  pl.BlockSpec(maskn.shape, fixed), pl.BlockSpec(tri.shape, fixed)]
                 + page_specs(cck_t) + page_specs(ccv_t) + page_specs(ccf_t),
        out_specs=pl.BlockSpec((None, C_HEADS, T8, HEAD_DIM), lambda b, p, pt: (b, 0, 0, 0)),
        scratch_shapes=[pltpu.VMEM((rows, LANES), F32), pltpu.VMEM((rows, LANES), F32),
                        pltpu.VMEM((rows, C_HEADS * HEAD_DIM), F32), pltpu.VMEM((C_HEADS, LANES), F32)],
    )
    return pl.pallas_call(
        kern,
        grid_spec=grid_spec,
        out_shape=jax.ShapeDtypeStruct((db, C_HEADS, T8, HEAD_DIM), F32),
        compiler_params=_cparams(("parallel", "arbitrary")),
        name="decode1",
    )(page_table.reshape(-1), q_bd, cq, kn, vn, cnew_t, maskn, tri,
      *([cck_t] * npp), *([ccv_t] * npp), *([ccf_t] * npp))


def _router_kernel(x_ref, g_ref, whi_ref, wlo_ref, b_ref, t_ref, route_ref):
    t = _rms_rows(x_ref[...], g_ref[...])
    thi, tlo = _split2(t)
    t_ref[...] = thi
    whi = whi_ref[...]
    logits = _dot(thi, whi) + _dot(tlo, whi) + _dot(thi, wlo_ref[...]) + b_ref[...]
    lane = _lane_iota(logits.shape)
    big = jnp.int32(1 << 20)

    def first_max(v):
        mx = jnp.max(v, axis=-1, keepdims=True)
        idx = jnp.min(jnp.where(v == mx, lane, big), axis=-1, keepdims=True)
        return mx, idx

    glog = jnp.where(lane < N_GROUPS, logits, -jnp.inf)
    gmax, gidx = first_max(glog)
    grp_w = 1.0 / jnp.sum(jnp.exp(glog - gmax), axis=-1, keepdims=True)
    el = lane - N_GROUPS
    in_grp = (el >= 0) & (el < N_EXPERTS) & (jnp.right_shift(el, 3) == gidx)
    v1 = jnp.where(in_grp, logits, -jnp.inf)
    top1, i1 = first_max(v1)
    v2 = jnp.where(lane == i1, -jnp.inf, v1)
    top2, i2 = first_max(v2)
    e2 = jnp.exp(top2 - top1)
    w1 = grp_w / (1.0 + e2)
    w2 = grp_w * e2 / (1.0 + e2)
    route = jnp.where(lane == 0, (i1 - N_GROUPS).astype(F32),
                      jnp.where(lane == 1, (i2 - N_GROUPS).astype(F32),
                                jnp.where(lane == 2, w1, jnp.where(lane == 3, w2, 0.0))))
    route_ref[...] = route


def _router(x, g, w_group, b_group, w_router, b_router, tm=512):
    n = x.shape[0]
    tm = _tile(n, tm)
    pad = LANES - N_GROUPS - N_EXPERTS
    w = jnp.concatenate([w_group, w_router, jnp.zeros((D_MODEL, pad), F32)], axis=1)
    whi = _mx(w)
    wlo = _mx(w - whi.astype(F32))
    b = jnp.concatenate([b_group, b_router, jnp.zeros((pad,), F32)])[None, :]
    row = lambda i: (i, 0)
    fixed = lambda i: (0, 0)
    return pl.pallas_call(
        _router_kernel,
        grid=(n // tm,),
        in_specs=[pl.BlockSpec((tm, D_MODEL), row), pl.BlockSpec((1, D_MODEL), fixed),
                  pl.BlockSpec((D_MODEL, LANES), fixed), pl.BlockSpec((D_MODEL, LANES), fixed),
                  pl.BlockSpec((1, LANES), fixed)],
        out_specs=[pl.BlockSpec((tm, D_MODEL), row), pl.BlockSpec((tm, LANES), row)],
        out_shape=[jax.ShapeDtypeStruct((n, D_MODEL), MXU_DT), jax.ShapeDtypeStruct((n, LANES), F32)],
        compiler_params=_cparams(("parallel",)),
        name="moe_router",
    )(x, g[None, :], whi, wlo, b)


def _moe_dense_kernel(x_ref, t_ref, route_ref, wgu_ref, wd_ref, o_ref):
    e = pl.program_id(1)

    @pl.when(e == 0)
    def _():
        o_ref[...] = x_ref[...]

    t = t_ref[...]
    r = route_ref[...]
    upd = None
    for k in range(EXPERTS_PER_STEP):
        au = _dot(t, wgu_ref[k])
        a = au[:, :EXPERT_FF]
        u = au[:, EXPERT_FF:]
        ef = (e * EXPERTS_PER_STEP + k).astype(F32)
        gate = jnp.where(r[:, 0:1] == ef, r[:, 2:3], 0.0) + jnp.where(r[:, 1:2] == ef, r[:, 3:4], 0.0)
        hdn = a * (1.0 / (1.0 + jnp.exp(-a))) * u * gate
        y = _dot(_mx(hdn), wd_ref[k])
        upd = y if upd is None else upd + y
    o_ref[...] += upd


def _moe_dense(x, t, route, w_gate, w_up, w_down, tm=1536):
    n = x.shape[0]
    tm = _tile(n, tm)
    wgu = _mx(jnp.concatenate([w_gate.reshape(N_EXPERTS, D_MODEL, EXPERT_FF),
                               w_up.reshape(N_EXPERTS, D_MODEL, EXPERT_FF)], axis=-1))
    wd = _mx(w_down.reshape(N_EXPERTS, EXPERT_FF, D_MODEL))
    row = lambda i, e: (i, 0)
    once = pl.Buffered(1)
    return pl.pallas_call(
        _moe_dense_kernel,
        grid=(n // tm, N_EXPERTS // EXPERTS_PER_STEP),
        in_specs=[pl.BlockSpec((tm, D_MODEL), row, pipeline_mode=once),
                  pl.BlockSpec((tm, D_MODEL), row, pipeline_mode=once),
                  pl.BlockSpec((tm, LANES), row, pipeline_mode=once),
                  pl.BlockSpec((EXPERTS_PER_STEP, D_MODEL, 2 * EXPERT_FF), lambda i, e: (e, 0, 0)),
                  pl.BlockSpec((EXPERTS_PER_STEP, EXPERT_FF, D_MODEL), lambda i, e: (e, 0, 0))],
        out_specs=pl.BlockSpec((tm, D_MODEL), row),
        out_shape=jax.ShapeDtypeStruct((n, D_MODEL), F32),
        compiler_params=_cparams(("parallel", "arbitrary")),
        name="moe_dense",
    )(x, t, route, wgu, wd)


def _hier_moe(x, g, w_group, b_group, w_router, b_router, w_gate, w_up, w_down):
    t, route = _router(x, g, w_group, b_group, w_router, b_router)
    return _moe_dense(x, t, route, w_gate, w_up, w_down)


def _decode_tables(rel_bias, past, t_new, lk):
    t8 = np.arange(T8)
    lane = np.arange(LANES)
    far = np.full((T8, LANES), REL_BUCKETS - 1, np.int32)
    d_last = (past + t8[:, None]) - (past - LANES + lane[None, :])
    d_new = t8[:, None] - lane[None, :]
    ok_new = (d_new >= 0) & (lane[None, :] < t_new)
    ta = _bias_expand(rel_bias, np.stack([far, _t5_bucket_np(d_last), _t5_bucket_np(d_new)]), 0, A_HEADS)
    ta = jnp.where(jnp.asarray(ok_new)[None, None] | (jnp.arange(3) < 2)[None, :, None, None], ta, NEG)
    taba = jnp.broadcast_to(jnp.transpose(ta, (1, 0, 2, 3))[:, :, None], (3, A_HEADS, 2, T8, LANES))
    taba = taba.reshape(3, A_HEADS * 2 * T8, LANES)
    kpos = np.arange(lk)
    d_b = (past + t8[:, None]) - kpos[None, :]
    tabb = _bias_expand(rel_bias, _t5_bucket_np(d_b), A_HEADS, B_HEADS).reshape(B_HEADS * T8, lk)
    valid = (kpos[None, :] < past) | ((d_b >= 0) & (kpos[None, :] < past + t_new))
    validb = jnp.asarray(np.where(valid, 0.0, NEG), F32)
    return taba, tabb, validb


def _pad_rows(x, rows):
    pad = [(0, 0)] * x.ndim
    pad[1] = (0, rows - x.shape[1])
    return jnp.pad(x, pad)


def kernel(x_prompt, x_sample, cache_a_k, cache_a_v, cache_b_k, cache_b_v, cache_b_kidx, cache_c_k, cache_c_v, cache_c_logf, page_table, rel_bias, ab_norm, ab_w_in, a_q_norm, a_k_norm, b_q_norm, b_k_norm, a_lambda_q1, a_lambda_k1, a_lambda_q2, a_lambda_k2, a_sub_norm, ab_w_out, c_norm, c_w_in, c_forget_bias, c_q_norm, c_k_norm, c_w_out, ffn_norm, moe_w_group, moe_b_group, moe_w_router, moe_b_router, moe_w_gate, moe_w_up, moe_w_down):
    nb, s, d = x_prompt.shape
    db, ts, _ = x_sample.shape
    npg = page_table.shape[1]
    pool, ps = cache_a_k.shape[1], cache_a_k.shape[2]
    past = npg * ps
    n_p = nb * s
    n_s = db * ts
    depth = ffn_norm.shape[0]
    topk_p = min(IDX_TOPK_MAX, s // 4)
    topk_s = min(IDX_TOPK_MAX, (past + ts) // 4)
    lk = past + LANES

    cak_t = jnp.transpose(cache_a_k, (0, 1, 3, 4, 5, 2)).reshape(-1, pool, 2 * A_HEADS * HEAD_DIM, ps)
    cav_r = cache_a_v.reshape(-1, pool, ps * A_HEADS, 2 * HEAD_DIM)
    cbk_t = jnp.transpose(cache_b_k, (0, 1, 3, 2))
    cbv_t = jnp.transpose(cache_b_v, (0, 1, 3, 2))
    cbi_t = jnp.transpose(cache_b_kidx, (0, 1, 3, 2))
    cck_t = jnp.transpose(cache_c_k, (0, 1, 3, 4, 2)).reshape(-1, pool, C_HEADS * HEAD_DIM, ps)
    ccv_t = jnp.transpose(cache_c_v, (0, 1, 3, 4, 2)).reshape(-1, pool, C_HEADS * HEAD_DIM, ps)
    ccf_t = jnp.transpose(cache_c_logf, (0, 1, 3, 2))

    x = jnp.concatenate([x_prompt.reshape(n_p, d), x_sample.reshape(n_s, d)], axis=0)
    outs_p = {k: [] for k in ("ak", "av", "bk", "bv", "bi", "ck", "cv", "cf")}
    outs_s = {k: [] for k in ("ak", "av", "bk", "bv", "bi", "ck", "cv", "cf")}

    def smp(a):
        return a[n_p:].reshape(db, ts, a.shape[1])

    for layer in range(depth):
        if layer % 2 == 0:
            e = layer // 2
            lam_init = 0.8 - 0.6 * math.exp(-0.3 * layer)
            lamv = jnp.stack([a_lambda_q1[e], a_lambda_k1[e], a_lambda_q2[e], a_lambda_k2[e]])
            gsub = a_sub_norm[e][None, :]
            (qa, ka_s, kat, katbf, qb, small, smallt, kk, vv, kiki, va, vabf, qi) = _proj0(
                x, ab_norm[e], ab_w_in[e], a_q_norm[e], a_k_norm[e], b_q_norm[e], b_k_norm[e], n_p, s)
            tq_a = min(FLASH_TQ, s)
            bias_a = _bias_expand(rel_bias, _prompt_bucket_tiles(tq_a), 0, A_HEADS)
            oa_p = _attn_a_prompt(qa, katbf, vabf, lamv, gsub, bias_a, nb, s, lam_init, tq=tq_a)
            tq_b = 256
            bias_b = _bias_expand(rel_bias, _prompt_bucket_tiles(tq_b), A_HEADS, B_HEADS)
            ob_p = _attn_b_prompt(qb, qi, small, kk, vv, kiki, bias_b, nb, s, topk_p, tq=tq_b)
            qa_s = _pad_rows(smp(qa), T8)
            hc = np.arange(2 * A_HEADS)
            colmask = jnp.asarray((np.arange(qa_s.shape[2])[None, :] // HEAD_DIM) == hc[:, None], MXU_DT)
            qa_bd = (qa_s[:, None, :, :] * colmask[None, :, None, :]).reshape(db, 2 * A_HEADS * T8, -1)

            def heads_rows(a, nh):
                a = _pad_rows(a, T8).reshape(db, T8, nh, HEAD_DIM)
                return jnp.transpose(a, (0, 2, 1, 3)).reshape(db, nh * T8, HEAD_DIM)

            qi64 = heads_rows(smp(qi), IDX_HEADS)
            qb64 = heads_rows(smp(qb), B_HEADS)
            small_s = smp(small)
            w8 = _pad_rows(small_s[:, :, 192:200], T8)
            ka_s3 = ka_s.reshape(db, ts, -1)
            va_s3 = smp(va)
            kna = _pad_rows(ka_s3, 16)
            vna = _pad_rows(va_s3, 16)
            bnew_t = jnp.transpose(_pad_rows(small_s[:, :, :192], LANES).reshape(db, LANES, 3, HEAD_DIM), (0, 2, 3, 1))
            taba, tabb, validb = _decode_tables(rel_bias, past, ts, lk)
            oa_d, ob_d = _decode0(page_table, lamv, gsub, qa_bd, qi64, qb64, w8, kna, vna, bnew_t,
                                  cak_t, cav_r, cbk_t, cbv_t, cbi_t, taba, tabb, validb, e, topk_s, lam_init)
            oa_s = jnp.transpose(oa_d[:, :, :ts], (0, 2, 1, 3)).reshape(n_s, A_HEADS * LANES)
            ob_s = jnp.transpose(ob_d[:, :, :ts], (0, 2, 1, 3)).reshape(n_s, B_HEADS * HEAD_DIM)
            w_out = ab_w_out[e]
            x = _outproj(x, [(oa_p, _mx(oa_s), w_out[:A_HEADS * LANES]),
                             (ob_p, _mx(ob_s), w_out[A_HEADS * LANES:])], n_p)
            outs_p["ak"].append(jnp.transpose(kat.reshape(nb, A_HEADS, 2, HEAD_DIM, s), (0, 4, 1, 2, 3)))
            outs_p["av"].append(va[:n_p].reshape(nb, s, A_HEADS, 2 * HEAD_DIM))
            outs_p["bk"].append(jnp.transpose(smallt[:, 0:64], (0, 2, 1)))
            outs_p["bv"].append(jnp.transpose(smallt[:, 64:128], (0, 2, 1)))
            outs_p["bi"].append(jnp.transpose(smallt[:, 128:192], (0, 2, 1)))
            outs_s["ak"].append(ka_s3.reshape(db, ts, A_HEADS, 2, HEAD_DIM))
            outs_s["av"].append(va_s3.reshape(db, ts, A_HEADS, 2 * HEAD_DIM))
            outs_s["bk"].append(small_s[:, :, 0:64])
            outs_s["bv"].append(small_s[:, :, 64:128])
            outs_s["bi"].append(small_s[:, :, 128:192])
        else:
            o = layer // 2
            q, k_s, kt, ktbf, v_s, vt, vbf, logf128, logft = _proj1(
                x, c_norm[o], c_w_in[o], c_forget_bias[o], c_q_norm[o], c_k_norm[o], n_p, s)
            logf = logf128[:, :C_HEADS]
            cum_p = _cumsum_prompt(logf128[:n_p], nb, s)[:, :C_HEADS]
            ckt = jnp.transpose(cum_p.reshape(nb, s, C_HEADS // 2, 2), (0, 2, 3, 1))
            oc_p = _attn_c_prompt(q, ktbf, vbf, cum_p, ckt, nb, s, tq=min(FLASH_TQ, s))
            logf_s = smp(logf)
            run = jnp.zeros_like(logf_s[:, 0])
            c_rows = []
            for t in range(ts):
                run = run + logf_s[:, t]
                c_rows.append(run)
            c_new = jnp.stack(c_rows, axis=1)
            cq = jnp.transpose(_pad_rows(c_new, T8), (0, 2, 1)).reshape(db, C_HEADS * T8, 1)
            cnew_t = jnp.transpose(_pad_rows(c_new, 16), (0, 2, 1))
            q_s = _pad_rows(smp(q), T8)
            hmask = jnp.asarray((np.arange(q_s.shape[2])[None, :] // HEAD_DIM) == np.arange(C_HEADS)[:, None], MXU_DT)
            q_bd = (q_s[:, None, :, :] * hmask[None, :, None, :]).reshape(db, C_HEADS * T8, -1)
            k_s3 = k_s.reshape(db, ts, -1)
            v_s3 = v_s.reshape(db, ts, -1)
            kn = _pad_rows(k_s3, 16)
            vn = _pad_rows(v_s3, 16)
            t8 = np.arange(T8)
            okn = (t8[:, None] >= np.arange(16)[None, :]) & (np.arange(16)[None, :] < ts)
            maskn = jnp.asarray(np.tile(np.where(okn, 0.0, NEG), (C_HEADS, 1)), F32)
            oc_d = _decode1(page_table, q_bd, cq, kn, vn, cnew_t, cck_t, ccv_t, ccf_t, maskn, o)
            oc_s = jnp.transpose(oc_d[:, :, :ts], (0, 2, 1, 3)).reshape(n_s, C_HEADS * HEAD_DIM)
            x = _outproj(x, [(oc_p, _mx(oc_s), c_w_out[o])], n_p)
            outs_p["ck"].append(jnp.transpose(kt.reshape(nb, C_HEADS, HEAD_DIM, s), (0, 3, 1, 2)))
            outs_p["cv"].append(jnp.transpose(vt.reshape(nb, C_HEADS, HEAD_DIM, s), (0, 3, 1, 2)))
            outs_p["cf"].append(jnp.transpose(logft[:, :C_HEADS], (0, 2, 1)))
            outs_s["ck"].append(k_s3.reshape(db, ts, C_HEADS, HEAD_DIM))
            outs_s["cv"].append(v_s3.reshape(db, ts, C_HEADS, HEAD_DIM))
            outs_s["cf"].append(logf_s)
        x = _hier_moe(x, ffn_norm[layer], moe_w_group[layer], moe_b_group[layer], moe_w_router[layer],
                      moe_b_router[layer], moe_w_gate[layer], moe_w_up[layer], moe_w_down[layer])

    keys = ("ak", "av", "bk", "bv", "bi", "ck", "cv", "cf")
    return ((x[:n_p].reshape(nb, s, d), x[n_p:].reshape(db, ts, d))
            + tuple(jnp.stack(outs_p[k]) for k in keys)
            + tuple(jnp.stack(outs_s[k]) for k in keys))
```

```python
import functools
import math

import numpy as np
import jax
import jax.numpy as jnp
from jax import lax
from jax.experimental import pallas as pl
from jax.experimental.pallas import tpu as pltpu

F32 = jnp.float32
I32 = jnp.int32
MXU_DT = jnp.bfloat16

D_MODEL = 1024
HEAD_DIM = 64
A_HEADS = 4
B_HEADS = 8
IDX_HEADS = 8
IDX_DIM = 64
IDX_TOPK_MAX = 256
C_HEADS = 16
REL_BUCKETS = 32
REL_MAX_DIST = 128
N_GROUPS = 4
EXPERTS_PER_GROUP = 8
N_EXPERTS = N_GROUPS * EXPERTS_PER_GROUP
EXPERT_FF = 256
EPS = 1e-6
NEG = -1e30
INT_MIN = -2 ** 31
QK_SCALE = HEAD_DIM ** -0.5
IDX_SCALE = IDX_DIM ** -0.5
IDX_HEAD_SCALE = IDX_HEADS ** -0.5
LANES = 128
T8 = 8
VMEM_LIMIT = 56 * 1024 * 1024
PAGES_PER_STEP = 16
FLASH_TQ = 512
FLASH_WIDE = 2
EXPERTS_PER_STEP = 4

def _tile(n, pref):
    best = 16
    for t in range(16, pref + 1, 16):
        if n % t == 0:
            best = t
    assert n % best == 0, (n, pref)
    return best


def _cparams(sem):
    return pltpu.CompilerParams(dimension_semantics=sem, vmem_limit_bytes=VMEM_LIMIT)


def _mx(x):
    return x.astype(MXU_DT)


def _dot(a, b):
    return jnp.dot(a, b, preferred_element_type=F32)


def _dot_t(a, b):
    return lax.dot_general(a, b, (((1,), (1,)), ((), ())), preferred_element_type=F32)


def _split2(x):
    hi = _mx(x)
    lo = _mx(x - hi.astype(F32))
    return hi, lo


def _split3(x):
    hi = _mx(x)
    r = x - hi.astype(F32)
    mid = _mx(r)
    lo = _mx(r - mid.astype(F32))
    return hi, mid, lo


def _lane_iota(shape):
    return lax.broadcasted_iota(I32, shape, len(shape) - 1)


def _row_iota(shape):
    return lax.broadcasted_iota(I32, shape, len(shape) - 2)


def _lanes(x, width):
    if width <= LANES:
        return x[:, :width]
    return jnp.tile(x, (1, width // LANES))


def _rms_rows(x, g):
    ms = jnp.mean(x * x, axis=-1, keepdims=True)
    return x * lax.rsqrt(ms + EPS) * g


def _seg_rsqrt(y, bd):
    hi, lo = _split2(y * y)
    ss = _dot(hi, bd) + _dot(lo, bd)
    return lax.rsqrt(ss * (1.0 / HEAD_DIM) + EPS)


def _block_diag_ones(n, seg):
    r = np.arange(n)
    return jnp.asarray((r[:, None] // seg) == (r[None, :] // seg), dtype=MXU_DT)


def _lambda(lamv_ref, lam_init):
    lv = lamv_ref[...]
    return (jnp.exp(jnp.sum(lv[0:1] * lv[1:2], axis=-1, keepdims=True))
            - jnp.exp(jnp.sum(lv[2:3] * lv[3:4], axis=-1, keepdims=True)) + lam_init)


def _t5_bucket_np(d):
    n = np.maximum(d, 0)
    exact = REL_BUCKETS // 2
    nf = np.maximum(n, 1).astype(np.float64)
    large = exact + (np.log(nf / exact) / math.log(REL_MAX_DIST / exact) * (REL_BUCKETS - exact)).astype(np.int64)
    large = np.minimum(large, REL_BUCKETS - 1)
    return np.where(n < exact, n, large).astype(np.int32)


def _bias_expand_kernel(relb_ref, idx_ref, o_ref, *, head0):
    h = pl.program_id(0) + head0
    idx = idx_ref[...]
    acc = jnp.zeros(idx.shape, F32)
    for b in range(REL_BUCKETS):
        acc = jnp.where(idx == b, relb_ref[b, h], acc)
    o_ref[...] = acc


def _bias_expand(rel_bias, idx_np, head0, n_heads):
    idx = jnp.asarray(idx_np, I32)
    nd = idx.ndim
    zeros = (0,) * nd
    return pl.pallas_call(
        functools.partial(_bias_expand_kernel, head0=head0),
        grid=(n_heads,),
        in_specs=[pl.BlockSpec(memory_space=pltpu.SMEM),
                  pl.BlockSpec(idx.shape, lambda h: zeros)],
        out_specs=pl.BlockSpec((None,) + idx.shape, lambda h: (h,) + zeros),
        out_shape=jax.ShapeDtypeStruct((n_heads,) + idx.shape, F32),
        compiler_params=_cparams(("arbitrary",)),
        name="bias_expand",
    )(rel_bias, idx)


def _prompt_bucket_tiles(t):
    r = np.arange(t)
    d = r[:, None] - r[None, :]
    return np.stack([_t5_bucket_np(d), _t5_bucket_np(d + t), np.full((t, t), REL_BUCKETS - 1, np.int32)])


def _store_t(y, is_prompt, t_refs, row_ref, c):
    if is_prompt:
        yt = jnp.transpose(y)
        for r in t_refs:
            r[c * 256:(c + 1) * 256, :] = yt.astype(r.dtype)
    elif row_ref is not None:
        row_ref[:, c * 256:(c + 1) * 256] = y


def _per_tile_kind(body, n_prompt_tiles):
    prompt_tile = pl.program_id(0) < n_prompt_tiles

    @pl.when(prompt_tile)
    def _():
        body(True)

    @pl.when(jnp.logical_not(prompt_tile))
    def _():
        body(False)


def _proj0_kernel(*refs, n_prompt_tiles):
    _per_tile_kind(functools.partial(_proj0_body, *refs), n_prompt_tiles)


def _proj0_body(x_ref, g_ref, w_ref, gain_ref, bd_ref,
                qa_ref, ka_ref, kat_ref, katbf_ref, qb_ref, small_ref, smallt_ref, kk_ref, vv_ref, kiki_ref,
                va_ref, vabf_ref, qi_ref, is_prompt):
    h = _mx(_rms_rows(x_ref[...], g_ref[...]))
    bd = bd_ref[...]

    def chunk(c):
        return _dot(h, w_ref[:, c * 256:(c + 1) * 256])

    def normed(c):
        y = chunk(c)
        return y * _seg_rsqrt(y, bd) * gain_ref[:, c * 256:(c + 1) * 256]

    for c in range(2):
        qa_ref[:, c * 256:(c + 1) * 256] = _mx(normed(c) * QK_SCALE)
    for c in range(2):
        _store_t(normed(2 + c), is_prompt, (kat_ref, katbf_ref), ka_ref, c)
    for c in range(2):
        qb_ref[:, c * 256:(c + 1) * 256] = _mx(normed(4 + c) * QK_SCALE)
    y = chunk(6)
    yn = y * _seg_rsqrt(y, bd) * gain_ref[:, 6 * 256:7 * 256]
    lane = _lane_iota(y.shape)
    y = jnp.where(lane < HEAD_DIM, yn, y)
    small_ref[...] = y
    _store_t(y, is_prompt, (smallt_ref,), None, 0)
    t0 = y[:, :LANES]
    t1 = y[:, LANES:]
    lo = _lane_iota(t0.shape) < HEAD_DIM
    r0 = pltpu.roll(t0, HEAD_DIM, 1)
    kk_ref[...] = _mx(jnp.where(lo, t0, r0))
    vv_ref[...] = _mx(jnp.where(lo, r0, t0))
    r1 = pltpu.roll(t1, HEAD_DIM, 1)
    kiki_ref[...] = _mx(jnp.where(lo, t1, r1))
    for c in range(2):
        y = chunk(7 + c)
        va_ref[:, c * 256:(c + 1) * 256] = y
        vabf_ref[:, c * 256:(c + 1) * 256] = _mx(y)
    for c in range(2):
        qi_ref[:, c * 256:(c + 1) * 256] = _mx(chunk(9 + c) * IDX_SCALE)


def _token_specs(n_p, n_s, s, tm):
    npt = n_p // tm
    tps = s // tm

    def t_map(i):
        ip = jnp.minimum(i, npt - 1)
        return (ip // tps, 0, ip % tps)

    def s_map(i):
        return (jnp.maximum(i - npt, 0), 0)

    return npt, t_map, s_map


def _proj0(x, g, w_in, a_qn, a_kn, b_qn, b_kn, n_p, s, tm=512):
    n = x.shape[0]
    n_s = n - n_p
    nb = n_p // s
    tm = _tile(math.gcd(n_p, n_s, s), tm)
    npt, t_map, s_map = _token_specs(n_p, n_s, s, tm)
    sp = np.cumsum([512, 512, 512, 512, 64, 64, 512, 64, 8])[:-1]
    wqa, wka, wva, wqb, wkb, wvb, wqi, wki, wwi = jnp.split(w_in, sp, axis=1)
    w = jnp.concatenate([wqa, wka, wqb, wkb, wvb, wki, wwi, jnp.zeros((D_MODEL, 56), F32), wva, wqi], axis=1)
    w = _mx(w)
    ncol = w.shape[1]
    gain = jnp.concatenate([jnp.tile(a_qn, 8), jnp.tile(a_kn, 8), jnp.tile(b_qn, 8), b_kn,
                            jnp.ones((192,), F32)])[None, :]
    bd = _block_diag_ones(256, HEAD_DIM)
    row = lambda i: (i, 0)
    fixed = lambda i: (0, 0)
    def rows(wd, dt):
        return pl.BlockSpec((tm, wd), row), jax.ShapeDtypeStruct((n, wd), dt)

    def rows_s(wd, dt):
        return pl.BlockSpec((tm, wd), s_map), jax.ShapeDtypeStruct((n_s, wd), dt)

    def cols_p(wd, dt):
        return pl.BlockSpec((None, wd, tm), t_map), jax.ShapeDtypeStruct((nb, wd, s), dt)

    outs = [rows(512, MXU_DT), rows_s(512, F32), cols_p(512, F32), cols_p(512, MXU_DT), rows(512, MXU_DT),
            rows(256, F32), cols_p(256, F32), rows(128, MXU_DT), rows(128, MXU_DT), rows(128, MXU_DT),
            rows(512, F32), rows(512, MXU_DT), rows(512, MXU_DT)]
    return pl.pallas_call(
        functools.partial(_proj0_kernel, n_prompt_tiles=npt),
        grid=(n // tm,),
        in_specs=[pl.BlockSpec((tm, D_MODEL), row), pl.BlockSpec((1, D_MODEL), fixed),
                  pl.BlockSpec((D_MODEL, ncol), fixed), pl.BlockSpec((1, gain.shape[1]), fixed),
                  pl.BlockSpec((256, 256), fixed)],
        out_specs=[o[0] for o in outs],
        out_shape=[o[1] for o in outs],
        compiler_params=_cparams(("arbitrary",)),
        name="proj0",
    )(x, g[None, :], w, gain, bd)


def _proj1_kernel(*refs, n_prompt_tiles):
    _per_tile_kind(functools.partial(_proj1_body, *refs), n_prompt_tiles)


def _proj1_body(x_ref, g_ref, w_ref, gain_ref, bf_ref, bd_ref,
                q_ref, k_ref, kt_ref, ktbf_ref, v_ref, vt_ref, vbf_ref, logf_ref, logft_ref, is_prompt):
    h = _mx(_rms_rows(x_ref[...], g_ref[...]))
    bd = bd_ref[...]

    def chunk(c):
        return _dot(h, w_ref[:, c * 256:(c + 1) * 256])

    def normed(c):
        y = chunk(c)
        return y * _seg_rsqrt(y, bd) * gain_ref[:, c * 256:(c + 1) * 256]

    for c in range(4):
        q_ref[:, c * 256:(c + 1) * 256] = _mx(normed(c) * QK_SCALE)
    for c in range(4):
        _store_t(normed(4 + c), is_prompt, (kt_ref, ktbf_ref), k_ref, c)
    for c in range(4):
        y = chunk(8 + c)
        vbf_ref[:, c * 256:(c + 1) * 256] = _mx(y)
        _store_t(y, is_prompt, (vt_ref,), v_ref, c)
    f = _dot(h, w_ref[:, 12 * 256:12 * 256 + LANES]) + bf_ref[...]
    logf = jnp.minimum(f, 0.0) - jnp.log(1.0 + jnp.exp(-jnp.abs(f)))
    logf_ref[...] = logf
    if is_prompt:
        logft_ref[...] = jnp.transpose(logf)


def _proj1(x, g, w_in, b_f, qn, kn, n_p, s, tm=512):
    n = x.shape[0]
    n_s = n - n_p
    nb = n_p // s
    tm = _tile(math.gcd(n_p, n_s, s), tm)
    npt, t_map, s_map = _token_specs(n_p, n_s, s, tm)
    w = _mx(jnp.concatenate([w_in, jnp.zeros((D_MODEL, LANES - C_HEADS), F32)], axis=1))
    ncol = w.shape[1]
    gain = jnp.concatenate([jnp.tile(qn, C_HEADS), jnp.tile(kn, C_HEADS)])[None, :]
    bf = jnp.concatenate([b_f, jnp.zeros((LANES - C_HEADS,), F32)])[None, :]
    bd = _block_diag_ones(256, HEAD_DIM)
    row = lambda i: (i, 0)
    fixed = lambda i: (0, 0)
    def rows(wd, dt):
        return pl.BlockSpec((tm, wd), row), jax.ShapeDtypeStruct((n, wd), dt)

    def rows_s(wd, dt):
        return pl.BlockSpec((tm, wd), s_map), jax.ShapeDtypeStruct((n_s, wd), dt)

    def cols_p(wd, dt):
        return pl.BlockSpec((None, wd, tm), t_map), jax.ShapeDtypeStruct((nb, wd, s), dt)

    outs = [rows(1024, MXU_DT), rows_s(1024, F32), cols_p(1024, F32), cols_p(1024, MXU_DT),
            rows_s(1024, F32), cols_p(1024, F32), rows(1024, MXU_DT), rows(LANES, F32), cols_p(LANES, F32)]
    return pl.pallas_call(
        functools.partial(_proj1_kernel, n_prompt_tiles=npt),
        grid=(n // tm,),
        in_specs=[pl.BlockSpec((tm, D_MODEL), row), pl.BlockSpec((1, D_MODEL), fixed),
                  pl.BlockSpec((D_MODEL, ncol), fixed), pl.BlockSpec((1, gain.shape[1]), fixed),
                  pl.BlockSpec((1, LANES), fixed), pl.BlockSpec((256, 256), fixed)],
        out_specs=[o[0] for o in outs],
        out_shape=[o[1] for o in outs],
        compiler_params=_cparams(("arbitrary",)),
        name="proj1",
    )(x, g[None, :], w, gain, bf, bd)


def _outproj_kernel(*refs, n_in, n_prompt_tiles):
    res_ref = refs[0]
    g_ref, whi_ref, wlo_ref, b_ref = refs[1 + 3 * n_in:5 + 3 * n_in]
    out_ref, t_ref, route_ref = refs[5 + 3 * n_in:]

    def body(is_prompt):
        acc = res_ref[...]
        for i in range(n_in):
            a_ref = refs[1 + 3 * i] if is_prompt else refs[2 + 3 * i]
            acc = acc + _dot(a_ref[...], refs[3 + 3 * i][...])
        out_ref[...] = acc
        t, route = _route_rows(acc, g_ref[...], whi_ref[...], wlo_ref[...], b_ref[...])
        t_ref[...] = t
        route_ref[...] = route

    _per_tile_kind(body, n_prompt_tiles)


def _outproj(res, terms, n_p, router, tm=512):
    n = res.shape[0]
    n_s = n - n_p
    tm = _tile(math.gcd(n_p, n_s), tm)
    npt = n_p // tm
    row = lambda i: (i, 0)
    fixed = lambda i: (0, 0)
    p_map = lambda i: (jnp.minimum(i, npt - 1), 0)
    s_map = lambda i: (jnp.maximum(i - npt, 0), 0)
    in_specs = [pl.BlockSpec((tm, D_MODEL), row)]
    args = [res]
    for a_p, a_s, w in terms:
        in_specs += [pl.BlockSpec((tm, a_p.shape[1]), p_map), pl.BlockSpec((tm, a_s.shape[1]), s_map),
                     pl.BlockSpec(w.shape, fixed)]
        args += [a_p, a_s, _mx(w)]
    in_specs += [pl.BlockSpec(r.shape, fixed) for r in router]
    args += list(router)
    return pl.pallas_call(
        functools.partial(_outproj_kernel, n_in=len(terms), n_prompt_tiles=npt),
        grid=(n // tm,),
        in_specs=in_specs,
        out_specs=[pl.BlockSpec((tm, D_MODEL), row), pl.BlockSpec((tm, D_MODEL), row),
                   pl.BlockSpec((tm, LANES), row)],
        out_shape=[jax.ShapeDtypeStruct((n, D_MODEL), F32), jax.ShapeDtypeStruct((n, D_MODEL), MXU_DT),
                   jax.ShapeDtypeStruct((n, LANES), F32)],
        compiler_params=_cparams(("parallel",)),
        name="outproj",
    )(*args)


def _stack_streams(q):
    lane = _lane_iota(q.shape)
    zero = jnp.zeros_like(q)
    return jnp.concatenate([jnp.where(lane < HEAD_DIM, q, zero), jnp.where(lane >= HEAD_DIM, q, zero)], axis=0)


def _flash_stacked(q2, kt_ref, v_ref, i, tq, tk, bias_fn, m_ref, l_ref, acc_ref):
    rows = 2 * tq
    m_ref[...] = jnp.full(m_ref.shape, NEG, F32)
    l_ref[...] = jnp.zeros(l_ref.shape, F32)
    acc_ref[...] = jnp.zeros(acc_ref.shape, F32)

    def tile(j, n_sub, masked):
        width = n_sub * tk
        off = pl.multiple_of(j * tk, tk)
        s = _dot(q2, kt_ref[:, pl.ds(off, width)]) + bias_fn(j, off, n_sub)
        if masked:
            r = _row_iota((rows, width))
            s = jnp.where(_lane_iota((rows, width)) <= jnp.where(r >= tq, r - tq, r), s, NEG)
        alpha, p = _online_update(s, m_ref, l_ref)
        acc_ref[...] = alpha * acc_ref[...] + _dot(_mx(p), v_ref[pl.ds(off, width), :])

    def body(jj, carry):
        tile(jj * FLASH_WIDE, FLASH_WIDE, False)
        return carry

    lax.fori_loop(0, i // FLASH_WIDE, body, 0)
    for u in range(1, FLASH_WIDE):
        @pl.when(i % FLASH_WIDE >= u)
        def _():
            tile(i - i % FLASH_WIDE + (u - 1), 1, False)
    tile(i, 1, True)


def _attn_a_kernel(lamv_ref, gsub_ref, q_ref, k_ref, v_ref, bias_ref, o_ref,
                   m_ref, l_ref, acc_ref, *, tq, tk, lam_init):
    i = pl.program_id(2)

    def bias_fn(j, off, n_sub):
        b = jnp.concatenate([bias_ref[jnp.minimum(i - j - u, 2)] for u in range(n_sub)], axis=1)
        return jnp.concatenate([b, b], axis=0)

    _flash_stacked(_stack_streams(q_ref[...]), k_ref, v_ref, i, tq, tk, bias_fn, m_ref, l_ref, acc_ref)
    lam = _lambda(lamv_ref, lam_init)
    o = acc_ref[:tq] / l_ref[:tq] - lam * (acc_ref[tq:] / l_ref[tq:])
    o = _rms_rows(o, gsub_ref[...]) * (1.0 - lam_init)
    o_ref[...] = _mx(o)


def _attn_a_prompt(qa, ka, va, lamv, gsub, bias_tiles, nb, s, lam_init, tq=256):
    tk = tq
    nq = s // tq
    kern = functools.partial(_attn_a_kernel, tq=tq, tk=tk, lam_init=lam_init)
    return pl.pallas_call(
        kern,
        grid=(nb, A_HEADS, nq),
        in_specs=[pl.BlockSpec((4, HEAD_DIM), lambda b, h, i: (0, 0)),
                  pl.BlockSpec((1, LANES), lambda b, h, i: (0, 0)),
                  pl.BlockSpec((tq, LANES), lambda b, h, i: (b * nq + i, h)),
                  pl.BlockSpec((None, LANES, s), lambda b, h, i: (b, h, 0)),
                  pl.BlockSpec((s, LANES), lambda b, h, i: (b, h)),
                  pl.BlockSpec((None, 3, tq, tk), lambda b, h, i: (h, 0, 0, 0))],
        out_specs=pl.BlockSpec((tq, LANES), lambda b, h, i: (b * nq + i, h)),
        out_shape=jax.ShapeDtypeStruct((nb * s, A_HEADS * LANES), MXU_DT),
        scratch_shapes=[pltpu.VMEM((2 * tq, LANES), F32), pltpu.VMEM((2 * tq, LANES), F32),
                        pltpu.VMEM((2 * tq, LANES), F32)],
        compiler_params=_cparams(("parallel", "parallel", "arbitrary")),
        name="attn_a_prompt",
    )(lamv, gsub, qa, ka, va, bias_tiles)


def _attn_c_kernel(q_ref, k_ref, v_ref, cq_ref, ckt_ref, o_ref, m_ref, l_ref, acc_ref, *, tq, tk):
    hp = pl.program_id(1)
    i = pl.program_id(2)
    cq_tile = cq_ref[...]
    lane16 = _lane_iota(cq_tile.shape)
    cq2 = jnp.concatenate(
        [jnp.broadcast_to(jnp.sum(jnp.where(lane16 == 2 * hp + c, cq_tile, 0.0), axis=-1, keepdims=True),
                          (tq, LANES)) for c in range(2)], axis=0)

    def bias_fn(j, off, n_sub):
        width = n_sub * tk
        ck = ckt_ref[:, pl.ds(off, width)]
        ck2 = jnp.concatenate([jnp.broadcast_to(ck[0:1], (tq, width)),
                               jnp.broadcast_to(ck[1:2], (tq, width))], axis=0)
        return _lanes(cq2, width) - ck2

    _flash_stacked(_stack_streams(q_ref[...]), k_ref, v_ref, i, tq, tk, bias_fn, m_ref, l_ref, acc_ref)
    lane = _lane_iota((tq, LANES))
    o = jnp.where(lane < HEAD_DIM, acc_ref[:tq] / l_ref[:tq], acc_ref[tq:] / l_ref[tq:])
    o_ref[...] = _mx(o)


def _attn_c_prompt(q, k, v, cq, ckt, nb, s, tq=256):
    tk = tq
    nq = s // tq
    nhp = C_HEADS // 2
    kern = functools.partial(_attn_c_kernel, tq=tq, tk=tk)
    return pl.pallas_call(
        kern,
        grid=(nb, nhp, nq),
        in_specs=[pl.BlockSpec((tq, LANES), lambda b, h, i: (b * nq + i, h)),
                  pl.BlockSpec((None, LANES, s), lambda b, h, i: (b, h, 0)),
                  pl.BlockSpec((s, LANES), lambda b, h, i: (b, h)),
                  pl.BlockSpec((tq, C_HEADS), lambda b, h, i: (b * nq + i, 0)),
                  pl.BlockSpec((None, None, 2, s), lambda b, h, i: (b, h, 0, 0))],
        out_specs=pl.BlockSpec((tq, LANES), lambda b, h, i: (b * nq + i, h)),
        out_shape=jax.ShapeDtypeStruct((nb * s, C_HEADS * HEAD_DIM), MXU_DT),
        scratch_shapes=[pltpu.VMEM((2 * tq, LANES), F32), pltpu.VMEM((2 * tq, LANES), F32),
                        pltpu.VMEM((2 * tq, LANES), F32)],
        compiler_params=_cparams(("parallel", "parallel", "arbitrary")),
        name="attn_c_prompt",
    )(q, k, v, cq, ckt)


def _score_keys(score):
    score = jnp.where(score == 0.0, 0.0, score)
    bits = pltpu.bitcast(score, I32)
    return bits ^ (jnp.right_shift(bits, 31) & 0x7FFFFFFF)


def _topk_select(keys_ref, width, kcount, active, col):
    kf = float(kcount)
    nbits_col = int(width - 1).bit_length()

    def count(pred):
        return jnp.sum(jnp.where(pred, 1.0, 0.0), axis=1, keepdims=True)

    t0 = jnp.where(count(keys_ref[:, :width] >= 0) >= kf, 0, INT_MIN).astype(I32)

    def body(it, t):
        cand = t + jnp.left_shift(jnp.int32(1), 30 - it)
        return jnp.where(count(keys_ref[:, :width] >= cand) >= kf, cand, t)

    t = lax.fori_loop(0, 31, body, t0)
    t = jnp.where(active, t, INT_MIN)
    keys = keys_ref[:, :width]
    gt = keys > t
    eq = keys == t
    need = kf - count(gt)
    excess = jnp.where(active, count(eq) - need, 0.0)

    def tie_break():
        def tb(it, jj):
            cand = jj + jnp.left_shift(jnp.int32(1), nbits_col - 1 - it)
            c = count((keys_ref[:, :width] == t) & (col < cand))
            return jnp.where(c < need, cand, jj)
        return lax.fori_loop(0, nbits_col, tb, jnp.zeros(t.shape, I32))

    jmax = lax.cond(jnp.max(excess) > 0.0, tie_break, lambda: jnp.full(t.shape, width, I32))
    return gt | (eq & (col <= jmax))


KEY_OF_NEG_INF = -2139095041
SELECT_UNIT = 2


def _attn_b_kernel(qb_ref, qi_ref, w_ref, kk_ref, vv_ref, kiki_ref, bias_ref, o_ref,
                   keys_ref, selm_ref, m_ref, l_ref, acc_ref, *, tq, s_len, topk):
    i = pl.program_id(1)
    cw = tq
    nh = B_HEADS
    n_chunks = s_len // cw
    lane = _lane_iota((tq, LANES))
    halves = (lane < HEAD_DIM, lane >= HEAD_DIM)

    def stack_heads(ref):
        parts = []
        for h in range(nh):
            t = ref[:, (h // 2) * LANES:(h // 2 + 1) * LANES]
            parts.append(jnp.where(halves[h % 2], t, jnp.zeros_like(t)))
        return jnp.concatenate(parts, axis=0)

    row = _row_iota((tq, cw)) + i * tq
    colc = _lane_iota((tq, cw))

    qi8 = stack_heads(qi_ref)
    wt = w_ref[...] * IDX_HEAD_SCALE
    wcol = jnp.concatenate([jnp.broadcast_to(wt[:, HEAD_DIM + h:HEAD_DIM + h + 1], (tq, LANES))
                            for h in range(nh)], axis=0)

    def score_chunk(j, carry):
        off = pl.multiple_of(j * cw, cw)
        d = jnp.maximum(_dot_t(qi8, kiki_ref[pl.ds(off, cw), :]), 0.0) * _lanes(wcol, cw)
        sc = d[0:tq]
        for h in range(1, nh):
            sc = sc + d[h * tq:(h + 1) * tq]
        keys_ref[:, pl.ds(off, cw)] = _score_keys(jnp.where((colc + j * cw) <= row, sc, -jnp.inf))
        return carry

    lax.fori_loop(0, i + 1, score_chunk, 0)
    unit = SELECT_UNIT if n_chunks % SELECT_UNIT == 0 else 1
    rem = (i + 1) % unit
    for u in range(1, unit):
        @pl.when((rem != 0) & (u <= unit - rem))
        def _():
            off = pl.multiple_of((i + u) * cw, cw)
            keys_ref[:, pl.ds(off, cw)] = jnp.full((tq, cw), KEY_OF_NEG_INF, I32)
    widths = [w * unit * cw for w in range(1, n_chunks // unit + 1)]

    qpos = _row_iota((tq, 1)) + i * tq

    def select_branch(width):
        def br():
            colw = _lane_iota((tq, width))
            causal = colw <= (_row_iota((tq, width)) + i * tq)
            sel = _topk_select(keys_ref, width, topk, qpos >= topk, colw)
            selm_ref[:, :width] = jnp.where(sel & causal, 0.0, NEG)
        return br

    def causal_only():
        col = _lane_iota((tq, s_len))
        selm_ref[...] = jnp.where(col <= (_row_iota((tq, s_len)) + i * tq), 0.0, NEG)

    branch = jnp.where((i + 1) * tq > topk, 1 + i // unit, 0)
    lax.switch(branch, [causal_only] + [select_branch(w) for w in widths])

    qb8 = stack_heads(qb_ref)
    m_ref[...] = jnp.full(m_ref.shape, NEG, F32)
    l_ref[...] = jnp.zeros(l_ref.shape, F32)
    acc_ref[...] = jnp.zeros(acc_ref.shape, F32)

    def attn_chunks(j, n_sub):
        width = n_sub * cw
        off = pl.multiple_of(j * cw, cw)
        bias = jnp.concatenate(
            [jnp.concatenate([bias_ref[h, jnp.minimum(i - j - u, 2)] for h in range(nh)], axis=0)
             for u in range(n_sub)], axis=1)
        selm = selm_ref[:, pl.ds(off, width)]
        s = _dot_t(qb8, kk_ref[pl.ds(off, width), :]) + bias + jnp.concatenate([selm] * nh, axis=0)
        alpha, p = _online_update(s, m_ref, l_ref)
        acc_ref[...] = alpha * acc_ref[...] + _dot(_mx(p), vv_ref[pl.ds(off, width), :])

    def wide_body(jj, carry):
        attn_chunks(jj * FLASH_WIDE, FLASH_WIDE)
        return carry

    n_causal = i + 1
    lax.fori_loop(0, n_causal // FLASH_WIDE, wide_body, 0)
    for u in range(1, FLASH_WIDE):
        @pl.when(n_causal % FLASH_WIDE >= u)
        def _():
            attn_chunks(n_causal - n_causal % FLASH_WIDE + (u - 1), 1)
    o = acc_ref[...] / l_ref[...]
    for hp in range(nh // 2):
        o_ref[:, hp * LANES:(hp + 1) * LANES] = _mx(
            jnp.where(halves[0], o[2 * hp * tq:(2 * hp + 1) * tq], o[(2 * hp + 1) * tq:(2 * hp + 2) * tq]))


def _attn_b_prompt(qb, qi, small, kk, vv, kiki, bias_tiles, nb, s, topk, tq=256):
    nq = s // tq
    kern = functools.partial(_attn_b_kernel, tq=tq, s_len=s, topk=topk)
    qrow = lambda b, i: (b * nq + i, 0)
    kv = lambda b, i: (b, 0)
    rows = B_HEADS * tq
    return pl.pallas_call(
        kern,
        grid=(nb, nq),
        in_specs=[pl.BlockSpec((tq, B_HEADS * HEAD_DIM), qrow),
                  pl.BlockSpec((tq, IDX_HEADS * IDX_DIM), qrow),
                  pl.BlockSpec((tq, LANES), lambda b, i: (b * nq + i, 1)),
                  pl.BlockSpec((s, LANES), kv), pl.BlockSpec((s, LANES), kv), pl.BlockSpec((s, LANES), kv),
                  pl.BlockSpec(bias_tiles.shape, lambda b, i: (0, 0, 0, 0))],
        out_specs=pl.BlockSpec((tq, B_HEADS * HEAD_DIM), qrow),
        out_shape=jax.ShapeDtypeStruct((nb * s, B_HEADS * HEAD_DIM), MXU_DT),
        scratch_shapes=[pltpu.VMEM((tq, s), I32), pltpu.VMEM((tq, s), F32),
                        pltpu.VMEM((rows, LANES), F32), pltpu.VMEM((rows, LANES), F32),
                        pltpu.VMEM((rows, LANES), F32)],
        compiler_params=_cparams(("parallel", "arbitrary")),
        name="attn_b_prompt",
    )(qb, qi, small, kk, vv, kiki, bias_tiles)


def _cumsum_rows_kernel(x_ref, tri_ref, o_ref, carry_ref):
    j = pl.program_id(1)

    @pl.when(j == 0)
    def _():
        carry_ref[...] = jnp.zeros(carry_ref.shape, F32)

    tri = tri_ref[...]
    hi, mid, lo = _split3(x_ref[...])
    cum = _dot(tri, hi) + _dot(tri, mid) + _dot(tri, lo) + carry_ref[...]
    o_ref[...] = cum
    carry_ref[...] = cum[-1:, :]


def _cumsum_prompt(x, nb, s, blk=128):
    nj = s // blk
    w = x.shape[1]
    r = np.arange(blk)
    tri = jnp.asarray(r[:, None] >= r[None, :], dtype=MXU_DT)
    return pl.pallas_call(
        _cumsum_rows_kernel,
        grid=(nb, nj),
        in_specs=[pl.BlockSpec((blk, w), lambda b, j: (b * nj + j, 0)),
                  pl.BlockSpec((blk, blk), lambda b, j: (0, 0))],
        out_specs=pl.BlockSpec((blk, w), lambda b, j: (b * nj + j, 0)),
        out_shape=jax.ShapeDtypeStruct((nb * s, w), F32),
        scratch_shapes=[pltpu.VMEM((1, w), F32)],
        compiler_params=_cparams(("parallel", "arbitrary")),
        name="cumsum_prompt",
    )(x, tri)


def _online_update(s, m_ref, l_ref):
    m_old = m_ref[...]
    m_new = jnp.maximum(m_old, jnp.max(s, axis=-1, keepdims=True))
    alpha = jnp.exp(m_old - m_new)
    p = jnp.exp(s - _lanes(m_new, s.shape[1]))
    l_ref[...] = alpha * l_ref[...] + jnp.sum(p, axis=-1, keepdims=True)
    m_ref[...] = m_new
    return alpha, p


def _decode0_kernel(*refs, n_steps, npp, ps, topk, lam_init, group):
    (pt_ref, lamv_ref, gsub_ref, qa_ref, qi_ref, qb_ref, w8_ref, kna_ref, vna_ref, bnew_ref,
     taba_ref, tabb_ref, validb_ref) = refs[:13]
    pages = refs[13:13 + 5 * npp]
    cak, cav, cbk, cbv, cbi = (pages[0:npp], pages[npp:2 * npp], pages[2 * npp:3 * npp],
                               pages[3 * npp:4 * npp], pages[4 * npp:5 * npp])
    oa_ref, ob_ref, m_ref, l_ref, acc_ref, kb_ref, vb_ref, ki_ref, keys_ref = refs[13 + 5 * npp:]
    g = pl.program_id(0) % group
    step = pl.program_id(1)
    past = n_steps * npp * ps
    lk = kb_ref.shape[2]

    @pl.when(step == 0)
    def _():
        m_ref[...] = jnp.full(m_ref.shape, NEG, F32)
        l_ref[...] = jnp.zeros(l_ref.shape, F32)
        acc_ref[...] = jnp.zeros(acc_ref.shape, F32)

    qa = qa_ref[...]
    far = taba_ref[0]
    tail = taba_ref[jnp.where(step == n_steps - 1, 1, 0)]
    s = jnp.concatenate([_dot(qa, _mx(cak[j][...])) + (tail if j == npp - 1 else far) for j in range(npp)], axis=1)
    alpha, p = _online_update(s, m_ref, l_ref)
    p = _mx(p)
    for h in range(A_HEADS):
        rows = slice(h * 2 * T8, (h + 1) * 2 * T8)
        upd = alpha[rows] * acc_ref[rows]
        for j in range(npp):
            vh = _mx(cav[j][pl.ds(h, ps, stride=A_HEADS), :])
            upd = upd + _dot(p[rows, j * ps:(j + 1) * ps], vh)
        acc_ref[rows] = upd

    for j in range(npp):
        off = pl.multiple_of((step * npp + j) * ps, ps)
        kb_ref[g, :, pl.ds(off, ps)] = _mx(cbk[j][...])
        vb_ref[g, :, pl.ds(off, ps)] = _mx(cbv[j][...])
        ki_ref[g, :, pl.ds(off, ps)] = _mx(cbi[j][...])

    @pl.when(step == n_steps - 1)
    def _():
        s_new = _dot_t(qa, _mx(kna_ref[...])) + taba_ref[2][:, :kna_ref.shape[0]]
        alpha2, p2 = _online_update(s_new, m_ref, l_ref)
        vn = _mx(vna_ref[...])
        lam = _lambda(lamv_ref, lam_init)
        for h in range(A_HEADS):
            rows = slice(h * 2 * T8, (h + 1) * 2 * T8)
            o16 = (alpha2[rows] * acc_ref[rows] + _dot(_mx(p2[rows]), vn[:, h * LANES:(h + 1) * LANES])) / l_ref[rows]
            o = o16[:T8] - lam * o16[T8:]
            oa_ref[h] = _rms_rows(o, gsub_ref[...]) * (1.0 - lam_init)

        bn = bnew_ref[g]
        kb_ref[g, :, past:past + LANES] = _mx(bn[0])
        vb_ref[g, :, past:past + LANES] = _mx(bn[1])
        ki_ref[g, :, past:past + LANES] = _mx(bn[2])

    @pl.when((step == n_steps - 1) & (g == group - 1))
    def _():
        valid = validb_ref[...] == 0.0
        for gg in range(group):
            dots = jnp.maximum(_dot(qi_ref[gg], ki_ref[gg]), 0.0)
            w8 = w8_ref[gg] * IDX_HEAD_SCALE
            score = jnp.zeros((T8, lk), F32)
            for h in range(IDX_HEADS):
                score = score + w8[:, h:h + 1] * dots[h * T8:(h + 1) * T8]
            keys_ref[gg * T8:(gg + 1) * T8, :] = _score_keys(jnp.where(valid, score, -jnp.inf))
        rows = group * T8
        sel = _topk_select(keys_ref, lk, topk, jnp.full((rows, 1), True), _lane_iota((rows, lk)))
        selm = jnp.where(sel & jnp.concatenate([valid] * group, axis=0), 0.0, NEG)
        for gg in range(group):
            sg = selm[gg * T8:(gg + 1) * T8]
            sb = _dot(qb_ref[gg], kb_ref[gg]) + tabb_ref[...] + jnp.concatenate([sg] * B_HEADS, axis=0)
            mb = jnp.max(sb, axis=-1, keepdims=True)
            pb = jnp.exp(sb - mb)
            lb = jnp.sum(pb, axis=-1, keepdims=True)
            ob = _dot_t(_mx(pb), vb_ref[gg]) / lb
            for h in range(B_HEADS):
                ob_ref[gg, h] = ob[h * T8:(h + 1) * T8]


def _decode0(page_table, lamv, gsub, qa_bd, qi64, qb64, w8, kna, vna, bnew_t,
             cak_t, cav_r, cbk_t, cbv_t, cbi_t, taba, tabb, validb, layer, topk, lam_init):
    db, npg = page_table.shape
    ps = cak_t.shape[3]
    npp = PAGES_PER_STEP if npg % PAGES_PER_STEP == 0 else 1
    n_steps = npg // npp
    lk = tabb.shape[1]
    group = max(gsz for gsz in (8, 4, 2, 1) if db % gsz == 0)
    per_b = lambda b, p, pt: (b, 0, 0)
    per_g = lambda b, p, pt: (b // group, 0, 0)
    per_g4 = lambda b, p, pt: (b // group, 0, 0, 0)
    fixed2 = lambda b, p, pt: (0, 0)
    fixed3 = lambda b, p, pt: (0, 0, 0)

    def page_specs(arr):
        blk = (None, None) + arr.shape[2:]
        return [pl.BlockSpec(blk, lambda b, p, pt, j=j: (layer, pt[b * npg + p * npp + j], 0, 0)) for j in range(npp)]

    caches = (cak_t, cav_r, cbk_t, cbv_t, cbi_t)
    kern = functools.partial(_decode0_kernel, n_steps=n_steps, npp=npp, ps=ps, topk=topk, lam_init=lam_init,
                             group=group)
    grid_spec = pltpu.PrefetchScalarGridSpec(
        num_scalar_prefetch=1,
        grid=(db, n_steps),
        in_specs=[pl.BlockSpec((4, HEAD_DIM), fixed2), pl.BlockSpec((1, LANES), fixed2),
                  pl.BlockSpec((None,) + qa_bd.shape[1:], per_b),
                  pl.BlockSpec((group,) + qi64.shape[1:], per_g),
                  pl.BlockSpec((group,) + qb64.shape[1:], per_g),
                  pl.BlockSpec((group,) + w8.shape[1:], per_g),
                  pl.BlockSpec((None,) + kna.shape[1:], per_b),
                  pl.BlockSpec((None,) + vna.shape[1:], per_b),
                  pl.BlockSpec((group,) + bnew_t.shape[1:], per_g4),
                  pl.BlockSpec(taba.shape, fixed3), pl.BlockSpec(tabb.shape, fixed2),
                  pl.BlockSpec(validb.shape, fixed2)]
                 + [sp for c in caches for sp in page_specs(c)],
        out_specs=[pl.BlockSpec((None, A_HEADS, T8, LANES), lambda b, p, pt: (b, 0, 0, 0)),
                   pl.BlockSpec((group, B_HEADS, T8, HEAD_DIM), per_g4)],
        scratch_shapes=[pltpu.VMEM((64, LANES), F32), pltpu.VMEM((64, LANES), F32),
                        pltpu.VMEM((64, LANES), F32),
                        pltpu.VMEM((group, HEAD_DIM, lk), MXU_DT), pltpu.VMEM((group, HEAD_DIM, lk), MXU_DT),
                        pltpu.VMEM((group, IDX_DIM, lk), MXU_DT), pltpu.VMEM((group * T8, lk), I32)],
    )
    return pl.pallas_call(
        kern,
        grid_spec=grid_spec,
        out_shape=[jax.ShapeDtypeStruct((db, A_HEADS, T8, LANES), F32),
                   jax.ShapeDtypeStruct((db, B_HEADS, T8, HEAD_DIM), F32)],
        compiler_params=_cparams(("arbitrary", "arbitrary")),
        name="decode0",
    )(page_table.reshape(-1), lamv, gsub, qa_bd, qi64, qb64, w8, kna, vna, bnew_t, taba, tabb, validb,
      *[c for c in caches for _ in range(npp)])


def _expand_rows(x):
    hh, ww = x.shape
    return jnp.broadcast_to(x[:, None, :], (hh, T8, ww)).reshape(hh * T8, ww)


def _decode1_kernel(*refs, n_steps, npp, ps):
    pt_ref, q_ref, cq_ref, kn_ref, vn_ref, cnew_ref, maskn_ref, tri_ref = refs[:8]
    ck = refs[8:8 + npp]
    cv = refs[8 + npp:8 + 2 * npp]
    cf = refs[8 + 2 * npp:8 + 3 * npp]
    o_ref, m_ref, l_ref, acc_ref, suf_ref = refs[8 + 3 * npp:]
    step = pl.program_id(1)

    @pl.when(step == 0)
    def _():
        m_ref[...] = jnp.full(m_ref.shape, NEG, F32)
        l_ref[...] = jnp.zeros(l_ref.shape, F32)
        acc_ref[...] = jnp.zeros(acc_ref.shape, F32)
        suf_ref[...] = jnp.zeros(suf_ref.shape, F32)

    q = q_ref[...]
    cq = jnp.broadcast_to(cq_ref[...], (q.shape[0], LANES))
    tri = tri_ref[...]
    running = suf_ref[...]
    sufs = [None] * npp
    for j in reversed(range(npp)):
        hi, mid, lo = _split3(cf[j][...])
        cum = _dot(hi, tri) + _dot(mid, tri) + _dot(lo, tri)
        tot = jnp.broadcast_to(cum[:, ps - 1:ps], cum.shape)
        sufs[j] = running + tot - cum
        running = running + tot
    suf_ref[...] = running
    s = jnp.concatenate([_dot(q, _mx(ck[j][...])) for j in range(npp)], axis=1)
    s = s + _lanes(cq, npp * ps) + _expand_rows(jnp.concatenate(sufs, axis=1))
    alpha, p = _online_update(s, m_ref, l_ref)
    p = _mx(p)
    upd = _lanes(alpha, acc_ref.shape[1]) * acc_ref[...]
    for j in range(npp):
        upd = upd + _dot_t(p[:, j * ps:(j + 1) * ps], _mx(cv[j][...]))
    acc_ref[...] = upd

    @pl.when(step == n_steps - 1)
    def _():
        nn = kn_ref.shape[0]
        s_new = (_dot_t(q, _mx(kn_ref[...])) + cq[:, :nn] - _expand_rows(cnew_ref[...]) + maskn_ref[...])
        alpha2, p2 = _online_update(s_new, m_ref, l_ref)
        o_full = ((_lanes(alpha2, acc_ref.shape[1]) * acc_ref[...] + _dot(_mx(p2), _mx(vn_ref[...])))
                  / _lanes(l_ref[...], acc_ref.shape[1]))
        for h in range(C_HEADS):
            o_ref[h] = o_full[h * T8:(h + 1) * T8, h * HEAD_DIM:(h + 1) * HEAD_DIM]


def _decode1(page_table, q_bd, cq, kn, vn, cnew_t, cck_t, ccv_t, ccf_t, maskn, layer):
    db, npg = page_table.shape
    ps = cck_t.shape[3]
    assert ps == LANES, ps
    npp = PAGES_PER_STEP if npg % PAGES_PER_STEP == 0 else 1
    n_steps = npg // npp
    per_b = lambda b, p, pt: (b, 0, 0)
    fixed = lambda b, p, pt: (0, 0)
    r = np.arange(ps)
    tri = jnp.asarray(r[:, None] <= r[None, :], dtype=MXU_DT)

    def page_specs(arr):
        blk = (None, None) + arr.shape[2:]
        return [pl.BlockSpec(blk, lambda b, p, pt, j=j: (layer, pt[b * npg + (n_steps - 1 - p) * npp + j], 0, 0))
                for j in range(npp)]

    kern = functools.partial(_decode1_kernel, n_steps=n_steps, npp=npp, ps=ps)
    rows = C_HEADS * T8
    grid_spec = pltpu.PrefetchScalarGridSpec(
        num_scalar_prefetch=1,
        grid=(db, n_steps),
        in_specs=[pl.BlockSpec((None,) + q_bd.shape[1:], per_b),
                  pl.BlockSpec((None,) + cq.shape[1:], per_b),
                  pl.BlockSpec((None,) + kn.shape[1:], per_b),
                  pl.BlockSpec((None,) + vn.shape[1:], per_b),
                  pl.BlockSpec((None,) + cnew_t.shape[1:], per_b),
                  pl.BlockSpec(maskn.shape, fixed), pl.BlockSpec(tri.shape, fixed)]
                 + page_specs(cck_t) + page_specs(ccv_t) + page_specs(ccf_t),
        out_specs=pl.BlockSpec((None, C_HEADS, T8, HEAD_DIM), lambda b, p, pt: (b, 0, 0, 0)),
        scratch_shapes=[pltpu.VMEM((rows, LANES), F32), pltpu.VMEM((rows, LANES), F32),
                        pltpu.VMEM((rows, C_HEADS * HEAD_DIM), F32), pltpu.VMEM((C_HEADS, LANES), F32)],
    )
    return pl.pallas_call(
        kern,
        grid_spec=grid_spec,
        out_shape=jax.ShapeDtypeStruct((db, C_HEADS, T8, HEAD_DIM), F32),
        compiler_params=_cparams(("parallel", "arbitrary")),
        name="decode1",
    )(page_table.reshape(-1), q_bd, cq, kn, vn, cnew_t, maskn, tri,
      *([cck_t] * npp), *([ccv_t] * npp), *([ccf_t] * npp))


def _route_rows(x, g, whi, wlo, b):
    t = _rms_rows(x, g)
    thi, tlo = _split2(t)
    logits = _dot(thi, whi) + _dot(tlo, whi) + _dot(thi, wlo) + b
    lane = _lane_iota(logits.shape)
    big = jnp.int32(1 << 20)

    def first_max(v):
        mx = jnp.max(v, axis=-1, keepdims=True)
        idx = jnp.min(jnp.where(v == mx, lane, big), axis=-1, keepdims=True)
        return mx, idx

    glog = jnp.where(lane < N_GROUPS, logits, -jnp.inf)
    gmax, gidx = first_max(glog)
    grp_w = 1.0 / jnp.sum(jnp.exp(glog - gmax), axis=-1, keepdims=True)
    el = lane - N_GROUPS
    in_grp = (el >= 0) & (el < N_EXPERTS) & (jnp.right_shift(el, 3) == gidx)
    v1 = jnp.where(in_grp, logits, -jnp.inf)
    top1, i1 = first_max(v1)
    v2 = jnp.where(lane == i1, -jnp.inf, v1)
    top2, i2 = first_max(v2)
    e2 = jnp.exp(top2 - top1)
    w1 = grp_w / (1.0 + e2)
    w2 = grp_w * e2 / (1.0 + e2)
    route = jnp.where(lane == 0, (i1 - N_GROUPS).astype(F32),
                      jnp.where(lane == 1, (i2 - N_GROUPS).astype(F32),
                                jnp.where(lane == 2, w1, jnp.where(lane == 3, w2, 0.0))))
    return thi, route


def _router_params(g, w_group, b_group, w_router, b_router):
    pad = LANES - N_GROUPS - N_EXPERTS
    w = jnp.concatenate([w_group, w_router, jnp.zeros((D_MODEL, pad), F32)], axis=1)
    whi = _mx(w)
    wlo = _mx(w - whi.astype(F32))
    b = jnp.concatenate([b_group, b_router, jnp.zeros((pad,), F32)])[None, :]
    return g[None, :], whi, wlo, b


def _moe_dense_kernel(x_ref, t_ref, route_ref, wgu_ref, wd_ref, o_ref):
    e = pl.program_id(1)

    @pl.when(e == 0)
    def _():
        o_ref[...] = x_ref[...]

    t = t_ref[...]
    r = route_ref[...]
    upd = None
    for k in range(EXPERTS_PER_STEP):
        au = _dot(t, wgu_ref[k])
        a = au[:, :EXPERT_FF]
        u = au[:, EXPERT_FF:]
        ef = (e * EXPERTS_PER_STEP + k).astype(F32)
        gate = jnp.where(r[:, 0:1] == ef, r[:, 2:3], 0.0) + jnp.where(r[:, 1:2] == ef, r[:, 3:4], 0.0)
        hdn = a * (1.0 / (1.0 + jnp.exp(-a))) * u * gate
        y = _dot(_mx(hdn), wd_ref[k])
        upd = y if upd is None else upd + y
    o_ref[...] += upd


def _moe_dense(x, t, route, w_gate, w_up, w_down, tm=1536):
    n = x.shape[0]
    tm = _tile(n, tm)
    wgu = _mx(jnp.concatenate([w_gate.reshape(N_EXPERTS, D_MODEL, EXPERT_FF),
                               w_up.reshape(N_EXPERTS, D_MODEL, EXPERT_FF)], axis=-1))
    wd = _mx(w_down.reshape(N_EXPERTS, EXPERT_FF, D_MODEL))
    row = lambda i, e: (i, 0)
    once = pl.Buffered(1)
    return pl.pallas_call(
        _moe_dense_kernel,
        grid=(n // tm, N_EXPERTS // EXPERTS_PER_STEP),
        in_specs=[pl.BlockSpec((tm, D_MODEL), row, pipeline_mode=once),
                  pl.BlockSpec((tm, D_MODEL), row, pipeline_mode=once),
                  pl.BlockSpec((tm, LANES), row, pipeline_mode=once),
                  pl.BlockSpec((EXPERTS_PER_STEP, D_MODEL, 2 * EXPERT_FF), lambda i, e: (e, 0, 0)),
                  pl.BlockSpec((EXPERTS_PER_STEP, EXPERT_FF, D_MODEL), lambda i, e: (e, 0, 0))],
        out_specs=pl.BlockSpec((tm, D_MODEL), row),
        out_shape=jax.ShapeDtypeStruct((n, D_MODEL), F32),
        compiler_params=_cparams(("parallel", "arbitrary")),
        name="moe_dense",
    )(x, t, route, wgu, wd)


def _decode_tables(rel_bias, past, t_new, lk):
    t8 = np.arange(T8)
    lane = np.arange(LANES)
    far = np.full((T8, LANES), REL_BUCKETS - 1, np.int32)
    d_last = (past + t8[:, None]) - (past - LANES + lane[None, :])
    d_new = t8[:, None] - lane[None, :]
    ok_new = (d_new >= 0) & (lane[None, :] < t_new)
    ta = _bias_expand(rel_bias, np.stack([far, _t5_bucket_np(d_last), _t5_bucket_np(d_new)]), 0, A_HEADS)
    ta = jnp.where(jnp.asarray(ok_new)[None, None] | (jnp.arange(3) < 2)[None, :, None, None], ta, NEG)
    taba = jnp.broadcast_to(jnp.transpose(ta, (1, 0, 2, 3))[:, :, None], (3, A_HEADS, 2, T8, LANES))
    taba = taba.reshape(3, A_HEADS * 2 * T8, LANES)
    kpos = np.arange(lk)
    d_b = (past + t8[:, None]) - kpos[None, :]
    tabb = _bias_expand(rel_bias, _t5_bucket_np(d_b), A_HEADS, B_HEADS).reshape(B_HEADS * T8, lk)
    valid = (kpos[None, :] < past) | ((d_b >= 0) & (kpos[None, :] < past + t_new))
    validb = jnp.asarray(np.where(valid, 0.0, NEG), F32)
    return taba, tabb, validb


def _pad_rows(x, rows):
    pad = [(0, 0)] * x.ndim
    pad[1] = (0, rows - x.shape[1])
    return jnp.pad(x, pad)


def kernel(x_prompt, x_sample, cache_a_k, cache_a_v, cache_b_k, cache_b_v, cache_b_kidx, cache_c_k, cache_c_v, cache_c_logf, page_table, rel_bias, ab_norm, ab_w_in, a_q_norm, a_k_norm, b_q_norm, b_k_norm, a_lambda_q1, a_lambda_k1, a_lambda_q2, a_lambda_k2, a_sub_norm, ab_w_out, c_norm, c_w_in, c_forget_bias, c_q_norm, c_k_norm, c_w_out, ffn_norm, moe_w_group, moe_b_group, moe_w_router, moe_b_router, moe_w_gate, moe_w_up, moe_w_down):
    nb, s, d = x_prompt.shape
    db, ts, _ = x_sample.shape
    npg = page_table.shape[1]
    pool, ps = cache_a_k.shape[1], cache_a_k.shape[2]
    past = npg * ps
    n_p = nb * s
    n_s = db * ts
    depth = ffn_norm.shape[0]
    topk_p = min(IDX_TOPK_MAX, s // 4)
    topk_s = min(IDX_TOPK_MAX, (past + ts) // 4)
    lk = past + LANES

    cak_t = jnp.transpose(cache_a_k, (0, 1, 3, 4, 5, 2)).reshape(-1, pool, 2 * A_HEADS * HEAD_DIM, ps)
    cav_r = cache_a_v.reshape(-1, pool, ps * A_HEADS, 2 * HEAD_DIM)
    cbk_t = jnp.transpose(cache_b_k, (0, 1, 3, 2))
    cbv_t = jnp.transpose(cache_b_v, (0, 1, 3, 2))
    cbi_t = jnp.transpose(cache_b_kidx, (0, 1, 3, 2))
    cck_t = jnp.transpose(cache_c_k, (0, 1, 3, 4, 2)).reshape(-1, pool, C_HEADS * HEAD_DIM, ps)
    ccv_t = jnp.transpose(cache_c_v, (0, 1, 3, 4, 2)).reshape(-1, pool, C_HEADS * HEAD_DIM, ps)
    ccf_t = jnp.transpose(cache_c_logf, (0, 1, 3, 2))

    x = jnp.concatenate([x_prompt.reshape(n_p, d), x_sample.reshape(n_s, d)], axis=0)
    outs_p = {k: [] for k in ("ak", "av", "bk", "bv", "bi", "ck", "cv", "cf")}
    outs_s = {k: [] for k in ("ak", "av", "bk", "bv", "bi", "ck", "cv", "cf")}

    def smp(a):
        return a[n_p:].reshape(db, ts, a.shape[1])

    for layer in range(depth):
        router = _router_params(ffn_norm[layer], moe_w_group[layer], moe_b_group[layer],
                                moe_w_router[layer], moe_b_router[layer])
        if layer % 2 == 0:
            e = layer // 2
            lam_init = 0.8 - 0.6 * math.exp(-0.3 * layer)
            lamv = jnp.stack([a_lambda_q1[e], a_lambda_k1[e], a_lambda_q2[e], a_lambda_k2[e]])
            gsub = a_sub_norm[e][None, :]
            (qa, ka_s, kat, katbf, qb, small, smallt, kk, vv, kiki, va, vabf, qi) = _proj0(
                x, ab_norm[e], ab_w_in[e], a_q_norm[e], a_k_norm[e], b_q_norm[e], b_k_norm[e], n_p, s)
            tq_a = min(FLASH_TQ, s)
            bias_a = _bias_expand(rel_bias, _prompt_bucket_tiles(tq_a), 0, A_HEADS)
            oa_p = _attn_a_prompt(qa, katbf, vabf, lamv, gsub, bias_a, nb, s, lam_init, tq=tq_a)
            tq_b = 256
            bias_b = _bias_expand(rel_bias, _prompt_bucket_tiles(tq_b), A_HEADS, B_HEADS)
            ob_p = _attn_b_prompt(qb, qi, small, kk, vv, kiki, bias_b, nb, s, topk_p, tq=tq_b)
            qa_s = _pad_rows(smp(qa), T8)
            hc = np.arange(2 * A_HEADS)
            colmask = jnp.asarray((np.arange(qa_s.shape[2])[None, :] // HEAD_DIM) == hc[:, None], MXU_DT)
            qa_bd = (qa_s[:, None, :, :] * colmask[None, :, None, :]).reshape(db, 2 * A_HEADS * T8, -1)

            def heads_rows(a, nh):
                a = _pad_rows(a, T8).reshape(db, T8, nh, HEAD_DIM)
                return jnp.transpose(a, (0, 2, 1, 3)).reshape(db, nh * T8, HEAD_DIM)

            qi64 = heads_rows(smp(qi), IDX_HEADS)
            qb64 = heads_rows(smp(qb), B_HEADS)
            small_s = smp(small)
            w8 = _pad_rows(small_s[:, :, 192:200], T8)
            ka_s3 = ka_s.reshape(db, ts, -1)
            va_s3 = smp(va)
            kna = _pad_rows(ka_s3, 16)
            vna = _pad_rows(va_s3, 16)
            bnew_t = jnp.transpose(_pad_rows(small_s[:, :, :192], LANES).reshape(db, LANES, 3, HEAD_DIM), (0, 2, 3, 1))
            taba, tabb, validb = _decode_tables(rel_bias, past, ts, lk)
            oa_d, ob_d = _decode0(page_table, lamv, gsub, qa_bd, qi64, qb64, w8, kna, vna, bnew_t,
                                  cak_t, cav_r, cbk_t, cbv_t, cbi_t, taba, tabb, validb, e, topk_s, lam_init)
            oa_s = jnp.transpose(oa_d[:, :, :ts], (0, 2, 1, 3)).reshape(n_s, A_HEADS * LANES)
            ob_s = jnp.transpose(ob_d[:, :, :ts], (0, 2, 1, 3)).reshape(n_s, B_HEADS * HEAD_DIM)
            w_out = ab_w_out[e]
            x, t_moe, route = _outproj(x, [(oa_p, _mx(oa_s), w_out[:A_HEADS * LANES]),
                                           (ob_p, _mx(ob_s), w_out[A_HEADS * LANES:])], n_p, router)
            outs_p["ak"].append(jnp.transpose(kat.reshape(nb, A_HEADS, 2, HEAD_DIM, s), (0, 4, 1, 2, 3)))
            outs_p["av"].append(va[:n_p].reshape(nb, s, A_HEADS, 2 * HEAD_DIM))
            outs_p["bk"].append(jnp.transpose(smallt[:, 0:64], (0, 2, 1)))
            outs_p["bv"].append(jnp.transpose(smallt[:, 64:128], (0, 2, 1)))
            outs_p["bi"].append(jnp.transpose(smallt[:, 128:192], (0, 2, 1)))
            outs_s["ak"].append(ka_s3.reshape(db, ts, A_HEADS, 2, HEAD_DIM))
            outs_s["av"].append(va_s3.reshape(db, ts, A_HEADS, 2 * HEAD_DIM))
            outs_s["bk"].append(small_s[:, :, 0:64])
            outs_s["bv"].append(small_s[:, :, 64:128])
            outs_s["bi"].append(small_s[:, :, 128:192])
        else:
            o = layer // 2
            q, k_s, kt, ktbf, v_s, vt, vbf, logf128, logft = _proj1(
                x, c_norm[o], c_w_in[o], c_forget_bias[o], c_q_norm[o], c_k_norm[o], n_p, s)
            logf = logf128[:, :C_HEADS]
            cum_p = _cumsum_prompt(logf128[:n_p], nb, s)[:, :C_HEADS]
            ckt = jnp.transpose(cum_p.reshape(nb, s, C_HEADS // 2, 2), (0, 2, 3, 1))
            oc_p = _attn_c_prompt(q, ktbf, vbf, cum_p, ckt, nb, s, tq=min(FLASH_TQ, s))
            logf_s = smp(logf)
            run = jnp.zeros_like(logf_s[:, 0])
            c_rows = []
            for t in range(ts):
                run = run + logf_s[:, t]
                c_rows.append(run)
            c_new = jnp.stack(c_rows, axis=1)
            cq = jnp.transpose(_pad_rows(c_new, T8), (0, 2, 1)).reshape(db, C_HEADS * T8, 1)
            cnew_t = jnp.transpose(_pad_rows(c_new, 16), (0, 2, 1))
            q_s = _pad_rows(smp(q), T8)
            hmask = jnp.asarray((np.arange(q_s.shape[2])[None, :] // HEAD_DIM) == np.arange(C_HEADS)[:, None], MXU_DT)
            q_bd = (q_s[:, None, :, :] * hmask[None, :, None, :]).reshape(db, C_HEADS * T8, -1)
            k_s3 = k_s.reshape(db, ts, -1)
            v_s3 = v_s.reshape(db, ts, -1)
            kn = _pad_rows(k_s3, 16)
            vn = _pad_rows(v_s3, 16)
            t8 = np.arange(T8)
            okn = (t8[:, None] >= np.arange(16)[None, :]) & (np.arange(16)[None, :] < ts)
            maskn = jnp.asarray(np.tile(np.where(okn, 0.0, NEG), (C_HEADS, 1)), F32)
            oc_d = _decode1(page_table, q_bd, cq, kn, vn, cnew_t, cck_t, ccv_t, ccf_t, maskn, o)
            oc_s = jnp.transpose(oc_d[:, :, :ts], (0, 2, 1, 3)).reshape(n_s, C_HEADS * HEAD_DIM)
            x, t_moe, route = _outproj(x, [(oc_p, _mx(oc_s), c_w_out[o])], n_p, router)
            outs_p["ck"].append(jnp.transpose(kt.reshape(nb, C_HEADS, HEAD_DIM, s), (0, 3, 1, 2)))
            outs_p["cv"].append(jnp.transpose(vt.reshape(nb, C_HEADS, HEAD_DIM, s), (0, 3, 1, 2)))
            outs_p["cf"].append(jnp.transpose(logft[:, :C_HEADS], (0, 2, 1)))
            outs_s["ck"].append(k_s3.reshape(db, ts, C_HEADS, HEAD_DIM))
            outs_s["cv"].append(v_s3.reshape(db, ts, C_HEADS, HEAD_DIM))
            outs_s["cf"].append(logf_s)
        x = _moe_dense(x, t_moe, route, moe_w_gate[layer], moe_w_up[layer], moe_w_down[layer])

    keys = ("ak", "av", "bk", "bv", "bi", "ck", "cv", "cf")
    return ((x[:n_p].reshape(nb, s, d), x[n_p:].reshape(db, ts, d))
            + tuple(jnp.stack(outs_p[k]) for k in keys)
            + tuple(jnp.stack(outs_s[k]) for k in keys))
```

```python
import functools
import math

import numpy as np
import jax
import jax.numpy as jnp
from jax import lax
from jax.experimental import pallas as pl
from jax.experimental.pallas import tpu as pltpu

F32 = jnp.float32
I32 = jnp.int32
MXU_DT = jnp.bfloat16

D_MODEL = 1024
HEAD_DIM = 64
A_HEADS = 4
B_HEADS = 8
IDX_HEADS = 8
IDX_DIM = 64
IDX_TOPK_MAX = 256
C_HEADS = 16
REL_BUCKETS = 32
REL_MAX_DIST = 128
N_GROUPS = 4
EXPERTS_PER_GROUP = 8
N_EXPERTS = N_GROUPS * EXPERTS_PER_GROUP
EXPERT_FF = 256
EPS = 1e-6
NEG = -1e30
INT_MIN = -2 ** 31
QK_SCALE = HEAD_DIM ** -0.5
IDX_SCALE = IDX_DIM ** -0.5
IDX_HEAD_SCALE = IDX_HEADS ** -0.5
LANES = 128
T8 = 8
VMEM_LIMIT = 56 * 1024 * 1024
PAGES_PER_STEP = 16
FLASH_TQ = 512
FLASH_WIDE = 2
EXPERTS_PER_STEP = 4

def _tile(n, pref):
    best = 16
    for t in range(16, pref + 1, 16):
        if n % t == 0:
            best = t
    assert n % best == 0, (n, pref)
    return best


def _cparams(sem):
    return pltpu.CompilerParams(dimension_semantics=sem, vmem_limit_bytes=VMEM_LIMIT)


def _mx(x):
    return x.astype(MXU_DT)


def _dot(a, b):
    return jnp.dot(a, b, preferred_element_type=F32)


def _dot_t(a, b):
    return lax.dot_general(a, b, (((1,), (1,)), ((), ())), preferred_element_type=F32)


def _split2(x):
    hi = _mx(x)
    lo = _mx(x - hi.astype(F32))
    return hi, lo


def _split3(x):
    hi = _mx(x)
    r = x - hi.astype(F32)
    mid = _mx(r)
    lo = _mx(r - mid.astype(F32))
    return hi, mid, lo


def _lane_iota(shape):
    return lax.broadcasted_iota(I32, shape, len(shape) - 1)


def _row_iota(shape):
    return lax.broadcasted_iota(I32, shape, len(shape) - 2)


def _lanes(x, width):
    if width <= LANES:
        return x[:, :width]
    return jnp.tile(x, (1, width // LANES))


def _rms_rows(x, g):
    ms = jnp.mean(x * x, axis=-1, keepdims=True)
    return x * lax.rsqrt(ms + EPS) * g


def _seg_rsqrt(y, bd):
    hi, lo = _split2(y * y)
    ss = _dot(hi, bd) + _dot(lo, bd)
    return lax.rsqrt(ss * (1.0 / HEAD_DIM) + EPS)


def _block_diag_ones(n, seg):
    r = np.arange(n)
    return jnp.asarray((r[:, None] // seg) == (r[None, :] // seg), dtype=MXU_DT)


def _lambda(lamv_ref, lam_init):
    lv = lamv_ref[...]
    return (jnp.exp(jnp.sum(lv[0:1] * lv[1:2], axis=-1, keepdims=True))
            - jnp.exp(jnp.sum(lv[2:3] * lv[3:4], axis=-1, keepdims=True)) + lam_init)


def _t5_bucket_np(d):
    n = np.maximum(d, 0)
    exact = REL_BUCKETS // 2
    nf = np.maximum(n, 1).astype(np.float64)
    large = exact + (np.log(nf / exact) / math.log(REL_MAX_DIST / exact) * (REL_BUCKETS - exact)).astype(np.int64)
    large = np.minimum(large, REL_BUCKETS - 1)
    return np.where(n < exact, n, large).astype(np.int32)


def _bias_expand_kernel(relb_ref, idx_ref, o_ref, *, head0):
    h = pl.program_id(0) + head0
    idx = idx_ref[...]
    acc = jnp.zeros(idx.shape, F32)
    for b in range(REL_BUCKETS):
        acc = jnp.where(idx == b, relb_ref[b, h], acc)
    o_ref[...] = acc


def _bias_expand(rel_bias, idx_np, head0, n_heads):
    idx = jnp.asarray(idx_np, I32)
    nd = idx.ndim
    zeros = (0,) * nd
    return pl.pallas_call(
        functools.partial(_bias_expand_kernel, head0=head0),
        grid=(n_heads,),
        in_specs=[pl.BlockSpec(memory_space=pltpu.SMEM),
                  pl.BlockSpec(idx.shape, lambda h: zeros)],
        out_specs=pl.BlockSpec((None,) + idx.shape, lambda h: (h,) + zeros),
        out_shape=jax.ShapeDtypeStruct((n_heads,) + idx.shape, F32),
        compiler_params=_cparams(("arbitrary",)),
        name="bias_expand",
    )(rel_bias, idx)


def _prompt_bucket_tiles(t):
    r = np.arange(t)
    d = r[:, None] - r[None, :]
    return np.stack([_t5_bucket_np(d), _t5_bucket_np(d + t), np.full((t, t), REL_BUCKETS - 1, np.int32)])


def _store_t(y, is_prompt, t_refs, row_ref, c):
    if is_prompt:
        yt = jnp.transpose(y)
        for r in t_refs:
            r[c * 256:(c + 1) * 256, :] = yt.astype(r.dtype)
    elif row_ref is not None:
        row_ref[:, c * 256:(c + 1) * 256] = y


def _per_tile_kind(body, n_prompt_tiles):
    prompt_tile = pl.program_id(0) < n_prompt_tiles

    @pl.when(prompt_tile)
    def _():
        body(True)

    @pl.when(jnp.logical_not(prompt_tile))
    def _():
        body(False)


def _proj0_kernel(*refs, n_prompt_tiles):
    _per_tile_kind(functools.partial(_proj0_body, *refs), n_prompt_tiles)


def _proj0_body(x_ref, g_ref, w_ref, gain_ref, bd_ref,
                qa_ref, ka_ref, kat_ref, katbf_ref, qb_ref, small_ref, smallt_ref, kk_ref, vv_ref, kiki_ref,
                va_ref, vabf_ref, qi_ref, is_prompt):
    h = _mx(_rms_rows(x_ref[...], g_ref[...]))
    bd = bd_ref[...]

    def chunk(c):
        return _dot(h, w_ref[:, c * 256:(c + 1) * 256])

    def normed(c):
        y = chunk(c)
        return y * _seg_rsqrt(y, bd) * gain_ref[:, c * 256:(c + 1) * 256]

    for c in range(2):
        qa_ref[:, c * 256:(c + 1) * 256] = _mx(normed(c) * QK_SCALE)
    for c in range(2):
        _store_t(normed(2 + c), is_prompt, (kat_ref, katbf_ref), ka_ref, c)
    for c in range(2):
        qb_ref[:, c * 256:(c + 1) * 256] = _mx(normed(4 + c) * QK_SCALE)
    y = chunk(6)
    yn = y * _seg_rsqrt(y, bd) * gain_ref[:, 6 * 256:7 * 256]
    lane = _lane_iota(y.shape)
    y = jnp.where(lane < HEAD_DIM, yn, y)
    small_ref[...] = y
    _store_t(y, is_prompt, (smallt_ref,), None, 0)
    t0 = y[:, :LANES]
    t1 = y[:, LANES:]
    lo = _lane_iota(t0.shape) < HEAD_DIM
    r0 = pltpu.roll(t0, HEAD_DIM, 1)
    kk_ref[...] = _mx(jnp.where(lo, t0, r0))
    vv_ref[...] = _mx(jnp.where(lo, r0, t0))
    r1 = pltpu.roll(t1, HEAD_DIM, 1)
    kiki_ref[...] = _mx(jnp.where(lo, t1, r1))
    for c in range(2):
        y = chunk(7 + c)
        va_ref[:, c * 256:(c + 1) * 256] = y
        vabf_ref[:, c * 256:(c + 1) * 256] = _mx(y)
    for c in range(2):
        qi_ref[:, c * 256:(c + 1) * 256] = _mx(chunk(9 + c) * IDX_SCALE)


def _token_specs(n_p, n_s, s, tm):
    npt = n_p // tm
    tps = s // tm

    def t_map(i):
        ip = jnp.minimum(i, npt - 1)
        return (ip // tps, 0, ip % tps)

    def s_map(i):
        return (jnp.maximum(i - npt, 0), 0)

    return npt, t_map, s_map


def _proj0(x, g, w_in, a_qn, a_kn, b_qn, b_kn, n_p, s, tm=512):
    n = x.shape[0]
    n_s = n - n_p
    nb = n_p // s
    tm = _tile(math.gcd(n_p, n_s, s), tm)
    npt, t_map, s_map = _token_specs(n_p, n_s, s, tm)
    sp = np.cumsum([512, 512, 512, 512, 64, 64, 512, 64, 8])[:-1]
    wqa, wka, wva, wqb, wkb, wvb, wqi, wki, wwi = jnp.split(w_in, sp, axis=1)
    w = jnp.concatenate([wqa, wka, wqb, wkb, wvb, wki, wwi, jnp.zeros((D_MODEL, 56), F32), wva, wqi], axis=1)
    w = _mx(w)
    ncol = w.shape[1]
    gain = jnp.concatenate([jnp.tile(a_qn, 8), jnp.tile(a_kn, 8), jnp.tile(b_qn, 8), b_kn,
                            jnp.ones((192,), F32)])[None, :]
    bd = _block_diag_ones(256, HEAD_DIM)
    row = lambda i: (i, 0)
    fixed = lambda i: (0, 0)
    def rows(wd, dt):
        return pl.BlockSpec((tm, wd), row), jax.ShapeDtypeStruct((n, wd), dt)

    def rows_s(wd, dt):
        return pl.BlockSpec((tm, wd), s_map), jax.ShapeDtypeStruct((n_s, wd), dt)

    def cols_p(wd, dt):
        return pl.BlockSpec((None, wd, tm), t_map), jax.ShapeDtypeStruct((nb, wd, s), dt)

    outs = [rows(512, MXU_DT), rows_s(512, F32), cols_p(512, F32), cols_p(512, MXU_DT), rows(512, MXU_DT),
            rows(256, F32), cols_p(256, F32), rows(128, MXU_DT), rows(128, MXU_DT), rows(128, MXU_DT),
            rows(512, F32), rows(512, MXU_DT), rows(512, MXU_DT)]
    return pl.pallas_call(
        functools.partial(_proj0_kernel, n_prompt_tiles=npt),
        grid=(n // tm,),
        in_specs=[pl.BlockSpec((tm, D_MODEL), row), pl.BlockSpec((1, D_MODEL), fixed),
                  pl.BlockSpec((D_MODEL, ncol), fixed), pl.BlockSpec((1, gain.shape[1]), fixed),
                  pl.BlockSpec((256, 256), fixed)],
        out_specs=[o[0] for o in outs],
        out_shape=[o[1] for o in outs],
        compiler_params=_cparams(("arbitrary",)),
        name="proj0",
    )(x, g[None, :], w, gain, bd)


def _proj1_kernel(*refs, n_prompt_tiles):
    _per_tile_kind(functools.partial(_proj1_body, *refs), n_prompt_tiles)


def _proj1_body(x_ref, g_ref, w_ref, gain_ref, bf_ref, bd_ref,
                q_ref, k_ref, kt_ref, ktbf_ref, v_ref, vt_ref, vbf_ref, logf_ref, logft_ref, is_prompt):
    h = _mx(_rms_rows(x_ref[...], g_ref[...]))
    bd = bd_ref[...]

    def chunk(c):
        return _dot(h, w_ref[:, c * 256:(c + 1) * 256])

    def normed(c):
        y = chunk(c)
        return y * _seg_rsqrt(y, bd) * gain_ref[:, c * 256:(c + 1) * 256]

    for c in range(4):
        q_ref[:, c * 256:(c + 1) * 256] = _mx(normed(c) * QK_SCALE)
    for c in range(4):
        _store_t(normed(4 + c), is_prompt, (kt_ref, ktbf_ref), k_ref, c)
    for c in range(4):
        y = chunk(8 + c)
        vbf_ref[:, c * 256:(c + 1) * 256] = _mx(y)
        _store_t(y, is_prompt, (vt_ref,), v_ref, c)
    f = _dot(h, w_ref[:, 12 * 256:12 * 256 + LANES]) + bf_ref[...]
    logf = jnp.minimum(f, 0.0) - jnp.log(1.0 + jnp.exp(-jnp.abs(f)))
    logf_ref[...] = logf
    if is_prompt:
        logft_ref[...] = jnp.transpose(logf)


def _proj1(x, g, w_in, b_f, qn, kn, n_p, s, tm=512):
    n = x.shape[0]
    n_s = n - n_p
    nb = n_p // s
    tm = _tile(math.gcd(n_p, n_s, s), tm)
    npt, t_map, s_map = _token_specs(n_p, n_s, s, tm)
    w = _mx(jnp.concatenate([w_in, jnp.zeros((D_MODEL, LANES - C_HEADS), F32)], axis=1))
    ncol = w.shape[1]
    gain = jnp.concatenate([jnp.tile(qn, C_HEADS), jnp.tile(kn, C_HEADS)])[None, :]
    bf = jnp.concatenate([b_f, jnp.zeros((LANES - C_HEADS,), F32)])[None, :]
    bd = _block_diag_ones(256, HEAD_DIM)
    row = lambda i: (i, 0)
    fixed = lambda i: (0, 0)
    def rows(wd, dt):
        return pl.BlockSpec((tm, wd), row), jax.ShapeDtypeStruct((n, wd), dt)

    def rows_s(wd, dt):
        return pl.BlockSpec((tm, wd), s_map), jax.ShapeDtypeStruct((n_s, wd), dt)

    def cols_p(wd, dt):
        return pl.BlockSpec((None, wd, tm), t_map), jax.ShapeDtypeStruct((nb, wd, s), dt)

    outs = [rows(1024, MXU_DT), rows_s(1024, F32), cols_p(1024, F32), cols_p(1024, MXU_DT),
            rows_s(1024, F32), cols_p(1024, F32), rows(1024, MXU_DT), rows(LANES, F32), cols_p(LANES, F32)]
    return pl.pallas_call(
        functools.partial(_proj1_kernel, n_prompt_tiles=npt),
        grid=(n // tm,),
        in_specs=[pl.BlockSpec((tm, D_MODEL), row), pl.BlockSpec((1, D_MODEL), fixed),
                  pl.BlockSpec((D_MODEL, ncol), fixed), pl.BlockSpec((1, gain.shape[1]), fixed),
                  pl.BlockSpec((1, LANES), fixed), pl.BlockSpec((256, 256), fixed)],
        out_specs=[o[0] for o in outs],
        out_shape=[o[1] for o in outs],
        compiler_params=_cparams(("arbitrary",)),
        name="proj1",
    )(x, g[None, :], w, gain, bf, bd)


def _outproj_kernel(*refs, n_in, n_prompt_tiles):
    res_ref = refs[0]
    g_ref, whi_ref, wlo_ref, b_ref = refs[1 + 3 * n_in:5 + 3 * n_in]
    out_ref, t_ref, route_ref = refs[5 + 3 * n_in:]

    def body(is_prompt):
        acc = res_ref[...]
        for i in range(n_in):
            a_ref = refs[1 + 3 * i] if is_prompt else refs[2 + 3 * i]
            acc = acc + _dot(a_ref[...], refs[3 + 3 * i][...])
        out_ref[...] = acc
        t, route = _route_rows(acc, g_ref[...], whi_ref[...], wlo_ref[...], b_ref[...])
        t_ref[...] = t
        route_ref[...] = route

    _per_tile_kind(body, n_prompt_tiles)


def _outproj(res, terms, n_p, router, tm=512):
    n = res.shape[0]
    n_s = n - n_p
    tm = _tile(math.gcd(n_p, n_s), tm)
    npt = n_p // tm
    row = lambda i: (i, 0)
    fixed = lambda i: (0, 0)
    p_map = lambda i: (jnp.minimum(i, npt - 1), 0)
    s_map = lambda i: (jnp.maximum(i - npt, 0), 0)
    in_specs = [pl.BlockSpec((tm, D_MODEL), row)]
    args = [res]
    for a_p, a_s, w in terms:
        in_specs += [pl.BlockSpec((tm, a_p.shape[1]), p_map), pl.BlockSpec((tm, a_s.shape[1]), s_map),
                     pl.BlockSpec(w.shape, fixed)]
        args += [a_p, a_s, _mx(w)]
    in_specs += [pl.BlockSpec(r.shape, fixed) for r in router]
    args += list(router)
    return pl.pallas_call(
        functools.partial(_outproj_kernel, n_in=len(terms), n_prompt_tiles=npt),
        grid=(n // tm,),
        in_specs=in_specs,
        out_specs=[pl.BlockSpec((tm, D_MODEL), row), pl.BlockSpec((tm, D_MODEL), row),
                   pl.BlockSpec((tm, LANES), row)],
        out_shape=[jax.ShapeDtypeStruct((n, D_MODEL), F32), jax.ShapeDtypeStruct((n, D_MODEL), MXU_DT),
                   jax.ShapeDtypeStruct((n, LANES), F32)],
        compiler_params=_cparams(("parallel",)),
        name="outproj",
    )(*args)


def _stack_streams(q):
    lane = _lane_iota(q.shape)
    zero = jnp.zeros_like(q)
    return jnp.concatenate([jnp.where(lane < HEAD_DIM, q, zero), jnp.where(lane >= HEAD_DIM, q, zero)], axis=0)


def _flash_stacked(q2, kt_ref, v_ref, i, tq, tk, bias_fn, m_ref, l_ref, acc_ref):
    rows = 2 * tq
    m_ref[...] = jnp.full(m_ref.shape, NEG, F32)
    l_ref[...] = jnp.zeros(l_ref.shape, F32)
    acc_ref[...] = jnp.zeros(acc_ref.shape, F32)

    def tile(j, n_sub, masked):
        width = n_sub * tk
        off = pl.multiple_of(j * tk, tk)
        s = _dot(q2, kt_ref[:, pl.ds(off, width)]) + bias_fn(j, off, n_sub)
        if masked:
            r = _row_iota((rows, width))
            s = jnp.where(_lane_iota((rows, width)) <= jnp.where(r >= tq, r - tq, r), s, NEG)
        alpha, p = _online_update(s, m_ref, l_ref)
        acc_ref[...] = alpha * acc_ref[...] + _dot(_mx(p), v_ref[pl.ds(off, width), :])

    def body(jj, carry):
        tile(jj * FLASH_WIDE, FLASH_WIDE, False)
        return carry

    lax.fori_loop(0, i // FLASH_WIDE, body, 0)
    for u in range(1, FLASH_WIDE):
        @pl.when(i % FLASH_WIDE >= u)
        def _():
            tile(i - i % FLASH_WIDE + (u - 1), 1, False)
    tile(i, 1, True)


def _attn_a_kernel(lamv_ref, gsub_ref, q_ref, k_ref, v_ref, bias_ref, o_ref,
                   m_ref, l_ref, acc_ref, *, tq, tk, lam_init):
    i = pl.program_id(2)

    def bias_fn(j, off, n_sub):
        b = jnp.concatenate([bias_ref[jnp.minimum(i - j - u, 2)] for u in range(n_sub)], axis=1)
        return jnp.concatenate([b, b], axis=0)

    _flash_stacked(_stack_streams(q_ref[...]), k_ref, v_ref, i, tq, tk, bias_fn, m_ref, l_ref, acc_ref)
    lam = _lambda(lamv_ref, lam_init)
    o = acc_ref[:tq] / l_ref[:tq] - lam * (acc_ref[tq:] / l_ref[tq:])
    o = _rms_rows(o, gsub_ref[...]) * (1.0 - lam_init)
    o_ref[...] = _mx(o)


def _attn_a_prompt(qa, ka, va, lamv, gsub, bias_tiles, nb, s, lam_init, tq=256):
    tk = tq
    nq = s // tq
    kern = functools.partial(_attn_a_kernel, tq=tq, tk=tk, lam_init=lam_init)
    return pl.pallas_call(
        kern,
        grid=(nb, A_HEADS, nq),
        in_specs=[pl.BlockSpec((4, HEAD_DIM), lambda b, h, i: (0, 0)),
                  pl.BlockSpec((1, LANES), lambda b, h, i: (0, 0)),
                  pl.BlockSpec((tq, LANES), lambda b, h, i: (b * nq + i, h)),
                  pl.BlockSpec((None, LANES, s), lambda b, h, i: (b, h, 0)),
                  pl.BlockSpec((s, LANES), lambda b, h, i: (b, h)),
                  pl.BlockSpec((None, 3, tq, tk), lambda b, h, i: (h, 0, 0, 0))],
        out_specs=pl.BlockSpec((tq, LANES), lambda b, h, i: (b * nq + i, h)),
        out_shape=jax.ShapeDtypeStruct((nb * s, A_HEADS * LANES), MXU_DT),
        scratch_shapes=[pltpu.VMEM((2 * tq, LANES), F32), pltpu.VMEM((2 * tq, LANES), F32),
                        pltpu.VMEM((2 * tq, LANES), F32)],
        compiler_params=_cparams(("parallel", "parallel", "arbitrary")),
        name="attn_a_prompt",
    )(lamv, gsub, qa, ka, va, bias_tiles)


def _attn_c_kernel(q_ref, k_ref, v_ref, cq_ref, ckt_ref, o_ref, m_ref, l_ref, acc_ref, *, tq, tk):
    hp = pl.program_id(1)
    i = pl.program_id(2)
    cq_tile = cq_ref[...]
    lane16 = _lane_iota(cq_tile.shape)
    cq2 = jnp.concatenate(
        [jnp.broadcast_to(jnp.sum(jnp.where(lane16 == 2 * hp + c, cq_tile, 0.0), axis=-1, keepdims=True),
                          (tq, LANES)) for c in range(2)], axis=0)

    def bias_fn(j, off, n_sub):
        width = n_sub * tk
        ck = ckt_ref[:, pl.ds(off, width)]
        ck2 = jnp.concatenate([jnp.broadcast_to(ck[0:1], (tq, width)),
                               jnp.broadcast_to(ck[1:2], (tq, width))], axis=0)
        return _lanes(cq2, width) - ck2

    _flash_stacked(_stack_streams(q_ref[...]), k_ref, v_ref, i, tq, tk, bias_fn, m_ref, l_ref, acc_ref)
    lane = _lane_iota((tq, LANES))
    o = jnp.where(lane < HEAD_DIM, acc_ref[:tq] / l_ref[:tq], acc_ref[tq:] / l_ref[tq:])
    o_ref[...] = _mx(o)


def _attn_c_prompt(q, k, v, cq, ckt, nb, s, tq=256):
    tk = tq
    nq = s // tq
    nhp = C_HEADS // 2
    kern = functools.partial(_attn_c_kernel, tq=tq, tk=tk)
    return pl.pallas_call(
        kern,
        grid=(nb, nhp, nq),
        in_specs=[pl.BlockSpec((tq, LANES), lambda b, h, i: (b * nq + i, h)),
                  pl.BlockSpec((None, LANES, s), lambda b, h, i: (b, h, 0)),
                  pl.BlockSpec((s, LANES), lambda b, h, i: (b, h)),
                  pl.BlockSpec((tq, C_HEADS), lambda b, h, i: (b * nq + i, 0)),
                  pl.BlockSpec((None, None, 2, s), lambda b, h, i: (b, h, 0, 0))],
        out_specs=pl.BlockSpec((tq, LANES), lambda b, h, i: (b * nq + i, h)),
        out_shape=jax.ShapeDtypeStruct((nb * s, C_HEADS * HEAD_DIM), MXU_DT),
        scratch_shapes=[pltpu.VMEM((2 * tq, LANES), F32), pltpu.VMEM((2 * tq, LANES), F32),
                        pltpu.VMEM((2 * tq, LANES), F32)],
        compiler_params=_cparams(("parallel", "parallel", "arbitrary")),
        name="attn_c_prompt",
    )(q, k, v, cq, ckt)


def _score_keys(score):
    score = jnp.where(score == 0.0, 0.0, score)
    bits = pltpu.bitcast(score, I32)
    return bits ^ (jnp.right_shift(bits, 31) & 0x7FFFFFFF)


def _topk_select(keys_ref, width, kcount, active, col):
    kf = float(kcount)
    nbits_col = int(width - 1).bit_length()

    def count(pred):
        return jnp.sum(jnp.where(pred, 1.0, 0.0), axis=1, keepdims=True)

    c0 = count(keys_ref[:, :width] >= 0)
    t0 = jnp.where(c0 >= kf, 0, INT_MIN).astype(I32)
    n0 = jnp.where(c0 >= kf, c0, float(width))

    def body(it, carry):
        t, n_t = carry
        cand = t + jnp.left_shift(jnp.int32(1), 30 - it)
        n_c = count(keys_ref[:, :width] >= cand)
        ok = n_c >= kf
        return jnp.where(ok, cand, t), jnp.where(ok, n_c, n_t)

    def block(bi, carry):
        settled = jnp.max(jnp.where(active, carry[1] - kf, 0.0)) <= 0.0
        lo = bi * SEARCH_BLOCK
        return lax.cond(settled, lambda c: c,
                        lambda c: lax.fori_loop(lo, jnp.minimum(lo + SEARCH_BLOCK, 31), body, c), carry)

    t, _ = lax.fori_loop(0, -(-31 // SEARCH_BLOCK), block, (t0, n0))
    t = jnp.where(active, t, INT_MIN)
    keys = keys_ref[:, :width]
    gt = keys > t
    eq = keys == t
    need = kf - count(gt)
    excess = jnp.where(active, count(eq) - need, 0.0)

    def tie_break():
        def tb(it, jj):
            cand = jj + jnp.left_shift(jnp.int32(1), nbits_col - 1 - it)
            c = count((keys_ref[:, :width] == t) & (col < cand))
            return jnp.where(c < need, cand, jj)
        return lax.fori_loop(0, nbits_col, tb, jnp.zeros(t.shape, I32))

    jmax = lax.cond(jnp.max(excess) > 0.0, tie_break, lambda: jnp.full(t.shape, width, I32))
    return gt | (eq & (col <= jmax))


KEY_OF_NEG_INF = -2139095041
SELECT_UNIT = 2
SEARCH_BLOCK = 4


def _attn_b_kernel(qb_ref, qi_ref, w_ref, kk_ref, vv_ref, kiki_ref, bias_ref, o_ref,
                   keys_ref, selm_ref, m_ref, l_ref, acc_ref, *, tq, s_len, topk):
    i = pl.program_id(1)
    cw = tq
    nh = B_HEADS
    n_chunks = s_len // cw
    lane = _lane_iota((tq, LANES))
    halves = (lane < HEAD_DIM, lane >= HEAD_DIM)

    def stack_heads(ref):
        parts = []
        for h in range(nh):
            t = ref[:, (h // 2) * LANES:(h // 2 + 1) * LANES]
            parts.append(jnp.where(halves[h % 2], t, jnp.zeros_like(t)))
        return jnp.concatenate(parts, axis=0)

    row = _row_iota((tq, cw)) + i * tq
    colc = _lane_iota((tq, cw))

    qi8 = stack_heads(qi_ref)
    wt = w_ref[...] * IDX_HEAD_SCALE
    wcol = jnp.concatenate([jnp.broadcast_to(wt[:, HEAD_DIM + h:HEAD_DIM + h + 1], (tq, LANES))
                            for h in range(nh)], axis=0)

    def score_chunk(j, carry):
        off = pl.multiple_of(j * cw, cw)
        d = jnp.maximum(_dot_t(qi8, kiki_ref[pl.ds(off, cw), :]), 0.0) * _lanes(wcol, cw)
        sc = d[0:tq]
        for h in range(1, nh):
            sc = sc + d[h * tq:(h + 1) * tq]
        keys_ref[:, pl.ds(off, cw)] = _score_keys(jnp.where((colc + j * cw) <= row, sc, -jnp.inf))
        return carry

    lax.fori_loop(0, i + 1, score_chunk, 0)
    unit = SELECT_UNIT if n_chunks % SELECT_UNIT == 0 else 1
    rem = (i + 1) % unit
    for u in range(1, unit):
        @pl.when((rem != 0) & (u <= unit - rem))
        def _():
            off = pl.multiple_of((i + u) * cw, cw)
            keys_ref[:, pl.ds(off, cw)] = jnp.full((tq, cw), KEY_OF_NEG_INF, I32)
    widths = [w * unit * cw for w in range(1, n_chunks // unit + 1)]

    qpos = _row_iota((tq, 1)) + i * tq

    def select_branch(width):
        def br():
            colw = _lane_iota((tq, width))
            causal = colw <= (_row_iota((tq, width)) + i * tq)
            sel = _topk_select(keys_ref, width, topk, qpos >= topk, colw)
            selm_ref[:, :width] = jnp.where(sel & causal, 0.0, NEG)
        return br

    def causal_only():
        col = _lane_iota((tq, s_len))
        selm_ref[...] = jnp.where(col <= (_row_iota((tq, s_len)) + i * tq), 0.0, NEG)

    branch = jnp.where((i + 1) * tq > topk, 1 + i // unit, 0)
    lax.switch(branch, [causal_only] + [select_branch(w) for w in widths])

    qb8 = stack_heads(qb_ref)
    m_ref[...] = jnp.full(m_ref.shape, NEG, F32)
    l_ref[...] = jnp.zeros(l_ref.shape, F32)
    acc_ref[...] = jnp.zeros(acc_ref.shape, F32)

    def attn_chunks(j, n_sub):
        width = n_sub * cw
        off = pl.multiple_of(j * cw, cw)
        bias = jnp.concatenate(
            [jnp.concatenate([bias_ref[h, jnp.minimum(i - j - u, 2)] for h in range(nh)], axis=0)
             for u in range(n_sub)], axis=1)
        selm = selm_ref[:, pl.ds(off, width)]
        s = _dot_t(qb8, kk_ref[pl.ds(off, width), :]) + bias + jnp.concatenate([selm] * nh, axis=0)
        alpha, p = _online_update(s, m_ref, l_ref)
        acc_ref[...] = alpha * acc_ref[...] + _dot(_mx(p), vv_ref[pl.ds(off, width), :])

    def wide_body(jj, carry):
        attn_chunks(jj * FLASH_WIDE, FLASH_WIDE)
        return carry

    n_causal = i + 1
    lax.fori_loop(0, n_causal // FLASH_WIDE, wide_body, 0)
    for u in range(1, FLASH_WIDE):
        @pl.when(n_causal % FLASH_WIDE >= u)
        def _():
            attn_chunks(n_causal - n_causal % FLASH_WIDE + (u - 1), 1)
    o = acc_ref[...] / l_ref[...]
    for hp in range(nh // 2):
        o_ref[:, hp * LANES:(hp + 1) * LANES] = _mx(
            jnp.where(halves[0], o[2 * hp * tq:(2 * hp + 1) * tq], o[(2 * hp + 1) * tq:(2 * hp + 2) * tq]))


def _attn_b_prompt(qb, qi, small, kk, vv, kiki, bias_tiles, nb, s, topk, tq=256):
    nq = s // tq
    kern = functools.partial(_attn_b_kernel, tq=tq, s_len=s, topk=topk)
    qrow = lambda b, i: (b * nq + i, 0)
    kv = lambda b, i: (b, 0)
    rows = B_HEADS * tq
    return pl.pallas_call(
        kern,
        grid=(nb, nq),
        in_specs=[pl.BlockSpec((tq, B_HEADS * HEAD_DIM), qrow),
                  pl.BlockSpec((tq, IDX_HEADS * IDX_DIM), qrow),
                  pl.BlockSpec((tq, LANES), lambda b, i: (b * nq + i, 1)),
                  pl.BlockSpec((s, LANES), kv), pl.BlockSpec((s, LANES), kv), pl.BlockSpec((s, LANES), kv),
                  pl.BlockSpec(bias_tiles.shape, lambda b, i: (0, 0, 0, 0))],
        out_specs=pl.BlockSpec((tq, B_HEADS * HEAD_DIM), qrow),
        out_shape=jax.ShapeDtypeStruct((nb * s, B_HEADS * HEAD_DIM), MXU_DT),
        scratch_shapes=[pltpu.VMEM((tq, s), I32), pltpu.VMEM((tq, s), F32),
                        pltpu.VMEM((rows, LANES), F32), pltpu.VMEM((rows, LANES), F32),
                        pltpu.VMEM((rows, LANES), F32)],
        compiler_params=_cparams(("parallel", "arbitrary")),
        name="attn_b_prompt",
    )(qb, qi, small, kk, vv, kiki, bias_tiles)


def _cumsum_rows_kernel(x_ref, tri_ref, o_ref, carry_ref):
    j = pl.program_id(1)

    @pl.when(j == 0)
    def _():
        carry_ref[...] = jnp.zeros(carry_ref.shape, F32)

    tri = tri_ref[...]
    hi, mid, lo = _split3(x_ref[...])
    cum = _dot(tri, hi) + _dot(tri, mid) + _dot(tri, lo) + carry_ref[...]
    o_ref[...] = cum
    carry_ref[...] = cum[-1:, :]


def _cumsum_prompt(x, nb, s, blk=128):
    nj = s // blk
    w = x.shape[1]
    r = np.arange(blk)
    tri = jnp.asarray(r[:, None] >= r[None, :], dtype=MXU_DT)
    return pl.pallas_call(
        _cumsum_rows_kernel,
        grid=(nb, nj),
        in_specs=[pl.BlockSpec((blk, w), lambda b, j: (b * nj + j, 0)),
                  pl.BlockSpec((blk, blk), lambda b, j: (0, 0))],
        out_specs=pl.BlockSpec((blk, w), lambda b, j: (b * nj + j, 0)),
        out_shape=jax.ShapeDtypeStruct((nb * s, w), F32),
        scratch_shapes=[pltpu.VMEM((1, w), F32)],
        compiler_params=_cparams(("parallel", "arbitrary")),
        name="cumsum_prompt",
    )(x, tri)


def _online_update(s, m_ref, l_ref):
    m_old = m_ref[...]
    m_new = jnp.maximum(m_old, jnp.max(s, axis=-1, keepdims=True))
    alpha = jnp.exp(m_old - m_new)
    p = jnp.exp(s - _lanes(m_new, s.shape[1]))
    l_ref[...] = alpha * l_ref[...] + jnp.sum(p, axis=-1, keepdims=True)
    m_ref[...] = m_new
    return alpha, p


def _decode0_kernel(*refs, n_steps, npp, ps, topk, lam_init, group):
    (pt_ref, lamv_ref, gsub_ref, qa_ref, qi_ref, qb_ref, w8_ref, kna_ref, vna_ref, bnew_ref,
     taba_ref, tabb_ref, validb_ref) = refs[:13]
    pages = refs[13:13 + 5 * npp]
    cak, cav, cbk, cbv, cbi = (pages[0:npp], pages[npp:2 * npp], pages[2 * npp:3 * npp],
                               pages[3 * npp:4 * npp], pages[4 * npp:5 * npp])
    oa_ref, ob_ref, m_ref, l_ref, acc_ref, kb_ref, vb_ref, ki_ref, keys_ref = refs[13 + 5 * npp:]
    g = pl.program_id(0) % group
    step = pl.program_id(1)
    past = n_steps * npp * ps
    lk = kb_ref.shape[2]

    @pl.when(step == 0)
    def _():
        m_ref[...] = jnp.full(m_ref.shape, NEG, F32)
        l_ref[...] = jnp.zeros(l_ref.shape, F32)
        acc_ref[...] = jnp.zeros(acc_ref.shape, F32)

    qa = qa_ref[...]
    far = taba_ref[0]
    tail = taba_ref[jnp.where(step == n_steps - 1, 1, 0)]
    s = jnp.concatenate([_dot(qa, _mx(cak[j][...])) + (tail if j == npp - 1 else far) for j in range(npp)], axis=1)
    alpha, p = _online_update(s, m_ref, l_ref)
    p = _mx(p)
    for h in range(A_HEADS):
        rows = slice(h * 2 * T8, (h + 1) * 2 * T8)
        upd = alpha[rows] * acc_ref[rows]
        for j in range(npp):
            vh = _mx(cav[j][pl.ds(h, ps, stride=A_HEADS), :])
            upd = upd + _dot(p[rows, j * ps:(j + 1) * ps], vh)
        acc_ref[rows] = upd

    for j in range(npp):
        off = pl.multiple_of((step * npp + j) * ps, ps)
        kb_ref[g, :, pl.ds(off, ps)] = _mx(cbk[j][...])
        vb_ref[g, :, pl.ds(off, ps)] = _mx(cbv[j][...])
        ki_ref[g, :, pl.ds(off, ps)] = _mx(cbi[j][...])

    @pl.when(step == n_steps - 1)
    def _():
        s_new = _dot_t(qa, _mx(kna_ref[...])) + taba_ref[2][:, :kna_ref.shape[0]]
        alpha2, p2 = _online_update(s_new, m_ref, l_ref)
        vn = _mx(vna_ref[...])
        lam = _lambda(lamv_ref, lam_init)
        for h in range(A_HEADS):
            rows = slice(h * 2 * T8, (h + 1) * 2 * T8)
            o16 = (alpha2[rows] * acc_ref[rows] + _dot(_mx(p2[rows]), vn[:, h * LANES:(h + 1) * LANES])) / l_ref[rows]
            o = o16[:T8] - lam * o16[T8:]
            oa_ref[h] = _rms_rows(o, gsub_ref[...]) * (1.0 - lam_init)

        bn = bnew_ref[g]
        kb_ref[g, :, past:past + LANES] = _mx(bn[0])
        vb_ref[g, :, past:past + LANES] = _mx(bn[1])
        ki_ref[g, :, past:past + LANES] = _mx(bn[2])

    @pl.when((step == n_steps - 1) & (g == group - 1))
    def _():
        valid = validb_ref[...] == 0.0
        for gg in range(group):
            dots = jnp.maximum(_dot(qi_ref[gg], ki_ref[gg]), 0.0)
            w8 = w8_ref[gg] * IDX_HEAD_SCALE
            score = jnp.zeros((T8, lk), F32)
            for h in range(IDX_HEADS):
                score = score + w8[:, h:h + 1] * dots[h * T8:(h + 1) * T8]
            keys_ref[gg * T8:(gg + 1) * T8, :] = _score_keys(jnp.where(valid, score, -jnp.inf))
        rows = group * T8
        sel = _topk_select(keys_ref, lk, topk, jnp.full((rows, 1), True), _lane_iota((rows, lk)))
        selm = jnp.where(sel & jnp.concatenate([valid] * group, axis=0), 0.0, NEG)
        for gg in range(group):
            sg = selm[gg * T8:(gg + 1) * T8]
            sb = _dot(qb_ref[gg], kb_ref[gg]) + tabb_ref[...] + jnp.concatenate([sg] * B_HEADS, axis=0)
            mb = jnp.max(sb, axis=-1, keepdims=True)
            pb = jnp.exp(sb - mb)
            lb = jnp.sum(pb, axis=-1, keepdims=True)
            ob = _dot_t(_mx(pb), vb_ref[gg]) / lb
            for h in range(B_HEADS):
                ob_ref[gg, h] = ob[h * T8:(h + 1) * T8]


def _decode0(page_table, lamv, gsub, qa_bd, qi64, qb64, w8, kna, vna, bnew_t,
             cak_t, cav_r, cbk_t, cbv_t, cbi_t, taba, tabb, validb, layer, topk, lam_init):
    db, npg = page_table.shape
    ps = cak_t.shape[3]
    npp = PAGES_PER_STEP if npg % PAGES_PER_STEP == 0 else 1
    n_steps = npg // npp
    lk = tabb.shape[1]
    group = max(gsz for gsz in (8, 4, 2, 1) if db % gsz == 0)
    per_b = lambda b, p, pt: (b, 0, 0)
    per_g = lambda b, p, pt: (b // group, 0, 0)
    per_g4 = lambda b, p, pt: (b // group, 0, 0, 0)
    fixed2 = lambda b, p, pt: (0, 0)
    fixed3 = lambda b, p, pt: (0, 0, 0)

    def page_specs(arr):
        blk = (None, None) + arr.shape[2:]
        return [pl.BlockSpec(blk, lambda b, p, pt, j=j: (layer, pt[b * npg + p * npp + j], 0, 0)) for j in range(npp)]

    caches = (cak_t, cav_r, cbk_t, cbv_t, cbi_t)
    kern = functools.partial(_decode0_kernel, n_steps=n_steps, npp=npp, ps=ps, topk=topk, lam_init=lam_init,
                             group=group)
    grid_spec = pltpu.PrefetchScalarGridSpec(
        num_scalar_prefetch=1,
        grid=(db, n_steps),
        in_specs=[pl.BlockSpec((4, HEAD_DIM), fixed2), pl.BlockSpec((1, LANES), fixed2),
                  pl.BlockSpec((None,) + qa_bd.shape[1:], per_b),
                  pl.BlockSpec((group,) + qi64.shape[1:], per_g),
                  pl.BlockSpec((group,) + qb64.shape[1:], per_g),
                  pl.BlockSpec((group,) + w8.shape[1:], per_g),
                  pl.BlockSpec((None,) + kna.shape[1:], per_b),
                  pl.BlockSpec((None,) + vna.shape[1:], per_b),
                  pl.BlockSpec((group,) + bnew_t.shape[1:], per_g4),
                  pl.BlockSpec(taba.shape, fixed3), pl.BlockSpec(tabb.shape, fixed2),
                  pl.BlockSpec(validb.shape, fixed2)]
                 + [sp for c in caches for sp in page_specs(c)],
        out_specs=[pl.BlockSpec((None, A_HEADS, T8, LANES), lambda b, p, pt: (b, 0, 0, 0)),
                   pl.BlockSpec((group, B_HEADS, T8, HEAD_DIM), per_g4)],
        scratch_shapes=[pltpu.VMEM((64, LANES), F32), pltpu.VMEM((64, LANES), F32),
                        pltpu.VMEM((64, LANES), F32),
                        pltpu.VMEM((group, HEAD_DIM, lk), MXU_DT), pltpu.VMEM((group, HEAD_DIM, lk), MXU_DT),
                        pltpu.VMEM((group, IDX_DIM, lk), MXU_DT), pltpu.VMEM((group * T8, lk), I32)],
    )
    return pl.pallas_call(
        kern,
        grid_spec=grid_spec,
        out_shape=[jax.ShapeDtypeStruct((db, A_HEADS, T8, LANES), F32),
                   jax.ShapeDtypeStruct((db, B_HEADS, T8, HEAD_DIM), F32)],
        compiler_params=_cparams(("arbitrary", "arbitrary")),
        name="decode0",
    )(page_table.reshape(-1), lamv, gsub, qa_bd, qi64, qb64, w8, kna, vna, bnew_t, taba, tabb, validb,
      *[c for c in caches for _ in range(npp)])


def _expand_rows(x):
    hh, ww = x.shape
    return jnp.broadcast_to(x[:, None, :], (hh, T8, ww)).reshape(hh * T8, ww)


def _decode1_kernel(*refs, n_steps, npp, ps):
    pt_ref, q_ref, cq_ref, kn_ref, vn_ref, cnew_ref, maskn_ref, tri_ref = refs[:8]
    ck = refs[8:8 + npp]
    cv = refs[8 + npp:8 + 2 * npp]
    cf = refs[8 + 2 * npp:8 + 3 * npp]
    o_ref, m_ref, l_ref, acc_ref, suf_ref = refs[8 + 3 * npp:]
    step = pl.program_id(1)

    @pl.when(step == 0)
    def _():
        m_ref[...] = jnp.full(m_ref.shape, NEG, F32)
        l_ref[...] = jnp.zeros(l_ref.shape, F32)
        acc_ref[...] = jnp.zeros(acc_ref.shape, F32)
        suf_ref[...] = jnp.zeros(suf_ref.shape, F32)

    q = q_ref[...]
    cq = jnp.broadcast_to(cq_ref[...], (q.shape[0], LANES))
    tri = tri_ref[...]
    running = suf_ref[...]
    sufs = [None] * npp
    for j in reversed(range(npp)):
        hi, mid, lo = _split3(cf[j][...])
        cum = _dot(hi, tri) + _dot(mid, tri) + _dot(lo, tri)
        tot = jnp.broadcast_to(cum[:, ps - 1:ps], cum.shape)
        sufs[j] = running + tot - cum
        running = running + tot
    suf_ref[...] = running
    s = jnp.concatenate([_dot(q, _mx(ck[j][...])) for j in range(npp)], axis=1)
    s = s + _lanes(cq, npp * ps) + _expand_rows(jnp.concatenate(sufs, axis=1))
    alpha, p = _online_update(s, m_ref, l_ref)
    p = _mx(p)
    upd = _lanes(alpha, acc_ref.shape[1]) * acc_ref[...]
    for j in range(npp):
        upd = upd + _dot_t(p[:, j * ps:(j + 1) * ps], _mx(cv[j][...]))
    acc_ref[...] = upd

    @pl.when(step == n_steps - 1)
    def _():
        nn = kn_ref.shape[0]
        s_new = (_dot_t(q, _mx(kn_ref[...])) + cq[:, :nn] - _expand_rows(cnew_ref[...]) + maskn_ref[...])
        alpha2, p2 = _online_update(s_new, m_ref, l_ref)
        o_full = ((_lanes(alpha2, acc_ref.shape[1]) * acc_ref[...] + _dot(_mx(p2), _mx(vn_ref[...])))
                  / _lanes(l_ref[...], acc_ref.shape[1]))
        for h in range(C_HEADS):
            o_ref[h] = o_full[h * T8:(h + 1) * T8, h * HEAD_DIM:(h + 1) * HEAD_DIM]


def _decode1(page_table, q_bd, cq, kn, vn, cnew_t, cck_t, ccv_t, ccf_t, maskn, layer):
    db, npg = page_table.shape
    ps = cck_t.shape[3]
    assert ps == LANES, ps
    npp = PAGES_PER_STEP if npg % PAGES_PER_STEP == 0 else 1
    n_steps = npg // npp
    per_b = lambda b, p, pt: (b, 0, 0)
    fixed = lambda b, p, pt: (0, 0)
    r = np.arange(ps)
    tri = jnp.asarray(r[:, None] <= r[None, :], dtype=MXU_DT)

    def page_specs(arr):
        blk = (None, None) + arr.shape[2:]
        return [pl.BlockSpec(blk, lambda b, p, pt, j=j: (layer, pt[b * npg + (n_steps - 1 - p) * npp + j], 0, 0))
                for j in range(npp)]

    kern = functools.partial(_decode1_kernel, n_steps=n_steps, npp=npp, ps=ps)
    rows = C_HEADS * T8
    grid_spec = pltpu.PrefetchScalarGridSpec(
        num_scalar_prefetch=1,
        grid=(db, n_steps),
        in_specs=[pl.BlockSpec((None,) + q_bd.shape[1:], per_b),
                  pl.BlockSpec((None,) + cq.shape[1:], per_b),
                  pl.BlockSpec((None,) + kn.shape[1:], per_b),
                  pl.BlockSpec((None,) + vn.shape[1:], per_b),
                  pl.BlockSpec((None,) + cnew_t.shape[1:], per_b),
                  pl.BlockSpec(maskn.shape, fixed), pl.BlockSpec(tri.shape, fixed)]
                 + page_specs(cck_t) + page_specs(ccv_t) + page_specs(ccf_t),
        out_specs=pl.BlockSpec((None, C_HEADS, T8, HEAD_DIM), lambda b, p, pt: (b, 0, 0, 0)),
        scratch_shapes=[pltpu.VMEM((rows, LANES), F32), pltpu.VMEM((rows, LANES), F32),
                        pltpu.VMEM((rows, C_HEADS * HEAD_DIM), F32), pltpu.VMEM((C_HEADS, LANES), F32)],
    )
    return pl.pallas_call(
        kern,
        grid_spec=grid_spec,
        out_shape=jax.ShapeDtypeStruct((db, C_HEADS, T8, HEAD_DIM), F32),
        compiler_params=_cparams(("parallel", "arbitrary")),
        name="decode1",
    )(page_table.reshape(-1), q_bd, cq, kn, vn, cnew_t, maskn, tri,
      *([cck_t] * npp), *([ccv_t] * npp), *([ccf_t] * npp))


def _route_rows(x, g, whi, wlo, b):
    t = _rms_rows(x, g)
    thi, tlo = _split2(t)
    logits = _dot(thi, whi) + _dot(tlo, whi) + _dot(thi, wlo) + b
    lane = _lane_iota(logits.shape)
    big = jnp.int32(1 << 20)

    def first_max(v):
        mx = jnp.max(v, axis=-1, keepdims=True)
        idx = jnp.min(jnp.where(v == mx, lane, big), axis=-1, keepdims=True)
        return mx, idx

    glog = jnp.where(lane < N_GROUPS, logits, -jnp.inf)
    gmax, gidx = first_max(glog)
    grp_w = 1.0 / jnp.sum(jnp.exp(glog - gmax), axis=-1, keepdims=True)
    el = lane - N_GROUPS
    in_grp = (el >= 0) & (el < N_EXPERTS) & (jnp.right_shift(el, 3) == gidx)
    v1 = jnp.where(in_grp, logits, -jnp.inf)
    top1, i1 = first_max(v1)
    v2 = jnp.where(lane == i1, -jnp.inf, v1)
    top2, i2 = first_max(v2)
    e2 = jnp.exp(top2 - top1)
    w1 = grp_w / (1.0 + e2)
    w2 = grp_w * e2 / (1.0 + e2)
    route = jnp.where(lane == 0, (i1 - N_GROUPS).astype(F32),
                      jnp.where(lane == 1, (i2 - N_GROUPS).astype(F32),
                                jnp.where(lane == 2, w1, jnp.where(lane == 3, w2, 0.0))))
    return thi, route


def _router_params(g, w_group, b_group, w_router, b_router):
    pad = LANES - N_GROUPS - N_EXPERTS
    w = jnp.concatenate([w_group, w_router, jnp.zeros((D_MODEL, pad), F32)], axis=1)
    whi = _mx(w)
    wlo = _mx(w - whi.astype(F32))
    b = jnp.concatenate([b_group, b_router, jnp.zeros((pad,), F32)])[None, :]
    return g[None, :], whi, wlo, b


def _moe_dense_kernel(x_ref, t_ref, route_ref, wgu_ref, wd_ref, o_ref):
    e = pl.program_id(1)

    @pl.when(e == 0)
    def _():
        o_ref[...] = x_ref[...]

    t = t_ref[...]
    r = route_ref[...]
    upd = None
    for k in range(EXPERTS_PER_STEP):
        au = _dot(t, wgu_ref[k])
        a = au[:, :EXPERT_FF]
        u = au[:, EXPERT_FF:]
        ef = (e * EXPERTS_PER_STEP + k).astype(F32)
        gate = jnp.where(r[:, 0:1] == ef, r[:, 2:3], 0.0) + jnp.where(r[:, 1:2] == ef, r[:, 3:4], 0.0)
        hdn = a * (1.0 / (1.0 + jnp.exp(-a))) * u * gate
        y = _dot(_mx(hdn), wd_ref[k])
        upd = y if upd is None else upd + y
    o_ref[...] += upd


def _moe_dense(x, t, route, w_gate, w_up, w_down, tm=1536):
    n = x.shape[0]
    tm = _tile(n, tm)
    wgu = _mx(jnp.concatenate([w_gate.reshape(N_EXPERTS, D_MODEL, EXPERT_FF),
                               w_up.reshape(N_EXPERTS, D_MODEL, EXPERT_FF)], axis=-1))
    wd = _mx(w_down.reshape(N_EXPERTS, EXPERT_FF, D_MODEL))
    row = lambda i, e: (i, 0)
    once = pl.Buffered(1)
    return pl.pallas_call(
        _moe_dense_kernel,
        grid=(n // tm, N_EXPERTS // EXPERTS_PER_STEP),
        in_specs=[pl.BlockSpec((tm, D_MODEL), row, pipeline_mode=once),
                  pl.BlockSpec((tm, D_MODEL), row, pipeline_mode=once),
                  pl.BlockSpec((tm, LANES), row, pipeline_mode=once),
                  pl.BlockSpec((EXPERTS_PER_STEP, D_MODEL, 2 * EXPERT_FF), lambda i, e: (e, 0, 0)),
                  pl.BlockSpec((EXPERTS_PER_STEP, EXPERT_FF, D_MODEL), lambda i, e: (e, 0, 0))],
        out_specs=pl.BlockSpec((tm, D_MODEL), row),
        out_shape=jax.ShapeDtypeStruct((n, D_MODEL), F32),
        compiler_params=_cparams(("parallel", "arbitrary")),
        name="moe_dense",
    )(x, t, route, wgu, wd)


def _decode_tables(rel_bias, past, t_new, lk):
    t8 = np.arange(T8)
    lane = np.arange(LANES)
    far = np.full((T8, LANES), REL_BUCKETS - 1, np.int32)
    d_last = (past + t8[:, None]) - (past - LANES + lane[None, :])
    d_new = t8[:, None] - lane[None, :]
    ok_new = (d_new >= 0) & (lane[None, :] < t_new)
    ta = _bias_expand(rel_bias, np.stack([far, _t5_bucket_np(d_last), _t5_bucket_np(d_new)]), 0, A_HEADS)
    ta = jnp.where(jnp.asarray(ok_new)[None, None] | (jnp.arange(3) < 2)[None, :, None, None], ta, NEG)
    taba = jnp.broadcast_to(jnp.transpose(ta, (1, 0, 2, 3))[:, :, None], (3, A_HEADS, 2, T8, LANES))
    taba = taba.reshape(3, A_HEADS * 2 * T8, LANES)
    kpos = np.arange(lk)
    d_b = (past + t8[:, None]) - kpos[None, :]
    tabb = _bias_expand(rel_bias, _t5_bucket_np(d_b), A_HEADS, B_HEADS).reshape(B_HEADS * T8, lk)
    valid = (kpos[None, :] < past) | ((d_b >= 0) & (kpos[None, :] < past + t_new))
    validb = jnp.asarray(np.where(valid, 0.0, NEG), F32)
    return taba, tabb, validb


def _pad_rows(x, rows):
    pad = [(0, 0)] * x.ndim
    pad[1] = (0, rows - x.shape[1])
    return jnp.pad(x, pad)


def kernel(x_prompt, x_sample, cache_a_k, cache_a_v, cache_b_k, cache_b_v, cache_b_kidx, cache_c_k, cache_c_v, cache_c_logf, page_table, rel_bias, ab_norm, ab_w_in, a_q_norm, a_k_norm, b_q_norm, b_k_norm, a_lambda_q1, a_lambda_k1, a_lambda_q2, a_lambda_k2, a_sub_norm, ab_w_out, c_norm, c_w_in, c_forget_bias, c_q_norm, c_k_norm, c_w_out, ffn_norm, moe_w_group, moe_b_group, moe_w_router, moe_b_router, moe_w_gate, moe_w_up, moe_w_down):
    nb, s, d = x_prompt.shape
    db, ts, _ = x_sample.shape
    npg = page_table.shape[1]
    pool, ps = cache_a_k.shape[1], cache_a_k.shape[2]
    past = npg * ps
    n_p = nb * s
    n_s = db * ts
    depth = ffn_norm.shape[0]
    topk_p = min(IDX_TOPK_MAX, s // 4)
    topk_s = min(IDX_TOPK_MAX, (past + ts) // 4)
    lk = past + LANES

    cak_t = jnp.transpose(cache_a_k, (0, 1, 3, 4, 5, 2)).reshape(-1, pool, 2 * A_HEADS * HEAD_DIM, ps)
    cav_r = cache_a_v.reshape(-1, pool, ps * A_HEADS, 2 * HEAD_DIM)
    cbk_t = jnp.transpose(cache_b_k, (0, 1, 3, 2))
    cbv_t = jnp.transpose(cache_b_v, (0, 1, 3, 2))
    cbi_t = jnp.transpose(cache_b_kidx, (0, 1, 3, 2))
    cck_t = jnp.transpose(cache_c_k, (0, 1, 3, 4, 2)).reshape(-1, pool, C_HEADS * HEAD_DIM, ps)
    ccv_t = jnp.transpose(cache_c_v, (0, 1, 3, 4, 2)).reshape(-1, pool, C_HEADS * HEAD_DIM, ps)
    ccf_t = jnp.transpose(cache_c_logf, (0, 1, 3, 2))

    x = jnp.concatenate([x_prompt.reshape(n_p, d), x_sample.reshape(n_s, d)], axis=0)
    outs_p = {k: [] for k in ("ak", "av", "bk", "bv", "bi", "ck", "cv", "cf")}
    outs_s = {k: [] for k in ("ak", "av", "bk", "bv", "bi", "ck", "cv", "cf")}

    def smp(a):
        return a[n_p:].reshape(db, ts, a.shape[1])

    for layer in range(depth):
        router = _router_params(ffn_norm[layer], moe_w_group[layer], moe_b_group[layer],
                                moe_w_router[layer], moe_b_router[layer])
        if layer % 2 == 0:
            e = layer // 2
            lam_init = 0.8 - 0.6 * math.exp(-0.3 * layer)
            lamv = jnp.stack([a_lambda_q1[e], a_lambda_k1[e], a_lambda_q2[e], a_lambda_k2[e]])
            gsub = a_sub_norm[e][None, :]
            (qa, ka_s, kat, katbf, qb, small, smallt, kk, vv, kiki, va, vabf, qi) = _proj0(
                x, ab_norm[e], ab_w_in[e], a_q_norm[e], a_k_norm[e], b_q_norm[e], b_k_norm[e], n_p, s)
            tq_a = min(FLASH_TQ, s)
            bias_a = _bias_expand(rel_bias, _prompt_bucket_tiles(tq_a), 0, A_HEADS)
            oa_p = _attn_a_prompt(qa, katbf, vabf, lamv, gsub, bias_a, nb, s, lam_init, tq=tq_a)
            tq_b = 256
            bias_b = _bias_expand(rel_bias, _prompt_bucket_tiles(tq_b), A_HEADS, B_HEADS)
            ob_p = _attn_b_prompt(qb, qi, small, kk, vv, kiki, bias_b, nb, s, topk_p, tq=tq_b)
            qa_s = _pad_rows(smp(qa), T8)
            hc = np.arange(2 * A_HEADS)
            colmask = jnp.asarray((np.arange(qa_s.shape[2])[None, :] // HEAD_DIM) == hc[:, None], MXU_DT)
            qa_bd = (qa_s[:, None, :, :] * colmask[None, :, None, :]).reshape(db, 2 * A_HEADS * T8, -1)

            def heads_rows(a, nh):
                a = _pad_rows(a, T8).reshape(db, T8, nh, HEAD_DIM)
                return jnp.transpose(a, (0, 2, 1, 3)).reshape(db, nh * T8, HEAD_DIM)

            qi64 = heads_rows(smp(qi), IDX_HEADS)
            qb64 = heads_rows(smp(qb), B_HEADS)
            small_s = smp(small)
            w8 = _pad_rows(small_s[:, :, 192:200], T8)
            ka_s3 = ka_s.reshape(db, ts, -1)
            va_s3 = smp(va)
            kna = _pad_rows(ka_s3, 16)
            vna = _pad_rows(va_s3, 16)
            bnew_t = jnp.transpose(_pad_rows(small_s[:, :, :192], LANES).reshape(db, LANES, 3, HEAD_DIM), (0, 2, 3, 1))
            taba, tabb, validb = _decode_tables(rel_bias, past, ts, lk)
            oa_d, ob_d = _decode0(page_table, lamv, gsub, qa_bd, qi64, qb64, w8, kna, vna, bnew_t,
                                  cak_t, cav_r, cbk_t, cbv_t, cbi_t, taba, tabb, validb, e, topk_s, lam_init)
            oa_s = jnp.transpose(oa_d[:, :, :ts], (0, 2, 1, 3)).reshape(n_s, A_HEADS * LANES)
            ob_s = jnp.transpose(ob_d[:, :, :ts], (0, 2, 1, 3)).reshape(n_s, B_HEADS * HEAD_DIM)
            w_out = ab_w_out[e]
            x, t_moe, route = _outproj(x, [(oa_p, _mx(oa_s), w_out[:A_HEADS * LANES]),
                                           (ob_p, _mx(ob_s), w_out[A_HEADS * LANES:])], n_p, router)
            outs_p["ak"].append(jnp.transpose(kat.reshape(nb, A_HEADS, 2, HEAD_DIM, s), (0, 4, 1, 2, 3)))
            outs_p["av"].append(va[:n_p].reshape(nb, s, A_HEADS, 2 * HEAD_DIM))
            outs_p["bk"].append(jnp.transpose(smallt[:, 0:64], (0, 2, 1)))
            outs_p["bv"].append(jnp.transpose(smallt[:, 64:128], (0, 2, 1)))
            outs_p["bi"].append(jnp.transpose(smallt[:, 128:192], (0, 2, 1)))
            outs_s["ak"].append(ka_s3.reshape(db, ts, A_HEADS, 2, HEAD_DIM))
            outs_s["av"].append(va_s3.reshape(db, ts, A_HEADS, 2 * HEAD_DIM))
            outs_s["bk"].append(small_s[:, :, 0:64])
            outs_s["bv"].append(small_s[:, :, 64:128])
            outs_s["bi"].append(small_s[:, :, 128:192])
        else:
            o = layer // 2
            q, k_s, kt, ktbf, v_s, vt, vbf, logf128, logft = _proj1(
                x, c_norm[o], c_w_in[o], c_forget_bias[o], c_q_norm[o], c_k_norm[o], n_p, s)
            logf = logf128[:, :C_HEADS]
            cum_p = _cumsum_prompt(logf128[:n_p], nb, s)[:, :C_HEADS]
            ckt = jnp.transpose(cum_p.reshape(nb, s, C_HEADS // 2, 2), (0, 2, 3, 1))
            oc_p = _attn_c_prompt(q, ktbf, vbf, cum_p, ckt, nb, s, tq=min(FLASH_TQ, s))
            logf_s = smp(logf)
            run = jnp.zeros_like(logf_s[:, 0])
            c_rows = []
            for t in range(ts):
                run = run + logf_s[:, t]
                c_rows.append(run)
            c_new = jnp.stack(c_rows, axis=1)
            cq = jnp.transpose(_pad_rows(c_new, T8), (0, 2, 1)).reshape(db, C_HEADS * T8, 1)
            cnew_t = jnp.transpose(_pad_rows(c_new, 16), (0, 2, 1))
            q_s = _pad_rows(smp(q), T8)
            hmask = jnp.asarray((np.arange(q_s.shape[2])[None, :] // HEAD_DIM) == np.arange(C_HEADS)[:, None], MXU_DT)
            q_bd = (q_s[:, None, :, :] * hmask[None, :, None, :]).reshape(db, C_HEADS * T8, -1)
            k_s3 = k_s.reshape(db, ts, -1)
            v_s3 = v_s.reshape(db, ts, -1)
            kn = _pad_rows(k_s3, 16)
            vn = _pad_rows(v_s3, 16)
            t8 = np.arange(T8)
            okn = (t8[:, None] >= np.arange(16)[None, :]) & (np.arange(16)[None, :] < ts)
            maskn = jnp.asarray(np.tile(np.where(okn, 0.0, NEG), (C_HEADS, 1)), F32)
            oc_d = _decode1(page_table, q_bd, cq, kn, vn, cnew_t, cck_t, ccv_t, ccf_t, maskn, o)
            oc_s = jnp.transpose(oc_d[:, :, :ts], (0, 2, 1, 3)).reshape(n_s, C_HEADS * HEAD_DIM)
            x, t_moe, route = _outproj(x, [(oc_p, _mx(oc_s), c_w_out[o])], n_p, router)
            outs_p["ck"].append(jnp.transpose(kt.reshape(nb, C_HEADS, HEAD_DIM, s), (0, 3, 1, 2)))
            outs_p["cv"].append(jnp.transpose(vt.reshape(nb, C_HEADS, HEAD_DIM, s), (0, 3, 1, 2)))
            outs_p["cf"].append(jnp.transpose(logft[:, :C_HEADS], (0, 2, 1)))
            outs_s["ck"].append(k_s3.reshape(db, ts, C_HEADS, HEAD_DIM))
            outs_s["cv"].append(v_s3.reshape(db, ts, C_HEADS, HEAD_DIM))
            outs_s["cf"].append(logf_s)
        x = _moe_dense(x, t_moe, route, moe_w_gate[layer], moe_w_up[layer], moe_w_down[layer])

    keys = ("ak", "av", "bk", "bv", "bi", "ck", "cv", "cf")
    return ((x[:n_p].reshape(nb, s, d), x[n_p:].reshape(db, ts, d))
            + tuple(jnp.stack(outs_p[k]) for k in keys)
            + tuple(jnp.stack(outs_s[k]) for k in keys))
```
